```python
import math
import jax
import jax.numpy as jnp
from jax import lax
import numpy as np

D_MODEL = 1024
BATCH = 4
SEQ = 4096
DEPTH = 1
DEC_BATCH = 1
DEC_SEQ = 16384
PAST_LEN = 128

GRID_W = 64
N_HEADS = 8
HEAD_DIM = 64
D_ATTN = N_HEADS * HEAD_DIM
WIN_R_MAX = 8
WIN_C = 16
D_HYENA = 512
SHORT_K = 3
FILTER_EMB = 33
FILTER_HIDDEN = 64
FILTER_OUT_SCALE = 0.05
DECAY_TARGET = 1e-2
FAST_DECAY_PCT = 0.3
SLOW_DECAY_PCT = 1.5
D_FF = -(-8 * D_MODEL // (3 * 256)) * 256
D_IN = 3 * D_ATTN + 3 * D_HYENA + 2 * D_MODEL
EPS = 1e-6

kernel_name = 'hybrid_natten_hyena_encoder'


def rmsnorm(x, g):
    xf = x.astype(jnp.float32)
    inv = lax.rsqrt(jnp.mean(xf * xf, axis=-1, keepdims=True) + EPS)
    return (xf * inv).astype(x.dtype) * g


def neighbourhood_attention(q, k, v, rpb):
    b, L = q.shape[0], q.shape[1]
    rows = L // GRID_W
    kr = min(WIN_R_MAX, rows)
    q = q.reshape(b, rows, GRID_W, N_HEADS, HEAD_DIM)
    k = k.reshape(b, rows, GRID_W, N_HEADS, HEAD_DIM)
    v = v.reshape(b, rows, GRID_W, N_HEADS, HEAD_DIM)
    r = jnp.arange(rows)
    row_start = jnp.clip(r - kr // 2, 0, rows - kr)
    key_rows = row_start[:, None] + jnp.arange(kr)[None, :]
    k_blk = k[:, key_rows]
    v_blk = v[:, key_rows]
    c = jnp.arange(GRID_W)
    col_start = jnp.clip(c - WIN_C // 2, 0, GRID_W - WIN_C)
    col_in = (c[None, :] >= col_start[:, None]) & (c[None, :] < col_start[:, None] + WIN_C)
    dr = key_rows - r[:, None] + (WIN_R_MAX - 1)
    dc = jnp.clip(c[None, :] - c[:, None], -(WIN_C - 1), WIN_C - 1) + (WIN_C - 1)
    bias = rpb[:, dr[:, None, :, None], dc[None, :, None, :]]
    bias = jnp.transpose(bias, (1, 0, 2, 3, 4)).astype(jnp.float32)
    s = jnp.einsum('brqhd,brikhd->brhqik', q, k_blk, preferred_element_type=jnp.float32)
    s = s * (HEAD_DIM ** -0.5) + bias
    s = jnp.where(col_in[:, None, :], s, -jnp.inf)
    p = jax.nn.softmax(s.reshape(s.shape[:4] + (kr * GRID_W,)), axis=-1).reshape(s.shape)
    o = jnp.einsum('brhqik,brikhd->brqhd', p.astype(v.dtype), v_blk)
    return o.reshape(b, L, D_ATTN)


def short_conv(x, w, bias):
    L = x.shape[1]
    pad = SHORT_K // 2
    xp = jnp.pad(x, ((0, 0), (pad, SHORT_K - 1 - pad), (0, 0)))
    y = xp[:, 0:L] * w[0]
    for j in range(1, SHORT_K):
        y = y + xp[:, j:j + L] * w[j]
    return y + bias


def implicit_filters(L, w1, b1, w2, b2, w3, b3, w4, freq):
    f32 = jnp.float32
    t = jnp.linspace(0.0, 1.0, L, dtype=f32)[:, None]
    bands = (FILTER_EMB - 1) // 2
    omega = 2.0 * math.pi * jnp.arange(L, dtype=f32) / L
    fb = jnp.linspace(1e-4, bands - 1, bands, dtype=f32)
    ang = omega[:, None] * fb[None, :]
    z = jnp.concatenate([t, jnp.cos(ang), -jnp.sin(ang)], axis=-1)
    fr = freq.astype(f32)
    h = jnp.sin(fr * (z @ w1.astype(f32) + b1.astype(f32)))
    h = jnp.sin(fr * (h @ w2.astype(f32) + b2.astype(f32)))
    h = jnp.sin(fr * (h @ w3.astype(f32) + b3.astype(f32)))
    h = h @ w4.astype(f32)
    max_decay = math.log(DECAY_TARGET) / FAST_DECAY_PCT
    min_decay = math.log(DECAY_TARGET) / SLOW_DECAY_PCT
    deltas = jnp.abs(jnp.linspace(min_decay, max_decay, D_HYENA, dtype=f32))
    decay = jnp.exp(-t * deltas[None, :])
    return h[:, :D_HYENA] * decay, h[:, D_HYENA:] * decay


def bidirectional_fftconv(u, h_fwd, h_bwd, d_skip):
    L = u.shape[1]
    kf = jnp.pad(h_fwd, ((0, L), (0, 0)))
    kb = jnp.concatenate([h_bwd[:1], jnp.zeros((L, D_HYENA), jnp.float32), h_bwd[:0:-1]], axis=0)
    k_f = jnp.fft.rfft(kf + kb, n=2 * L, axis=0)
    uf = u.astype(jnp.float32)
    y = jnp.fft.irfft(jnp.fft.rfft(uf, n=2 * L, axis=1) * k_f[None], n=2 * L, axis=1)[:, :L]
    return (y + uf * d_skip.astype(jnp.float32)).astype(u.dtype)


def hybrid_layer(x, norm_mix, w_in, rpb, conv_w, conv_b, filt_w1, filt_b1, filt_w2, filt_b2,
                 filt_w3, filt_b3, filt_w4, filt_freq, hyena_d, w_br_attn, w_br_hyena, w_out,
                 norm_ffn, w_gate, w_up, w_down):
    b, L, _ = x.shape
    h = rmsnorm(x, norm_mix)
    z = h @ w_in
    q, k, v, hy, g_attn, g_hyena = jnp.split(
        z, [D_ATTN, 2 * D_ATTN, 3 * D_ATTN, 3 * D_ATTN + 3 * D_HYENA,
            3 * D_ATTN + 3 * D_HYENA + D_MODEL], axis=-1)
    heads = (b, L, N_HEADS, HEAD_DIM)
    y_attn = neighbourhood_attention(q.reshape(heads), k.reshape(heads), v.reshape(heads), rpb)
    hy = short_conv(hy, conv_w, conv_b)
    x0, x1, hv = jnp.split(hy, 3, axis=-1)
    h_fwd, h_bwd = implicit_filters(L, filt_w1, filt_b1, filt_w2, filt_b2, filt_w3, filt_b3,
                                    filt_w4, filt_freq)
    y_hyena = x0 * bidirectional_fftconv(x1 * hv, h_fwd, h_bwd, hyena_d)
    merged = (jax.nn.sigmoid(g_attn) * (y_attn @ w_br_attn)
              + jax.nn.sigmoid(g_hyena) * (y_hyena @ w_br_hyena))
    x = x + merged @ w_out
    h = rmsnorm(x, norm_ffn)
    x = x + (jax.nn.silu(h @ w_gate) * (h @ w_up)) @ w_down
    return x


def trunk(x, norm_mix, w_in, rpb, conv_w, conv_b, filt_w1, filt_b1, filt_w2, filt_b2,
          filt_w3, filt_b3, filt_w4, filt_freq, hyena_d, w_br_attn, w_br_hyena, w_out,
          norm_ffn, w_gate, w_up, w_down, norm_final):
    for l in range(DEPTH):
        x = hybrid_layer(x, norm_mix[l], w_in[l], rpb[l], conv_w[l], conv_b[l],
                         filt_w1[l], filt_b1[l], filt_w2[l], filt_b2[l], filt_w3[l], filt_b3[l],
                         filt_w4[l], filt_freq[l], hyena_d[l], w_br_attn[l], w_br_hyena[l],
                         w_out[l], norm_ffn[l], w_gate[l], w_up[l], w_down[l])
    return rmsnorm(x, norm_final)


def setup_inputs(seed: int = 0) -> dict:
    key = jax.random.key(seed)
    ks = jax.random.split(key, 24)

    def nrm(k, shape, scale):
        return jax.random.normal(k, shape, jnp.float32) * scale

    return {
        'x_prompt': nrm(ks[0], (BATCH, SEQ, D_MODEL), 1.0),
        'x_sample': nrm(ks[1], (DEC_BATCH, DEC_SEQ, D_MODEL), 1.0),
        'norm_mix': 1.0 + nrm(ks[2], (DEPTH, D_MODEL), 0.01),
        'w_in': nrm(ks[3], (DEPTH, D_MODEL, D_IN), D_MODEL ** -0.5),
        'rpb': nrm(ks[4], (DEPTH, N_HEADS, 2 * WIN_R_MAX - 1, 2 * WIN_C - 1), 0.02),
        'conv_w': nrm(ks[5], (DEPTH, SHORT_K, 3 * D_HYENA), SHORT_K ** -0.5),
        'conv_b': nrm(ks[6], (DEPTH, 3 * D_HYENA), 0.01),
        'filt_w1': nrm(ks[7], (DEPTH, FILTER_EMB, FILTER_HIDDEN), FILTER_EMB ** -0.5),
        'filt_b1': nrm(ks[8], (DEPTH, FILTER_HIDDEN), 0.1),
        'filt_w2': nrm(ks[9], (DEPTH, FILTER_HIDDEN, FILTER_HIDDEN), FILTER_HIDDEN ** -0.5),
        'filt_b2': nrm(ks[10], (DEPTH, FILTER_HIDDEN), 0.1),
        'filt_w3': nrm(ks[11], (DEPTH, FILTER_HIDDEN, FILTER_HIDDEN), FILTER_HIDDEN ** -0.5),
        'filt_b3': nrm(ks[12], (DEPTH, FILTER_HIDDEN), 0.1),
        'filt_w4': nrm(ks[13], (DEPTH, FILTER_HIDDEN, 2 * D_HYENA), FILTER_HIDDEN ** -0.5 * FILTER_OUT_SCALE),
        'filt_freq': 1.0 + nrm(ks[14], (DEPTH, FILTER_HIDDEN), 0.01),
        'hyena_d': nrm(ks[15], (DEPTH, D_HYENA), 1.0),
        'w_br_attn': nrm(ks[16], (DEPTH, D_ATTN, D_MODEL), D_ATTN ** -0.5),
        'w_br_hyena': nrm(ks[17], (DEPTH, D_HYENA, D_MODEL), D_HYENA ** -0.5),
        'w_out': nrm(ks[18], (DEPTH, D_MODEL, D_MODEL), D_MODEL ** -0.5),
        'norm_ffn': 1.0 + nrm(ks[19], (DEPTH, D_MODEL), 0.01),
        'w_gate': nrm(ks[20], (DEPTH, D_MODEL, D_FF), D_MODEL ** -0.5),
        'w_up': nrm(ks[21], (DEPTH, D_MODEL, D_FF), D_MODEL ** -0.5),
        'w_down': nrm(ks[22], (DEPTH, D_FF, D_MODEL), D_FF ** -0.5),
        'norm_final': 1.0 + nrm(ks[23], (D_MODEL,), 0.01),
    }


def reference(x_prompt, x_sample, norm_mix, w_in, rpb, conv_w, conv_b, filt_w1, filt_b1,
              filt_w2, filt_b2, filt_w3, filt_b3, filt_w4, filt_freq, hyena_d, w_br_attn,
              w_br_hyena, w_out, norm_ffn, w_gate, w_up, w_down, norm_final):
    y_prompt = trunk(x_prompt, norm_mix, w_in, rpb, conv_w, conv_b, filt_w1, filt_b1, filt_w2,
                     filt_b2, filt_w3, filt_b3, filt_w4, filt_freq, hyena_d, w_br_attn,
                     w_br_hyena, w_out, norm_ffn, w_gate, w_up, w_down, norm_final)
    y_sample = trunk(x_sample, norm_mix, w_in, rpb, conv_w, conv_b, filt_w1, filt_b1, filt_w2,
                     filt_b2, filt_w3, filt_b3, filt_w4, filt_freq, hyena_d, w_br_attn,
                     w_br_hyena, w_out, norm_ffn, w_gate, w_up, w_down, norm_final)
    return (y_prompt, y_sample)
```

```python
import functools
import math

import numpy as np
import jax
import jax.numpy as jnp
from jax import lax
from jax.experimental import pallas as pl
from jax.experimental.pallas import tpu as pltpu

F32 = jnp.float32
BF16 = jnp.bfloat16

GRID_W = 64
N_HEADS = 8
HEAD_DIM = 64
D_ATTN = N_HEADS * HEAD_DIM
WIN_R = 8
WIN_C = 16
D_HYENA = 512
SHORT_K = 3
FILTER_EMB = 33
FILTER_HIDDEN = 64
DECAY_TARGET = 1e-2
FAST_DECAY_PCT = 0.3
SLOW_DECAY_PCT = 1.5
EPS = 1e-6

FFT_N2 = 128
ATT_ROWS = 8
NEG_BIG = -1e30
VMEM_LIMIT = 56 * 1024 * 1024


def _cparams(sem):
    return pltpu.CompilerParams(dimension_semantics=sem, vmem_limit_bytes=VMEM_LIMIT)


def _norm_inproj_kernel(x_ref, g_ref, w_ref, o_ref, h_ref):
    @pl.when(pl.program_id(1) == 0)
    def _():
        x = x_ref[...]
        inv = lax.rsqrt(jnp.mean(x * x, axis=-1, keepdims=True) + EPS)
        h_ref[...] = ((x * inv) * g_ref[...]).astype(BF16)

    o_ref[...] = jnp.dot(h_ref[...], w_ref[...], preferred_element_type=F32)


def _norm_inproj(x2, g, w_bf16):
    n, d = x2.shape
    d_in = w_bf16.shape[1]
    tm = min(1024, n)
    tn = 1280
    return pl.pallas_call(
        _norm_inproj_kernel,
        grid=(n // tm, d_in // tn),
        in_specs=[
            pl.BlockSpec((tm, d), lambda i, j: (i, 0)),
            pl.BlockSpec((1, d), lambda i, j: (0, 0)),
            pl.BlockSpec((d, tn), lambda i, j: (0, j)),
        ],
        out_specs=pl.BlockSpec((tm, tn), lambda i, j: (i, j)),
        out_shape=jax.ShapeDtypeStruct((n, d_in), F32),
        scratch_shapes=[pltpu.VMEM((tm, d), BF16)],
        compiler_params=_cparams(("parallel", "arbitrary")),
        name="norm_inproj",
    )(x2, g.reshape(1, d), w_bf16)


def _build_bias_tiles(rpb_ref, p_ref):
    q = lax.broadcasted_iota(jnp.int32, (GRID_W, 128), 0)
    lane = lax.broadcasted_iota(jnp.int32, (GRID_W, 128), 1)
    k = lane & (GRID_W - 1)
    first_half = lane < GRID_W
    cs = jnp.clip(q - WIN_C // 2, 0, GRID_W - WIN_C)
    valid = (k >= cs) & (k < cs + WIN_C)
    dc = jnp.clip(k - q, -(WIN_C - 1), WIN_C - 1) + (WIN_C - 1)
    n_dr = 2 * WIN_R - 1
    n_dc = 2 * WIN_C - 1

    def body(idx, carry):
        par = idx // 64
        h = (idx // 8) % 8
        g = idx % 8
        d0 = jnp.minimum(2 * g + par, n_dr - 1)
        d1 = jnp.minimum(2 * g + par + 1, n_dr - 1)
        base0 = (h * n_dr + d0) * n_dc
        base1 = (h * n_dr + d1) * n_dc
        acc = jnp.zeros((GRID_W, 128), F32)
        for off in range(n_dc):
            val = jnp.where(first_half, rpb_ref[base0 + off], rpb_ref[base1 + off])
            acc = jnp.where(dc == off, val, acc)
        p_ref[idx] = jnp.where(valid, acc, NEG_BIG)
        return carry

    lax.fori_loop(0, 2 * N_HEADS * 8, body, 0)


def _attn_kernel(rpb_ref, q_ref, kp_ref, kc_ref, kn_ref, vp_ref, vc_ref, vn_ref, o_ref,
                 kcat_ref, vcat_ref, p_ref, *, rows):
    b = pl.program_id(0)
    i = pl.program_id(1)
    blk = ATT_ROWS * GRID_W

    @pl.when((b == 0) & (i == 0))
    def _():
        _build_bias_tiles(rpb_ref, p_ref)

    kcat_ref[0:blk] = kp_ref[0].astype(BF16)
    kcat_ref[blk:2 * blk] = kc_ref[0].astype(BF16)
    kcat_ref[2 * blk:3 * blk] = kn_ref[0].astype(BF16)
    vcat_ref[0:blk] = vp_ref[0].astype(BF16)
    vcat_ref[blk:2 * blk] = vc_ref[0].astype(BF16)
    vcat_ref[2 * blk:3 * blk] = vn_ref[0].astype(BF16)

    lane = lax.broadcasted_iota(jnp.int32, (GRID_W, 128), 1)
    first_half = lane < HEAD_DIM
    nkeys = WIN_R * GRID_W

    def row_body(rr, carry):
        r = i * ATT_ROWS + rr
        rs = jnp.clip(r - WIN_R // 2, 0, rows - WIN_R)
        ls = rs - (i - 1) * ATT_ROWS
        s = rs - r + (WIN_R - 1)
        par = s & 1
        g0 = s >> 1
        koff = pl.multiple_of(ls * GRID_W, GRID_W)
        qoff = pl.multiple_of(rr * GRID_W, GRID_W)
        q = q_ref[0, pl.ds(qoff, GRID_W), :] * (HEAD_DIM ** -0.5)
        kwin = kcat_ref[pl.ds(koff, nkeys), :]
        vwin = vcat_ref[pl.ds(koff, nkeys), :]
        outs = []
        for p in range(N_HEADS // 2):
            qp = q[:, 128 * p:128 * (p + 1)]
            kp = kwin[:, 128 * p:128 * (p + 1)]
            vp = vwin[:, 128 * p:128 * (p + 1)]
            o_pair = None
            for hh in range(2):
                h = 2 * p + hh
                keep = first_half if hh == 0 else jnp.logical_not(first_half)
                qh = jnp.where(keep, qp, 0.0).astype(BF16)
                sc = lax.dot_general(qh, kp, (((1,), (1,)), ((), ())),
                                     preferred_element_type=F32)
                base = (par * N_HEADS + h) * 8 + g0
                ch = [sc[:, 128 * c:128 * (c + 1)] + p_ref[base + c] for c in range(4)]
                m = jnp.maximum(jnp.maximum(ch[0], ch[1]), jnp.maximum(ch[2], ch[3]))
                m = jnp.max(m, axis=-1, keepdims=True)
                e = [jnp.exp(c - m) for c in ch]
                den = jnp.sum((e[0] + e[1]) + (e[2] + e[3]), axis=-1, keepdims=True)
                pm = jnp.concatenate(e, axis=1).astype(BF16)
                o = jnp.dot(pm, vp, preferred_element_type=F32) / den
                o_pair = o if hh == 0 else jnp.where(first_half, o_pair, o)
            outs.append(o_pair)
        o_ref[0, pl.ds(qoff, GRID_W), :] = jnp.concatenate(outs, axis=1).astype(o_ref.dtype)
        return carry

    lax.fori_loop(0, ATT_ROWS, row_body, 0)


def _attention(z3, rpb_flat):
    bsz, L, _ = z3.shape
    rows = L // GRID_W
    assert rows % ATT_ROWS == 0 and rows >= 2 * ATT_ROWS
    nblk = rows // ATT_ROWS
    blk = ATT_ROWS * GRID_W

    def spec(col, shift):
        return pl.BlockSpec(
            (1, blk, D_ATTN),
            lambda b, i: (b, jnp.clip(i + shift, 0, nblk - 1), col))

    return pl.pallas_call(
        functools.partial(_attn_kernel, rows=rows),
        grid=(bsz, nblk),
        in_specs=[
            pl.BlockSpec(memory_space=pltpu.SMEM),
            spec(0, 0),
            spec(1, -1), spec(1, 0), spec(1, 1),
            spec(2, -1), spec(2, 0), spec(2, 1),
        ],
        out_specs=pl.BlockSpec((1, blk, D_ATTN), lambda b, i: (b, i, 0)),
        out_shape=jax.ShapeDtypeStruct((bsz, L, D_ATTN), BF16),
        scratch_shapes=[
            pltpu.VMEM((3 * blk, D_ATTN), BF16),
            pltpu.VMEM((3 * blk, D_ATTN), BF16),
            pltpu.VMEM((2 * N_HEADS * 8, GRID_W, 128), F32),
        ],
        compiler_params=_cparams(("arbitrary", "arbitrary")),
        name="nbr_attention",
    )(rpb_flat, z3, z3, z3, z3, z3, z3, z3)


def _conv_gate_kernel(xm_ref, xp_ref, xn_ref, w_ref, b_ref, u_ref, x0_ref, *, nt):
    i = pl.program_id(1)
    x = xm_ref[0]
    tm = x.shape[0]
    row = lax.broadcasted_iota(jnp.int32, x.shape, 0)
    prev_row = jnp.where(i == 0, 0.0, xp_ref[0, 7:8, :])
    next_row = jnp.where(i == nt - 1, 0.0, xn_ref[0, 0:1, :])
    xm1 = jnp.where(row == 0, prev_row, pltpu.roll(x, 1, 0))
    xp1 = jnp.where(row == tm - 1, next_row, pltpu.roll(x, tm - 1, 0))
    y = xm1 * w_ref[0:1, :] + x * w_ref[1:2, :] + xp1 * w_ref[2:3, :] + b_ref[...]
    c = D_HYENA
    x0_ref[0] = y[:, 0:c]
    u_ref[0] = y[:, c:2 * c] * y[:, 2 * c:3 * c]


def _conv_gate(z3, conv_w, conv_b):
    bsz, L, _ = z3.shape
    c3 = 3 * D_HYENA
    tm = min(512, L)
    nt = L // tm
    hb = tm // 8
    nh = L // 8
    return pl.pallas_call(
        functools.partial(_conv_gate_kernel, nt=nt),
        grid=(bsz, nt),
        in_specs=[
            pl.BlockSpec((1, tm, c3), lambda b, i: (b, i, 1)),
            pl.BlockSpec((1, 8, c3), lambda b, i: (b, jnp.maximum(i * hb - 1, 0), 1)),
            pl.BlockSpec((1, 8, c3), lambda b, i: (b, jnp.minimum((i + 1) * hb, nh - 1), 1)),
            pl.BlockSpec((SHORT_K, c3), lambda b, i: (0, 0)),
            pl.BlockSpec((1, c3), lambda b, i: (0, 0)),
        ],
        out_specs=[
            pl.BlockSpec((1, tm, D_HYENA), lambda b, i: (b, i, 0)),
            pl.BlockSpec((1, tm, D_HYENA), lambda b, i: (b, i, 0)),
        ],
        out_shape=[
            jax.ShapeDtypeStruct((bsz, L, D_HYENA), F32),
            jax.ShapeDtypeStruct((bsz, L, D_HYENA), F32),
        ],
        compiler_params=_cparams(("parallel", "parallel")),
        name="conv_gate",
    )(z3, z3, z3, conv_w, conv_b.reshape(1, c3))


def _filter_kernel(fb_ref, w1t_ref, w1c_ref, w1s_ref, b1_ref, w2_ref, b2_ref, w3_ref, b3_ref,
                   w4_ref, fr_ref, dl_ref, o_ref, *, L, tp):
    j = pl.program_id(0)
    hi = lax.Precision.HIGHEST
    pos = (lax.broadcasted_iota(jnp.int32, (1, tp), 1) + j * tp).astype(F32)
    t = pos / (L - 1.0)
    omega = (2.0 * math.pi) * pos / float(L)
    ang = fb_ref[...] * omega
    fr = fr_ref[...]
    pre = (w1t_ref[...] * t
           + jnp.dot(w1c_ref[...], jnp.cos(ang), precision=hi, preferred_element_type=F32)
           - jnp.dot(w1s_ref[...], jnp.sin(ang), precision=hi, preferred_element_type=F32))
    h = jnp.sin(fr * (pre + b1_ref[...]))
    h = jnp.sin(fr * (jnp.dot(w2_ref[...], h, precision=hi, preferred_element_type=F32) + b2_ref[...]))
    h = jnp.sin(fr * (jnp.dot(w3_ref[...], h, precision=hi, preferred_element_type=F32) + b3_ref[...]))
    out = jnp.dot(h.T, w4_ref[...], precision=hi, preferred_element_type=F32)
    tcol = (lax.broadcasted_iota(jnp.int32, (tp, 1), 0) + j * tp).astype(F32) / (L - 1.0)
    decay = jnp.exp(-tcol * dl_ref[...])
    c = D_HYENA
    o_ref[:, 0:c] = out[:, 0:c] * decay
    o_ref[:, c:2 * c] = out[:, c:2 * c] * decay


def _implicit_filters(L, w1, b1, w2, b2, w3, b3, w4, freq):
    bands = (FILTER_EMB - 1) // 2
    fh = FILTER_HIDDEN
    tp = min(512, L)
    fb = jnp.linspace(1e-4, bands - 1, bands, dtype=F32).reshape(bands, 1)
    max_decay = math.log(DECAY_TARGET) / FAST_DECAY_PCT
    min_decay = math.log(DECAY_TARGET) / SLOW_DECAY_PCT
    deltas = jnp.abs(jnp.linspace(min_decay, max_decay, D_HYENA, dtype=F32)).reshape(1, D_HYENA)
    w1 = w1.astype(F32)
    args = (
        fb,
        w1[0:1, :].T,
        w1[1:1 + bands, :].T,
        w1[1 + bands:, :].T,
        b1.astype(F32).reshape(fh, 1),
        w2.astype(F32).T, b2.astype(F32).reshape(fh, 1),
        w3.astype(F32).T, b3.astype(F32).reshape(fh, 1),
        w4.astype(F32),
        freq.astype(F32).reshape(fh, 1),
        deltas,
    )

    def full(a):
        return pl.BlockSpec(a.shape, lambda j: (0,) * a.ndim)

    return pl.pallas_call(
        functools.partial(_filter_kernel, L=L, tp=tp),
        grid=(L // tp,),
        in_specs=[full(a) for a in args],
        out_specs=pl.BlockSpec((tp, 2 * D_HYENA), lambda j: (j, 0)),
        out_shape=jax.ShapeDtypeStruct((L, 2 * D_HYENA), F32),
        compiler_params=_cparams(("parallel",)),
        name="implicit_filter",
    )(*args)


@functools.lru_cache(maxsize=None)
def _fft_tables(L):
    n = 2 * L
    n2 = FFT_N2
    n1 = n // n2
    n1h = n1 // 2
    k1 = np.arange(n1)
    th = 2.0 * np.pi * (np.outer(k1, np.arange(n1h)) % n1) / n1
    f1 = np.concatenate([np.cos(th), -np.sin(th)], axis=0)
    tw = 2.0 * np.pi * (np.outer(k1, np.arange(n2)) % n) / n
    twr = np.cos(tw)[:, :, None]
    twi = -np.sin(tw)[:, :, None]
    t2 = 2.0 * np.pi * (np.outer(np.arange(n2), np.arange(n2)) % n2) / n2
    fr, fi = np.cos(t2), -np.sin(t2)
    f2 = np.block([[fr, -fi], [fi, fr]])
    f2inv = np.block([[fr, fi], [-fi, fr]])
    thi = 2.0 * np.pi * (np.outer(np.arange(n1h), k1) % n1) / n1
    f1inv = np.concatenate([np.cos(thi), -np.sin(thi)], axis=1) / n
    return dict(
        n1=n1, n1h=n1h,
        f1=np.asarray(f1, np.float32), f1inv=np.asarray(f1inv, np.float32),
        f2=np.asarray(f2, np.float32), f2inv=np.asarray(f2inv, np.float32),
        twr=np.asarray(twr, np.float32), twi=np.asarray(twi, np.float32),
    )


def _fft_stage1_kernel(f_ref, u_ref, o_ref):
    o_ref[0] = jnp.dot(f_ref[...], u_ref[0].astype(BF16),
                       preferred_element_type=F32).astype(o_ref.dtype)


def _fft_stage1(uview, f1, tc):
    bsz, n1h, cols = uview.shape
    rows_out = f1.shape[0]
    return pl.pallas_call(
        _fft_stage1_kernel,
        grid=(bsz, cols // tc),
        in_specs=[
            pl.BlockSpec(f1.shape, lambda b, j: (0, 0)),
            pl.BlockSpec((1, n1h, tc), lambda b, j: (b, 0, j)),
        ],
        out_specs=pl.BlockSpec((1, rows_out, tc), lambda b, j: (b, 0, j)),
        out_shape=jax.ShapeDtypeStruct((bsz, rows_out, cols), BF16),
        compiler_params=_cparams(("parallel", "parallel")),
        name="fft_stage1",
    )(f1, uview)


def _twiddle(ar, ai, tr, ti):
    return ar * tr - ai * ti, ar * ti + ai * tr


def _filter_spectrum_kernel(a_ref, twr_ref, twi_ref, f2_ref, o_ref):
    ar = a_ref[0, 0, 0].astype(F32)
    ai = a_ref[0, 1, 0].astype(F32)
    xr, xi = _twiddle(ar, ai, twr_ref[0], twi_ref[0])
    x = jnp.concatenate([xr, xi], axis=0).astype(BF16)
    big = jnp.dot(f2_ref[...], x, preferred_element_type=F32)
    n2, c = FFT_N2, D_HYENA
    o_ref[0, 0] = big[0:n2, 0:c] + big[0:n2, c:2 * c]
    o_ref[0, 1] = big[n2:2 * n2, 0:c] - big[n2:2 * n2, c:2 * c]


def _filter_spectrum(hfilt, tb):
    L, c2 = hfilt.shape
    n1, n1h = tb["n1"], tb["n1h"]
    a = _fft_stage1(hfilt.reshape(1, n1h, FFT_N2 * c2), tb["f1"], tc=4096)
    a5 = a.reshape(1, 2, n1, FFT_N2, c2)
    return pl.pallas_call(
        _filter_spectrum_kernel,
        grid=(n1,),
        in_specs=[
            pl.BlockSpec((1, 2, 1, FFT_N2, c2), lambda k: (0, 0, k, 0, 0)),
            pl.BlockSpec((1, FFT_N2, 1), lambda k: (k, 0, 0)),
            pl.BlockSpec((1, FFT_N2, 1), lambda k: (k, 0, 0)),
            pl.BlockSpec((2 * FFT_N2, 2 * FFT_N2), lambda k: (0, 0)),
        ],
        out_specs=pl.BlockSpec((1, 2, FFT_N2, D_HYENA), lambda k: (k, 0, 0, 0)),
        out_shape=jax.ShapeDtypeStruct((n1, 2, FFT_N2, D_HYENA), F32),
        compiler_params=_cparams(("parallel",)),
        name="filter_spectrum",
    )(a5, tb["twr"], tb["twi"], tb["f2"])


def _fft_mid_kernel(a_ref, kf_ref, twr_ref, twi_ref, f2_ref, f2i_ref, o_ref):
    n2 = FFT_N2
    tr, ti = twr_ref[0], twi_ref[0]
    xr, xi = _twiddle(a_ref[0, 0, 0].astype(F32), a_ref[0, 1, 0].astype(F32), tr, ti)
    x = jnp.concatenate([xr, xi], axis=0).astype(BF16)
    big = jnp.dot(f2_ref[...], x, preferred_element_type=F32)
    sr, si = big[0:n2], big[n2:2 * n2]
    kr, ki = kf_ref[0, 0], kf_ref[0, 1]
    y = jnp.concatenate([sr * kr - si * ki, sr * ki + si * kr], axis=0).astype(BF16)
    back = jnp.dot(f2i_ref[...], y, preferred_element_type=F32)
    br, bi = back[0:n2], back[n2:2 * n2]
    o_ref[0, 0, 0] = (br * tr + bi * ti).astype(o_ref.dtype)
    o_ref[0, 1, 0] = (bi * tr - br * ti).astype(o_ref.dtype)


def _fft_mid(a5, kf, tb):
    bsz, _, n1, n2, c = a5.shape
    return pl.pallas_call(
        _fft_mid_kernel,
        grid=(n1, bsz),
        in_specs=[
            pl.BlockSpec((1, 2, 1, n2, c), lambda k, b: (b, 0, k, 0, 0)),
            pl.BlockSpec((1, 2, n2, c), lambda k, b: (k, 0, 0, 0)),
            pl.BlockSpec((1, n2, 1), lambda k, b: (k, 0, 0)),
            pl.BlockSpec((1, n2, 1), lambda k, b: (k, 0, 0)),
            pl.BlockSpec((2 * n2, 2 * n2), lambda k, b: (0, 0)),
            pl.BlockSpec((2 * n2, 2 * n2), lambda k, b: (0, 0)),
        ],
        out_specs=pl.BlockSpec((1, 2, 1, n2, c), lambda k, b: (b, 0, k, 0, 0)),
        out_shape=jax.ShapeDtypeStruct(a5.shape, BF16),
        compiler_params=_cparams(("parallel", "parallel")),
        name="fft_mid",
    )(a5, kf, tb["twr"], tb["twi"], tb["f2"], tb["f2inv"])


def _fft_out_kernel(f_ref, b_ref, u_ref, x0_ref, d_ref, o_ref):
    y = jnp.dot(f_ref[...], b_ref[0], preferred_element_type=F32)
    o_ref[0] = (x0_ref[0] * (y + u_ref[0] * d_ref[...])).astype(o_ref.dtype)


def _fft_out(bview, uview, x0view, dtile, f1inv, tc):
    bsz, n1h, cols = uview.shape
    rows_in = bview.shape[1]
    return pl.pallas_call(
        _fft_out_kernel,
        grid=(bsz, cols // tc),
        in_specs=[
            pl.BlockSpec(f1inv.shape, lambda b, j: (0, 0)),
            pl.BlockSpec((1, rows_in, tc), lambda b, j: (b, 0, j)),
            pl.BlockSpec((1, n1h, tc), lambda b, j: (b, 0, j)),
            pl.BlockSpec((1, n1h, tc), lambda b, j: (b, 0, j)),
            pl.BlockSpec((1, tc), lambda b, j: (0, 0)),
        ],
        out_specs=pl.BlockSpec((1, n1h, tc), lambda b, j: (b, 0, j)),
        out_shape=jax.ShapeDtypeStruct((bsz, n1h, cols), BF16),
        compiler_params=_cparams(("parallel", "parallel")),
        name="fft_out",
    )(f1inv, bview, uview, x0view, dtile)


def _hyena_conv(u, x0, kf, d_skip, tb):
    bsz, L, c = u.shape
    n1, n1h = tb["n1"], tb["n1h"]
    cols = FFT_N2 * c
    tc = 4096
    uview = u.reshape(bsz, n1h, cols)
    a = _fft_stage1(uview, tb["f1"], tc)
    bm = _fft_mid(a.reshape(bsz, 2, n1, FFT_N2, c), kf, tb)
    dtile = jnp.tile(d_skip.astype(F32), tc // c).reshape(1, tc)
    y = _fft_out(bm.reshape(bsz, 2 * n1, cols), uview, x0.reshape(bsz, n1h, cols), dtile,
                 tb["f1inv"], tc)
    return y.reshape(bsz, L, c)


def _merge_kernel(ya_ref, yh_ref, ga_ref, gh_ref, x_ref, wa_ref, wh_ref, wo_ref, o_ref):
    pa = jnp.dot(ya_ref[...], wa_ref[...], preferred_element_type=F32)
    ph = jnp.dot(yh_ref[...], wh_ref[...], preferred_element_type=F32)
    merged = jax.nn.sigmoid(ga_ref[...]) * pa + jax.nn.sigmoid(gh_ref[...]) * ph
    o_ref[...] = x_ref[...] + jnp.dot(merged.astype(BF16), wo_ref[...],
                                      preferred_element_type=F32)


def _merge(ya, yh, z, x2, wa, wh, wo):
    n, d = x2.shape
    tm = min(512, n)
    ga_blk = (D_ATTN * 3 + D_HYENA * 3) // d
    return pl.pallas_call(
        _merge_kernel,
        grid=(n // tm,),
        in_specs=[
            pl.BlockSpec((tm, D_ATTN), lambda i: (i, 0)),
            pl.BlockSpec((tm, D_HYENA), lambda i: (i, 0)),
            pl.BlockSpec((tm, d), lambda i: (i, ga_blk)),
            pl.BlockSpec((tm, d), lambda i: (i, ga_blk + 1)),
            pl.BlockSpec((tm, d), lambda i: (i, 0)),
            pl.BlockSpec(wa.shape, lambda i: (0, 0)),
            pl.BlockSpec(wh.shape, lambda i: (0, 0)),
            pl.BlockSpec(wo.shape, lambda i: (0, 0)),
        ],
        out_specs=pl.BlockSpec((tm, d), lambda i: (i, 0)),
        out_shape=jax.ShapeDtypeStruct((n, d), F32),
        compiler_params=_cparams(("parallel",)),
        name="merge_outproj",
    )(ya, yh, z, z, x2, wa, wh, wo)


def _rms(x, g):
    inv = lax.rsqrt(jnp.mean(x * x, axis=-1, keepdims=True) + EPS)
    return (x * inv) * g


def _ffn_kernel(x_ref, g_ref, wg_ref, wu_ref, wd_ref, gf_ref, o_ref, h_ref, acc_ref):
    j = pl.program_id(1)

    @pl.when(j == 0)
    def _():
        h_ref[...] = _rms(x_ref[...], g_ref[...]).astype(BF16)
        acc_ref[...] = jnp.zeros_like(acc_ref)

    h = h_ref[...]
    gate = jnp.dot(h, wg_ref[...], preferred_element_type=F32)
    up = jnp.dot(h, wu_ref[...], preferred_element_type=F32)
    act = (gate * jax.nn.sigmoid(gate) * up).astype(BF16)
    acc_ref[...] += jnp.dot(act, wd_ref[...], preferred_element_type=F32)

    @pl.when(j == pl.num_programs(1) - 1)
    def _():
        o_ref[...] = _rms(x_ref[...] + acc_ref[...], gf_ref[...])


def _ffn(x1, g, wg, wu, wd, gf):
    n, d = x1.shape
    dff = wg.shape[1]
    tm = min(512, n)
    tf = dff // 2
    return pl.pallas_call(
        _ffn_kernel,
        grid=(n // tm, dff // tf),
        in_specs=[
            pl.BlockSpec((tm, d), lambda i, j: (i, 0)),
            pl.BlockSpec((1, d), lambda i, j: (0, 0)),
            pl.BlockSpec((d, tf), lambda i, j: (0, j)),
            pl.BlockSpec((d, tf), lambda i, j: (0, j)),
            pl.BlockSpec((tf, d), lambda i, j: (j, 0)),
            pl.BlockSpec((1, d), lambda i, j: (0, 0)),
        ],
        out_specs=pl.BlockSpec((tm, d), lambda i, j: (i, 0)),
        out_shape=jax.ShapeDtypeStruct((n, d), F32),
        scratch_shapes=[pltpu.VMEM((tm, d), BF16), pltpu.VMEM((tm, d), F32)],
        compiler_params=_cparams(("parallel", "arbitrary")),
        name="ffn_final",
    )(x1, g.reshape(1, d), wg, wu, wd, gf.reshape(1, d))


def _layer(x2, bsz, L, p):
    n, d = x2.shape
    z = _norm_inproj(x2, p["norm_mix"], p["w_in"])
    z3 = z.reshape(bsz, L, z.shape[1])
    ya = _attention(z3, p["rpb"])
    u, x0 = _conv_gate(z3, p["conv_w"], p["conv_b"])
    tb = dict(_fft_tables(L))
    for name in ("f1", "f1inv", "f2", "f2inv"):
        tb[name] = jnp.asarray(tb[name]).astype(BF16)
    hfilt = _implicit_filters(L, *p["filt"])
    kf = _filter_spectrum(hfilt, tb)
    yh = _hyena_conv(u, x0, kf, p["hyena_d"], tb)
    x1 = _merge(ya.reshape(n, D_ATTN), yh.reshape(n, D_HYENA), z, x2,
                p["w_br_attn"], p["w_br_hyena"], p["w_out"])
    return x1


def _trunk(x, layers, norm_final):
    bsz, L, d = x.shape
    x2 = x.reshape(bsz * L, d)
    depth = len(layers)
    for li, p in enumerate(layers):
        x1 = _layer(x2, bsz, L, p)
        gf = norm_final if li == depth - 1 else None
        assert gf is not None, "only the last layer fuses the final norm"
        x2 = _ffn(x1, p["norm_ffn"], p["w_gate"], p["w_up"], p["w_down"], gf)
    return x2.reshape(bsz, L, d)


def kernel(x_prompt, x_sample, norm_mix, w_in, rpb, conv_w, conv_b, filt_w1, filt_b1, filt_w2,
           filt_b2, filt_w3, filt_b3, filt_w4, filt_freq, hyena_d, w_br_attn, w_br_hyena, w_out,
           norm_ffn, w_gate, w_up, w_down, norm_final):
    depth = w_in.shape[0]
    assert depth == 1
    layers = []
    for l in range(depth):
        layers.append(dict(
            norm_mix=norm_mix[l], w_in=w_in[l].astype(BF16), rpb=rpb[l].reshape(-1),
            conv_w=conv_w[l], conv_b=conv_b[l],
            filt=(filt_w1[l], filt_b1[l], filt_w2[l], filt_b2[l], filt_w3[l], filt_b3[l],
                  filt_w4[l], filt_freq[l]),
            hyena_d=hyena_d[l],
            w_br_attn=w_br_attn[l].astype(BF16), w_br_hyena=w_br_hyena[l].astype(BF16),
            w_out=w_out[l].astype(BF16), norm_ffn=norm_ffn[l],
            w_gate=w_gate[l].astype(BF16), w_up=w_up[l].astype(BF16),
            w_down=w_down[l].astype(BF16),
        ))
    y_prompt = _trunk(x_prompt, layers, norm_final)
    y_sample = _trunk(x_sample, layers, norm_final)
    return (y_prompt, y_sample)
```

```python
import functools
import math

import numpy as np
import jax
import jax.numpy as jnp
from jax import lax
from jax.experimental import pallas as pl
from jax.experimental.pallas import tpu as pltpu

F32 = jnp.float32
BF16 = jnp.bfloat16

GRID_W = 64
N_HEADS = 8
HEAD_DIM = 64
D_ATTN = N_HEADS * HEAD_DIM
WIN_R = 8
WIN_C = 16
D_HYENA = 512
SHORT_K = 3
FILTER_EMB = 33
FILTER_HIDDEN = 64
DECAY_TARGET = 1e-2
FAST_DECAY_PCT = 0.3
SLOW_DECAY_PCT = 1.5
EPS = 1e-6

FFT_N2 = 128
ATT_ROWS = 8
NEG_BIG = -1e30
VMEM_LIMIT = 56 * 1024 * 1024


def _cparams(sem):
    return pltpu.CompilerParams(dimension_semantics=sem, vmem_limit_bytes=VMEM_LIMIT)


def _norm_inproj_kernel(x_ref, g_ref, w_ref, o_ref, h_ref):
    @pl.when(pl.program_id(1) == 0)
    def _():
        x = x_ref[...]
        inv = lax.rsqrt(jnp.mean(x * x, axis=-1, keepdims=True) + EPS)
        h_ref[...] = ((x * inv) * g_ref[...]).astype(BF16)

    o_ref[...] = jnp.dot(h_ref[...], w_ref[...], preferred_element_type=F32).astype(o_ref.dtype)


def _norm_inproj(x2, g, w_bf16):
    n, d = x2.shape
    d_in = w_bf16.shape[1]
    tm = min(1024, n)
    tn = 1280
    return pl.pallas_call(
        _norm_inproj_kernel,
        grid=(n // tm, d_in // tn),
        in_specs=[
            pl.BlockSpec((tm, d), lambda i, j: (i, 0)),
            pl.BlockSpec((1, d), lambda i, j: (0, 0)),
            pl.BlockSpec((d, tn), lambda i, j: (0, j)),
        ],
        out_specs=pl.BlockSpec((tm, tn), lambda i, j: (i, j)),
        out_shape=jax.ShapeDtypeStruct((n, d_in), BF16),
        scratch_shapes=[pltpu.VMEM((tm, d), BF16)],
        compiler_params=_cparams(("parallel", "arbitrary")),
        name="norm_inproj",
    )(x2, g.reshape(1, d), w_bf16)


N_DR = 2 * WIN_R - 1
N_DC = 2 * WIN_C - 1
KV_ROWS = 2 * ATT_ROWS


def _build_bias_tiles(rpb_ref, pt_ref):
    k = lax.broadcasted_iota(jnp.int32, (GRID_W, 128), 0)
    lane = lax.broadcasted_iota(jnp.int32, (GRID_W, 128), 1)
    q = lane & (GRID_W - 1)
    first_half = lane < GRID_W
    cs = jnp.clip(q - WIN_C // 2, 0, GRID_W - WIN_C)
    valid = (k >= cs) & (k < cs + WIN_C)
    dc = jnp.clip(k - q, -(WIN_C - 1), WIN_C - 1) + (WIN_C - 1)

    def body(idx, carry):
        p = idx // N_DR
        d = idx % N_DR
        base0 = ((2 * p) * N_DR + d) * N_DC
        base1 = ((2 * p + 1) * N_DR + d) * N_DC
        acc = jnp.zeros((GRID_W, 128), F32)
        for off in range(N_DC):
            val = jnp.where(first_half, rpb_ref[base0 + off], rpb_ref[base1 + off])
            acc = jnp.where(dc == off, val, acc)
        pt_ref[idx] = jnp.where(valid, acc, NEG_BIG)
        return carry

    lax.fori_loop(0, (N_HEADS // 2) * N_DR, body, 0)


def _kv_window_start(i, rows):
    return jnp.clip(i * ATT_ROWS - WIN_R // 2, 0, rows - KV_ROWS)


def _attn_kernel(rpb_ref, q_ref, k_ref, v_ref, o_ref, pt_ref, *, rows):
    b = pl.program_id(0)
    i = pl.program_id(1)

    @pl.when((b == 0) & (i == 0))
    def _():
        _build_bias_tiles(rpb_ref, pt_ref)

    lane = lax.broadcasted_iota(jnp.int32, (GRID_W, 128), 1)
    first_half = lane < HEAD_DIM
    eye = (lax.broadcasted_iota(jnp.int32, (128, 128), 0)
           == lax.broadcasted_iota(jnp.int32, (128, 128), 1)).astype(BF16)
    ones = jnp.ones((WIN_R * GRID_W, 128), BF16)
    nkeys = WIN_R * GRID_W
    wstart = _kv_window_start(i, rows)
    nt_dims = (((1,), (1,)), ((), ()))

    def row_body(rr, carry):
        r = i * ATT_ROWS + rr
        rs = jnp.clip(r - WIN_R // 2, 0, rows - WIN_R)
        s = rs - r + (WIN_R - 1)
        koff = pl.multiple_of((rs - wstart) * GRID_W, GRID_W)
        qoff = pl.multiple_of(rr * GRID_W, GRID_W)
        q = q_ref[pl.ds(qoff, GRID_W), :] * (HEAD_DIM ** -0.5)
        pairs = range(N_HEADS // 2)
        sts = []
        for p in pairs:
            qp = q[:, 128 * p:128 * (p + 1)]
            zero = jnp.zeros_like(qp)
            wt = jnp.concatenate([jnp.where(first_half, qp, zero),
                                  jnp.where(first_half, zero, qp)], axis=0)
            kp = k_ref[pl.ds(koff, nkeys), 128 * p:128 * (p + 1)]
            sts.append(lax.dot_general(kp, wt, nt_dims, preferred_element_type=F32))
        pms = []
        for p in pairs:
            st = sts[p]
            ch = [st[GRID_W * c:GRID_W * (c + 1)] + pt_ref[p * N_DR + s + c] for c in range(WIN_R)]
            m = ch[0]
            for c in range(1, WIN_R):
                m = jnp.maximum(m, ch[c])
            m = jnp.max(m, axis=0, keepdims=True)
            pt = jnp.concatenate([jnp.exp(c - m) for c in ch], axis=0).astype(BF16)
            pms.append(lax.dot_general(eye, pt, nt_dims, preferred_element_type=F32).astype(BF16))
        outs = []
        for p in pairs:
            vp = v_ref[pl.ds(koff, nkeys), 128 * p:128 * (p + 1)]
            ov = jnp.dot(pms[p], jnp.concatenate([vp, ones], axis=1), preferred_element_type=F32)
            o = ov[:, 0:128] / ov[:, 128:256]
            outs.append(jnp.where(first_half, o[0:GRID_W], o[GRID_W:2 * GRID_W]))
        o_ref[pl.ds(qoff, GRID_W), :] = jnp.concatenate(outs, axis=1).astype(o_ref.dtype)
        return carry

    lax.fori_loop(0, ATT_ROWS, row_body, 0)


def _attention(z3, rpb_flat):
    bsz, L, _ = z3.shape
    rows = L // GRID_W
    assert rows % ATT_ROWS == 0 and rows >= KV_ROWS
    nblk = rows // ATT_ROWS
    blk = ATT_ROWS * GRID_W

    def window(col):
        return pl.BlockSpec(
            (pl.Squeezed(), pl.Element(KV_ROWS * GRID_W), pl.Element(D_ATTN)),
            lambda b, i: (b, _kv_window_start(i, rows) * GRID_W, col * D_ATTN))

    return pl.pallas_call(
        functools.partial(_attn_kernel, rows=rows),
        grid=(bsz, nblk),
        in_specs=[
            pl.BlockSpec(memory_space=pltpu.SMEM),
            pl.BlockSpec((pl.Squeezed(), blk, D_ATTN), lambda b, i: (b, i, 0)),
            window(1),
            window(2),
        ],
        out_specs=pl.BlockSpec((pl.Squeezed(), blk, D_ATTN), lambda b, i: (b, i, 0)),
        out_shape=jax.ShapeDtypeStruct((bsz, L, D_ATTN), BF16),
        scratch_shapes=[pltpu.VMEM(((N_HEADS // 2) * N_DR, GRID_W, 128), F32)],
        compiler_params=_cparams(("arbitrary", "arbitrary")),
        name="nbr_attention",
    )(rpb_flat, z3, z3, z3)


def _conv_gate_kernel(xm_ref, xp_ref, xn_ref, w_ref, b_ref, u_ref, x0_ref, *, nt):
    i = pl.program_id(1)
    x = xm_ref[0].astype(F32)
    tm = x.shape[0]
    row = lax.broadcasted_iota(jnp.int32, x.shape, 0)
    halo = xp_ref.shape[1]
    prev_row = jnp.where(i == 0, 0.0, xp_ref[0].astype(F32)[halo - 1:halo, :])
    next_row = jnp.where(i == nt - 1, 0.0, xn_ref[0].astype(F32)[0:1, :])
    xm1 = jnp.where(row == 0, prev_row, pltpu.roll(x, 1, 0))
    xp1 = jnp.where(row == tm - 1, next_row, pltpu.roll(x, tm - 1, 0))
    y = xm1 * w_ref[0:1, :] + x * w_ref[1:2, :] + xp1 * w_ref[2:3, :] + b_ref[...]
    c = D_HYENA
    x0_ref[0] = y[:, 0:c]
    u_ref[0] = y[:, c:2 * c] * y[:, 2 * c:3 * c]


def _conv_gate(z3, conv_w, conv_b):
    bsz, L, _ = z3.shape
    c3 = 3 * D_HYENA
    tm = min(512, L)
    nt = L // tm
    halo = 16
    hb = tm // halo
    nh = L // halo
    return pl.pallas_call(
        functools.partial(_conv_gate_kernel, nt=nt),
        grid=(bsz, nt),
        in_specs=[
            pl.BlockSpec((1, tm, c3), lambda b, i: (b, i, 1)),
            pl.BlockSpec((1, halo, c3), lambda b, i: (b, jnp.maximum(i * hb - 1, 0), 1)),
            pl.BlockSpec((1, halo, c3), lambda b, i: (b, jnp.minimum((i + 1) * hb, nh - 1), 1)),
            pl.BlockSpec((SHORT_K, c3), lambda b, i: (0, 0)),
            pl.BlockSpec((1, c3), lambda b, i: (0, 0)),
        ],
        out_specs=[
            pl.BlockSpec((1, tm, D_HYENA), lambda b, i: (b, i, 0)),
            pl.BlockSpec((1, tm, D_HYENA), lambda b, i: (b, i, 0)),
        ],
        out_shape=[
            jax.ShapeDtypeStruct((bsz, L, D_HYENA), F32),
            jax.ShapeDtypeStruct((bsz, L, D_HYENA), F32),
        ],
        compiler_params=_cparams(("parallel", "parallel")),
        name="conv_gate",
    )(z3, z3, z3, conv_w, conv_b.reshape(1, c3))


def _filter_kernel(fb_ref, w1t_ref, w1c_ref, w1s_ref, b1_ref, w2_ref, b2_ref, w3_ref, b3_ref,
                   w4_ref, fr_ref, dl_ref, o_ref, *, L, tp):
    j = pl.program_id(0)
    hi = lax.Precision.HIGHEST
    pos = (lax.broadcasted_iota(jnp.int32, (1, tp), 1) + j * tp).astype(F32)
    t = pos / (L - 1.0)
    omega = (2.0 * math.pi) * pos / float(L)
    ang = fb_ref[...] * omega
    fr = fr_ref[...]
    pre = (w1t_ref[...] * t
           + jnp.dot(w1c_ref[...], jnp.cos(ang), precision=hi, preferred_element_type=F32)
           - jnp.dot(w1s_ref[...], jnp.sin(ang), precision=hi, preferred_element_type=F32))
    h = jnp.sin(fr * (pre + b1_ref[...]))
    h = jnp.sin(fr * (jnp.dot(w2_ref[...], h, precision=hi, preferred_element_type=F32) + b2_ref[...]))
    h = jnp.sin(fr * (jnp.dot(w3_ref[...], h, precision=hi, preferred_element_type=F32) + b3_ref[...]))
    out = jnp.dot(h.T, w4_ref[...], precision=hi, preferred_element_type=F32)
    tcol = (lax.broadcasted_iota(jnp.int32, (tp, 1), 0) + j * tp).astype(F32) / (L - 1.0)
    decay = jnp.exp(-tcol * dl_ref[...])
    c = D_HYENA
    o_ref[:, 0:c] = out[:, 0:c] * decay
    o_ref[:, c:2 * c] = out[:, c:2 * c] * decay


def _implicit_filters(L, w1, b1, w2, b2, w3, b3, w4, freq):
    bands = (FILTER_EMB - 1) // 2
    fh = FILTER_HIDDEN
    tp = min(512, L)
    fb = jnp.linspace(1e-4, bands - 1, bands, dtype=F32).reshape(bands, 1)
    max_decay = math.log(DECAY_TARGET) / FAST_DECAY_PCT
    min_decay = math.log(DECAY_TARGET) / SLOW_DECAY_PCT
    deltas = jnp.abs(jnp.linspace(min_decay, max_decay, D_HYENA, dtype=F32)).reshape(1, D_HYENA)
    w1 = w1.astype(F32)
    args = (
        fb,
        w1[0:1, :].T,
        w1[1:1 + bands, :].T,
        w1[1 + bands:, :].T,
        b1.astype(F32).reshape(fh, 1),
        w2.astype(F32).T, b2.astype(F32).reshape(fh, 1),
        w3.astype(F32).T, b3.astype(F32).reshape(fh, 1),
        w4.astype(F32),
        freq.astype(F32).reshape(fh, 1),
        deltas,
    )

    def full(a):
        return pl.BlockSpec(a.shape, lambda j: (0,) * a.ndim)

    return pl.pallas_call(
        functools.partial(_filter_kernel, L=L, tp=tp),
        grid=(L // tp,),
        in_specs=[full(a) for a in args],
        out_specs=pl.BlockSpec((tp, 2 * D_HYENA), lambda j: (j, 0)),
        out_shape=jax.ShapeDtypeStruct((L, 2 * D_HYENA), F32),
        compiler_params=_cparams(("parallel",)),
        name="implicit_filter",
    )(*args)


@functools.lru_cache(maxsize=None)
def _fft_tables(L):
    n = 2 * L
    n2 = FFT_N2
    n1 = n // n2
    n1h = n1 // 2
    k1 = np.arange(n1)
    th = 2.0 * np.pi * (np.outer(k1, np.arange(n1h)) % n1) / n1
    f1 = np.concatenate([np.cos(th), -np.sin(th)], axis=0)
    tw = 2.0 * np.pi * (np.outer(k1, np.arange(n2)) % n) / n
    twr = np.cos(tw)[:, :, None]
    twi = -np.sin(tw)[:, :, None]
    t2 = 2.0 * np.pi * (np.outer(np.arange(n2), np.arange(n2)) % n2) / n2
    fr, fi = np.cos(t2), -np.sin(t2)
    f2 = np.block([[fr, -fi], [fi, fr]])
    f2inv = np.block([[fr, fi], [-fi, fr]])
    thi = 2.0 * np.pi * (np.outer(np.arange(n1h), k1) % n1) / n1
    f1inv = np.concatenate([np.cos(thi), -np.sin(thi)], axis=1) / n
    return dict(
        n1=n1, n1h=n1h,
        f1=np.asarray(f1, np.float32), f1inv=np.asarray(f1inv, np.float32),
        f2=np.asarray(f2, np.float32), f2inv=np.asarray(f2inv, np.float32),
        twr=np.asarray(twr, np.float32), twi=np.asarray(twi, np.float32),
    )


def _fft_stage1_kernel(f_ref, u_ref, o_ref):
    o_ref[0] = jnp.dot(f_ref[...], u_ref[0].astype(BF16),
                       preferred_element_type=F32).astype(o_ref.dtype)


def _fft_stage1(uview, f1, tc):
    bsz, n1h, cols = uview.shape
    rows_out = f1.shape[0]
    return pl.pallas_call(
        _fft_stage1_kernel,
        grid=(bsz, cols // tc),
        in_specs=[
            pl.BlockSpec(f1.shape, lambda b, j: (0, 0)),
            pl.BlockSpec((1, n1h, tc), lambda b, j: (b, 0, j)),
        ],
        out_specs=pl.BlockSpec((1, rows_out, tc), lambda b, j: (b, 0, j)),
        out_shape=jax.ShapeDtypeStruct((bsz, rows_out, cols), BF16),
        compiler_params=_cparams(("parallel", "parallel")),
        name="fft_stage1",
    )(f1, uview)


def _twiddle(ar, ai, tr, ti):
    return ar * tr - ai * ti, ar * ti + ai * tr


def _filter_spectrum_kernel(a_ref, twr_ref, twi_ref, f2_ref, o_ref):
    ar = a_ref[0, 0, 0].astype(F32)
    ai = a_ref[0, 1, 0].astype(F32)
    xr, xi = _twiddle(ar, ai, twr_ref[0], twi_ref[0])
    x = jnp.concatenate([xr, xi], axis=0).astype(BF16)
    big = jnp.dot(f2_ref[...], x, preferred_element_type=F32)
    n2, c = FFT_N2, D_HYENA
    o_ref[0, 0] = big[0:n2, 0:c] + big[0:n2, c:2 * c]
    o_ref[0, 1] = big[n2:2 * n2, 0:c] - big[n2:2 * n2, c:2 * c]


def _filter_spectrum(hfilt, tb):
    L, c2 = hfilt.shape
    n1, n1h = tb["n1"], tb["n1h"]
    a = _fft_stage1(hfilt.reshape(1, n1h, FFT_N2 * c2), tb["f1"], tc=4096)
    a5 = a.reshape(1, 2, n1, FFT_N2, c2)
    return pl.pallas_call(
        _filter_spectrum_kernel,
        grid=(n1,),
        in_specs=[
            pl.BlockSpec((1, 2, 1, FFT_N2, c2), lambda k: (0, 0, k, 0, 0)),
            pl.BlockSpec((1, FFT_N2, 1), lambda k: (k, 0, 0)),
            pl.BlockSpec((1, FFT_N2, 1), lambda k: (k, 0, 0)),
            pl.BlockSpec((2 * FFT_N2, 2 * FFT_N2), lambda k: (0, 0)),
        ],
        out_specs=pl.BlockSpec((1, 2, FFT_N2, D_HYENA), lambda k: (k, 0, 0, 0)),
        out_shape=jax.ShapeDtypeStruct((n1, 2, FFT_N2, D_HYENA), F32),
        compiler_params=_cparams(("parallel",)),
        name="filter_spectrum",
    )(a5, tb["twr"], tb["twi"], tb["f2"])


def _fft_mid_kernel(a_ref, kf_ref, twr_ref, twi_ref, f2_ref, f2i_ref, o_ref):
    n2 = FFT_N2
    tr, ti = twr_ref[0], twi_ref[0]
    xr, xi = _twiddle(a_ref[0, 0, 0].astype(F32), a_ref[0, 1, 0].astype(F32), tr, ti)
    x = jnp.concatenate([xr, xi], axis=0).astype(BF16)
    big = jnp.dot(f2_ref[...], x, preferred_element_type=F32)
    sr, si = big[0:n2], big[n2:2 * n2]
    kr, ki = kf_ref[0, 0], kf_ref[0, 1]
    y = jnp.concatenate([sr * kr - si * ki, sr * ki + si * kr], axis=0).astype(BF16)
    back = jnp.dot(f2i_ref[...], y, preferred_element_type=F32)
    br, bi = back[0:n2], back[n2:2 * n2]
    o_ref[0, 0, 0] = (br * tr + bi * ti).astype(o_ref.dtype)
    o_ref[0, 1, 0] = (bi * tr - br * ti).astype(o_ref.dtype)


def _fft_mid(a5, kf, tb):
    bsz, _, n1, n2, c = a5.shape
    return pl.pallas_call(
        _fft_mid_kernel,
        grid=(n1, bsz),
        in_specs=[
            pl.BlockSpec((1, 2, 1, n2, c), lambda k, b: (b, 0, k, 0, 0)),
            pl.BlockSpec((1, 2, n2, c), lambda k, b: (k, 0, 0, 0)),
            pl.BlockSpec((1, n2, 1), lambda k, b: (k, 0, 0)),
            pl.BlockSpec((1, n2, 1), lambda k, b: (k, 0, 0)),
            pl.BlockSpec((2 * n2, 2 * n2), lambda k, b: (0, 0)),
            pl.BlockSpec((2 * n2, 2 * n2), lambda k, b: (0, 0)),
        ],
        out_specs=pl.BlockSpec((1, 2, 1, n2, c), lambda k, b: (b, 0, k, 0, 0)),
        out_shape=jax.ShapeDtypeStruct(a5.shape, BF16),
        compiler_params=_cparams(("parallel", "parallel")),
        name="fft_mid",
    )(a5, kf, tb["twr"], tb["twi"], tb["f2"], tb["f2inv"])


def _fft_out_kernel(f_ref, b_ref, u_ref, x0_ref, d_ref, o_ref):
    y = jnp.dot(f_ref[...], b_ref[0], preferred_element_type=F32)
    o_ref[0] = (x0_ref[0] * (y + u_ref[0] * d_ref[...])).astype(o_ref.dtype)


def _fft_out(bview, uview, x0view, dtile, f1inv, tc):
    bsz, n1h, cols = uview.shape
    rows_in = bview.shape[1]
    return pl.pallas_call(
        _fft_out_kernel,
        grid=(bsz, cols // tc),
        in_specs=[
            pl.BlockSpec(f1inv.shape, lambda b, j: (0, 0)),
            pl.BlockSpec((1, rows_in, tc), lambda b, j: (b, 0, j)),
            pl.BlockSpec((1, n1h, tc), lambda b, j: (b, 0, j)),
            pl.BlockSpec((1, n1h, tc), lambda b, j: (b, 0, j)),
            pl.BlockSpec((1, tc), lambda b, j: (0, 0)),
        ],
        out_specs=pl.BlockSpec((1, n1h, tc), lambda b, j: (b, 0, j)),
        out_shape=jax.ShapeDtypeStruct((bsz, n1h, cols), BF16),
        compiler_params=_cparams(("parallel", "parallel")),
        name="fft_out",
    )(f1inv, bview, uview, x0view, dtile)


def _hyena_conv(u, x0, kf, d_skip, tb):
    bsz, L, c = u.shape
    n1, n1h = tb["n1"], tb["n1h"]
    cols = FFT_N2 * c
    tc = 4096
    uview = u.reshape(bsz, n1h, cols)
    a = _fft_stage1(uview, tb["f1"], tc)
    bm = _fft_mid(a.reshape(bsz, 2, n1, FFT_N2, c), kf, tb)
    dtile = jnp.tile(d_skip.astype(F32), tc // c).reshape(1, tc)
    y = _fft_out(bm.reshape(bsz, 2 * n1, cols), uview, x0.reshape(bsz, n1h, cols), dtile,
                 tb["f1inv"], tc)
    return y.reshape(bsz, L, c)


def _merge_kernel(ya_ref, yh_ref, ga_ref, gh_ref, x_ref, wa_ref, wh_ref, wo_ref, o_ref):
    pa = jnp.dot(ya_ref[...], wa_ref[...], preferred_element_type=F32)
    ph = jnp.dot(yh_ref[...], wh_ref[...], preferred_element_type=F32)
    merged = (jax.nn.sigmoid(ga_ref[...].astype(F32)) * pa
              + jax.nn.sigmoid(gh_ref[...].astype(F32)) * ph)
    o_ref[...] = x_ref[...] + jnp.dot(merged.astype(BF16), wo_ref[...],
                                      preferred_element_type=F32)


def _merge(ya, yh, z, x2, wa, wh, wo):
    n, d = x2.shape
    tm = min(512, n)
    ga_blk = (D_ATTN * 3 + D_HYENA * 3) // d
    return pl.pallas_call(
        _merge_kernel,
        grid=(n // tm,),
        in_specs=[
            pl.BlockSpec((tm, D_ATTN), lambda i: (i, 0)),
            pl.BlockSpec((tm, D_HYENA), lambda i: (i, 0)),
            pl.BlockSpec((tm, d), lambda i: (i, ga_blk)),
            pl.BlockSpec((tm, d), lambda i: (i, ga_blk + 1)),
            pl.BlockSpec((tm, d), lambda i: (i, 0)),
            pl.BlockSpec(wa.shape, lambda i: (0, 0)),
            pl.BlockSpec(wh.shape, lambda i: (0, 0)),
            pl.BlockSpec(wo.shape, lambda i: (0, 0)),
        ],
        out_specs=pl.BlockSpec((tm, d), lambda i: (i, 0)),
        out_shape=jax.ShapeDtypeStruct((n, d), F32),
        compiler_params=_cparams(("parallel",)),
        name="merge_outproj",
    )(ya, yh, z, z, x2, wa, wh, wo)


def _rms(x, g):
    inv = lax.rsqrt(jnp.mean(x * x, axis=-1, keepdims=True) + EPS)
    return (x * inv) * g


def _ffn_kernel(x_ref, g_ref, wg_ref, wu_ref, wd_ref, gf_ref, o_ref, h_ref, acc_ref):
    j = pl.program_id(1)

    @pl.when(j == 0)
    def _():
        h_ref[...] = _rms(x_ref[...], g_ref[...]).astype(BF16)
        acc_ref[...] = jnp.zeros_like(acc_ref)

    h = h_ref[...]
    gate = jnp.dot(h, wg_ref[...], preferred_element_type=F32)
    up = jnp.dot(h, wu_ref[...], preferred_element_type=F32)
    act = (gate * jax.nn.sigmoid(gate) * up).astype(BF16)
    acc_ref[...] += jnp.dot(act, wd_ref[...], preferred_element_type=F32)

    @pl.when(j == pl.num_programs(1) - 1)
    def _():
        o_ref[...] = _rms(x_ref[...] + acc_ref[...], gf_ref[...])


def _ffn(x1, g, wg, wu, wd, gf):
    n, d = x1.shape
    dff = wg.shape[1]
    tm = min(512, n)
    tf = dff // 2
    return pl.pallas_call(
        _ffn_kernel,
        grid=(n // tm, dff // tf),
        in_specs=[
            pl.BlockSpec((tm, d), lambda i, j: (i, 0)),
            pl.BlockSpec((1, d), lambda i, j: (0, 0)),
            pl.BlockSpec((d, tf), lambda i, j: (0, j)),
            pl.BlockSpec((d, tf), lambda i, j: (0, j)),
            pl.BlockSpec((tf, d), lambda i, j: (j, 0)),
            pl.BlockSpec((1, d), lambda i, j: (0, 0)),
        ],
        out_specs=pl.BlockSpec((tm, d), lambda i, j: (i, 0)),
        out_shape=jax.ShapeDtypeStruct((n, d), F32),
        scratch_shapes=[pltpu.VMEM((tm, d), BF16), pltpu.VMEM((tm, d), F32)],
        compiler_params=_cparams(("parallel", "arbitrary")),
        name="ffn_final",
    )(x1, g.reshape(1, d), wg, wu, wd, gf.reshape(1, d))


def _layer(x2, bsz, L, p):
    n, d = x2.shape
    z = _norm_inproj(x2, p["norm_mix"], p["w_in"])
    z3 = z.reshape(bsz, L, z.shape[1])
    ya = _attention(z3, p["rpb"])
    u, x0 = _conv_gate(z3, p["conv_w"], p["conv_b"])
    tb = dict(_fft_tables(L))
    for name in ("f1", "f1inv", "f2", "f2inv"):
        tb[name] = jnp.asarray(tb[name]).astype(BF16)
    hfilt = _implicit_filters(L, *p["filt"])
    kf = _filter_spectrum(hfilt, tb)
    yh = _hyena_conv(u, x0, kf, p["hyena_d"], tb)
    x1 = _merge(ya.reshape(n, D_ATTN), yh.reshape(n, D_HYENA), z, x2,
                p["w_br_attn"], p["w_br_hyena"], p["w_out"])
    return x1


def _trunk(x, layers, norm_final):
    bsz, L, d = x.shape
    x2 = x.reshape(bsz * L, d)
    depth = len(layers)
    for li, p in enumerate(layers):
        x1 = _layer(x2, bsz, L, p)
        gf = norm_final if li == depth - 1 else None
        assert gf is not None, "only the last layer fuses the final norm"
        x2 = _ffn(x1, p["norm_ffn"], p["w_gate"], p["w_up"], p["w_down"], gf)
    return x2.reshape(bsz, L, d)


def kernel(x_prompt, x_sample, norm_mix, w_in, rpb, conv_w, conv_b, filt_w1, filt_b1, filt_w2,
           filt_b2, filt_w3, filt_b3, filt_w4, filt_freq, hyena_d, w_br_attn, w_br_hyena, w_out,
           norm_ffn, w_gate, w_up, w_down, norm_final):
    depth = w_in.shape[0]
    assert depth == 1
    layers = []
    for l in range(depth):
        layers.append(dict(
            norm_mix=norm_mix[l], w_in=w_in[l].astype(BF16), rpb=rpb[l].reshape(-1),
            conv_w=conv_w[l], conv_b=conv_b[l],
            filt=(filt_w1[l], filt_b1[l], filt_w2[l], filt_b2[l], filt_w3[l], filt_b3[l],
                  filt_w4[l], filt_freq[l]),
            hyena_d=hyena_d[l],
            w_br_attn=w_br_attn[l].astype(BF16), w_br_hyena=w_br_hyena[l].astype(BF16),
            w_out=w_out[l].astype(BF16), norm_ffn=norm_ffn[l],
            w_gate=w_gate[l].astype(BF16), w_up=w_up[l].astype(BF16),
            w_down=w_down[l].astype(BF16),
        ))
    y_prompt = _trunk(x_prompt, layers, norm_final)
    y_sample = _trunk(x_sample, layers, norm_final)
    return (y_prompt, y_sample)
```

```python
import functools
import math

import numpy as np
import jax
import jax.numpy as jnp
from jax import lax
from jax.experimental import pallas as pl
from jax.experimental.pallas import tpu as pltpu

F32 = jnp.float32
BF16 = jnp.bfloat16

GRID_W = 64
N_HEADS = 8
HEAD_DIM = 64
D_ATTN = N_HEADS * HEAD_DIM
WIN_R = 8
WIN_C = 16
D_HYENA = 512
SHORT_K = 3
FILTER_EMB = 33
FILTER_HIDDEN = 64
DECAY_TARGET = 1e-2
FAST_DECAY_PCT = 0.3
SLOW_DECAY_PCT = 1.5
EPS = 1e-6

FFT_N2 = 128
ATT_ROWS = 8
NEG_BIG = -1e30
VMEM_LIMIT = 56 * 1024 * 1024


def _cparams(sem):
    return pltpu.CompilerParams(dimension_semantics=sem, vmem_limit_bytes=VMEM_LIMIT)


def _norm_inproj_kernel(x_ref, g_ref, w_ref, o_ref, h_ref):
    @pl.when(pl.program_id(1) == 0)
    def _():
        x = x_ref[...]
        inv = lax.rsqrt(jnp.mean(x * x, axis=-1, keepdims=True) + EPS)
        h_ref[...] = ((x * inv) * g_ref[...]).astype(BF16)

    o_ref[...] = jnp.dot(h_ref[...], w_ref[...], preferred_element_type=F32).astype(o_ref.dtype)


def _norm_inproj(x2, g, w_bf16):
    n, d = x2.shape
    d_in = w_bf16.shape[1]
    tm = min(1024, n)
    tn = 1280
    return pl.pallas_call(
        _norm_inproj_kernel,
        grid=(n // tm, d_in // tn),
        in_specs=[
            pl.BlockSpec((tm, d), lambda i, j: (i, 0)),
            pl.BlockSpec((1, d), lambda i, j: (0, 0)),
            pl.BlockSpec((d, tn), lambda i, j: (0, j)),
        ],
        out_specs=pl.BlockSpec((tm, tn), lambda i, j: (i, j)),
        out_shape=jax.ShapeDtypeStruct((n, d_in), BF16),
        scratch_shapes=[pltpu.VMEM((tm, d), BF16)],
        compiler_params=_cparams(("parallel", "arbitrary")),
        name="norm_inproj",
    )(x2, g.reshape(1, d), w_bf16)


N_DR = 2 * WIN_R - 1
N_DC = 2 * WIN_C - 1
KV_ROWS = 2 * ATT_ROWS


def _build_bias_tiles(rpb_ref, pt_ref):
    k = lax.broadcasted_iota(jnp.int32, (GRID_W, 128), 0)
    lane = lax.broadcasted_iota(jnp.int32, (GRID_W, 128), 1)
    q = lane & (GRID_W - 1)
    first_half = lane < GRID_W
    cs = jnp.clip(q - WIN_C // 2, 0, GRID_W - WIN_C)
    valid = (k >= cs) & (k < cs + WIN_C)
    dc = jnp.clip(k - q, -(WIN_C - 1), WIN_C - 1) + (WIN_C - 1)

    def body(idx, carry):
        p = idx // N_DR
        d = idx % N_DR
        base0 = ((2 * p) * N_DR + d) * N_DC
        base1 = ((2 * p + 1) * N_DR + d) * N_DC
        acc = jnp.zeros((GRID_W, 128), F32)
        for off in range(N_DC):
            val = jnp.where(first_half, rpb_ref[base0 + off], rpb_ref[base1 + off])
            acc = jnp.where(dc == off, val, acc)
        pt_ref[idx] = jnp.where(valid, acc, NEG_BIG)
        return carry

    lax.fori_loop(0, (N_HEADS // 2) * N_DR, body, 0)


def _kv_window_start(i, rows):
    return jnp.clip(i * ATT_ROWS - WIN_R // 2, 0, rows - KV_ROWS)


def _attn_kernel(rpb_ref, q_ref, k_ref, v_ref, o_ref, pt_ref, *, rows):
    b = pl.program_id(0)
    i = pl.program_id(1)

    @pl.when((b == 0) & (i == 0))
    def _():
        _build_bias_tiles(rpb_ref, pt_ref)

    lane = lax.broadcasted_iota(jnp.int32, (GRID_W, 128), 1)
    first_half = lane < HEAD_DIM
    eye = (lax.broadcasted_iota(jnp.int32, (128, 128), 0)
           == lax.broadcasted_iota(jnp.int32, (128, 128), 1)).astype(BF16)
    ones = jnp.ones((WIN_R * GRID_W, 128), BF16)
    nkeys = WIN_R * GRID_W
    wstart = _kv_window_start(i, rows)
    nt_dims = (((1,), (1,)), ((), ()))

    def row_body(rr, carry):
        r = i * ATT_ROWS + rr
        rs = jnp.clip(r - WIN_R // 2, 0, rows - WIN_R)
        s = rs - r + (WIN_R - 1)
        koff = pl.multiple_of((rs - wstart) * GRID_W, GRID_W)
        qoff = pl.multiple_of(rr * GRID_W, GRID_W)
        q = q_ref[pl.ds(qoff, GRID_W), :] * (HEAD_DIM ** -0.5)
        pairs = range(N_HEADS // 2)
        sts = []
        for p in pairs:
            qp = q[:, 128 * p:128 * (p + 1)]
            zero = jnp.zeros_like(qp)
            wt = jnp.concatenate([jnp.where(first_half, qp, zero),
                                  jnp.where(first_half, zero, qp)], axis=0)
            kp = k_ref[pl.ds(koff, nkeys), 128 * p:128 * (p + 1)]
            sts.append(lax.dot_general(kp, wt, nt_dims, preferred_element_type=F32))
        pms = []
        for p in pairs:
            st = sts[p]
            ch = [st[GRID_W * c:GRID_W * (c + 1)] + pt_ref[p * N_DR + s + c] for c in range(WIN_R)]
            m = ch[0]
            for c in range(1, WIN_R):
                m = jnp.maximum(m, ch[c])
            m = jnp.max(m, axis=0, keepdims=True)
            pt = jnp.concatenate([jnp.exp(c - m) for c in ch], axis=0).astype(BF16)
            pms.append(lax.dot_general(eye, pt, nt_dims, preferred_element_type=F32).astype(BF16))
        outs = []
        for p in pairs:
            vp = v_ref[pl.ds(koff, nkeys), 128 * p:128 * (p + 1)]
            ov = jnp.dot(pms[p], jnp.concatenate([vp, ones], axis=1), preferred_element_type=F32)
            o = ov[:, 0:128] / ov[:, 128:256]
            outs.append(jnp.where(first_half, o[0:GRID_W], o[GRID_W:2 * GRID_W]))
        o_ref[pl.ds(qoff, GRID_W), :] = jnp.concatenate(outs, axis=1).astype(o_ref.dtype)
        return carry

    lax.fori_loop(0, ATT_ROWS, row_body, 0)


def _attention(z3, rpb_flat):
    bsz, L, _ = z3.shape
    rows = L // GRID_W
    assert rows % ATT_ROWS == 0 and rows >= KV_ROWS
    nblk = rows // ATT_ROWS
    blk = ATT_ROWS * GRID_W

    def window(col):
        return pl.BlockSpec(
            (pl.Squeezed(), pl.Element(KV_ROWS * GRID_W), pl.Element(D_ATTN)),
            lambda b, i: (b, _kv_window_start(i, rows) * GRID_W, col * D_ATTN))

    return pl.pallas_call(
        functools.partial(_attn_kernel, rows=rows),
        grid=(bsz, nblk),
        in_specs=[
            pl.BlockSpec(memory_space=pltpu.SMEM),
            pl.BlockSpec((pl.Squeezed(), blk, D_ATTN), lambda b, i: (b, i, 0)),
            window(1),
            window(2),
        ],
        out_specs=pl.BlockSpec((pl.Squeezed(), blk, D_ATTN), lambda b, i: (b, i, 0)),
        out_shape=jax.ShapeDtypeStruct((bsz, L, D_ATTN), BF16),
        scratch_shapes=[pltpu.VMEM(((N_HEADS // 2) * N_DR, GRID_W, 128), F32)],
        compiler_params=_cparams(("arbitrary", "arbitrary")),
        name="nbr_attention",
    )(rpb_flat, z3, z3, z3)


def _conv_gate_kernel(xm_ref, xp_ref, xn_ref, w_ref, b_ref, u_ref, x0_ref, *, nt):
    i = pl.program_id(1)
    x = xm_ref[0].astype(F32)
    tm = x.shape[0]
    row = lax.broadcasted_iota(jnp.int32, x.shape, 0)
    halo = xp_ref.shape[1]
    prev_row = jnp.where(i == 0, 0.0, xp_ref[0].astype(F32)[halo - 1:halo, :])
    next_row = jnp.where(i == nt - 1, 0.0, xn_ref[0].astype(F32)[0:1, :])
    xm1 = jnp.where(row == 0, prev_row, pltpu.roll(x, 1, 0))
    xp1 = jnp.where(row == tm - 1, next_row, pltpu.roll(x, tm - 1, 0))
    y = xm1 * w_ref[0:1, :] + x * w_ref[1:2, :] + xp1 * w_ref[2:3, :] + b_ref[...]
    c = D_HYENA
    x0_ref[0] = y[:, 0:c]
    u_ref[0] = y[:, c:2 * c] * y[:, 2 * c:3 * c]


def _conv_gate(z3, conv_w, conv_b):
    bsz, L, _ = z3.shape
    c3 = 3 * D_HYENA
    tm = min(512, L)
    nt = L // tm
    halo = 16
    hb = tm // halo
    nh = L // halo
    return pl.pallas_call(
        functools.partial(_conv_gate_kernel, nt=nt),
        grid=(bsz, nt),
        in_specs=[
            pl.BlockSpec((1, tm, c3), lambda b, i: (b, i, 1)),
            pl.BlockSpec((1, halo, c3), lambda b, i: (b, jnp.maximum(i * hb - 1, 0), 1)),
            pl.BlockSpec((1, halo, c3), lambda b, i: (b, jnp.minimum((i + 1) * hb, nh - 1), 1)),
            pl.BlockSpec((SHORT_K, c3), lambda b, i: (0, 0)),
            pl.BlockSpec((1, c3), lambda b, i: (0, 0)),
        ],
        out_specs=[
            pl.BlockSpec((1, tm, D_HYENA), lambda b, i: (b, i, 0)),
            pl.BlockSpec((1, tm, D_HYENA), lambda b, i: (b, i, 0)),
        ],
        out_shape=[
            jax.ShapeDtypeStruct((bsz, L, D_HYENA), F32),
            jax.ShapeDtypeStruct((bsz, L, D_HYENA), F32),
        ],
        compiler_params=_cparams(("parallel", "parallel")),
        name="conv_gate",
    )(z3, z3, z3, conv_w, conv_b.reshape(1, c3))


def _filter_kernel(fb_ref, w1t_ref, w1c_ref, w1s_ref, b1_ref, w2_ref, b2_ref, w3_ref, b3_ref,
                   w4_ref, fr_ref, dl_ref, o_ref, *, L, tp):
    j = pl.program_id(0)
    hi = lax.Precision.HIGHEST
    pos = (lax.broadcasted_iota(jnp.int32, (1, tp), 1) + j * tp).astype(F32)
    t = pos / (L - 1.0)
    omega = (2.0 * math.pi) * pos / float(L)
    ang = fb_ref[...] * omega
    fr = fr_ref[...]
    pre = (w1t_ref[...] * t
           + jnp.dot(w1c_ref[...], jnp.cos(ang), precision=hi, preferred_element_type=F32)
           - jnp.dot(w1s_ref[...], jnp.sin(ang), precision=hi, preferred_element_type=F32))
    h = jnp.sin(fr * (pre + b1_ref[...]))
    h = jnp.sin(fr * (jnp.dot(w2_ref[...], h, precision=hi, preferred_element_type=F32) + b2_ref[...]))
    h = jnp.sin(fr * (jnp.dot(w3_ref[...], h, precision=hi, preferred_element_type=F32) + b3_ref[...]))
    out = jnp.dot(h.T, w4_ref[...], precision=hi, preferred_element_type=F32)
    tcol = (lax.broadcasted_iota(jnp.int32, (tp, 1), 0) + j * tp).astype(F32) / (L - 1.0)
    decay = jnp.exp(-tcol * dl_ref[...])
    c = D_HYENA
    o_ref[:, 0:c] = out[:, 0:c] * decay
    o_ref[:, c:2 * c] = out[:, c:2 * c] * decay


def _implicit_filters(L, w1, b1, w2, b2, w3, b3, w4, freq):
    bands = (FILTER_EMB - 1) // 2
    fh = FILTER_HIDDEN
    tp = min(512, L)
    fb = jnp.linspace(1e-4, bands - 1, bands, dtype=F32).reshape(bands, 1)
    max_decay = math.log(DECAY_TARGET) / FAST_DECAY_PCT
    min_decay = math.log(DECAY_TARGET) / SLOW_DECAY_PCT
    deltas = jnp.abs(jnp.linspace(min_decay, max_decay, D_HYENA, dtype=F32)).reshape(1, D_HYENA)
    w1 = w1.astype(F32)
    args = (
        fb,
        w1[0:1, :].T,
        w1[1:1 + bands, :].T,
        w1[1 + bands:, :].T,
        b1.astype(F32).reshape(fh, 1),
        w2.astype(F32).T, b2.astype(F32).reshape(fh, 1),
        w3.astype(F32).T, b3.astype(F32).reshape(fh, 1),
        w4.astype(F32),
        freq.astype(F32).reshape(fh, 1),
        deltas,
    )

    def full(a):
        return pl.BlockSpec(a.shape, lambda j: (0,) * a.ndim)

    return pl.pallas_call(
        functools.partial(_filter_kernel, L=L, tp=tp),
        grid=(L // tp,),
        in_specs=[full(a) for a in args],
        out_specs=pl.BlockSpec((tp, 2 * D_HYENA), lambda j: (j, 0)),
        out_shape=jax.ShapeDtypeStruct((L, 2 * D_HYENA), F32),
        compiler_params=_cparams(("parallel",)),
        name="implicit_filter",
    )(*args)


FFT_LANES = 128
FFT_T1 = 4


@functools.lru_cache(maxsize=None)
def _fft_tables(L):
    n = 2 * L
    n2 = FFT_N2
    n1 = n // n2
    n1h = n1 // 2
    odd = 2 * np.arange(n1h) + 1
    th = 2.0 * np.pi * (np.outer(odd, np.arange(n1h)) % (2 * n1)) / (2 * n1)
    f1 = np.stack([np.cos(th), -np.sin(th)], axis=0)
    tw = 2.0 * np.pi * (np.outer(odd, np.arange(n2)) % (2 * n)) / (2 * n)
    twr = np.cos(tw)[:, :, None]
    twi = -np.sin(tw)[:, :, None]
    t2 = 2.0 * np.pi * (np.outer(np.arange(n2), np.arange(n2)) % n2) / n2
    fr, fi = np.cos(t2), -np.sin(t2)
    f2 = np.block([[fr, -fi], [fi, fr]])
    f2inv = np.block([[fr, fi], [-fi, fr]])
    f1inv = np.stack([np.cos(th.T), -np.sin(th.T)], axis=0) * (2.0 / n)
    return dict(
        n1=n1, n1h=n1h,
        f1=np.asarray(f1, np.float32), f1inv=np.asarray(f1inv, np.float32),
        f2=np.asarray(f2, np.float32), f2inv=np.asarray(f2inv, np.float32),
        twr=np.asarray(twr, np.float32), twi=np.asarray(twi, np.float32),
    )


def _strided_dft(f, src_ref, n1h, emit):
    def body(i, carry):
        a = 2 * i
        rhs = jnp.concatenate([src_ref[pl.ds(a, n1h, stride=FFT_N2), :],
                               src_ref[pl.ds(a + 1, n1h, stride=FFT_N2), :]], axis=1)
        res = jnp.dot(f, rhs.astype(BF16), preferred_element_type=F32)
        emit(a, res[:, 0:FFT_LANES])
        emit(a + 1, res[:, FFT_LANES:2 * FFT_LANES])
        return carry

    lax.fori_loop(0, FFT_N2 // 2, body, 0, unroll=4)


def _fft_stage1_kernel(f_ref, u_ref, o_ref, *, n1h):
    def emit(a, val):
        o_ref[pl.ds(a, n1h, stride=FFT_N2), :] = val

    _strided_dft(f_ref[0], u_ref, n1h, emit)


def _fft_stage1(u3, f1):
    bsz, L, c = u3.shape
    n1h = L // FFT_N2
    out = pl.pallas_call(
        functools.partial(_fft_stage1_kernel, n1h=n1h),
        grid=(bsz, c // FFT_LANES, 2),
        in_specs=[
            pl.BlockSpec((1, n1h, n1h), lambda b, j, ri: (ri, 0, 0)),
            pl.BlockSpec((pl.Squeezed(), L, FFT_LANES), lambda b, j, ri: (b, 0, j)),
        ],
        out_specs=pl.BlockSpec((pl.Squeezed(), L, FFT_LANES), lambda b, j, ri: (b, ri, j)),
        out_shape=jax.ShapeDtypeStruct((bsz, 2 * L, c), F32),
        compiler_params=_cparams(("parallel", "parallel", "arbitrary")),
        name="fft_stage1",
    )(f1, u3)
    return out.reshape(bsz, 2, n1h, FFT_N2, c)


def _twiddle(ar, ai, tr, ti):
    return ar * tr - ai * ti, ar * ti + ai * tr


def _stage2_inputs(a_ref, twr_ref, twi_ref):
    xs = []
    for j in range(FFT_T1):
        xr, xi = _twiddle(a_ref[0, j], a_ref[1, j], twr_ref[j], twi_ref[j])
        xs.append(jnp.concatenate([xr, xi], axis=0).astype(BF16))
    return xs


def _filter_spectrum_kernel(a_ref, twr_ref, twi_ref, f2_ref, o_ref):
    n2, c = FFT_N2, D_HYENA
    f2 = f2_ref[...]
    bigs = [jnp.dot(f2, x, preferred_element_type=F32)
            for x in _stage2_inputs(a_ref, twr_ref, twi_ref)]
    for j, big in enumerate(bigs):
        o_ref[j, 0] = big[0:n2, 0:c] + big[0:n2, c:2 * c]
        o_ref[j, 1] = big[n2:2 * n2, 0:c] - big[n2:2 * n2, c:2 * c]


def _filter_spectrum(hfilt, tb):
    L, c2 = hfilt.shape
    n1h = tb["n1h"]
    a5 = _fft_stage1(hfilt.reshape(1, L, c2), tb["f1"])
    return pl.pallas_call(
        _filter_spectrum_kernel,
        grid=(n1h // FFT_T1,),
        in_specs=[
            pl.BlockSpec((pl.Squeezed(), 2, FFT_T1, FFT_N2, c2), lambda k: (0, 0, k, 0, 0)),
            pl.BlockSpec((FFT_T1, FFT_N2, 1), lambda k: (k, 0, 0)),
            pl.BlockSpec((FFT_T1, FFT_N2, 1), lambda k: (k, 0, 0)),
            pl.BlockSpec((2 * FFT_N2, 2 * FFT_N2), lambda k: (0, 0)),
        ],
        out_specs=pl.BlockSpec((FFT_T1, 2, FFT_N2, D_HYENA), lambda k: (k, 0, 0, 0)),
        out_shape=jax.ShapeDtypeStruct((n1h, 2, FFT_N2, D_HYENA), F32),
        compiler_params=_cparams(("parallel",)),
        name="filter_spectrum",
    )(a5, tb["twr"], tb["twi"], tb["f2"])


def _fft_mid_kernel(a_ref, kf_ref, twr_ref, twi_ref, f2_ref, f2i_ref, o_ref):
    n2 = FFT_N2
    f2, f2i = f2_ref[...], f2i_ref[...]
    bigs = [jnp.dot(f2, x, preferred_element_type=F32)
            for x in _stage2_inputs(a_ref, twr_ref, twi_ref)]
    ys = []
    for j, big in enumerate(bigs):
        sr, si = big[0:n2], big[n2:2 * n2]
        kr, ki = kf_ref[j, 0], kf_ref[j, 1]
        ys.append(jnp.concatenate([sr * kr - si * ki, sr * ki + si * kr], axis=0).astype(BF16))
    backs = [jnp.dot(f2i, y, preferred_element_type=F32) for y in ys]
    for j, back in enumerate(backs):
        br, bi = back[0:n2], back[n2:2 * n2]
        tr, ti = twr_ref[j], twi_ref[j]
        o_ref[0, j] = br * tr + bi * ti
        o_ref[1, j] = bi * tr - br * ti


def _fft_mid(a5, kf, tb):
    bsz, _, n1h, n2, c = a5.shape
    slot = pl.BlockSpec((pl.Squeezed(), 2, FFT_T1, n2, c), lambda k, b: (b, 0, k, 0, 0))
    return pl.pallas_call(
        _fft_mid_kernel,
        grid=(n1h // FFT_T1, bsz),
        in_specs=[
            slot,
            pl.BlockSpec((FFT_T1, 2, n2, c), lambda k, b: (k, 0, 0, 0)),
            pl.BlockSpec((FFT_T1, n2, 1), lambda k, b: (k, 0, 0)),
            pl.BlockSpec((FFT_T1, n2, 1), lambda k, b: (k, 0, 0)),
            pl.BlockSpec((2 * n2, 2 * n2), lambda k, b: (0, 0)),
            pl.BlockSpec((2 * n2, 2 * n2), lambda k, b: (0, 0)),
        ],
        out_specs=slot,
        out_shape=jax.ShapeDtypeStruct(a5.shape, F32),
        compiler_params=_cparams(("parallel", "parallel")),
        name="fft_mid",
    )(a5, kf, tb["twr"], tb["twi"], tb["f2"], tb["f2inv"])


def _fft_out_kernel(f_ref, b_ref, o_ref, *, n1h):
    ri = pl.program_id(2)

    @pl.when(ri == 0)
    def _():
        def emit(a, val):
            o_ref[pl.ds(a, n1h, stride=FFT_N2), :] = val

        _strided_dft(f_ref[0], b_ref, n1h, emit)

    @pl.when(ri == 1)
    def _():
        def emit(a, val):
            at = pl.ds(a, n1h, stride=FFT_N2)
            o_ref[at, :] = o_ref[at, :] + val

        _strided_dft(f_ref[0], b_ref, n1h, emit)


def _fft_out(b5, f1inv):
    bsz, _, n1h, n2, c = b5.shape
    L = n1h * n2
    return pl.pallas_call(
        functools.partial(_fft_out_kernel, n1h=n1h),
        grid=(bsz, c // FFT_LANES, 2),
        in_specs=[
            pl.BlockSpec((1, n1h, n1h), lambda b, j, ri: (ri, 0, 0)),
            pl.BlockSpec((pl.Squeezed(), L, FFT_LANES), lambda b, j, ri: (b, ri, j)),
        ],
        out_specs=pl.BlockSpec((pl.Squeezed(), L, FFT_LANES), lambda b, j, ri: (b, 0, j)),
        out_shape=jax.ShapeDtypeStruct((bsz, L, c), F32),
        compiler_params=_cparams(("parallel", "parallel", "arbitrary")),
        name="fft_out",
    )(f1inv, b5.reshape(bsz, 2 * L, c))


def _hyena_conv(u, kf, tb):
    a5 = _fft_stage1(u, tb["f1"])
    b5 = _fft_mid(a5, kf, tb)
    return _fft_out(b5, tb["f1inv"])


def _merge_kernel(ya_ref, yc_ref, u_ref, x0_ref, dk_ref, ga_ref, gh_ref, x_ref, wa_ref, wh_ref,
                  wo_ref, o_ref):
    yh = x0_ref[...] * (yc_ref[...] + u_ref[...] * dk_ref[...])
    pa = jnp.dot(ya_ref[...], wa_ref[...], preferred_element_type=F32)
    ph = jnp.dot(yh.astype(BF16), wh_ref[...], preferred_element_type=F32)
    merged = (jax.nn.sigmoid(ga_ref[...].astype(F32)) * pa
              + jax.nn.sigmoid(gh_ref[...].astype(F32)) * ph)
    o_ref[...] = x_ref[...] + jnp.dot(merged.astype(BF16), wo_ref[...],
                                      preferred_element_type=F32)


def _merge(ya, yc, u, x0, d_skip, z, x2, wa, wh, wo):
    n, d = x2.shape
    tm = min(512, n)
    ga_blk = (D_ATTN * 3 + D_HYENA * 3) // d
    tok = pl.BlockSpec((tm, D_HYENA), lambda i: (i, 0))
    return pl.pallas_call(
        _merge_kernel,
        grid=(n // tm,),
        in_specs=[
            pl.BlockSpec((tm, D_ATTN), lambda i: (i, 0)),
            tok, tok, tok,
            pl.BlockSpec((1, D_HYENA), lambda i: (0, 0)),
            pl.BlockSpec((tm, d), lambda i: (i, ga_blk)),
            pl.BlockSpec((tm, d), lambda i: (i, ga_blk + 1)),
            pl.BlockSpec((tm, d), lambda i: (i, 0)),
            pl.BlockSpec(wa.shape, lambda i: (0, 0)),
            pl.BlockSpec(wh.shape, lambda i: (0, 0)),
            pl.BlockSpec(wo.shape, lambda i: (0, 0)),
        ],
        out_specs=pl.BlockSpec((tm, d), lambda i: (i, 0)),
        out_shape=jax.ShapeDtypeStruct((n, d), F32),
        compiler_params=_cparams(("parallel",)),
        name="merge_outproj",
    )(ya, yc, u, x0, d_skip.astype(F32).reshape(1, D_HYENA), z, z, x2, wa, wh, wo)


def _rms(x, g):
    inv = lax.rsqrt(jnp.mean(x * x, axis=-1, keepdims=True) + EPS)
    return (x * inv) * g


def _ffn_kernel(x_ref, g_ref, wg_ref, wu_ref, wd_ref, gf_ref, o_ref, h_ref, acc_ref):
    j = pl.program_id(1)

    @pl.when(j == 0)
    def _():
        h_ref[...] = _rms(x_ref[...], g_ref[...]).astype(BF16)
        acc_ref[...] = jnp.zeros_like(acc_ref)

    h = h_ref[...]
    gate = jnp.dot(h, wg_ref[...], preferred_element_type=F32)
    up = jnp.dot(h, wu_ref[...], preferred_element_type=F32)
    act = (gate * jax.nn.sigmoid(gate) * up).astype(BF16)
    acc_ref[...] += jnp.dot(act, wd_ref[...], preferred_element_type=F32)

    @pl.when(j == pl.num_programs(1) - 1)
    def _():
        o_ref[...] = _rms(x_ref[...] + acc_ref[...], gf_ref[...])


def _ffn(x1, g, wg, wu, wd, gf):
    n, d = x1.shape
    dff = wg.shape[1]
    tm = min(512, n)
    tf = dff // 2
    return pl.pallas_call(
        _ffn_kernel,
        grid=(n // tm, dff // tf),
        in_specs=[
            pl.BlockSpec((tm, d), lambda i, j: (i, 0)),
            pl.BlockSpec((1, d), lambda i, j: (0, 0)),
            pl.BlockSpec((d, tf), lambda i, j: (0, j)),
            pl.BlockSpec((d, tf), lambda i, j: (0, j)),
            pl.BlockSpec((tf, d), lambda i, j: (j, 0)),
            pl.BlockSpec((1, d), lambda i, j: (0, 0)),
        ],
        out_specs=pl.BlockSpec((tm, d), lambda i, j: (i, 0)),
        out_shape=jax.ShapeDtypeStruct((n, d), F32),
        scratch_shapes=[pltpu.VMEM((tm, d), BF16), pltpu.VMEM((tm, d), F32)],
        compiler_params=_cparams(("parallel", "arbitrary")),
        name="ffn_final",
    )(x1, g.reshape(1, d), wg, wu, wd, gf.reshape(1, d))


def _layer(x2, bsz, L, p):
    n, d = x2.shape
    z = _norm_inproj(x2, p["norm_mix"], p["w_in"])
    z3 = z.reshape(bsz, L, z.shape[1])
    ya = _attention(z3, p["rpb"])
    u, x0 = _conv_gate(z3, p["conv_w"], p["conv_b"])
    tb = dict(_fft_tables(L))
    for name in ("f1", "f1inv", "f2", "f2inv"):
        tb[name] = jnp.asarray(tb[name]).astype(BF16)
    hfilt = _implicit_filters(L, *p["filt"])
    kf = _filter_spectrum(hfilt, tb)
    yc = _hyena_conv(u, kf, tb)
    x1 = _merge(ya.reshape(n, D_ATTN), yc.reshape(n, D_HYENA), u.reshape(n, D_HYENA),
                x0.reshape(n, D_HYENA), p["hyena_d"], z, x2,
                p["w_br_attn"], p["w_br_hyena"], p["w_out"])
    return x1


def _trunk(x, layers, norm_final):
    bsz, L, d = x.shape
    x2 = x.reshape(bsz * L, d)
    depth = len(layers)
    for li, p in enumerate(layers):
        x1 = _layer(x2, bsz, L, p)
        gf = norm_final if li == depth - 1 else None
        assert gf is not None, "only the last layer fuses the final norm"
        x2 = _ffn(x1, p["norm_ffn"], p["w_gate"], p["w_up"], p["w_down"], gf)
    return x2.reshape(bsz, L, d)


def kernel(x_prompt, x_sample, norm_mix, w_in, rpb, conv_w, conv_b, filt_w1, filt_b1, filt_w2,
           filt_b2, filt_w3, filt_b3, filt_w4, filt_freq, hyena_d, w_br_attn, w_br_hyena, w_out,
           norm_ffn, w_gate, w_up, w_down, norm_final):
    depth = w_in.shape[0]
    assert depth == 1
    layers = []
    for l in range(depth):
        layers.append(dict(
            norm_mix=norm_mix[l], w_in=w_in[l].astype(BF16), rpb=rpb[l].reshape(-1),
            conv_w=conv_w[l], conv_b=conv_b[l],
            filt=(filt_w1[l], filt_b1[l], filt_w2[l], filt_b2[l], filt_w3[l], filt_b3[l],
                  filt_w4[l], filt_freq[l]),
            hyena_d=hyena_d[l],
            w_br_attn=w_br_attn[l].astype(BF16), w_br_hyena=w_br_hyena[l].astype(BF16),
            w_out=w_out[l].astype(BF16), norm_ffn=norm_ffn[l],
            w_gate=w_gate[l].astype(BF16), w_up=w_up[l].astype(BF16),
            w_down=w_down[l].astype(BF16),
        ))
    y_prompt = _trunk(x_prompt, layers, norm_final)
    y_sample = _trunk(x_sample, layers, norm_final)
    return (y_prompt, y_sample)
```

```python
import functools
import math

import numpy as np
import jax
import jax.numpy as jnp
from jax import lax
from jax.experimental import pallas as pl
from jax.experimental.pallas import tpu as pltpu

F32 = jnp.float32
BF16 = jnp.bfloat16

GRID_W = 64
N_HEADS = 8
HEAD_DIM = 64
D_ATTN = N_HEADS * HEAD_DIM
WIN_R = 8
WIN_C = 16
D_HYENA = 512
SHORT_K = 3
FILTER_EMB = 33
FILTER_HIDDEN = 64
DECAY_TARGET = 1e-2
FAST_DECAY_PCT = 0.3
SLOW_DECAY_PCT = 1.5
EPS = 1e-6

FFT_N2 = 128
ATT_ROWS = 8
NEG_BIG = -1e30
VMEM_LIMIT = 56 * 1024 * 1024


def _cparams(sem):
    return pltpu.CompilerParams(dimension_semantics=sem, vmem_limit_bytes=VMEM_LIMIT)


def _norm_inproj_kernel(x_ref, g_ref, w_ref, o_ref, h_ref):
    @pl.when(pl.program_id(1) == 0)
    def _():
        x = x_ref[...]
        inv = lax.rsqrt(jnp.mean(x * x, axis=-1, keepdims=True) + EPS)
        h_ref[...] = ((x * inv) * g_ref[...]).astype(BF16)

    o_ref[...] = jnp.dot(h_ref[...], w_ref[...], preferred_element_type=F32).astype(o_ref.dtype)


def _norm_inproj(x2, g, w_bf16):
    n, d = x2.shape
    d_in = w_bf16.shape[1]
    tm = min(1024, n)
    tn = 1280
    return pl.pallas_call(
        _norm_inproj_kernel,
        grid=(n // tm, d_in // tn),
        in_specs=[
            pl.BlockSpec((tm, d), lambda i, j: (i, 0)),
            pl.BlockSpec((1, d), lambda i, j: (0, 0)),
            pl.BlockSpec((d, tn), lambda i, j: (0, j)),
        ],
        out_specs=pl.BlockSpec((tm, tn), lambda i, j: (i, j)),
        out_shape=jax.ShapeDtypeStruct((n, d_in), BF16),
        scratch_shapes=[pltpu.VMEM((tm, d), BF16)],
        compiler_params=_cparams(("parallel", "arbitrary")),
        name="norm_inproj",
    )(x2, g.reshape(1, d), w_bf16)


N_DR = 2 * WIN_R - 1
N_DC = 2 * WIN_C - 1
KV_ROWS = 2 * ATT_ROWS


def _build_bias_tiles(rpb_ref, pt_ref):
    k = lax.broadcasted_iota(jnp.int32, (GRID_W, 128), 0)
    lane = lax.broadcasted_iota(jnp.int32, (GRID_W, 128), 1)
    q = lane & (GRID_W - 1)
    first_half = lane < GRID_W
    cs = jnp.clip(q - WIN_C // 2, 0, GRID_W - WIN_C)
    valid = (k >= cs) & (k < cs + WIN_C)
    dc = jnp.clip(k - q, -(WIN_C - 1), WIN_C - 1) + (WIN_C - 1)

    def body(idx, carry):
        p = idx // N_DR
        d = idx % N_DR
        base0 = ((2 * p) * N_DR + d) * N_DC
        base1 = ((2 * p + 1) * N_DR + d) * N_DC
        acc = jnp.zeros((GRID_W, 128), F32)
        for off in range(N_DC):
            val = jnp.where(first_half, rpb_ref[base0 + off], rpb_ref[base1 + off])
            acc = jnp.where(dc == off, val, acc)
        pt_ref[idx] = jnp.where(valid, acc, NEG_BIG)
        return carry

    lax.fori_loop(0, (N_HEADS // 2) * N_DR, body, 0)


def _kv_window_start(i, rows):
    return jnp.clip(i * ATT_ROWS - WIN_R // 2, 0, rows - KV_ROWS)


def _attn_kernel(rpb_ref, q_ref, k_ref, v_ref, o_ref, pt_ref, *, rows):
    b = pl.program_id(0)
    i = pl.program_id(1)

    @pl.when((b == 0) & (i == 0))
    def _():
        _build_bias_tiles(rpb_ref, pt_ref)

    lane = lax.broadcasted_iota(jnp.int32, (GRID_W, 128), 1)
    first_half = lane < HEAD_DIM
    eye = (lax.broadcasted_iota(jnp.int32, (128, 128), 0)
           == lax.broadcasted_iota(jnp.int32, (128, 128), 1)).astype(BF16)
    ones = jnp.ones((WIN_R * GRID_W, 128), BF16)
    nkeys = WIN_R * GRID_W
    wstart = _kv_window_start(i, rows)
    nt_dims = (((1,), (1,)), ((), ()))

    def row_body(rr, carry):
        r = i * ATT_ROWS + rr
        rs = jnp.clip(r - WIN_R // 2, 0, rows - WIN_R)
        s = rs - r + (WIN_R - 1)
        koff = pl.multiple_of((rs - wstart) * GRID_W, GRID_W)
        qoff = pl.multiple_of(rr * GRID_W, GRID_W)
        q = q_ref[pl.ds(qoff, GRID_W), :] * (HEAD_DIM ** -0.5)
        pairs = range(N_HEADS // 2)
        sts = []
        for p in pairs:
            qp = q[:, 128 * p:128 * (p + 1)]
            zero = jnp.zeros_like(qp)
            wt = jnp.concatenate([jnp.where(first_half, qp, zero),
                                  jnp.where(first_half, zero, qp)], axis=0)
            kp = k_ref[pl.ds(koff, nkeys), 128 * p:128 * (p + 1)]
            sts.append(lax.dot_general(kp, wt, nt_dims, preferred_element_type=F32))
        pms = []
        for p in pairs:
            st = sts[p]
            ch = [st[GRID_W * c:GRID_W * (c + 1)] + pt_ref[p * N_DR + s + c] for c in range(WIN_R)]
            m = ch[0]
            for c in range(1, WIN_R):
                m = jnp.maximum(m, ch[c])
            m = jnp.max(m, axis=0, keepdims=True)
            pt = jnp.concatenate([jnp.exp(c - m) for c in ch], axis=0).astype(BF16)
            pms.append(lax.dot_general(eye, pt, nt_dims, preferred_element_type=F32).astype(BF16))
        outs = []
        for p in pairs:
            vp = v_ref[pl.ds(koff, nkeys), 128 * p:128 * (p + 1)]
            ov = jnp.dot(pms[p], jnp.concatenate([vp, ones], axis=1), preferred_element_type=F32)
            o = ov[:, 0:128] / ov[:, 128:256]
            outs.append(jnp.where(first_half, o[0:GRID_W], o[GRID_W:2 * GRID_W]))
        o_ref[pl.ds(qoff, GRID_W), :] = jnp.concatenate(outs, axis=1).astype(o_ref.dtype)
        return carry

    lax.fori_loop(0, ATT_ROWS, row_body, 0)


def _attention(z3, rpb_flat):
    bsz, L, _ = z3.shape
    rows = L // GRID_W
    assert rows % ATT_ROWS == 0 and rows >= KV_ROWS
    nblk = rows // ATT_ROWS
    blk = ATT_ROWS * GRID_W

    def window(col):
        return pl.BlockSpec(
            (pl.Squeezed(), pl.Element(KV_ROWS * GRID_W), pl.Element(D_ATTN)),
            lambda b, i: (b, _kv_window_start(i, rows) * GRID_W, col * D_ATTN))

    return pl.pallas_call(
        functools.partial(_attn_kernel, rows=rows),
        grid=(bsz, nblk),
        in_specs=[
            pl.BlockSpec(memory_space=pltpu.SMEM),
            pl.BlockSpec((pl.Squeezed(), blk, D_ATTN), lambda b, i: (b, i, 0)),
            window(1),
            window(2),
        ],
        out_specs=pl.BlockSpec((pl.Squeezed(), blk, D_ATTN), lambda b, i: (b, i, 0)),
        out_shape=jax.ShapeDtypeStruct((bsz, L, D_ATTN), BF16),
        scratch_shapes=[pltpu.VMEM(((N_HEADS // 2) * N_DR, GRID_W, 128), F32)],
        compiler_params=_cparams(("arbitrary", "arbitrary")),
        name="nbr_attention",
    )(rpb_flat, z3, z3, z3)


def _conv_gate_kernel(xm_ref, xp_ref, xn_ref, w_ref, b_ref, u_ref, x0_ref, *, nt):
    i = pl.program_id(1)
    x = xm_ref[0].astype(F32)
    tm = x.shape[0]
    row = lax.broadcasted_iota(jnp.int32, x.shape, 0)
    halo = xp_ref.shape[1]
    prev_row = jnp.where(i == 0, 0.0, xp_ref[0].astype(F32)[halo - 1:halo, :])
    next_row = jnp.where(i == nt - 1, 0.0, xn_ref[0].astype(F32)[0:1, :])
    xm1 = jnp.where(row == 0, prev_row, pltpu.roll(x, 1, 0))
    xp1 = jnp.where(row == tm - 1, next_row, pltpu.roll(x, tm - 1, 0))
    y = xm1 * w_ref[0:1, :] + x * w_ref[1:2, :] + xp1 * w_ref[2:3, :] + b_ref[...]
    c = D_HYENA
    x0_ref[0] = y[:, 0:c]
    u_ref[0] = y[:, c:2 * c] * y[:, 2 * c:3 * c]


def _conv_gate(z3, conv_w, conv_b):
    bsz, L, _ = z3.shape
    c3 = 3 * D_HYENA
    tm = min(512, L)
    nt = L // tm
    halo = 16
    hb = tm // halo
    nh = L // halo
    return pl.pallas_call(
        functools.partial(_conv_gate_kernel, nt=nt),
        grid=(bsz, nt),
        in_specs=[
            pl.BlockSpec((1, tm, c3), lambda b, i: (b, i, 1)),
            pl.BlockSpec((1, halo, c3), lambda b, i: (b, jnp.maximum(i * hb - 1, 0), 1)),
            pl.BlockSpec((1, halo, c3), lambda b, i: (b, jnp.minimum((i + 1) * hb, nh - 1), 1)),
            pl.BlockSpec((SHORT_K, c3), lambda b, i: (0, 0)),
            pl.BlockSpec((1, c3), lambda b, i: (0, 0)),
        ],
        out_specs=[
            pl.BlockSpec((1, tm, D_HYENA), lambda b, i: (b, i, 0)),
            pl.BlockSpec((1, tm, D_HYENA), lambda b, i: (b, i, 0)),
        ],
        out_shape=[
            jax.ShapeDtypeStruct((bsz, L, D_HYENA), F32),
            jax.ShapeDtypeStruct((bsz, L, D_HYENA), F32),
        ],
        compiler_params=_cparams(("parallel", "parallel")),
        name="conv_gate",
    )(z3, z3, z3, conv_w, conv_b.reshape(1, c3))


def _filter_kernel(fb_ref, w1t_ref, w1c_ref, w1s_ref, b1_ref, w2_ref, b2_ref, w3_ref, b3_ref,
                   w4_ref, fr_ref, dl_ref, o_ref, *, L, tp):
    j = pl.program_id(0)
    hi = lax.Precision.HIGHEST
    pos = (lax.broadcasted_iota(jnp.int32, (1, tp), 1) + j * tp).astype(F32)
    t = pos / (L - 1.0)
    omega = (2.0 * math.pi) * pos / float(L)
    ang = fb_ref[...] * omega
    fr = fr_ref[...]
    pre = (w1t_ref[...] * t
           + jnp.dot(w1c_ref[...], jnp.cos(ang), precision=hi, preferred_element_type=F32)
           - jnp.dot(w1s_ref[...], jnp.sin(ang), precision=hi, preferred_element_type=F32))
    h = jnp.sin(fr * (pre + b1_ref[...]))
    h = jnp.sin(fr * (jnp.dot(w2_ref[...], h, precision=hi, preferred_element_type=F32) + b2_ref[...]))
    h = jnp.sin(fr * (jnp.dot(w3_ref[...], h, precision=hi, preferred_element_type=F32) + b3_ref[...]))
    out = jnp.dot(h.T, w4_ref[...], precision=hi, preferred_element_type=F32)
    tcol = (lax.broadcasted_iota(jnp.int32, (tp, 1), 0) + j * tp).astype(F32) / (L - 1.0)
    decay = jnp.exp(-tcol * dl_ref[...])
    c = D_HYENA
    o_ref[:, 0:c] = out[:, 0:c] * decay
    o_ref[:, c:2 * c] = out[:, c:2 * c] * decay


def _implicit_filters(L, w1, b1, w2, b2, w3, b3, w4, freq):
    bands = (FILTER_EMB - 1) // 2
    fh = FILTER_HIDDEN
    tp = min(512, L)
    fb = jnp.linspace(1e-4, bands - 1, bands, dtype=F32).reshape(bands, 1)
    max_decay = math.log(DECAY_TARGET) / FAST_DECAY_PCT
    min_decay = math.log(DECAY_TARGET) / SLOW_DECAY_PCT
    deltas = jnp.abs(jnp.linspace(min_decay, max_decay, D_HYENA, dtype=F32)).reshape(1, D_HYENA)
    w1 = w1.astype(F32)
    args = (
        fb,
        w1[0:1, :].T,
        w1[1:1 + bands, :].T,
        w1[1 + bands:, :].T,
        b1.astype(F32).reshape(fh, 1),
        w2.astype(F32).T, b2.astype(F32).reshape(fh, 1),
        w3.astype(F32).T, b3.astype(F32).reshape(fh, 1),
        w4.astype(F32),
        freq.astype(F32).reshape(fh, 1),
        deltas,
    )

    def full(a):
        return pl.BlockSpec(a.shape, lambda j: (0,) * a.ndim)

    return pl.pallas_call(
        functools.partial(_filter_kernel, L=L, tp=tp),
        grid=(L // tp,),
        in_specs=[full(a) for a in args],
        out_specs=pl.BlockSpec((tp, 2 * D_HYENA), lambda j: (j, 0)),
        out_shape=jax.ShapeDtypeStruct((L, 2 * D_HYENA), F32),
        compiler_params=_cparams(("parallel",)),
        name="implicit_filter",
    )(*args)


FFT_LANES = 128
FFT_T1 = 4


@functools.lru_cache(maxsize=None)
def _fft_tables(L):
    n = 2 * L
    n2 = FFT_N2
    n1 = n // n2
    n1h = n1 // 2
    odd = 2 * np.arange(n1h) + 1
    th = 2.0 * np.pi * (np.outer(odd, np.arange(n1h)) % (2 * n1)) / (2 * n1)
    f1 = np.stack([np.cos(th), -np.sin(th)], axis=0)
    tw = 2.0 * np.pi * (np.outer(odd, np.arange(n2)) % (2 * n)) / (2 * n)
    twr = np.cos(tw)[:, :, None]
    twi = -np.sin(tw)[:, :, None]
    t2 = 2.0 * np.pi * (np.outer(np.arange(n2), np.arange(n2)) % n2) / n2
    fr, fi = np.cos(t2), -np.sin(t2)
    f2 = np.block([[fr, -fi], [fi, fr]])
    f2inv = np.block([[fr, fi], [-fi, fr]])
    f1inv = np.stack([np.cos(th.T), -np.sin(th.T)], axis=0) * (2.0 / n)
    return dict(
        n1=n1, n1h=n1h,
        f1=np.asarray(f1, np.float32), f1inv=np.asarray(f1inv, np.float32),
        f2=np.asarray(f2, np.float32), f2inv=np.asarray(f2inv, np.float32),
        twr=np.asarray(twr, np.float32), twi=np.asarray(twi, np.float32),
    )


FFT_A = 8


def _outer_dft(f, src_ref, emit):
    def body(i, carry):
        a0 = pl.multiple_of(i * FFT_A, FFT_A)
        t = pltpu.einshape("nal->anl", src_ref[:, pl.ds(a0, FFT_A), :])
        outs = []
        for j in range(0, FFT_A, 2):
            rhs = jnp.concatenate([t[j], t[j + 1]], axis=1).astype(BF16)
            res = jnp.dot(f, rhs, preferred_element_type=F32)
            outs += [res[:, 0:FFT_LANES], res[:, FFT_LANES:2 * FFT_LANES]]
        emit(a0, pltpu.einshape("anl->nal", jnp.stack(outs, axis=0)))
        return carry

    lax.fori_loop(0, FFT_N2 // FFT_A, body, 0, unroll=2)


def _fft_stage1_kernel(f_ref, u_ref, o_ref):
    def emit(a0, val):
        o_ref[:, pl.ds(a0, FFT_A), :] = val

    _outer_dft(f_ref[0], u_ref, emit)


def _fft_stage1(u3, f1):
    bsz, L, c = u3.shape
    n1h = L // FFT_N2
    return pl.pallas_call(
        _fft_stage1_kernel,
        grid=(bsz, c // FFT_LANES, 2),
        in_specs=[
            pl.BlockSpec((1, n1h, n1h), lambda b, j, ri: (ri, 0, 0)),
            pl.BlockSpec((pl.Squeezed(), n1h, FFT_N2, FFT_LANES), lambda b, j, ri: (b, 0, 0, j)),
        ],
        out_specs=pl.BlockSpec((pl.Squeezed(), pl.Squeezed(), n1h, FFT_N2, FFT_LANES),
                               lambda b, j, ri: (b, ri, 0, 0, j)),
        out_shape=jax.ShapeDtypeStruct((bsz, 2, n1h, FFT_N2, c), F32),
        compiler_params=_cparams(("parallel", "parallel", "arbitrary")),
        name="fft_stage1",
    )(f1, u3.reshape(bsz, n1h, FFT_N2, c))


def _twiddle(ar, ai, tr, ti):
    return ar * tr - ai * ti, ar * ti + ai * tr


def _stage2_inputs(a_ref, twr_ref, twi_ref):
    xs = []
    for j in range(FFT_T1):
        xr, xi = _twiddle(a_ref[0, j], a_ref[1, j], twr_ref[j], twi_ref[j])
        xs.append(jnp.concatenate([xr, xi], axis=0).astype(BF16))
    return xs


def _filter_spectrum_kernel(a_ref, twr_ref, twi_ref, f2_ref, o_ref):
    n2, c = FFT_N2, D_HYENA
    f2 = f2_ref[...]
    bigs = [jnp.dot(f2, x, preferred_element_type=F32)
            for x in _stage2_inputs(a_ref, twr_ref, twi_ref)]
    for j, big in enumerate(bigs):
        o_ref[j, 0] = big[0:n2, 0:c] + big[0:n2, c:2 * c]
        o_ref[j, 1] = big[n2:2 * n2, 0:c] - big[n2:2 * n2, c:2 * c]


def _filter_spectrum(hfilt, tb):
    L, c2 = hfilt.shape
    n1h = tb["n1h"]
    a5 = _fft_stage1(hfilt.reshape(1, L, c2), tb["f1"])
    return pl.pallas_call(
        _filter_spectrum_kernel,
        grid=(n1h // FFT_T1,),
        in_specs=[
            pl.BlockSpec((pl.Squeezed(), 2, FFT_T1, FFT_N2, c2), lambda k: (0, 0, k, 0, 0)),
            pl.BlockSpec((FFT_T1, FFT_N2, 1), lambda k: (k, 0, 0)),
            pl.BlockSpec((FFT_T1, FFT_N2, 1), lambda k: (k, 0, 0)),
            pl.BlockSpec((2 * FFT_N2, 2 * FFT_N2), lambda k: (0, 0)),
        ],
        out_specs=pl.BlockSpec((FFT_T1, 2, FFT_N2, D_HYENA), lambda k: (k, 0, 0, 0)),
        out_shape=jax.ShapeDtypeStruct((n1h, 2, FFT_N2, D_HYENA), F32),
        compiler_params=_cparams(("parallel",)),
        name="filter_spectrum",
    )(a5, tb["twr"], tb["twi"], tb["f2"])


def _fft_mid_kernel(a_ref, kf_ref, twr_ref, twi_ref, f2_ref, f2i_ref, o_ref):
    n2 = FFT_N2
    f2, f2i = f2_ref[...], f2i_ref[...]
    bigs = [jnp.dot(f2, x, preferred_element_type=F32)
            for x in _stage2_inputs(a_ref, twr_ref, twi_ref)]
    ys = []
    for j, big in enumerate(bigs):
        sr, si = big[0:n2], big[n2:2 * n2]
        kr, ki = kf_ref[j, 0], kf_ref[j, 1]
        ys.append(jnp.concatenate([sr * kr - si * ki, sr * ki + si * kr], axis=0).astype(BF16))
    backs = [jnp.dot(f2i, y, preferred_element_type=F32) for y in ys]
    for j, back in enumerate(backs):
        br, bi = back[0:n2], back[n2:2 * n2]
        tr, ti = twr_ref[j], twi_ref[j]
        o_ref[0, j] = br * tr + bi * ti
        o_ref[1, j] = bi * tr - br * ti


def _fft_mid(a5, kf, tb):
    bsz, _, n1h, n2, c = a5.shape
    slot = pl.BlockSpec((pl.Squeezed(), 2, FFT_T1, n2, c), lambda k, b: (b, 0, k, 0, 0))
    return pl.pallas_call(
        _fft_mid_kernel,
        grid=(n1h // FFT_T1, bsz),
        in_specs=[
            slot,
            pl.BlockSpec((FFT_T1, 2, n2, c), lambda k, b: (k, 0, 0, 0)),
            pl.BlockSpec((FFT_T1, n2, 1), lambda k, b: (k, 0, 0)),
            pl.BlockSpec((FFT_T1, n2, 1), lambda k, b: (k, 0, 0)),
            pl.BlockSpec((2 * n2, 2 * n2), lambda k, b: (0, 0)),
            pl.BlockSpec((2 * n2, 2 * n2), lambda k, b: (0, 0)),
        ],
        out_specs=slot,
        out_shape=jax.ShapeDtypeStruct(a5.shape, F32),
        compiler_params=_cparams(("parallel", "parallel")),
        name="fft_mid",
    )(a5, kf, tb["twr"], tb["twi"], tb["f2"], tb["f2inv"])


def _fft_out_kernel(f_ref, b_ref, o_ref):
    ri = pl.program_id(2)

    @pl.when(ri == 0)
    def _():
        def emit(a0, val):
            o_ref[:, pl.ds(a0, FFT_A), :] = val

        _outer_dft(f_ref[0], b_ref, emit)

    @pl.when(ri == 1)
    def _():
        def emit(a0, val):
            o_ref[:, pl.ds(a0, FFT_A), :] += val

        _outer_dft(f_ref[0], b_ref, emit)


def _fft_out(b5, f1inv):
    bsz, _, n1h, n2, c = b5.shape
    return pl.pallas_call(
        _fft_out_kernel,
        grid=(bsz, c // FFT_LANES, 2),
        in_specs=[
            pl.BlockSpec((1, n1h, n1h), lambda b, j, ri: (ri, 0, 0)),
            pl.BlockSpec((pl.Squeezed(), pl.Squeezed(), n1h, n2, FFT_LANES),
                         lambda b, j, ri: (b, ri, 0, 0, j)),
        ],
        out_specs=pl.BlockSpec((pl.Squeezed(), n1h, n2, FFT_LANES), lambda b, j, ri: (b, 0, 0, j)),
        out_shape=jax.ShapeDtypeStruct((bsz, n1h, n2, c), F32),
        compiler_params=_cparams(("parallel", "parallel", "arbitrary")),
        name="fft_out",
    )(f1inv, b5).reshape(bsz, n1h * n2, c)


def _hyena_conv(u, kf, tb):
    a5 = _fft_stage1(u, tb["f1"])
    b5 = _fft_mid(a5, kf, tb)
    return _fft_out(b5, tb["f1inv"])


def _merge_kernel(ya_ref, yc_ref, u_ref, x0_ref, dk_ref, ga_ref, gh_ref, x_ref, wa_ref, wh_ref,
                  wo_ref, o_ref):
    yh = x0_ref[...] * (yc_ref[...] + u_ref[...] * dk_ref[...])
    pa = jnp.dot(ya_ref[...], wa_ref[...], preferred_element_type=F32)
    ph = jnp.dot(yh.astype(BF16), wh_ref[...], preferred_element_type=F32)
    merged = (jax.nn.sigmoid(ga_ref[...].astype(F32)) * pa
              + jax.nn.sigmoid(gh_ref[...].astype(F32)) * ph)
    o_ref[...] = x_ref[...] + jnp.dot(merged.astype(BF16), wo_ref[...],
                                      preferred_element_type=F32)


def _merge(ya, yc, u, x0, d_skip, z, x2, wa, wh, wo):
    n, d = x2.shape
    tm = min(512, n)
    ga_blk = (D_ATTN * 3 + D_HYENA * 3) // d
    tok = pl.BlockSpec((tm, D_HYENA), lambda i: (i, 0))
    return pl.pallas_call(
        _merge_kernel,
        grid=(n // tm,),
        in_specs=[
            pl.BlockSpec((tm, D_ATTN), lambda i: (i, 0)),
            tok, tok, tok,
            pl.BlockSpec((1, D_HYENA), lambda i: (0, 0)),
            pl.BlockSpec((tm, d), lambda i: (i, ga_blk)),
            pl.BlockSpec((tm, d), lambda i: (i, ga_blk + 1)),
            pl.BlockSpec((tm, d), lambda i: (i, 0)),
            pl.BlockSpec(wa.shape, lambda i: (0, 0)),
            pl.BlockSpec(wh.shape, lambda i: (0, 0)),
            pl.BlockSpec(wo.shape, lambda i: (0, 0)),
        ],
        out_specs=pl.BlockSpec((tm, d), lambda i: (i, 0)),
        out_shape=jax.ShapeDtypeStruct((n, d), F32),
        compiler_params=_cparams(("parallel",)),
        name="merge_outproj",
    )(ya, yc, u, x0, d_skip.astype(F32).reshape(1, D_HYENA), z, z, x2, wa, wh, wo)


def _rms(x, g):
    inv = lax.rsqrt(jnp.mean(x * x, axis=-1, keepdims=True) + EPS)
    return (x * inv) * g


def _ffn_kernel(x_ref, g_ref, wg_ref, wu_ref, wd_ref, gf_ref, o_ref, h_ref, acc_ref):
    j = pl.program_id(1)

    @pl.when(j == 0)
    def _():
        h_ref[...] = _rms(x_ref[...], g_ref[...]).astype(BF16)
        acc_ref[...] = jnp.zeros_like(acc_ref)

    h = h_ref[...]
    gate = jnp.dot(h, wg_ref[...], preferred_element_type=F32)
    up = jnp.dot(h, wu_ref[...], preferred_element_type=F32)
    act = (gate * jax.nn.sigmoid(gate) * up).astype(BF16)
    acc_ref[...] += jnp.dot(act, wd_ref[...], preferred_element_type=F32)

    @pl.when(j == pl.num_programs(1) - 1)
    def _():
        o_ref[...] = _rms(x_ref[...] + acc_ref[...], gf_ref[...])


def _ffn(x1, g, wg, wu, wd, gf):
    n, d = x1.shape
    dff = wg.shape[1]
    tm = min(512, n)
    tf = dff // 2
    return pl.pallas_call(
        _ffn_kernel,
        grid=(n // tm, dff // tf),
        in_specs=[
            pl.BlockSpec((tm, d), lambda i, j: (i, 0)),
            pl.BlockSpec((1, d), lambda i, j: (0, 0)),
            pl.BlockSpec((d, tf), lambda i, j: (0, j)),
            pl.BlockSpec((d, tf), lambda i, j: (0, j)),
            pl.BlockSpec((tf, d), lambda i, j: (j, 0)),
            pl.BlockSpec((1, d), lambda i, j: (0, 0)),
        ],
        out_specs=pl.BlockSpec((tm, d), lambda i, j: (i, 0)),
        out_shape=jax.ShapeDtypeStruct((n, d), F32),
        scratch_shapes=[pltpu.VMEM((tm, d), BF16), pltpu.VMEM((tm, d), F32)],
        compiler_params=_cparams(("parallel", "arbitrary")),
        name="ffn_final",
    )(x1, g.reshape(1, d), wg, wu, wd, gf.reshape(1, d))


def _layer(x2, bsz, L, p):
    n, d = x2.shape
    z = _norm_inproj(x2, p["norm_mix"], p["w_in"])
    z3 = z.reshape(bsz, L, z.shape[1])
    ya = _attention(z3, p["rpb"])
    u, x0 = _conv_gate(z3, p["conv_w"], p["conv_b"])
    tb = dict(_fft_tables(L))
    for name in ("f1", "f1inv", "f2", "f2inv"):
        tb[name] = jnp.asarray(tb[name]).astype(BF16)
    hfilt = _implicit_filters(L, *p["filt"])
    kf = _filter_spectrum(hfilt, tb)
    yc = _hyena_conv(u, kf, tb)
    x1 = _merge(ya.reshape(n, D_ATTN), yc.reshape(n, D_HYENA), u.reshape(n, D_HYENA),
                x0.reshape(n, D_HYENA), p["hyena_d"], z, x2,
                p["w_br_attn"], p["w_br_hyena"], p["w_out"])
    return x1


def _trunk(x, layers, norm_final):
    bsz, L, d = x.shape
    x2 = x.reshape(bsz * L, d)
    depth = len(layers)
    for li, p in enumerate(layers):
        x1 = _layer(x2, bsz, L, p)
        gf = norm_final if li == depth - 1 else None
        assert gf is not None, "only the last layer fuses the final norm"
        x2 = _ffn(x1, p["norm_ffn"], p["w_gate"], p["w_up"], p["w_down"], gf)
    return x2.reshape(bsz, L, d)


def kernel(x_prompt, x_sample, norm_mix, w_in, rpb, conv_w, conv_b, filt_w1, filt_b1, filt_w2,
           filt_b2, filt_w3, filt_b3, filt_w4, filt_freq, hyena_d, w_br_attn, w_br_hyena, w_out,
           norm_ffn, w_gate, w_up, w_down, norm_final):
    depth = w_in.shape[0]
    assert depth == 1
    layers = []
    for l in range(depth):
        layers.append(dict(
            norm_mix=norm_mix[l], w_in=w_in[l].astype(BF16), rpb=rpb[l].reshape(-1),
            conv_w=conv_w[l], conv_b=conv_b[l],
            filt=(filt_w1[l], filt_b1[l], filt_w2[l], filt_b2[l], filt_w3[l], filt_b3[l],
                  filt_w4[l], filt_freq[l]),
            hyena_d=hyena_d[l],
            w_br_attn=w_br_attn[l].astype(BF16), w_br_hyena=w_br_hyena[l].astype(BF16),
            w_out=w_out[l].astype(BF16), norm_ffn=norm_ffn[l],
            w_gate=w_gate[l].astype(BF16), w_up=w_up[l].astype(BF16),
            w_down=w_down[l].astype(BF16),
        ))
    y_prompt = _trunk(x_prompt, layers, norm_final)
    y_sample = _trunk(x_sample, layers, norm_final)
    return (y_prompt, y_sample)
```

```python
import functools
import math

import numpy as np
import jax
import jax.numpy as jnp
from jax import lax
from jax.experimental import pallas as pl
from jax.experimental.pallas import tpu as pltpu

F32 = jnp.float32
BF16 = jnp.bfloat16

GRID_W = 64
N_HEADS = 8
HEAD_DIM = 64
D_ATTN = N_HEADS * HEAD_DIM
WIN_R = 8
WIN_C = 16
D_HYENA = 512
SHORT_K = 3
FILTER_EMB = 33
FILTER_HIDDEN = 64
DECAY_TARGET = 1e-2
FAST_DECAY_PCT = 0.3
SLOW_DECAY_PCT = 1.5
EPS = 1e-6

FFT_N2 = 128
ATT_ROWS = 8
NEG_BIG = -1e30
VMEM_LIMIT = 56 * 1024 * 1024


def _cparams(sem):
    return pltpu.CompilerParams(dimension_semantics=sem, vmem_limit_bytes=VMEM_LIMIT)


def _norm_inproj_kernel(x_ref, g_ref, w_ref, o_ref, h_ref):
    @pl.when(pl.program_id(1) == 0)
    def _():
        x = x_ref[...]
        inv = lax.rsqrt(jnp.mean(x * x, axis=-1, keepdims=True) + EPS)
        h_ref[...] = ((x * inv) * g_ref[...]).astype(BF16)

    o_ref[...] = jnp.dot(h_ref[...], w_ref[...], preferred_element_type=F32).astype(o_ref.dtype)


def _norm_inproj(x2, g, w_bf16):
    n, d = x2.shape
    d_in = w_bf16.shape[1]
    tm = min(1024, n)
    tn = 1280
    return pl.pallas_call(
        _norm_inproj_kernel,
        grid=(n // tm, d_in // tn),
        in_specs=[
            pl.BlockSpec((tm, d), lambda i, j: (i, 0)),
            pl.BlockSpec((1, d), lambda i, j: (0, 0)),
            pl.BlockSpec((d, tn), lambda i, j: (0, j)),
        ],
        out_specs=pl.BlockSpec((tm, tn), lambda i, j: (i, j)),
        out_shape=jax.ShapeDtypeStruct((n, d_in), BF16),
        scratch_shapes=[pltpu.VMEM((tm, d), BF16)],
        compiler_params=_cparams(("parallel", "arbitrary")),
        name="norm_inproj",
    )(x2, g.reshape(1, d), w_bf16)


N_DR = 2 * WIN_R - 1
N_DC = 2 * WIN_C - 1
KV_ROWS = 2 * ATT_ROWS


def _build_bias_tiles(rpb_ref, pt_ref):
    k = lax.broadcasted_iota(jnp.int32, (GRID_W, 128), 0)
    lane = lax.broadcasted_iota(jnp.int32, (GRID_W, 128), 1)
    q = lane & (GRID_W - 1)
    first_half = lane < GRID_W
    cs = jnp.clip(q - WIN_C // 2, 0, GRID_W - WIN_C)
    valid = (k >= cs) & (k < cs + WIN_C)
    dc = jnp.clip(k - q, -(WIN_C - 1), WIN_C - 1) + (WIN_C - 1)

    def body(idx, carry):
        p = idx // N_DR
        d = idx % N_DR
        base0 = ((2 * p) * N_DR + d) * N_DC
        base1 = ((2 * p + 1) * N_DR + d) * N_DC
        acc = jnp.zeros((GRID_W, 128), F32)
        for off in range(N_DC):
            val = jnp.where(first_half, rpb_ref[base0 + off], rpb_ref[base1 + off])
            acc = jnp.where(dc == off, val, acc)
        pt_ref[idx] = jnp.where(valid, acc, NEG_BIG)
        return carry

    lax.fori_loop(0, (N_HEADS // 2) * N_DR, body, 0)


def _kv_window_start(i, rows):
    return jnp.clip(i * ATT_ROWS - WIN_R // 2, 0, rows - KV_ROWS)


def _attn_kernel(rpb_ref, q_ref, k_ref, v_ref, o_ref, pt_ref, *, rows):
    b = pl.program_id(0)
    i = pl.program_id(1)

    @pl.when((b == 0) & (i == 0))
    def _():
        _build_bias_tiles(rpb_ref, pt_ref)

    lane = lax.broadcasted_iota(jnp.int32, (GRID_W, 128), 1)
    first_half = lane < HEAD_DIM
    eye = (lax.broadcasted_iota(jnp.int32, (128, 128), 0)
           == lax.broadcasted_iota(jnp.int32, (128, 128), 1)).astype(BF16)
    ones = jnp.ones((WIN_R * GRID_W, 128), BF16)
    nkeys = WIN_R * GRID_W
    wstart = _kv_window_start(i, rows)
    nt_dims = (((1,), (1,)), ((), ()))

    def row_body(rr, carry):
        r = i * ATT_ROWS + rr
        rs = jnp.clip(r - WIN_R // 2, 0, rows - WIN_R)
        s = rs - r + (WIN_R - 1)
        koff = pl.multiple_of((rs - wstart) * GRID_W, GRID_W)
        qoff = pl.multiple_of(rr * GRID_W, GRID_W)
        q = q_ref[pl.ds(qoff, GRID_W), :] * (HEAD_DIM ** -0.5)
        pairs = range(N_HEADS // 2)
        sts = []
        for p in pairs:
            qp = q[:, 128 * p:128 * (p + 1)]
            zero = jnp.zeros_like(qp)
            wt = jnp.concatenate([jnp.where(first_half, qp, zero),
                                  jnp.where(first_half, zero, qp)], axis=0)
            kp = k_ref[pl.ds(koff, nkeys), 128 * p:128 * (p + 1)]
            sts.append(lax.dot_general(kp, wt, nt_dims, preferred_element_type=F32))
        pms = []
        for p in pairs:
            st = sts[p]
            ch = [st[GRID_W * c:GRID_W * (c + 1)] + pt_ref[p * N_DR + s + c] for c in range(WIN_R)]
            m = ch[0]
            for c in range(1, WIN_R):
                m = jnp.maximum(m, ch[c])
            m = jnp.max(m, axis=0, keepdims=True)
            pt = jnp.concatenate([jnp.exp(c - m) for c in ch], axis=0).astype(BF16)
            pms.append(lax.dot_general(eye, pt, nt_dims, preferred_element_type=F32).astype(BF16))
        outs = []
        for p in pairs:
            vp = v_ref[pl.ds(koff, nkeys), 128 * p:128 * (p + 1)]
            ov = jnp.dot(pms[p], jnp.concatenate([vp, ones], axis=1), preferred_element_type=F32)
            o = ov[:, 0:128] / ov[:, 128:256]
            outs.append(jnp.where(first_half, o[0:GRID_W], o[GRID_W:2 * GRID_W]))
        o_ref[pl.ds(qoff, GRID_W), :] = jnp.concatenate(outs, axis=1).astype(o_ref.dtype)
        return carry

    lax.fori_loop(0, ATT_ROWS, row_body, 0, unroll=2)


def _attention(z3, rpb_flat):
    bsz, L, _ = z3.shape
    rows = L // GRID_W
    assert rows % ATT_ROWS == 0 and rows >= KV_ROWS
    nblk = rows // ATT_ROWS
    blk = ATT_ROWS * GRID_W

    def window(col):
        return pl.BlockSpec(
            (pl.Squeezed(), pl.Element(KV_ROWS * GRID_W), pl.Element(D_ATTN)),
            lambda b, i: (b, _kv_window_start(i, rows) * GRID_W, col * D_ATTN))

    return pl.pallas_call(
        functools.partial(_attn_kernel, rows=rows),
        grid=(bsz, nblk),
        in_specs=[
            pl.BlockSpec(memory_space=pltpu.SMEM),
            pl.BlockSpec((pl.Squeezed(), blk, D_ATTN), lambda b, i: (b, i, 0)),
            window(1),
            window(2),
        ],
        out_specs=pl.BlockSpec((pl.Squeezed(), blk, D_ATTN), lambda b, i: (b, i, 0)),
        out_shape=jax.ShapeDtypeStruct((bsz, L, D_ATTN), BF16),
        scratch_shapes=[pltpu.VMEM(((N_HEADS // 2) * N_DR, GRID_W, 128), F32)],
        compiler_params=_cparams(("arbitrary", "arbitrary")),
        name="nbr_attention",
    )(rpb_flat, z3, z3, z3)


def _conv_gate_kernel(xm_ref, xp_ref, xn_ref, w_ref, b_ref, u_ref, x0_ref, *, nt):
    i = pl.program_id(1)
    x = xm_ref[0].astype(F32)
    tm = x.shape[0]
    row = lax.broadcasted_iota(jnp.int32, x.shape, 0)
    halo = xp_ref.shape[1]
    prev_row = jnp.where(i == 0, 0.0, xp_ref[0].astype(F32)[halo - 1:halo, :])
    next_row = jnp.where(i == nt - 1, 0.0, xn_ref[0].astype(F32)[0:1, :])
    xm1 = jnp.where(row == 0, prev_row, pltpu.roll(x, 1, 0))
    xp1 = jnp.where(row == tm - 1, next_row, pltpu.roll(x, tm - 1, 0))
    y = xm1 * w_ref[0:1, :] + x * w_ref[1:2, :] + xp1 * w_ref[2:3, :] + b_ref[...]
    c = D_HYENA
    x0_ref[0] = y[:, 0:c]
    u_ref[0] = y[:, c:2 * c] * y[:, 2 * c:3 * c]


def _conv_gate(z3, conv_w, conv_b):
    bsz, L, _ = z3.shape
    c3 = 3 * D_HYENA
    tm = min(512, L)
    nt = L // tm
    halo = 16
    hb = tm // halo
    nh = L // halo
    return pl.pallas_call(
        functools.partial(_conv_gate_kernel, nt=nt),
        grid=(bsz, nt),
        in_specs=[
            pl.BlockSpec((1, tm, c3), lambda b, i: (b, i, 1)),
            pl.BlockSpec((1, halo, c3), lambda b, i: (b, jnp.maximum(i * hb - 1, 0), 1)),
            pl.BlockSpec((1, halo, c3), lambda b, i: (b, jnp.minimum((i + 1) * hb, nh - 1), 1)),
            pl.BlockSpec((SHORT_K, c3), lambda b, i: (0, 0)),
            pl.BlockSpec((1, c3), lambda b, i: (0, 0)),
        ],
        out_specs=[
            pl.BlockSpec((1, tm, D_HYENA), lambda b, i: (b, i, 0)),
            pl.BlockSpec((1, tm, D_HYENA), lambda b, i: (b, i, 0)),
        ],
        out_shape=[
            jax.ShapeDtypeStruct((bsz, L, D_HYENA), F32),
            jax.ShapeDtypeStruct((bsz, L, D_HYENA), F32),
        ],
        compiler_params=_cparams(("parallel", "parallel")),
        name="conv_gate",
    )(z3, z3, z3, conv_w, conv_b.reshape(1, c3))


def _filter_kernel(fb_ref, w1t_ref, w1c_ref, w1s_ref, b1_ref, w2_ref, b2_ref, w3_ref, b3_ref,
                   w4_ref, fr_ref, dl_ref, o_ref, *, L, tp):
    j = pl.program_id(0)
    hi = lax.Precision.HIGHEST
    pos = (lax.broadcasted_iota(jnp.int32, (1, tp), 1) + j * tp).astype(F32)
    t = pos / (L - 1.0)
    omega = (2.0 * math.pi) * pos / float(L)
    ang = fb_ref[...] * omega
    fr = fr_ref[...]
    pre = (w1t_ref[...] * t
           + jnp.dot(w1c_ref[...], jnp.cos(ang), precision=hi, preferred_element_type=F32)
           - jnp.dot(w1s_ref[...], jnp.sin(ang), precision=hi, preferred_element_type=F32))
    h = jnp.sin(fr * (pre + b1_ref[...]))
    h = jnp.sin(fr * (jnp.dot(w2_ref[...], h, precision=hi, preferred_element_type=F32) + b2_ref[...]))
    h = jnp.sin(fr * (jnp.dot(w3_ref[...], h, precision=hi, preferred_element_type=F32) + b3_ref[...]))
    out = jnp.dot(h.T, w4_ref[...], precision=hi, preferred_element_type=F32)
    tcol = (lax.broadcasted_iota(jnp.int32, (tp, 1), 0) + j * tp).astype(F32) / (L - 1.0)
    decay = jnp.exp(-tcol * dl_ref[...])
    c = D_HYENA
    o_ref[:, 0:c] = out[:, 0:c] * decay
    o_ref[:, c:2 * c] = out[:, c:2 * c] * decay


def _implicit_filters(L, w1, b1, w2, b2, w3, b3, w4, freq):
    bands = (FILTER_EMB - 1) // 2
    fh = FILTER_HIDDEN
    tp = min(512, L)
    fb = jnp.linspace(1e-4, bands - 1, bands, dtype=F32).reshape(bands, 1)
    max_decay = math.log(DECAY_TARGET) / FAST_DECAY_PCT
    min_decay = math.log(DECAY_TARGET) / SLOW_DECAY_PCT
    deltas = jnp.abs(jnp.linspace(min_decay, max_decay, D_HYENA, dtype=F32)).reshape(1, D_HYENA)
    w1 = w1.astype(F32)
    args = (
        fb,
        w1[0:1, :].T,
        w1[1:1 + bands, :].T,
        w1[1 + bands:, :].T,
        b1.astype(F32).reshape(fh, 1),
        w2.astype(F32).T, b2.astype(F32).reshape(fh, 1),
        w3.astype(F32).T, b3.astype(F32).reshape(fh, 1),
        w4.astype(F32),
        freq.astype(F32).reshape(fh, 1),
        deltas,
    )

    def full(a):
        return pl.BlockSpec(a.shape, lambda j: (0,) * a.ndim)

    return pl.pallas_call(
        functools.partial(_filter_kernel, L=L, tp=tp),
        grid=(L // tp,),
        in_specs=[full(a) for a in args],
        out_specs=pl.BlockSpec((tp, 2 * D_HYENA), lambda j: (j, 0)),
        out_shape=jax.ShapeDtypeStruct((L, 2 * D_HYENA), F32),
        compiler_params=_cparams(("parallel",)),
        name="implicit_filter",
    )(*args)


FFT_LANES = 128
FFT_T1 = 4


@functools.lru_cache(maxsize=None)
def _fft_tables(L):
    n = 2 * L
    n2 = FFT_N2
    n1 = n // n2
    n1h = n1 // 2
    odd = 2 * np.arange(n1h) + 1
    th = 2.0 * np.pi * (np.outer(odd, np.arange(n1h)) % (2 * n1)) / (2 * n1)
    f1 = np.stack([np.cos(th), -np.sin(th)], axis=0)
    tw = 2.0 * np.pi * (np.outer(odd, np.arange(n2)) % (2 * n)) / (2 * n)
    twr = np.cos(tw)[:, :, None]
    twi = -np.sin(tw)[:, :, None]
    t2 = 2.0 * np.pi * (np.outer(np.arange(n2), np.arange(n2)) % n2) / n2
    fr, fi = np.cos(t2), -np.sin(t2)
    f2 = np.block([[fr, -fi], [fi, fr]])
    f2inv = np.block([[fr, fi], [-fi, fr]])
    f1inv = np.stack([np.cos(th.T), -np.sin(th.T)], axis=0) * (2.0 / n)
    return dict(
        n1=n1, n1h=n1h,
        f1=np.asarray(f1, np.float32), f1inv=np.asarray(f1inv, np.float32),
        f2=np.asarray(f2, np.float32), f2inv=np.asarray(f2inv, np.float32),
        twr=np.asarray(twr, np.float32), twi=np.asarray(twi, np.float32),
    )


FFT_A = 16
SLOT_DTYPE = BF16


def _outer_dft(f, src_ref, emit):
    def body(i, carry):
        a0 = pl.multiple_of(i * FFT_A, FFT_A)
        t = jnp.swapaxes(src_ref[:, pl.ds(a0, FFT_A), :].astype(F32), 0, 1)
        outs = []
        for j in range(0, FFT_A, 2):
            rhs = jnp.concatenate([t[j], t[j + 1]], axis=1).astype(BF16)
            res = jnp.dot(f, rhs, preferred_element_type=F32)
            outs += [res[:, 0:FFT_LANES], res[:, FFT_LANES:2 * FFT_LANES]]
        emit(a0, jnp.swapaxes(jnp.stack(outs, axis=0), 0, 1))
        return carry

    lax.fori_loop(0, FFT_N2 // FFT_A, body, 0, unroll=2)


def _fft_stage1_kernel(f_ref, u_ref, o_ref):
    def emit(a0, val):
        o_ref[:, pl.ds(a0, FFT_A), :] = val.astype(o_ref.dtype)

    _outer_dft(f_ref[0], u_ref, emit)


def _fft_stage1(u3, f1):
    bsz, L, c = u3.shape
    n1h = L // FFT_N2
    return pl.pallas_call(
        _fft_stage1_kernel,
        grid=(bsz, c // FFT_LANES, 2),
        in_specs=[
            pl.BlockSpec((1, n1h, n1h), lambda b, j, ri: (ri, 0, 0)),
            pl.BlockSpec((pl.Squeezed(), n1h, FFT_N2, FFT_LANES), lambda b, j, ri: (b, 0, 0, j)),
        ],
        out_specs=pl.BlockSpec((pl.Squeezed(), pl.Squeezed(), n1h, FFT_N2, FFT_LANES),
                               lambda b, j, ri: (b, ri, 0, 0, j)),
        out_shape=jax.ShapeDtypeStruct((bsz, 2, n1h, FFT_N2, c), SLOT_DTYPE),
        compiler_params=_cparams(("parallel", "parallel", "arbitrary")),
        name="fft_stage1",
    )(f1, u3.reshape(bsz, n1h, FFT_N2, c))


def _twiddle(ar, ai, tr, ti):
    return ar * tr - ai * ti, ar * ti + ai * tr


def _stage2_inputs(a_ref, twr_ref, twi_ref):
    xs = []
    for j in range(FFT_T1):
        xr, xi = _twiddle(a_ref[0, j].astype(F32), a_ref[1, j].astype(F32),
                          twr_ref[j], twi_ref[j])
        xs.append(jnp.concatenate([xr, xi], axis=0).astype(BF16))
    return xs


def _filter_spectrum_kernel(a_ref, twr_ref, twi_ref, f2_ref, o_ref):
    n2, c = FFT_N2, D_HYENA
    f2 = f2_ref[...]
    bigs = [jnp.dot(f2, x, preferred_element_type=F32)
            for x in _stage2_inputs(a_ref, twr_ref, twi_ref)]
    for j, big in enumerate(bigs):
        o_ref[j, 0] = big[0:n2, 0:c] + big[0:n2, c:2 * c]
        o_ref[j, 1] = big[n2:2 * n2, 0:c] - big[n2:2 * n2, c:2 * c]


def _filter_spectrum(hfilt, tb):
    L, c2 = hfilt.shape
    n1h = tb["n1h"]
    a5 = _fft_stage1(hfilt.reshape(1, L, c2), tb["f1"])
    return pl.pallas_call(
        _filter_spectrum_kernel,
        grid=(n1h // FFT_T1,),
        in_specs=[
            pl.BlockSpec((pl.Squeezed(), 2, FFT_T1, FFT_N2, c2), lambda k: (0, 0, k, 0, 0)),
            pl.BlockSpec((FFT_T1, FFT_N2, 1), lambda k: (k, 0, 0)),
            pl.BlockSpec((FFT_T1, FFT_N2, 1), lambda k: (k, 0, 0)),
            pl.BlockSpec((2 * FFT_N2, 2 * FFT_N2), lambda k: (0, 0)),
        ],
        out_specs=pl.BlockSpec((FFT_T1, 2, FFT_N2, D_HYENA), lambda k: (k, 0, 0, 0)),
        out_shape=jax.ShapeDtypeStruct((n1h, 2, FFT_N2, D_HYENA), F32),
        compiler_params=_cparams(("parallel",)),
        name="filter_spectrum",
    )(a5, tb["twr"], tb["twi"], tb["f2"])


def _fft_mid_kernel(a_ref, kf_ref, twr_ref, twi_ref, f2_ref, f2i_ref, o_ref):
    n2 = FFT_N2
    f2, f2i = f2_ref[...], f2i_ref[...]
    bigs = [jnp.dot(f2, x, preferred_element_type=F32)
            for x in _stage2_inputs(a_ref, twr_ref, twi_ref)]
    ys = []
    for j, big in enumerate(bigs):
        sr, si = big[0:n2], big[n2:2 * n2]
        kr, ki = kf_ref[j, 0], kf_ref[j, 1]
        ys.append(jnp.concatenate([sr * kr - si * ki, sr * ki + si * kr], axis=0).astype(BF16))
    backs = [jnp.dot(f2i, y, preferred_element_type=F32) for y in ys]
    for j, back in enumerate(backs):
        br, bi = back[0:n2], back[n2:2 * n2]
        tr, ti = twr_ref[j], twi_ref[j]
        o_ref[0, j] = (br * tr + bi * ti).astype(o_ref.dtype)
        o_ref[1, j] = (bi * tr - br * ti).astype(o_ref.dtype)


def _fft_mid(a5, kf, tb):
    bsz, _, n1h, n2, c = a5.shape
    slot = pl.BlockSpec((pl.Squeezed(), 2, FFT_T1, n2, c), lambda k, b: (b, 0, k, 0, 0))
    return pl.pallas_call(
        _fft_mid_kernel,
        grid=(n1h // FFT_T1, bsz),
        in_specs=[
            slot,
            pl.BlockSpec((FFT_T1, 2, n2, c), lambda k, b: (k, 0, 0, 0)),
            pl.BlockSpec((FFT_T1, n2, 1), lambda k, b: (k, 0, 0)),
            pl.BlockSpec((FFT_T1, n2, 1), lambda k, b: (k, 0, 0)),
            pl.BlockSpec((2 * n2, 2 * n2), lambda k, b: (0, 0)),
            pl.BlockSpec((2 * n2, 2 * n2), lambda k, b: (0, 0)),
        ],
        out_specs=slot,
        out_shape=jax.ShapeDtypeStruct(a5.shape, SLOT_DTYPE),
        compiler_params=_cparams(("parallel", "parallel")),
        name="fft_mid",
    )(a5, kf, tb["twr"], tb["twi"], tb["f2"], tb["f2inv"])


def _fft_out_kernel(f_ref, b_ref, o_ref):
    ri = pl.program_id(2)

    @pl.when(ri == 0)
    def _():
        def emit(a0, val):
            o_ref[:, pl.ds(a0, FFT_A), :] = val

        _outer_dft(f_ref[0], b_ref, emit)

    @pl.when(ri == 1)
    def _():
        def emit(a0, val):
            o_ref[:, pl.ds(a0, FFT_A), :] += val

        _outer_dft(f_ref[0], b_ref, emit)


def _fft_out(b5, f1inv):
    bsz, _, n1h, n2, c = b5.shape
    return pl.pallas_call(
        _fft_out_kernel,
        grid=(bsz, c // FFT_LANES, 2),
        in_specs=[
            pl.BlockSpec((1, n1h, n1h), lambda b, j, ri: (ri, 0, 0)),
            pl.BlockSpec((pl.Squeezed(), pl.Squeezed(), n1h, n2, FFT_LANES),
                         lambda b, j, ri: (b, ri, 0, 0, j)),
        ],
        out_specs=pl.BlockSpec((pl.Squeezed(), n1h, n2, FFT_LANES), lambda b, j, ri: (b, 0, 0, j)),
        out_shape=jax.ShapeDtypeStruct((bsz, n1h, n2, c), F32),
        compiler_params=_cparams(("parallel", "parallel", "arbitrary")),
        name="fft_out",
    )(f1inv, b5).reshape(bsz, n1h * n2, c)


def _hyena_conv(u, kf, tb):
    a5 = _fft_stage1(u, tb["f1"])
    b5 = _fft_mid(a5, kf, tb)
    return _fft_out(b5, tb["f1inv"])


def _merge_kernel(ya_ref, yc_ref, u_ref, x0_ref, dk_ref, ga_ref, gh_ref, x_ref, wa_ref, wh_ref,
                  wo_ref, o_ref):
    yh = x0_ref[...] * (yc_ref[...] + u_ref[...] * dk_ref[...])
    pa = jnp.dot(ya_ref[...], wa_ref[...], preferred_element_type=F32)
    ph = jnp.dot(yh.astype(BF16), wh_ref[...], preferred_element_type=F32)
    merged = (jax.nn.sigmoid(ga_ref[...].astype(F32)) * pa
              + jax.nn.sigmoid(gh_ref[...].astype(F32)) * ph)
    o_ref[...] = x_ref[...] + jnp.dot(merged.astype(BF16), wo_ref[...],
                                      preferred_element_type=F32)


def _merge(ya, yc, u, x0, d_skip, z, x2, wa, wh, wo):
    n, d = x2.shape
    tm = min(512, n)
    ga_blk = (D_ATTN * 3 + D_HYENA * 3) // d
    tok = pl.BlockSpec((tm, D_HYENA), lambda i: (i, 0))
    return pl.pallas_call(
        _merge_kernel,
        grid=(n // tm,),
        in_specs=[
            pl.BlockSpec((tm, D_ATTN), lambda i: (i, 0)),
            tok, tok, tok,
            pl.BlockSpec((1, D_HYENA), lambda i: (0, 0)),
            pl.BlockSpec((tm, d), lambda i: (i, ga_blk)),
            pl.BlockSpec((tm, d), lambda i: (i, ga_blk + 1)),
            pl.BlockSpec((tm, d), lambda i: (i, 0)),
            pl.BlockSpec(wa.shape, lambda i: (0, 0)),
            pl.BlockSpec(wh.shape, lambda i: (0, 0)),
            pl.BlockSpec(wo.shape, lambda i: (0, 0)),
        ],
        out_specs=pl.BlockSpec((tm, d), lambda i: (i, 0)),
        out_shape=jax.ShapeDtypeStruct((n, d), F32),
        compiler_params=_cparams(("parallel",)),
        name="merge_outproj",
    )(ya, yc, u, x0, d_skip.astype(F32).reshape(1, D_HYENA), z, z, x2, wa, wh, wo)


def _rms(x, g):
    inv = lax.rsqrt(jnp.mean(x * x, axis=-1, keepdims=True) + EPS)
    return (x * inv) * g


def _ffn_kernel(x_ref, g_ref, wg_ref, wu_ref, wd_ref, gf_ref, o_ref, h_ref, acc_ref):
    j = pl.program_id(1)

    @pl.when(j == 0)
    def _():
        h_ref[...] = _rms(x_ref[...], g_ref[...]).astype(BF16)
        acc_ref[...] = jnp.zeros_like(acc_ref)

    h = h_ref[...]
    gate = jnp.dot(h, wg_ref[...], preferred_element_type=F32)
    up = jnp.dot(h, wu_ref[...], preferred_element_type=F32)
    act = (gate * jax.nn.sigmoid(gate) * up).astype(BF16)
    acc_ref[...] += jnp.dot(act, wd_ref[...], preferred_element_type=F32)

    @pl.when(j == pl.num_programs(1) - 1)
    def _():
        o_ref[...] = _rms(x_ref[...] + acc_ref[...], gf_ref[...])


def _ffn(x1, g, wg, wu, wd, gf):
    n, d = x1.shape
    dff = wg.shape[1]
    tm = min(512, n)
    tf = dff // 2
    return pl.pallas_call(
        _ffn_kernel,
        grid=(n // tm, dff // tf),
        in_specs=[
            pl.BlockSpec((tm, d), lambda i, j: (i, 0)),
            pl.BlockSpec((1, d), lambda i, j: (0, 0)),
            pl.BlockSpec((d, tf), lambda i, j: (0, j)),
            pl.BlockSpec((d, tf), lambda i, j: (0, j)),
            pl.BlockSpec((tf, d), lambda i, j: (j, 0)),
            pl.BlockSpec((1, d), lambda i, j: (0, 0)),
        ],
        out_specs=pl.BlockSpec((tm, d), lambda i, j: (i, 0)),
        out_shape=jax.ShapeDtypeStruct((n, d), F32),
        scratch_shapes=[pltpu.VMEM((tm, d), BF16), pltpu.VMEM((tm, d), F32)],
        compiler_params=_cparams(("parallel", "arbitrary")),
        name="ffn_final",
    )(x1, g.reshape(1, d), wg, wu, wd, gf.reshape(1, d))


def _layer(x2, bsz, L, p):
    n, d = x2.shape
    z = _norm_inproj(x2, p["norm_mix"], p["w_in"])
    z3 = z.reshape(bsz, L, z.shape[1])
    ya = _attention(z3, p["rpb"])
    u, x0 = _conv_gate(z3, p["conv_w"], p["conv_b"])
    tb = dict(_fft_tables(L))
    for name in ("f1", "f1inv", "f2", "f2inv"):
        tb[name] = jnp.asarray(tb[name]).astype(BF16)
    hfilt = _implicit_filters(L, *p["filt"])
    kf = _filter_spectrum(hfilt, tb)
    yc = _hyena_conv(u, kf, tb)
    x1 = _merge(ya.reshape(n, D_ATTN), yc.reshape(n, D_HYENA), u.reshape(n, D_HYENA),
                x0.reshape(n, D_HYENA), p["hyena_d"], z, x2,
                p["w_br_attn"], p["w_br_hyena"], p["w_out"])
    return x1


def _trunk(x, layers, norm_final):
    bsz, L, d = x.shape
    x2 = x.reshape(bsz * L, d)
    depth = len(layers)
    for li, p in enumerate(layers):
        x1 = _layer(x2, bsz, L, p)
        gf = norm_final if li == depth - 1 else None
        assert gf is not None, "only the last layer fuses the final norm"
        x2 = _ffn(x1, p["norm_ffn"], p["w_gate"], p["w_up"], p["w_down"], gf)
    return x2.reshape(bsz, L, d)


def kernel(x_prompt, x_sample, norm_mix, w_in, rpb, conv_w, conv_b, filt_w1, filt_b1, filt_w2,
           filt_b2, filt_w3, filt_b3, filt_w4, filt_freq, hyena_d, w_br_attn, w_br_hyena, w_out,
           norm_ffn, w_gate, w_up, w_down, norm_final):
    depth = w_in.shape[0]
    assert depth == 1
    layers = []
    for l in range(depth):
        layers.append(dict(
            norm_mix=norm_mix[l], w_in=w_in[l].astype(BF16), rpb=rpb[l].reshape(-1),
            conv_w=conv_w[l], conv_b=conv_b[l],
            filt=(filt_w1[l], filt_b1[l], filt_w2[l], filt_b2[l], filt_w3[l], filt_b3[l],
                  filt_w4[l], filt_freq[l]),
            hyena_d=hyena_d[l],
            w_br_attn=w_br_attn[l].astype(BF16), w_br_hyena=w_br_hyena[l].astype(BF16),
            w_out=w_out[l].astype(BF16), norm_ffn=norm_ffn[l],
            w_gate=w_gate[l].astype(BF16), w_up=w_up[l].astype(BF16),
            w_down=w_down[l].astype(BF16),
        ))
    y_prompt = _trunk(x_prompt, layers, norm_final)
    y_sample = _trunk(x_sample, layers, norm_final)
    return (y_prompt, y_sample)
```

```python
import functools
import math

import numpy as np
import jax
import jax.numpy as jnp
from jax import lax
from jax.experimental import pallas as pl
from jax.experimental.pallas import tpu as pltpu

F32 = jnp.float32
BF16 = jnp.bfloat16

GRID_W = 64
N_HEADS = 8
HEAD_DIM = 64
D_ATTN = N_HEADS * HEAD_DIM
WIN_R = 8
WIN_C = 16
D_HYENA = 512
SHORT_K = 3
FILTER_EMB = 33
FILTER_HIDDEN = 64
DECAY_TARGET = 1e-2
FAST_DECAY_PCT = 0.3
SLOW_DECAY_PCT = 1.5
EPS = 1e-6

FFT_N2 = 128
ATT_ROWS = 8
NEG_BIG = -1e30
VMEM_LIMIT = 56 * 1024 * 1024


def _cparams(sem):
    return pltpu.CompilerParams(dimension_semantics=sem, vmem_limit_bytes=VMEM_LIMIT)


def _norm_inproj_kernel(x_ref, g_ref, w_ref, o_ref, h_ref):
    @pl.when(pl.program_id(1) == 0)
    def _():
        x = x_ref[...]
        inv = lax.rsqrt(jnp.mean(x * x, axis=-1, keepdims=True) + EPS)
        h_ref[...] = ((x * inv) * g_ref[...]).astype(BF16)

    o_ref[...] = jnp.dot(h_ref[...], w_ref[...], preferred_element_type=F32).astype(o_ref.dtype)


def _norm_inproj(x2, g, w_bf16):
    n, d = x2.shape
    d_in = w_bf16.shape[1]
    tm = min(1024, n)
    tn = 1280
    return pl.pallas_call(
        _norm_inproj_kernel,
        grid=(n // tm, d_in // tn),
        in_specs=[
            pl.BlockSpec((tm, d), lambda i, j: (i, 0)),
            pl.BlockSpec((1, d), lambda i, j: (0, 0)),
            pl.BlockSpec((d, tn), lambda i, j: (0, j)),
        ],
        out_specs=pl.BlockSpec((tm, tn), lambda i, j: (i, j)),
        out_shape=jax.ShapeDtypeStruct((n, d_in), BF16),
        scratch_shapes=[pltpu.VMEM((tm, d), BF16)],
        compiler_params=_cparams(("parallel", "arbitrary")),
        name="norm_inproj",
    )(x2, g.reshape(1, d), w_bf16)


N_DR = 2 * WIN_R - 1
N_DC = 2 * WIN_C - 1
KV_ROWS = 2 * ATT_ROWS


def _build_bias_tiles(rpb_ref, pt_ref):
    k = lax.broadcasted_iota(jnp.int32, (GRID_W, 128), 0)
    lane = lax.broadcasted_iota(jnp.int32, (GRID_W, 128), 1)
    q = lane & (GRID_W - 1)
    first_half = lane < GRID_W
    cs = jnp.clip(q - WIN_C // 2, 0, GRID_W - WIN_C)
    valid = (k >= cs) & (k < cs + WIN_C)
    dc = jnp.clip(k - q, -(WIN_C - 1), WIN_C - 1) + (WIN_C - 1)

    def body(idx, carry):
        p = idx // N_DR
        d = idx % N_DR
        base0 = ((2 * p) * N_DR + d) * N_DC
        base1 = ((2 * p + 1) * N_DR + d) * N_DC
        acc = jnp.zeros((GRID_W, 128), F32)
        for off in range(N_DC):
            val = jnp.where(first_half, rpb_ref[base0 + off], rpb_ref[base1 + off])
            acc = jnp.where(dc == off, val, acc)
        pt_ref[idx] = jnp.where(valid, acc, NEG_BIG)
        return carry

    lax.fori_loop(0, (N_HEADS // 2) * N_DR, body, 0)


def _kv_window_start(i, rows):
    return jnp.clip(i * ATT_ROWS - WIN_R // 2, 0, rows - KV_ROWS)


def _attn_kernel(rpb_ref, q_ref, k_ref, v_ref, o_ref, pt_ref, *, rows):
    b = pl.program_id(0)
    i = pl.program_id(1)

    @pl.when((b == 0) & (i == 0))
    def _():
        _build_bias_tiles(rpb_ref, pt_ref)

    lane = lax.broadcasted_iota(jnp.int32, (GRID_W, 128), 1)
    first_half = lane < HEAD_DIM
    eye = (lax.broadcasted_iota(jnp.int32, (128, 128), 0)
           == lax.broadcasted_iota(jnp.int32, (128, 128), 1)).astype(BF16)
    ones = jnp.ones((WIN_R * GRID_W, 128), BF16)
    nkeys = WIN_R * GRID_W
    wstart = _kv_window_start(i, rows)
    nt_dims = (((1,), (1,)), ((), ()))

    def row_body(rr, carry):
        r = i * ATT_ROWS + rr
        rs = jnp.clip(r - WIN_R // 2, 0, rows - WIN_R)
        s = rs - r + (WIN_R - 1)
        koff = pl.multiple_of((rs - wstart) * GRID_W, GRID_W)
        qoff = pl.multiple_of(rr * GRID_W, GRID_W)
        q = q_ref[pl.ds(qoff, GRID_W), :] * (HEAD_DIM ** -0.5)
        pairs = range(N_HEADS // 2)
        sts = []
        for p in pairs:
            qp = q[:, 128 * p:128 * (p + 1)]
            zero = jnp.zeros_like(qp)
            wt = jnp.concatenate([jnp.where(first_half, qp, zero),
                                  jnp.where(first_half, zero, qp)], axis=0)
            kp = k_ref[pl.ds(koff, nkeys), 128 * p:128 * (p + 1)]
            sts.append(lax.dot_general(kp, wt, nt_dims, preferred_element_type=F32))
        pms = []
        for p in pairs:
            st = sts[p]
            ch = [st[GRID_W * c:GRID_W * (c + 1)] + pt_ref[p * N_DR + s + c] for c in range(WIN_R)]
            m = ch[0]
            for c in range(1, WIN_R):
                m = jnp.maximum(m, ch[c])
            m = jnp.max(m, axis=0, keepdims=True)
            pt = jnp.concatenate([jnp.exp(c - m) for c in ch], axis=0).astype(BF16)
            pms.append(lax.dot_general(eye, pt, nt_dims, preferred_element_type=F32).astype(BF16))
        outs = []
        for p in pairs:
            vp = v_ref[pl.ds(koff, nkeys), 128 * p:128 * (p + 1)]
            ov = jnp.dot(pms[p], jnp.concatenate([vp, ones], axis=1), preferred_element_type=F32)
            o = ov[:, 0:128] / ov[:, 128:256]
            outs.append(jnp.where(first_half, o[0:GRID_W], o[GRID_W:2 * GRID_W]))
        o_ref[pl.ds(qoff, GRID_W), :] = jnp.concatenate(outs, axis=1).astype(o_ref.dtype)
        return carry

    lax.fori_loop(0, ATT_ROWS, row_body, 0, unroll=4)


def _attention(z3, rpb_flat):
    bsz, L, _ = z3.shape
    rows = L // GRID_W
    assert rows % ATT_ROWS == 0 and rows >= KV_ROWS
    nblk = rows // ATT_ROWS
    blk = ATT_ROWS * GRID_W

    def window(col):
        return pl.BlockSpec(
            (pl.Squeezed(), pl.Element(KV_ROWS * GRID_W), pl.Element(D_ATTN)),
            lambda b, i: (b, _kv_window_start(i, rows) * GRID_W, col * D_ATTN))

    return pl.pallas_call(
        functools.partial(_attn_kernel, rows=rows),
        grid=(bsz, nblk),
        in_specs=[
            pl.BlockSpec(memory_space=pltpu.SMEM),
            pl.BlockSpec((pl.Squeezed(), blk, D_ATTN), lambda b, i: (b, i, 0)),
            window(1),
            window(2),
        ],
        out_specs=pl.BlockSpec((pl.Squeezed(), blk, D_ATTN), lambda b, i: (b, i, 0)),
        out_shape=jax.ShapeDtypeStruct((bsz, L, D_ATTN), BF16),
        scratch_shapes=[pltpu.VMEM(((N_HEADS // 2) * N_DR, GRID_W, 128), F32)],
        compiler_params=_cparams(("arbitrary", "arbitrary")),
        name="nbr_attention",
    )(rpb_flat, z3, z3, z3)


def _conv_gate_kernel(xm_ref, xp_ref, xn_ref, w_ref, b_ref, u_ref, x0_ref, *, nt):
    i = pl.program_id(1)
    x = xm_ref[0].astype(F32)
    tm = x.shape[0]
    row = lax.broadcasted_iota(jnp.int32, x.shape, 0)
    halo = xp_ref.shape[1]
    prev_row = jnp.where(i == 0, 0.0, xp_ref[0].astype(F32)[halo - 1:halo, :])
    next_row = jnp.where(i == nt - 1, 0.0, xn_ref[0].astype(F32)[0:1, :])
    xm1 = jnp.where(row == 0, prev_row, pltpu.roll(x, 1, 0))
    xp1 = jnp.where(row == tm - 1, next_row, pltpu.roll(x, tm - 1, 0))
    y = xm1 * w_ref[0:1, :] + x * w_ref[1:2, :] + xp1 * w_ref[2:3, :] + b_ref[...]
    c = D_HYENA
    x0_ref[0] = y[:, 0:c].astype(x0_ref.dtype)
    u_ref[0] = (y[:, c:2 * c] * y[:, 2 * c:3 * c]).astype(u_ref.dtype)


def _conv_gate(z3, conv_w, conv_b):
    bsz, L, _ = z3.shape
    c3 = 3 * D_HYENA
    tm = min(512, L)
    nt = L // tm
    halo = 16
    hb = tm // halo
    nh = L // halo
    return pl.pallas_call(
        functools.partial(_conv_gate_kernel, nt=nt),
        grid=(bsz, nt),
        in_specs=[
            pl.BlockSpec((1, tm, c3), lambda b, i: (b, i, 1)),
            pl.BlockSpec((1, halo, c3), lambda b, i: (b, jnp.maximum(i * hb - 1, 0), 1)),
            pl.BlockSpec((1, halo, c3), lambda b, i: (b, jnp.minimum((i + 1) * hb, nh - 1), 1)),
            pl.BlockSpec((SHORT_K, c3), lambda b, i: (0, 0)),
            pl.BlockSpec((1, c3), lambda b, i: (0, 0)),
        ],
        out_specs=[
            pl.BlockSpec((1, tm, D_HYENA), lambda b, i: (b, i, 0)),
            pl.BlockSpec((1, tm, D_HYENA), lambda b, i: (b, i, 0)),
        ],
        out_shape=[
            jax.ShapeDtypeStruct((bsz, L, D_HYENA), BF16),
            jax.ShapeDtypeStruct((bsz, L, D_HYENA), BF16),
        ],
        compiler_params=_cparams(("parallel", "parallel")),
        name="conv_gate",
    )(z3, z3, z3, conv_w, conv_b.reshape(1, c3))


TWO_OVER_PI = 0.6366197723675814
PIO2_1 = 1.5703125
PIO2_2 = 4.837512969970703125e-4
PIO2_3 = 7.54978995489188216e-8
TRIG_FAST_LIMIT = 4096.0


def _quadrant_value(x, shift):
    kf = jnp.floor(x * TWO_OVER_PI + 0.5)
    r = ((x - kf * PIO2_1) - kf * PIO2_2) - kf * PIO2_3
    z = r * r
    s = r + r * z * (-1.6666654611e-1 + z * (8.3321608736e-3 + z * -1.9515295891e-4))
    c = (1.0 - 0.5 * z) + z * z * (4.166664568298827e-2
                                   + z * (-1.388731625493765e-3 + z * 2.443315711809948e-5))
    k = kf + shift
    half = jnp.floor(k * 0.5)
    odd = k - 2.0 * half
    flip = half - 2.0 * jnp.floor(half * 0.5)
    return (s + odd * (c - s)) * (1.0 - 2.0 * flip)


def _sin_small(x):
    return _quadrant_value(x, 0.0)


def _cos_small(x):
    return _quadrant_value(x, 1.0)


def _sin(x):
    return lax.cond(jnp.max(jnp.abs(x)) < TRIG_FAST_LIMIT, _sin_small, jnp.sin, x)


def _cos(x):
    return lax.cond(jnp.max(jnp.abs(x)) < TRIG_FAST_LIMIT, _cos_small, jnp.cos, x)


def _dot_split(a, b):
    a_hi = a.astype(BF16)
    b_hi = b.astype(BF16)
    a_lo = (a - a_hi.astype(F32)).astype(BF16)
    b_lo = (b - b_hi.astype(F32)).astype(BF16)
    return (jnp.dot(a_hi, b_hi, preferred_element_type=F32)
            + (jnp.dot(a_hi, b_lo, preferred_element_type=F32)
               + jnp.dot(a_lo, b_hi, preferred_element_type=F32)))


def _filter_kernel(fb_ref, w1t_ref, w1c_ref, w1s_ref, b1_ref, w2_ref, b2_ref, w3_ref, b3_ref,
                   w4_ref, fr_ref, dl_ref, o_ref, *, L, tp):
    j = pl.program_id(0)
    pos = (lax.broadcasted_iota(jnp.int32, (1, tp), 1) + j * tp).astype(F32)
    t = pos / (L - 1.0)
    omega = (2.0 * math.pi) * pos / float(L)
    ang = fb_ref[...] * omega
    fr = fr_ref[...]
    pre = (w1t_ref[...] * t
           + _dot_split(w1c_ref[...], _cos(ang)) - _dot_split(w1s_ref[...], _sin(ang)))
    h = _sin(fr * (pre + b1_ref[...]))
    h = _sin(fr * (_dot_split(w2_ref[...], h) + b2_ref[...]))
    h = _sin(fr * (_dot_split(w3_ref[...], h) + b3_ref[...]))
    out = _dot_split(h.T, w4_ref[...])
    tcol = (lax.broadcasted_iota(jnp.int32, (tp, 1), 0) + j * tp).astype(F32) / (L - 1.0)
    decay = jnp.exp(-tcol * dl_ref[...])
    c = D_HYENA
    o_ref[:, 0:c] = out[:, 0:c] * decay
    o_ref[:, c:2 * c] = out[:, c:2 * c] * decay


def _implicit_filters(L, w1, b1, w2, b2, w3, b3, w4, freq):
    bands = (FILTER_EMB - 1) // 2
    fh = FILTER_HIDDEN
    tp = min(512, L)
    fb = jnp.linspace(1e-4, bands - 1, bands, dtype=F32).reshape(bands, 1)
    max_decay = math.log(DECAY_TARGET) / FAST_DECAY_PCT
    min_decay = math.log(DECAY_TARGET) / SLOW_DECAY_PCT
    deltas = jnp.abs(jnp.linspace(min_decay, max_decay, D_HYENA, dtype=F32)).reshape(1, D_HYENA)
    w1 = w1.astype(F32)
    args = (
        fb,
        w1[0:1, :].T,
        w1[1:1 + bands, :].T,
        w1[1 + bands:, :].T,
        b1.astype(F32).reshape(fh, 1),
        w2.astype(F32).T, b2.astype(F32).reshape(fh, 1),
        w3.astype(F32).T, b3.astype(F32).reshape(fh, 1),
        w4.astype(F32),
        freq.astype(F32).reshape(fh, 1),
        deltas,
    )

    def full(a):
        return pl.BlockSpec(a.shape, lambda j: (0,) * a.ndim)

    return pl.pallas_call(
        functools.partial(_filter_kernel, L=L, tp=tp),
        grid=(L // tp,),
        in_specs=[full(a) for a in args],
        out_specs=pl.BlockSpec((tp, 2 * D_HYENA), lambda j: (j, 0)),
        out_shape=jax.ShapeDtypeStruct((L, 2 * D_HYENA), F32),
        compiler_params=_cparams(("parallel",)),
        name="implicit_filter",
    )(*args)


FFT_LANES = 128
FFT_T1 = 4


@functools.lru_cache(maxsize=None)
def _fft_tables(L):
    n = 2 * L
    n2 = FFT_N2
    n1 = n // n2
    n1h = n1 // 2
    odd = 2 * np.arange(n1h) + 1
    th = 2.0 * np.pi * (np.outer(odd, np.arange(n1h)) % (2 * n1)) / (2 * n1)
    f1 = np.concatenate([np.cos(th), -np.sin(th)], axis=0)
    tw = 2.0 * np.pi * (np.outer(odd, np.arange(n2)) % (2 * n)) / (2 * n)
    twr = np.cos(tw)[:, :, None]
    twi = -np.sin(tw)[:, :, None]
    t2 = 2.0 * np.pi * (np.outer(np.arange(n2), np.arange(n2)) % n2) / n2
    fr, fi = np.cos(t2), -np.sin(t2)
    f2 = np.block([[fr, -fi], [fi, fr]])
    f2inv = np.block([[fr, fi], [-fi, fr]])
    f1inv = np.concatenate([np.cos(th.T), -np.sin(th.T)], axis=1) * (2.0 / n)
    return dict(
        n1=n1, n1h=n1h,
        f1=np.asarray(f1, np.float32), f1inv=np.asarray(f1inv, np.float32),
        f2=np.asarray(f2, np.float32), f2inv=np.asarray(f2inv, np.float32),
        twr=np.asarray(twr, np.float32), twi=np.asarray(twi, np.float32),
    )


FFT_A = 16
SLOT_DTYPE = BF16


def _outer_dft(f, src_ref, dst_ref):
    n_in, n1h = src_ref.shape[0], src_ref.shape[1]
    n_out = dst_ref.shape[0]

    def body(i, carry):
        a0 = pl.multiple_of(i * FFT_A, FFT_A)
        ts = [jnp.swapaxes(src_ref[p, :, pl.ds(a0, FFT_A), :].astype(F32), 0, 1)
              for p in range(n_in)]
        outs = [[] for _ in range(n_out)]
        for j in range(0, FFT_A, 2):
            rhs = jnp.concatenate([jnp.concatenate([t[j], t[j + 1]], axis=1) for t in ts],
                                  axis=0).astype(BF16)
            res = jnp.dot(f, rhs, preferred_element_type=F32)
            for q in range(n_out):
                blk = res[q * n1h:(q + 1) * n1h]
                outs[q] += [blk[:, 0:FFT_LANES], blk[:, FFT_LANES:2 * FFT_LANES]]
        for q in range(n_out):
            dst_ref[q, :, pl.ds(a0, FFT_A), :] = jnp.swapaxes(
                jnp.stack(outs[q], axis=0), 0, 1).astype(dst_ref.dtype)
        return carry

    lax.fori_loop(0, FFT_N2 // FFT_A, body, 0, unroll=2)


def _fft_stage1_kernel(f_ref, u_ref, o_ref):
    _outer_dft(f_ref[...], u_ref, o_ref)


def _fft_stage1(u3, f1):
    bsz, L, c = u3.shape
    n1h = L // FFT_N2
    return pl.pallas_call(
        _fft_stage1_kernel,
        grid=(bsz, c // FFT_LANES),
        in_specs=[
            pl.BlockSpec(f1.shape, lambda b, j: (0, 0)),
            pl.BlockSpec((1, n1h, FFT_N2, FFT_LANES), lambda b, j: (b, 0, 0, j)),
        ],
        out_specs=pl.BlockSpec((pl.Squeezed(), 2, n1h, FFT_N2, FFT_LANES),
                               lambda b, j: (b, 0, 0, 0, j)),
        out_shape=jax.ShapeDtypeStruct((bsz, 2, n1h, FFT_N2, c), SLOT_DTYPE),
        compiler_params=_cparams(("parallel", "parallel")),
        name="fft_stage1",
    )(f1, u3.reshape(bsz, n1h, FFT_N2, c))


def _twiddle(ar, ai, tr, ti):
    return ar * tr - ai * ti, ar * ti + ai * tr


def _stage2_inputs(a_ref, twr_ref, twi_ref):
    xs = []
    for j in range(FFT_T1):
        xr, xi = _twiddle(a_ref[0, j].astype(F32), a_ref[1, j].astype(F32),
                          twr_ref[j], twi_ref[j])
        xs.append(jnp.concatenate([xr, xi], axis=0).astype(BF16))
    return xs


def _filter_spectrum_kernel(a_ref, twr_ref, twi_ref, f2_ref, o_ref):
    n2, c = FFT_N2, D_HYENA
    f2 = f2_ref[...]
    bigs = [jnp.dot(f2, x, preferred_element_type=F32)
            for x in _stage2_inputs(a_ref, twr_ref, twi_ref)]
    for j, big in enumerate(bigs):
        o_ref[j, 0] = big[0:n2, 0:c] + big[0:n2, c:2 * c]
        o_ref[j, 1] = big[n2:2 * n2, 0:c] - big[n2:2 * n2, c:2 * c]


def _filter_spectrum(hfilt, tb):
    L, c2 = hfilt.shape
    n1h = tb["n1h"]
    a5 = _fft_stage1(hfilt.reshape(1, L, c2), tb["f1"])
    return pl.pallas_call(
        _filter_spectrum_kernel,
        grid=(n1h // FFT_T1,),
        in_specs=[
            pl.BlockSpec((pl.Squeezed(), 2, FFT_T1, FFT_N2, c2), lambda k: (0, 0, k, 0, 0)),
            pl.BlockSpec((FFT_T1, FFT_N2, 1), lambda k: (k, 0, 0)),
            pl.BlockSpec((FFT_T1, FFT_N2, 1), lambda k: (k, 0, 0)),
            pl.BlockSpec((2 * FFT_N2, 2 * FFT_N2), lambda k: (0, 0)),
        ],
        out_specs=pl.BlockSpec((FFT_T1, 2, FFT_N2, D_HYENA), lambda k: (k, 0, 0, 0)),
        out_shape=jax.ShapeDtypeStruct((n1h, 2, FFT_N2, D_HYENA), F32),
        compiler_params=_cparams(("parallel",)),
        name="filter_spectrum",
    )(a5, tb["twr"], tb["twi"], tb["f2"])


def _fft_mid_kernel(a_ref, kf_ref, twr_ref, twi_ref, f2_ref, f2i_ref, o_ref):
    n2 = FFT_N2
    f2, f2i = f2_ref[...], f2i_ref[...]
    bigs = [jnp.dot(f2, x, preferred_element_type=F32)
            for x in _stage2_inputs(a_ref, twr_ref, twi_ref)]
    ys = []
    for j, big in enumerate(bigs):
        sr, si = big[0:n2], big[n2:2 * n2]
        kr, ki = kf_ref[j, 0], kf_ref[j, 1]
        ys.append(jnp.concatenate([sr * kr - si * ki, sr * ki + si * kr], axis=0).astype(BF16))
    backs = [jnp.dot(f2i, y, preferred_element_type=F32) for y in ys]
    for j, back in enumerate(backs):
        br, bi = back[0:n2], back[n2:2 * n2]
        tr, ti = twr_ref[j], twi_ref[j]
        o_ref[0, j] = (br * tr + bi * ti).astype(o_ref.dtype)
        o_ref[1, j] = (bi * tr - br * ti).astype(o_ref.dtype)


def _fft_mid(a5, kf, tb):
    bsz, _, n1h, n2, c = a5.shape
    slot = pl.BlockSpec((pl.Squeezed(), 2, FFT_T1, n2, c), lambda k, b: (b, 0, k, 0, 0))
    return pl.pallas_call(
        _fft_mid_kernel,
        grid=(n1h // FFT_T1, bsz),
        in_specs=[
            slot,
            pl.BlockSpec((FFT_T1, 2, n2, c), lambda k, b: (k, 0, 0, 0)),
            pl.BlockSpec((FFT_T1, n2, 1), lambda k, b: (k, 0, 0)),
            pl.BlockSpec((FFT_T1, n2, 1), lambda k, b: (k, 0, 0)),
            pl.BlockSpec((2 * n2, 2 * n2), lambda k, b: (0, 0)),
            pl.BlockSpec((2 * n2, 2 * n2), lambda k, b: (0, 0)),
        ],
        out_specs=slot,
        out_shape=jax.ShapeDtypeStruct(a5.shape, SLOT_DTYPE),
        compiler_params=_cparams(("parallel", "parallel")),
        name="fft_mid",
    )(a5, kf, tb["twr"], tb["twi"], tb["f2"], tb["f2inv"])


def _fft_out_kernel(f_ref, b_ref, o_ref):
    _outer_dft(f_ref[...], b_ref, o_ref)


def _fft_out(b5, f1inv):
    bsz, _, n1h, n2, c = b5.shape
    return pl.pallas_call(
        _fft_out_kernel,
        grid=(bsz, c // FFT_LANES),
        in_specs=[
            pl.BlockSpec(f1inv.shape, lambda b, j: (0, 0)),
            pl.BlockSpec((pl.Squeezed(), 2, n1h, n2, FFT_LANES), lambda b, j: (b, 0, 0, 0, j)),
        ],
        out_specs=pl.BlockSpec((1, n1h, n2, FFT_LANES), lambda b, j: (b, 0, 0, j)),
        out_shape=jax.ShapeDtypeStruct((bsz, n1h, n2, c), BF16),
        compiler_params=_cparams(("parallel", "parallel")),
        name="fft_out",
    )(f1inv, b5).reshape(bsz, n1h * n2, c)


def _hyena_conv(u, kf, tb):
    a5 = _fft_stage1(u, tb["f1"])
    b5 = _fft_mid(a5, kf, tb)
    return _fft_out(b5, tb["f1inv"])


def _merge_kernel(ya_ref, yc_ref, u_ref, x0_ref, dk_ref, ga_ref, gh_ref, x_ref, wa_ref, wh_ref,
                  wo_ref, o_ref):
    yh = x0_ref[...].astype(F32) * (yc_ref[...].astype(F32)
                                    + u_ref[...].astype(F32) * dk_ref[...])
    pa = jnp.dot(ya_ref[...], wa_ref[...], preferred_element_type=F32)
    ph = jnp.dot(yh.astype(BF16), wh_ref[...], preferred_element_type=F32)
    merged = (jax.nn.sigmoid(ga_ref[...].astype(F32)) * pa
              + jax.nn.sigmoid(gh_ref[...].astype(F32)) * ph)
    o_ref[...] = x_ref[...] + jnp.dot(merged.astype(BF16), wo_ref[...],
                                      preferred_element_type=F32)


def _merge(ya, yc, u, x0, d_skip, z, x2, wa, wh, wo):
    n, d = x2.shape
    tm = min(512, n)
    ga_blk = (D_ATTN * 3 + D_HYENA * 3) // d
    tok = pl.BlockSpec((tm, D_HYENA), lambda i: (i, 0))
    return pl.pallas_call(
        _merge_kernel,
        grid=(n // tm,),
        in_specs=[
            pl.BlockSpec((tm, D_ATTN), lambda i: (i, 0)),
            tok, tok, tok,
            pl.BlockSpec((1, D_HYENA), lambda i: (0, 0)),
            pl.BlockSpec((tm, d), lambda i: (i, ga_blk)),
            pl.BlockSpec((tm, d), lambda i: (i, ga_blk + 1)),
            pl.BlockSpec((tm, d), lambda i: (i, 0)),
            pl.BlockSpec(wa.shape, lambda i: (0, 0)),
            pl.BlockSpec(wh.shape, lambda i: (0, 0)),
            pl.BlockSpec(wo.shape, lambda i: (0, 0)),
        ],
        out_specs=pl.BlockSpec((tm, d), lambda i: (i, 0)),
        out_shape=jax.ShapeDtypeStruct((n, d), F32),
        compiler_params=_cparams(("parallel",)),
        name="merge_outproj",
    )(ya, yc, u, x0, d_skip.astype(F32).reshape(1, D_HYENA), z, z, x2, wa, wh, wo)


def _rms(x, g):
    inv = lax.rsqrt(jnp.mean(x * x, axis=-1, keepdims=True) + EPS)
    return (x * inv) * g


def _ffn_kernel(x_ref, g_ref, wg_ref, wu_ref, wd_ref, gf_ref, o_ref, h_ref, acc_ref):
    j = pl.program_id(1)

    @pl.when(j == 0)
    def _():
        h_ref[...] = _rms(x_ref[...], g_ref[...]).astype(BF16)
        acc_ref[...] = jnp.zeros_like(acc_ref)

    h = h_ref[...]
    gate = jnp.dot(h, wg_ref[...], preferred_element_type=F32)
    up = jnp.dot(h, wu_ref[...], preferred_element_type=F32)
    act = (gate * jax.nn.sigmoid(gate) * up).astype(BF16)
    acc_ref[...] += jnp.dot(act, wd_ref[...], preferred_element_type=F32)

    @pl.when(j == pl.num_programs(1) - 1)
    def _():
        o_ref[...] = _rms(x_ref[...] + acc_ref[...], gf_ref[...])


def _ffn(x1, g, wg, wu, wd, gf):
    n, d = x1.shape
    dff = wg.shape[1]
    tm = min(512, n)
    tf = dff // 2
    return pl.pallas_call(
        _ffn_kernel,
        grid=(n // tm, dff // tf),
        in_specs=[
            pl.BlockSpec((tm, d), lambda i, j: (i, 0)),
            pl.BlockSpec((1, d), lambda i, j: (0, 0)),
            pl.BlockSpec((d, tf), lambda i, j: (0, j)),
            pl.BlockSpec((d, tf), lambda i, j: (0, j)),
            pl.BlockSpec((tf, d), lambda i, j: (j, 0)),
            pl.BlockSpec((1, d), lambda i, j: (0, 0)),
        ],
        out_specs=pl.BlockSpec((tm, d), lambda i, j: (i, 0)),
        out_shape=jax.ShapeDtypeStruct((n, d), F32),
        scratch_shapes=[pltpu.VMEM((tm, d), BF16), pltpu.VMEM((tm, d), F32)],
        compiler_params=_cparams(("parallel", "arbitrary")),
        name="ffn_final",
    )(x1, g.reshape(1, d), wg, wu, wd, gf.reshape(1, d))


def _layer(x2, bsz, L, p):
    n, d = x2.shape
    z = _norm_inproj(x2, p["norm_mix"], p["w_in"])
    z3 = z.reshape(bsz, L, z.shape[1])
    ya = _attention(z3, p["rpb"])
    u, x0 = _conv_gate(z3, p["conv_w"], p["conv_b"])
    tb = dict(_fft_tables(L))
    for name in ("f1", "f1inv", "f2", "f2inv"):
        tb[name] = jnp.asarray(tb[name]).astype(BF16)
    hfilt = _implicit_filters(L, *p["filt"])
    kf = _filter_spectrum(hfilt, tb)
    yc = _hyena_conv(u, kf, tb)
    x1 = _merge(ya.reshape(n, D_ATTN), yc.reshape(n, D_HYENA), u.reshape(n, D_HYENA),
                x0.reshape(n, D_HYENA), p["hyena_d"], z, x2,
                p["w_br_attn"], p["w_br_hyena"], p["w_out"])
    return x1


def _trunk(x, layers, norm_final):
    bsz, L, d = x.shape
    x2 = x.reshape(bsz * L, d)
    depth = len(layers)
    for li, p in enumerate(layers):
        x1 = _layer(x2, bsz, L, p)
        gf = norm_final if li == depth - 1 else None
        assert gf is not None, "only the last layer fuses the final norm"
        x2 = _ffn(x1, p["norm_ffn"], p["w_gate"], p["w_up"], p["w_down"], gf)
    return x2.reshape(bsz, L, d)


def kernel(x_prompt, x_sample, norm_mix, w_in, rpb, conv_w, conv_b, filt_w1, filt_b1, filt_w2,
           filt_b2, filt_w3, filt_b3, filt_w4, filt_freq, hyena_d, w_br_attn, w_br_hyena, w_out,
           norm_ffn, w_gate, w_up, w_down, norm_final):
    depth = w_in.shape[0]
    assert depth == 1
    layers = []
    for l in range(depth):
        layers.append(dict(
            norm_mix=norm_mix[l], w_in=w_in[l].astype(BF16), rpb=rpb[l].reshape(-1),
            conv_w=conv_w[l], conv_b=conv_b[l],
            filt=(filt_w1[l], filt_b1[l], filt_w2[l], filt_b2[l], filt_w3[l], filt_b3[l],
                  filt_w4[l], filt_freq[l]),
            hyena_d=hyena_d[l],
            w_br_attn=w_br_attn[l].astype(BF16), w_br_hyena=w_br_hyena[l].astype(BF16),
            w_out=w_out[l].astype(BF16), norm_ffn=norm_ffn[l],
            w_gate=w_gate[l].astype(BF16), w_up=w_up[l].astype(BF16),
            w_down=w_down[l].astype(BF16),
        ))
    y_prompt = _trunk(x_prompt, layers, norm_final)
    y_sample = _trunk(x_sample, layers, norm_final)
    return (y_prompt, y_sample)
```

```python
import functools
import math

import numpy as np
import jax
import jax.numpy as jnp
from jax import lax
from jax.experimental import pallas as pl
from jax.experimental.pallas import tpu as pltpu

F32 = jnp.float32
BF16 = jnp.bfloat16

GRID_W = 64
N_HEADS = 8
HEAD_DIM = 64
D_ATTN = N_HEADS * HEAD_DIM
WIN_R = 8
WIN_C = 16
D_HYENA = 512
SHORT_K = 3
FILTER_EMB = 33
FILTER_HIDDEN = 64
DECAY_TARGET = 1e-2
FAST_DECAY_PCT = 0.3
SLOW_DECAY_PCT = 1.5
EPS = 1e-6

FFT_N2 = 128
ATT_ROWS = 8
NEG_BIG = -1e30
VMEM_LIMIT = 56 * 1024 * 1024


def _cparams(sem):
    return pltpu.CompilerParams(dimension_semantics=sem, vmem_limit_bytes=VMEM_LIMIT)


def _norm_inproj_kernel(x_ref, g_ref, w_ref, o_ref, h_ref):
    @pl.when(pl.program_id(1) == 0)
    def _():
        x = x_ref[...]
        inv = lax.rsqrt(jnp.mean(x * x, axis=-1, keepdims=True) + EPS)
        h_ref[...] = ((x * inv) * g_ref[...]).astype(BF16)

    o_ref[...] = jnp.dot(h_ref[...], w_ref[...], preferred_element_type=F32).astype(o_ref.dtype)


def _norm_inproj(x2, g, w_bf16):
    n, d = x2.shape
    d_in = w_bf16.shape[1]
    tm = min(1024, n)
    tn = 2560
    return pl.pallas_call(
        _norm_inproj_kernel,
        grid=(n // tm, d_in // tn),
        in_specs=[
            pl.BlockSpec((tm, d), lambda i, j: (i, 0)),
            pl.BlockSpec((1, d), lambda i, j: (0, 0)),
            pl.BlockSpec((d, tn), lambda i, j: (0, j)),
        ],
        out_specs=pl.BlockSpec((tm, tn), lambda i, j: (i, j)),
        out_shape=jax.ShapeDtypeStruct((n, d_in), BF16),
        scratch_shapes=[pltpu.VMEM((tm, d), BF16)],
        compiler_params=_cparams(("parallel", "arbitrary")),
        name="norm_inproj",
    )(x2, g.reshape(1, d), w_bf16)


N_DR = 2 * WIN_R - 1
N_DC = 2 * WIN_C - 1
LOG2_E = 1.4426950408889634
Q_SCALE = (HEAD_DIM ** -0.5) * LOG2_E
KV_ROWS = 2 * ATT_ROWS


def _build_bias_tiles(rpb_ref, pt_ref):
    k = lax.broadcasted_iota(jnp.int32, (GRID_W, 128), 0)
    lane = lax.broadcasted_iota(jnp.int32, (GRID_W, 128), 1)
    q = lane & (GRID_W - 1)
    first_half = lane < GRID_W
    cs = jnp.clip(q - WIN_C // 2, 0, GRID_W - WIN_C)
    valid = (k >= cs) & (k < cs + WIN_C)
    dc = jnp.clip(k - q, -(WIN_C - 1), WIN_C - 1) + (WIN_C - 1)

    def body(idx, carry):
        p = idx // N_DR
        d = idx % N_DR
        base0 = ((2 * p) * N_DR + d) * N_DC
        base1 = ((2 * p + 1) * N_DR + d) * N_DC
        acc = jnp.zeros((GRID_W, 128), F32)
        for off in range(N_DC):
            val = jnp.where(first_half, rpb_ref[base0 + off], rpb_ref[base1 + off])
            acc = jnp.where(dc == off, val, acc)
        pt_ref[idx] = jnp.where(valid, acc * LOG2_E, NEG_BIG)
        return carry

    lax.fori_loop(0, (N_HEADS // 2) * N_DR, body, 0)


def _kv_window_start(i, rows):
    return jnp.clip(i * ATT_ROWS - WIN_R // 2, 0, rows - KV_ROWS)


def _attn_kernel(rpb_ref, q_ref, k_ref, v_ref, o_ref, pt_ref, *, rows):
    b = pl.program_id(0)
    i = pl.program_id(1)

    @pl.when((b == 0) & (i == 0))
    def _():
        _build_bias_tiles(rpb_ref, pt_ref)

    lane = lax.broadcasted_iota(jnp.int32, (GRID_W, 128), 1)
    first_half = lane < HEAD_DIM
    ones = jnp.ones((WIN_R * GRID_W, 128), BF16)
    nkeys = WIN_R * GRID_W
    wstart = _kv_window_start(i, rows)
    nt_dims = (((1,), (1,)), ((), ()))

    def row_body(rr, carry):
        r = i * ATT_ROWS + rr
        rs = jnp.clip(r - WIN_R // 2, 0, rows - WIN_R)
        s = rs - r + (WIN_R - 1)
        koff = pl.multiple_of((rs - wstart) * GRID_W, GRID_W)
        qoff = pl.multiple_of(rr * GRID_W, GRID_W)
        q = q_ref[pl.ds(qoff, GRID_W), :]
        pairs = range(N_HEADS // 2)
        sts = []
        for p in pairs:
            qp = q[:, 128 * p:128 * (p + 1)]
            zero = jnp.zeros_like(qp)
            wt = jnp.concatenate([jnp.where(first_half, qp, zero),
                                  jnp.where(first_half, zero, qp)], axis=0)
            kp = k_ref[pl.ds(koff, nkeys), 128 * p:128 * (p + 1)]
            sts.append(lax.dot_general(kp, wt, nt_dims, preferred_element_type=F32))
        pms = []
        for p in pairs:
            st = sts[p]
            ch = [st[GRID_W * c:GRID_W * (c + 1)] + pt_ref[p * N_DR + s + c] for c in range(WIN_R)]
            m = ch[0]
            for c in range(1, WIN_R):
                m = jnp.maximum(m, ch[c])
            m = jnp.max(m, axis=0, keepdims=True)
            pt = jnp.concatenate([jnp.exp2(c - m) for c in ch], axis=0).astype(BF16)
            pms.append(pt.T)
        outs = []
        for p in pairs:
            vp = v_ref[pl.ds(koff, nkeys), 128 * p:128 * (p + 1)]
            ov = jnp.dot(pms[p], jnp.concatenate([vp, ones], axis=1), preferred_element_type=F32)
            o = ov[:, 0:128] / ov[:, 128:256]
            outs.append(jnp.where(first_half, o[0:GRID_W], o[GRID_W:2 * GRID_W]))
        o_ref[pl.ds(qoff, GRID_W), :] = jnp.concatenate(outs, axis=1).astype(o_ref.dtype)
        return carry

    lax.fori_loop(0, ATT_ROWS, row_body, 0, unroll=4)


def _attention(z3, rpb_flat):
    bsz, L, _ = z3.shape
    rows = L // GRID_W
    assert rows % ATT_ROWS == 0 and rows >= KV_ROWS
    nblk = rows // ATT_ROWS
    blk = ATT_ROWS * GRID_W

    def window(col):
        return pl.BlockSpec(
            (pl.Squeezed(), pl.Element(KV_ROWS * GRID_W), pl.Element(D_ATTN)),
            lambda b, i: (b, _kv_window_start(i, rows) * GRID_W, col * D_ATTN))

    return pl.pallas_call(
        functools.partial(_attn_kernel, rows=rows),
        grid=(bsz, nblk),
        in_specs=[
            pl.BlockSpec(memory_space=pltpu.SMEM),
            pl.BlockSpec((pl.Squeezed(), blk, D_ATTN), lambda b, i: (b, i, 0)),
            window(1),
            window(2),
        ],
        out_specs=pl.BlockSpec((pl.Squeezed(), blk, D_ATTN), lambda b, i: (b, i, 0)),
        out_shape=jax.ShapeDtypeStruct((bsz, L, D_ATTN), BF16),
        scratch_shapes=[pltpu.VMEM(((N_HEADS // 2) * N_DR, GRID_W, 128), F32)],
        compiler_params=_cparams(("arbitrary", "arbitrary")),
        name="nbr_attention",
    )(rpb_flat, z3, z3, z3)


def _conv_gate_kernel(xm_ref, xp_ref, xn_ref, w_ref, b_ref, u_ref, x0_ref, *, nt):
    i = pl.program_id(1)
    x = xm_ref[0].astype(F32)
    tm = x.shape[0]
    row = lax.broadcasted_iota(jnp.int32, x.shape, 0)
    halo = xp_ref.shape[1]
    prev_row = jnp.where(i == 0, 0.0, xp_ref[0].astype(F32)[halo - 1:halo, :])
    next_row = jnp.where(i == nt - 1, 0.0, xn_ref[0].astype(F32)[0:1, :])
    xm1 = jnp.where(row == 0, prev_row, pltpu.roll(x, 1, 0))
    xp1 = jnp.where(row == tm - 1, next_row, pltpu.roll(x, tm - 1, 0))
    y = xm1 * w_ref[0:1, :] + x * w_ref[1:2, :] + xp1 * w_ref[2:3, :] + b_ref[...]
    c = D_HYENA
    x0_ref[0] = y[:, 0:c].astype(x0_ref.dtype)
    u_ref[0] = (y[:, c:2 * c] * y[:, 2 * c:3 * c]).astype(u_ref.dtype)


def _conv_gate(z3, conv_w, conv_b):
    bsz, L, _ = z3.shape
    c3 = 3 * D_HYENA
    tm = min(512, L)
    nt = L // tm
    halo = 16
    hb = tm // halo
    nh = L // halo
    return pl.pallas_call(
        functools.partial(_conv_gate_kernel, nt=nt),
        grid=(bsz, nt),
        in_specs=[
            pl.BlockSpec((1, tm, c3), lambda b, i: (b, i, 1)),
            pl.BlockSpec((1, halo, c3), lambda b, i: (b, jnp.maximum(i * hb - 1, 0), 1)),
            pl.BlockSpec((1, halo, c3), lambda b, i: (b, jnp.minimum((i + 1) * hb, nh - 1), 1)),
            pl.BlockSpec((SHORT_K, c3), lambda b, i: (0, 0)),
            pl.BlockSpec((1, c3), lambda b, i: (0, 0)),
        ],
        out_specs=[
            pl.BlockSpec((1, tm, D_HYENA), lambda b, i: (b, i, 0)),
            pl.BlockSpec((1, tm, D_HYENA), lambda b, i: (b, i, 0)),
        ],
        out_shape=[
            jax.ShapeDtypeStruct((bsz, L, D_HYENA), BF16),
            jax.ShapeDtypeStruct((bsz, L, D_HYENA), BF16),
        ],
        compiler_params=_cparams(("parallel", "parallel")),
        name="conv_gate",
    )(z3, z3, z3, conv_w, conv_b.reshape(1, c3))


TWO_OVER_PI = 0.6366197723675814
PIO2_1 = 1.5703125
PIO2_2 = 4.837512969970703125e-4
PIO2_3 = 7.54978995489188216e-8
TRIG_FAST_LIMIT = 4096.0


def _quadrant_value(x, shift):
    kf = jnp.floor(x * TWO_OVER_PI + 0.5)
    r = ((x - kf * PIO2_1) - kf * PIO2_2) - kf * PIO2_3
    z = r * r
    s = r + r * z * (-1.6666654611e-1 + z * (8.3321608736e-3 + z * -1.9515295891e-4))
    c = (1.0 - 0.5 * z) + z * z * (4.166664568298827e-2
                                   + z * (-1.388731625493765e-3 + z * 2.443315711809948e-5))
    k = kf + shift
    half = jnp.floor(k * 0.5)
    odd = k - 2.0 * half
    flip = half - 2.0 * jnp.floor(half * 0.5)
    return (s + odd * (c - s)) * (1.0 - 2.0 * flip)


def _sin_small(x):
    return _quadrant_value(x, 0.0)


def _cos_small(x):
    return _quadrant_value(x, 1.0)


def _sin(x):
    return lax.cond(jnp.max(jnp.abs(x)) < TRIG_FAST_LIMIT, _sin_small, jnp.sin, x)


def _cos(x):
    return lax.cond(jnp.max(jnp.abs(x)) < TRIG_FAST_LIMIT, _cos_small, jnp.cos, x)


def _dot_split(a, b):
    a_hi = a.astype(BF16)
    b_hi = b.astype(BF16)
    a_lo = (a - a_hi.astype(F32)).astype(BF16)
    b_lo = (b - b_hi.astype(F32)).astype(BF16)
    return (jnp.dot(a_hi, b_hi, preferred_element_type=F32)
            + (jnp.dot(a_hi, b_lo, preferred_element_type=F32)
               + jnp.dot(a_lo, b_hi, preferred_element_type=F32)))


def _filter_kernel(fb_ref, w1t_ref, w1c_ref, w1s_ref, b1_ref, w2_ref, b2_ref, w3_ref, b3_ref,
                   w4_ref, fr_ref, dl_ref, o_ref, *, L, tp):
    j = pl.program_id(0)
    pos = (lax.broadcasted_iota(jnp.int32, (1, tp), 1) + j * tp).astype(F32)
    t = pos / (L - 1.0)
    omega = (2.0 * math.pi) * pos / float(L)
    ang = fb_ref[...] * omega
    fr = fr_ref[...]
    pre = (w1t_ref[...] * t
           + _dot_split(w1c_ref[...], _cos(ang)) - _dot_split(w1s_ref[...], _sin(ang)))
    h = _sin(fr * (pre + b1_ref[...]))
    h = _sin(fr * (_dot_split(w2_ref[...], h) + b2_ref[...]))
    h = _sin(fr * (_dot_split(w3_ref[...], h) + b3_ref[...]))
    out = _dot_split(h.T, w4_ref[...])
    tcol = (lax.broadcasted_iota(jnp.int32, (tp, 1), 0) + j * tp).astype(F32) / (L - 1.0)
    decay = jnp.exp(-tcol * dl_ref[...])
    c = D_HYENA
    o_ref[:, 0:c] = out[:, 0:c] * decay
    o_ref[:, c:2 * c] = out[:, c:2 * c] * decay


def _implicit_filters(L, w1, b1, w2, b2, w3, b3, w4, freq):
    bands = (FILTER_EMB - 1) // 2
    fh = FILTER_HIDDEN
    tp = min(512, L)
    fb = jnp.linspace(1e-4, bands - 1, bands, dtype=F32).reshape(bands, 1)
    max_decay = math.log(DECAY_TARGET) / FAST_DECAY_PCT
    min_decay = math.log(DECAY_TARGET) / SLOW_DECAY_PCT
    deltas = jnp.abs(jnp.linspace(min_decay, max_decay, D_HYENA, dtype=F32)).reshape(1, D_HYENA)
    w1 = w1.astype(F32)
    args = (
        fb,
        w1[0:1, :].T,
        w1[1:1 + bands, :].T,
        w1[1 + bands:, :].T,
        b1.astype(F32).reshape(fh, 1),
        w2.astype(F32).T, b2.astype(F32).reshape(fh, 1),
        w3.astype(F32).T, b3.astype(F32).reshape(fh, 1),
        w4.astype(F32),
        freq.astype(F32).reshape(fh, 1),
        deltas,
    )

    def full(a):
        return pl.BlockSpec(a.shape, lambda j: (0,) * a.ndim)

    return pl.pallas_call(
        functools.partial(_filter_kernel, L=L, tp=tp),
        grid=(L // tp,),
        in_specs=[full(a) for a in args],
        out_specs=pl.BlockSpec((tp, 2 * D_HYENA), lambda j: (j, 0)),
        out_shape=jax.ShapeDtypeStruct((L, 2 * D_HYENA), F32),
        compiler_params=_cparams(("parallel",)),
        name="implicit_filter",
    )(*args)


FFT_LANES = 128
FFT_T1 = 4


@functools.lru_cache(maxsize=None)
def _fft_tables(L):
    n = 2 * L
    n2 = FFT_N2
    n1 = n // n2
    n1h = n1 // 2
    odd = 2 * np.arange(n1h) + 1
    th = 2.0 * np.pi * (np.outer(odd, np.arange(n1h)) % (2 * n1)) / (2 * n1)
    f1 = np.concatenate([np.cos(th), -np.sin(th)], axis=0)
    tw = 2.0 * np.pi * (np.outer(odd, np.arange(n2)) % (2 * n)) / (2 * n)
    twr = np.cos(tw)[:, :, None]
    twi = -np.sin(tw)[:, :, None]
    t2 = 2.0 * np.pi * (np.outer(np.arange(n2), np.arange(n2)) % n2) / n2
    fr, fi = np.cos(t2), -np.sin(t2)
    f2 = np.block([[fr, -fi], [fi, fr]])
    f2inv = np.block([[fr, fi], [-fi, fr]])
    f1inv = np.concatenate([np.cos(th.T), -np.sin(th.T)], axis=1) * (2.0 / n)
    return dict(
        n1=n1, n1h=n1h,
        f1=np.asarray(f1, np.float32), f1inv=np.asarray(f1inv, np.float32),
        f2=np.asarray(f2, np.float32), f2inv=np.asarray(f2inv, np.float32),
        twr=np.asarray(twr, np.float32), twi=np.asarray(twi, np.float32),
    )


FFT_A = 16
SLOT_DTYPE = BF16


def _outer_dft(f, src_ref, dst_ref):
    n_in, n1h = src_ref.shape[0], src_ref.shape[1]
    n_out = dst_ref.shape[0]

    def body(i, carry):
        a0 = pl.multiple_of(i * FFT_A, FFT_A)
        ts = [jnp.swapaxes(src_ref[p, :, pl.ds(a0, FFT_A), :].astype(F32), 0, 1)
              for p in range(n_in)]
        outs = [[] for _ in range(n_out)]
        for j in range(0, FFT_A, 2):
            rhs = jnp.concatenate([jnp.concatenate([t[j], t[j + 1]], axis=1) for t in ts],
                                  axis=0).astype(BF16)
            res = jnp.dot(f, rhs, preferred_element_type=F32)
            for q in range(n_out):
                blk = res[q * n1h:(q + 1) * n1h]
                outs[q] += [blk[:, 0:FFT_LANES], blk[:, FFT_LANES:2 * FFT_LANES]]
        for q in range(n_out):
            dst_ref[q, :, pl.ds(a0, FFT_A), :] = jnp.swapaxes(
                jnp.stack(outs[q], axis=0), 0, 1).astype(dst_ref.dtype)
        return carry

    lax.fori_loop(0, FFT_N2 // FFT_A, body, 0, unroll=2)


def _fft_stage1_kernel(f_ref, u_ref, o_ref):
    _outer_dft(f_ref[...], u_ref, o_ref)


def _fft_stage1(u3, f1):
    bsz, L, c = u3.shape
    n1h = L // FFT_N2
    return pl.pallas_call(
        _fft_stage1_kernel,
        grid=(bsz, c // FFT_LANES),
        in_specs=[
            pl.BlockSpec(f1.shape, lambda b, j: (0, 0)),
            pl.BlockSpec((1, n1h, FFT_N2, FFT_LANES), lambda b, j: (b, 0, 0, j)),
        ],
        out_specs=pl.BlockSpec((pl.Squeezed(), 2, n1h, FFT_N2, FFT_LANES),
                               lambda b, j: (b, 0, 0, 0, j)),
        out_shape=jax.ShapeDtypeStruct((bsz, 2, n1h, FFT_N2, c), SLOT_DTYPE),
        compiler_params=_cparams(("parallel", "parallel")),
        name="fft_stage1",
    )(f1, u3.reshape(bsz, n1h, FFT_N2, c))


def _twiddle(ar, ai, tr, ti):
    return ar * tr - ai * ti, ar * ti + ai * tr


def _stage2_inputs(a_ref, twr_ref, twi_ref):
    xs = []
    for j in range(FFT_T1):
        xr, xi = _twiddle(a_ref[0, j].astype(F32), a_ref[1, j].astype(F32),
                          twr_ref[j], twi_ref[j])
        xs.append(jnp.concatenate([xr, xi], axis=0).astype(BF16))
    return xs


def _filter_spectrum_kernel(a_ref, twr_ref, twi_ref, f2_ref, o_ref):
    n2, c = FFT_N2, D_HYENA
    f2 = f2_ref[...]
    bigs = [jnp.dot(f2, x, preferred_element_type=F32)
            for x in _stage2_inputs(a_ref, twr_ref, twi_ref)]
    for j, big in enumerate(bigs):
        o_ref[j, 0] = big[0:n2, 0:c] + big[0:n2, c:2 * c]
        o_ref[j, 1] = big[n2:2 * n2, 0:c] - big[n2:2 * n2, c:2 * c]


def _filter_spectrum(hfilt, tb):
    L, c2 = hfilt.shape
    n1h = tb["n1h"]
    a5 = _fft_stage1(hfilt.reshape(1, L, c2), tb["f1"])
    return pl.pallas_call(
        _filter_spectrum_kernel,
        grid=(n1h // FFT_T1,),
        in_specs=[
            pl.BlockSpec((pl.Squeezed(), 2, FFT_T1, FFT_N2, c2), lambda k: (0, 0, k, 0, 0)),
            pl.BlockSpec((FFT_T1, FFT_N2, 1), lambda k: (k, 0, 0)),
            pl.BlockSpec((FFT_T1, FFT_N2, 1), lambda k: (k, 0, 0)),
            pl.BlockSpec((2 * FFT_N2, 2 * FFT_N2), lambda k: (0, 0)),
        ],
        out_specs=pl.BlockSpec((FFT_T1, 2, FFT_N2, D_HYENA), lambda k: (k, 0, 0, 0)),
        out_shape=jax.ShapeDtypeStruct((n1h, 2, FFT_N2, D_HYENA), F32),
        compiler_params=_cparams(("parallel",)),
        name="filter_spectrum",
    )(a5, tb["twr"], tb["twi"], tb["f2"])


def _fft_mid_kernel(a_ref, kf_ref, twr_ref, twi_ref, f2_ref, f2i_ref, o_ref):
    n2 = FFT_N2
    f2, f2i = f2_ref[...], f2i_ref[...]
    bigs = [jnp.dot(f2, x, preferred_element_type=F32)
            for x in _stage2_inputs(a_ref, twr_ref, twi_ref)]
    ys = []
    for j, big in enumerate(bigs):
        sr, si = big[0:n2], big[n2:2 * n2]
        kr, ki = kf_ref[j, 0], kf_ref[j, 1]
        ys.append(jnp.concatenate([sr * kr - si * ki, sr * ki + si * kr], axis=0).astype(BF16))
    backs = [jnp.dot(f2i, y, preferred_element_type=F32) for y in ys]
    for j, back in enumerate(backs):
        br, bi = back[0:n2], back[n2:2 * n2]
        tr, ti = twr_ref[j], twi_ref[j]
        o_ref[0, j] = (br * tr + bi * ti).astype(o_ref.dtype)
        o_ref[1, j] = (bi * tr - br * ti).astype(o_ref.dtype)


def _fft_mid(a5, kf, tb):
    bsz, _, n1h, n2, c = a5.shape
    slot = pl.BlockSpec((pl.Squeezed(), 2, FFT_T1, n2, c), lambda k, b: (b, 0, k, 0, 0))
    return pl.pallas_call(
        _fft_mid_kernel,
        grid=(n1h // FFT_T1, bsz),
        in_specs=[
            slot,
            pl.BlockSpec((FFT_T1, 2, n2, c), lambda k, b: (k, 0, 0, 0)),
            pl.BlockSpec((FFT_T1, n2, 1), lambda k, b: (k, 0, 0)),
            pl.BlockSpec((FFT_T1, n2, 1), lambda k, b: (k, 0, 0)),
            pl.BlockSpec((2 * n2, 2 * n2), lambda k, b: (0, 0)),
            pl.BlockSpec((2 * n2, 2 * n2), lambda k, b: (0, 0)),
        ],
        out_specs=slot,
        out_shape=jax.ShapeDtypeStruct(a5.shape, SLOT_DTYPE),
        compiler_params=_cparams(("parallel", "parallel")),
        name="fft_mid",
    )(a5, kf, tb["twr"], tb["twi"], tb["f2"], tb["f2inv"])


def _fft_out_kernel(f_ref, b_ref, o_ref):
    _outer_dft(f_ref[...], b_ref, o_ref)


def _fft_out(b5, f1inv):
    bsz, _, n1h, n2, c = b5.shape
    return pl.pallas_call(
        _fft_out_kernel,
        grid=(bsz, c // FFT_LANES),
        in_specs=[
            pl.BlockSpec(f1inv.shape, lambda b, j: (0, 0)),
            pl.BlockSpec((pl.Squeezed(), 2, n1h, n2, FFT_LANES), lambda b, j: (b, 0, 0, 0, j)),
        ],
        out_specs=pl.BlockSpec((1, n1h, n2, FFT_LANES), lambda b, j: (b, 0, 0, j)),
        out_shape=jax.ShapeDtypeStruct((bsz, n1h, n2, c), BF16),
        compiler_params=_cparams(("parallel", "parallel")),
        name="fft_out",
    )(f1inv, b5).reshape(bsz, n1h * n2, c)


def _hyena_conv(u, kf, tb):
    a5 = _fft_stage1(u, tb["f1"])
    b5 = _fft_mid(a5, kf, tb)
    return _fft_out(b5, tb["f1inv"])


def _merge_kernel(ya_ref, yc_ref, u_ref, x0_ref, dk_ref, ga_ref, gh_ref, x_ref, wa_ref, wh_ref,
                  wo_ref, o_ref):
    yh = x0_ref[...].astype(F32) * (yc_ref[...].astype(F32)
                                    + u_ref[...].astype(F32) * dk_ref[...])
    pa = jnp.dot(ya_ref[...], wa_ref[...], preferred_element_type=F32)
    ph = jnp.dot(yh.astype(BF16), wh_ref[...], preferred_element_type=F32)
    merged = (jax.nn.sigmoid(ga_ref[...].astype(F32)) * pa
              + jax.nn.sigmoid(gh_ref[...].astype(F32)) * ph)
    o_ref[...] = x_ref[...] + jnp.dot(merged.astype(BF16), wo_ref[...],
                                      preferred_element_type=F32)


def _merge(ya, yc, u, x0, d_skip, z, x2, wa, wh, wo):
    n, d = x2.shape
    tm = min(512, n)
    ga_blk = (D_ATTN * 3 + D_HYENA * 3) // d
    tok = pl.BlockSpec((tm, D_HYENA), lambda i: (i, 0))
    return pl.pallas_call(
        _merge_kernel,
        grid=(n // tm,),
        in_specs=[
            pl.BlockSpec((tm, D_ATTN), lambda i: (i, 0)),
            tok, tok, tok,
            pl.BlockSpec((1, D_HYENA), lambda i: (0, 0)),
            pl.BlockSpec((tm, d), lambda i: (i, ga_blk)),
            pl.BlockSpec((tm, d), lambda i: (i, ga_blk + 1)),
            pl.BlockSpec((tm, d), lambda i: (i, 0)),
            pl.BlockSpec(wa.shape, lambda i: (0, 0)),
            pl.BlockSpec(wh.shape, lambda i: (0, 0)),
            pl.BlockSpec(wo.shape, lambda i: (0, 0)),
        ],
        out_specs=pl.BlockSpec((tm, d), lambda i: (i, 0)),
        out_shape=jax.ShapeDtypeStruct((n, d), F32),
        compiler_params=_cparams(("parallel",)),
        name="merge_outproj",
    )(ya, yc, u, x0, d_skip.astype(F32).reshape(1, D_HYENA), z, z, x2, wa, wh, wo)


def _rms(x, g):
    inv = lax.rsqrt(jnp.mean(x * x, axis=-1, keepdims=True) + EPS)
    return (x * inv) * g


def _ffn_kernel(x_ref, g_ref, wg_ref, wu_ref, wd_ref, gf_ref, o_ref, h_ref, acc_ref):
    j = pl.program_id(1)

    @pl.when(j == 0)
    def _():
        h_ref[...] = _rms(x_ref[...], g_ref[...]).astype(BF16)
        acc_ref[...] = jnp.zeros_like(acc_ref)

    h = h_ref[...]
    gate = jnp.dot(h, wg_ref[...], preferred_element_type=F32)
    up = jnp.dot(h, wu_ref[...], preferred_element_type=F32)
    act = (gate * jax.nn.sigmoid(gate) * up).astype(BF16)
    acc_ref[...] += jnp.dot(act, wd_ref[...], preferred_element_type=F32)

    @pl.when(j == pl.num_programs(1) - 1)
    def _():
        o_ref[...] = _rms(x_ref[...] + acc_ref[...], gf_ref[...])


def _ffn(x1, g, wg, wu, wd, gf):
    n, d = x1.shape
    dff = wg.shape[1]
    tm = min(512, n)
    tf = dff // 2
    return pl.pallas_call(
        _ffn_kernel,
        grid=(n // tm, dff // tf),
        in_specs=[
            pl.BlockSpec((tm, d), lambda i, j: (i, 0)),
            pl.BlockSpec((1, d), lambda i, j: (0, 0)),
            pl.BlockSpec((d, tf), lambda i, j: (0, j)),
            pl.BlockSpec((d, tf), lambda i, j: (0, j)),
            pl.BlockSpec((tf, d), lambda i, j: (j, 0)),
            pl.BlockSpec((1, d), lambda i, j: (0, 0)),
        ],
        out_specs=pl.BlockSpec((tm, d), lambda i, j: (i, 0)),
        out_shape=jax.ShapeDtypeStruct((n, d), F32),
        scratch_shapes=[pltpu.VMEM((tm, d), BF16), pltpu.VMEM((tm, d), F32)],
        compiler_params=_cparams(("parallel", "arbitrary")),
        name="ffn_final",
    )(x1, g.reshape(1, d), wg, wu, wd, gf.reshape(1, d))


def _layer(x2, bsz, L, p):
    n, d = x2.shape
    z = _norm_inproj(x2, p["norm_mix"], p["w_in"])
    z3 = z.reshape(bsz, L, z.shape[1])
    ya = _attention(z3, p["rpb"])
    u, x0 = _conv_gate(z3, p["conv_w"], p["conv_b"])
    tb = dict(_fft_tables(L))
    for name in ("f1", "f1inv", "f2", "f2inv"):
        tb[name] = jnp.asarray(tb[name]).astype(BF16)
    hfilt = _implicit_filters(L, *p["filt"])
    kf = _filter_spectrum(hfilt, tb)
    yc = _hyena_conv(u, kf, tb)
    x1 = _merge(ya.reshape(n, D_ATTN), yc.reshape(n, D_HYENA), u.reshape(n, D_HYENA),
                x0.reshape(n, D_HYENA), p["hyena_d"], z, x2,
                p["w_br_attn"], p["w_br_hyena"], p["w_out"])
    return x1


def _trunk(x, layers, norm_final):
    bsz, L, d = x.shape
    x2 = x.reshape(bsz * L, d)
    depth = len(layers)
    for li, p in enumerate(layers):
        x1 = _layer(x2, bsz, L, p)
        gf = norm_final if li == depth - 1 else None
        assert gf is not None, "only the last layer fuses the final norm"
        x2 = _ffn(x1, p["norm_ffn"], p["w_gate"], p["w_up"], p["w_down"], gf)
    return x2.reshape(bsz, L, d)


def kernel(x_prompt, x_sample, norm_mix, w_in, rpb, conv_w, conv_b, filt_w1, filt_b1, filt_w2,
           filt_b2, filt_w3, filt_b3, filt_w4, filt_freq, hyena_d, w_br_attn, w_br_hyena, w_out,
           norm_ffn, w_gate, w_up, w_down, norm_final):
    depth = w_in.shape[0]
    assert depth == 1
    layers = []
    col = lax.broadcasted_iota(jnp.int32, (1, w_in.shape[2]), 1)
    col_scale = jnp.where(col < D_ATTN, Q_SCALE, 1.0).astype(F32)
    for l in range(depth):
        layers.append(dict(
            norm_mix=norm_mix[l], w_in=(w_in[l] * col_scale).astype(BF16), rpb=rpb[l].reshape(-1),
            conv_w=conv_w[l], conv_b=conv_b[l],
            filt=(filt_w1[l], filt_b1[l], filt_w2[l], filt_b2[l], filt_w3[l], filt_b3[l],
                  filt_w4[l], filt_freq[l]),
            hyena_d=hyena_d[l],
            w_br_attn=w_br_attn[l].astype(BF16), w_br_hyena=w_br_hyena[l].astype(BF16),
            w_out=w_out[l].astype(BF16), norm_ffn=norm_ffn[l],
            w_gate=w_gate[l].astype(BF16), w_up=w_up[l].astype(BF16),
            w_down=w_down[l].astype(BF16),
        ))
    y_prompt = _trunk(x_prompt, layers, norm_final)
    y_sample = _trunk(x_sample, layers, norm_final)
    return (y_prompt, y_sample)
```

```python
import functools
import math

import numpy as np
import jax
import jax.numpy as jnp
from jax import lax
from jax.experimental import pallas as pl
from jax.experimental.pallas import tpu as pltpu

F32 = jnp.float32
BF16 = jnp.bfloat16

GRID_W = 64
N_HEADS = 8
HEAD_DIM = 64
D_ATTN = N_HEADS * HEAD_DIM
WIN_R = 8
WIN_C = 16
D_HYENA = 512
SHORT_K = 3
FILTER_EMB = 33
FILTER_HIDDEN = 64
DECAY_TARGET = 1e-2
FAST_DECAY_PCT = 0.3
SLOW_DECAY_PCT = 1.5
EPS = 1e-6

FFT_N2 = 128
ATT_ROWS = 8
NEG_BIG = -1e30
VMEM_LIMIT = 56 * 1024 * 1024


def _cparams(sem):
    return pltpu.CompilerParams(dimension_semantics=sem, vmem_limit_bytes=VMEM_LIMIT)


def _norm_inproj_kernel(x_ref, g_ref, w_ref, o_ref, h_ref):
    @pl.when(pl.program_id(1) == 0)
    def _():
        x = x_ref[...]
        inv = lax.rsqrt(jnp.mean(x * x, axis=-1, keepdims=True) + EPS)
        h_ref[...] = ((x * inv) * g_ref[...]).astype(BF16)

    o_ref[...] = jnp.dot(h_ref[...], w_ref[...], preferred_element_type=F32).astype(o_ref.dtype)


def _norm_inproj(x2, g, w_bf16):
    n, d = x2.shape
    d_in = w_bf16.shape[1]
    tm = min(1024, n)
    tn = 2560
    return pl.pallas_call(
        _norm_inproj_kernel,
        grid=(n // tm, d_in // tn),
        in_specs=[
            pl.BlockSpec((tm, d), lambda i, j: (i, 0)),
            pl.BlockSpec((1, d), lambda i, j: (0, 0)),
            pl.BlockSpec((d, tn), lambda i, j: (0, j)),
        ],
        out_specs=pl.BlockSpec((tm, tn), lambda i, j: (i, j)),
        out_shape=jax.ShapeDtypeStruct((n, d_in), BF16),
        scratch_shapes=[pltpu.VMEM((tm, d), BF16)],
        compiler_params=_cparams(("parallel", "arbitrary")),
        name="norm_inproj",
    )(x2, g.reshape(1, d), w_bf16)


N_DR = 2 * WIN_R - 1
N_DC = 2 * WIN_C - 1
LOG2_E = 1.4426950408889634
Q_SCALE = (HEAD_DIM ** -0.5) * LOG2_E
KV_ROWS = 2 * ATT_ROWS


def _build_bias_tiles(rpb_ref, pt_ref):
    k = lax.broadcasted_iota(jnp.int32, (GRID_W, 128), 0)
    lane = lax.broadcasted_iota(jnp.int32, (GRID_W, 128), 1)
    q = lane & (GRID_W - 1)
    first_half = lane < GRID_W
    cs = jnp.clip(q - WIN_C // 2, 0, GRID_W - WIN_C)
    valid = (k >= cs) & (k < cs + WIN_C)
    dc = jnp.clip(k - q, -(WIN_C - 1), WIN_C - 1) + (WIN_C - 1)

    def body(idx, carry):
        p = idx // N_DR
        d = idx % N_DR
        base0 = ((2 * p) * N_DR + d) * N_DC
        base1 = ((2 * p + 1) * N_DR + d) * N_DC
        acc = jnp.zeros((GRID_W, 128), F32)
        for off in range(N_DC):
            val = jnp.where(first_half, rpb_ref[base0 + off], rpb_ref[base1 + off])
            acc = jnp.where(dc == off, val, acc)
        pt_ref[idx] = jnp.where(valid, acc * LOG2_E, NEG_BIG)
        return carry

    lax.fori_loop(0, (N_HEADS // 2) * N_DR, body, 0)


def _kv_window_start(i, rows):
    return jnp.clip(i * ATT_ROWS - WIN_R // 2, 0, rows - KV_ROWS)


def _attn_kernel(rpb_ref, q_ref, k_ref, v_ref, o_ref, pt_ref, *, rows):
    b = pl.program_id(0)
    i = pl.program_id(1)

    @pl.when((b == 0) & (i == 0))
    def _():
        _build_bias_tiles(rpb_ref, pt_ref)

    lane = lax.broadcasted_iota(jnp.int32, (GRID_W, 128), 1)
    first_half = lane < HEAD_DIM
    ones = jnp.ones((WIN_R * GRID_W, 128), BF16)
    nkeys = WIN_R * GRID_W
    wstart = _kv_window_start(i, rows)
    nt_dims = (((1,), (1,)), ((), ()))

    def row_body(rr, carry):
        r = i * ATT_ROWS + rr
        rs = jnp.clip(r - WIN_R // 2, 0, rows - WIN_R)
        s = rs - r + (WIN_R - 1)
        koff = pl.multiple_of((rs - wstart) * GRID_W, GRID_W)
        qoff = pl.multiple_of(rr * GRID_W, GRID_W)
        q = q_ref[pl.ds(qoff, GRID_W), :]
        pairs = range(N_HEADS // 2)
        sts = []
        for p in pairs:
            qp = q[:, 128 * p:128 * (p + 1)]
            zero = jnp.zeros_like(qp)
            wt = jnp.concatenate([jnp.where(first_half, qp, zero),
                                  jnp.where(first_half, zero, qp)], axis=0)
            kp = k_ref[pl.ds(koff, nkeys), 128 * p:128 * (p + 1)]
            sts.append(lax.dot_general(kp, wt, nt_dims, preferred_element_type=F32))
        pms = []
        for p in pairs:
            st = sts[p]
            ch = [st[GRID_W * c:GRID_W * (c + 1)] + pt_ref[p * N_DR + s + c] for c in range(WIN_R)]
            m = ch[0]
            for c in range(1, WIN_R):
                m = jnp.maximum(m, ch[c])
            m = jnp.max(m, axis=0, keepdims=True)
            pt = jnp.concatenate([jnp.exp2(c - m) for c in ch], axis=0).astype(BF16)
            pms.append(pt.T)
        outs = []
        for p in pairs:
            vp = v_ref[pl.ds(koff, nkeys), 128 * p:128 * (p + 1)]
            ov = jnp.dot(pms[p], jnp.concatenate([vp, ones], axis=1), preferred_element_type=F32)
            o = ov[:, 0:128] / ov[:, 128:256]
            outs.append(jnp.where(first_half, o[0:GRID_W], o[GRID_W:2 * GRID_W]))
        o_ref[pl.ds(qoff, GRID_W), :] = jnp.concatenate(outs, axis=1).astype(o_ref.dtype)
        return carry

    lax.fori_loop(0, ATT_ROWS, row_body, 0, unroll=4)


def _attention(z3, rpb_flat):
    bsz, L, _ = z3.shape
    rows = L // GRID_W
    assert rows % ATT_ROWS == 0 and rows >= KV_ROWS
    nblk = rows // ATT_ROWS
    blk = ATT_ROWS * GRID_W

    def window(col):
        return pl.BlockSpec(
            (pl.Squeezed(), pl.Element(KV_ROWS * GRID_W), pl.Element(D_ATTN)),
            lambda b, i: (b, _kv_window_start(i, rows) * GRID_W, col * D_ATTN))

    return pl.pallas_call(
        functools.partial(_attn_kernel, rows=rows),
        grid=(bsz, nblk),
        in_specs=[
            pl.BlockSpec(memory_space=pltpu.SMEM),
            pl.BlockSpec((pl.Squeezed(), blk, D_ATTN), lambda b, i: (b, i, 0)),
            window(1),
            window(2),
        ],
        out_specs=pl.BlockSpec((pl.Squeezed(), blk, D_ATTN), lambda b, i: (b, i, 0)),
        out_shape=jax.ShapeDtypeStruct((bsz, L, D_ATTN), BF16),
        scratch_shapes=[pltpu.VMEM(((N_HEADS // 2) * N_DR, GRID_W, 128), F32)],
        compiler_params=_cparams(("arbitrary", "arbitrary")),
        name="nbr_attention",
    )(rpb_flat, z3, z3, z3)


def _conv_gate_kernel(xm_ref, xp_ref, xn_ref, w_ref, b_ref, u_ref, x0_ref, *, nt):
    i = pl.program_id(1)
    x = xm_ref[0].astype(F32)
    tm = x.shape[0]
    row = lax.broadcasted_iota(jnp.int32, x.shape, 0)
    halo = xp_ref.shape[1]
    prev_row = jnp.where(i == 0, 0.0, xp_ref[0].astype(F32)[halo - 1:halo, :])
    next_row = jnp.where(i == nt - 1, 0.0, xn_ref[0].astype(F32)[0:1, :])
    xm1 = jnp.where(row == 0, prev_row, pltpu.roll(x, 1, 0))
    xp1 = jnp.where(row == tm - 1, next_row, pltpu.roll(x, tm - 1, 0))
    y = xm1 * w_ref[0:1, :] + x * w_ref[1:2, :] + xp1 * w_ref[2:3, :] + b_ref[...]
    c = D_HYENA
    x0_ref[0] = y[:, 0:c].astype(x0_ref.dtype)
    u_ref[0] = (y[:, c:2 * c] * y[:, 2 * c:3 * c]).astype(u_ref.dtype)


def _conv_gate(z3, conv_w, conv_b):
    bsz, L, _ = z3.shape
    c3 = 3 * D_HYENA
    tm = min(512, L)
    nt = L // tm
    halo = 16
    hb = tm // halo
    nh = L // halo
    return pl.pallas_call(
        functools.partial(_conv_gate_kernel, nt=nt),
        grid=(bsz, nt),
        in_specs=[
            pl.BlockSpec((1, tm, c3), lambda b, i: (b, i, 1)),
            pl.BlockSpec((1, halo, c3), lambda b, i: (b, jnp.maximum(i * hb - 1, 0), 1)),
            pl.BlockSpec((1, halo, c3), lambda b, i: (b, jnp.minimum((i + 1) * hb, nh - 1), 1)),
            pl.BlockSpec((SHORT_K, c3), lambda b, i: (0, 0)),
            pl.BlockSpec((1, c3), lambda b, i: (0, 0)),
        ],
        out_specs=[
            pl.BlockSpec((1, tm, D_HYENA), lambda b, i: (b, i, 0)),
            pl.BlockSpec((1, tm, D_HYENA), lambda b, i: (b, i, 0)),
        ],
        out_shape=[
            jax.ShapeDtypeStruct((bsz, L, D_HYENA), BF16),
            jax.ShapeDtypeStruct((bsz, L, D_HYENA), BF16),
        ],
        compiler_params=_cparams(("parallel", "parallel")),
        name="conv_gate",
    )(z3, z3, z3, conv_w, conv_b.reshape(1, c3))


TWO_OVER_PI = 0.6366197723675814
PIO2_1 = 1.5703125
PIO2_2 = 4.837512969970703125e-4
PIO2_3 = 7.54978995489188216e-8
TRIG_FAST_LIMIT = 4096.0


def _quadrant_value(x, shift):
    kf = jnp.floor(x * TWO_OVER_PI + 0.5)
    r = ((x - kf * PIO2_1) - kf * PIO2_2) - kf * PIO2_3
    z = r * r
    s = r + r * z * (-1.6666654611e-1 + z * (8.3321608736e-3 + z * -1.9515295891e-4))
    c = (1.0 - 0.5 * z) + z * z * (4.166664568298827e-2
                                   + z * (-1.388731625493765e-3 + z * 2.443315711809948e-5))
    k = kf + shift
    half = jnp.floor(k * 0.5)
    odd = k - 2.0 * half
    flip = half - 2.0 * jnp.floor(half * 0.5)
    return (s + odd * (c - s)) * (1.0 - 2.0 * flip)


def _sin_small(x):
    return _quadrant_value(x, 0.0)


def _cos_small(x):
    return _quadrant_value(x, 1.0)


def _sin(x):
    return lax.cond(jnp.max(jnp.abs(x)) < TRIG_FAST_LIMIT, _sin_small, jnp.sin, x)


def _cos(x):
    return lax.cond(jnp.max(jnp.abs(x)) < TRIG_FAST_LIMIT, _cos_small, jnp.cos, x)


def _dot_split(a, b):
    a_hi = a.astype(BF16)
    b_hi = b.astype(BF16)
    a_lo = (a - a_hi.astype(F32)).astype(BF16)
    b_lo = (b - b_hi.astype(F32)).astype(BF16)
    return (jnp.dot(a_hi, b_hi, preferred_element_type=F32)
            + (jnp.dot(a_hi, b_lo, preferred_element_type=F32)
               + jnp.dot(a_lo, b_hi, preferred_element_type=F32)))


def _filter_kernel(fb_ref, w1t_ref, w1c_ref, w1s_ref, b1_ref, w2_ref, b2_ref, w3_ref, b3_ref,
                   w4_ref, fr_ref, dl_ref, o_ref, *, L, tp):
    j = pl.program_id(0)
    pos = (lax.broadcasted_iota(jnp.int32, (1, tp), 1) + j * tp).astype(F32)
    t = pos / (L - 1.0)
    omega = (2.0 * math.pi) * pos / float(L)
    ang = fb_ref[...] * omega
    fr = fr_ref[...]
    pre = (w1t_ref[...] * t
           + _dot_split(w1c_ref[...], _cos(ang)) - _dot_split(w1s_ref[...], _sin(ang)))
    h = _sin(fr * (pre + b1_ref[...]))
    h = _sin(fr * (_dot_split(w2_ref[...], h) + b2_ref[...]))
    h = _sin(fr * (_dot_split(w3_ref[...], h) + b3_ref[...]))
    out = _dot_split(h.T, w4_ref[...])
    tcol = (lax.broadcasted_iota(jnp.int32, (tp, 1), 0) + j * tp).astype(F32) / (L - 1.0)
    decay = jnp.exp(-tcol * dl_ref[...])
    c = D_HYENA
    o_ref[:, 0:c] = out[:, 0:c] * decay
    o_ref[:, c:2 * c] = out[:, c:2 * c] * decay


def _implicit_filters(L, w1, b1, w2, b2, w3, b3, w4, freq):
    bands = (FILTER_EMB - 1) // 2
    fh = FILTER_HIDDEN
    tp = min(512, L)
    fb = jnp.linspace(1e-4, bands - 1, bands, dtype=F32).reshape(bands, 1)
    max_decay = math.log(DECAY_TARGET) / FAST_DECAY_PCT
    min_decay = math.log(DECAY_TARGET) / SLOW_DECAY_PCT
    deltas = jnp.abs(jnp.linspace(min_decay, max_decay, D_HYENA, dtype=F32)).reshape(1, D_HYENA)
    w1 = w1.astype(F32)
    args = (
        fb,
        w1[0:1, :].T,
        w1[1:1 + bands, :].T,
        w1[1 + bands:, :].T,
        b1.astype(F32).reshape(fh, 1),
        w2.astype(F32).T, b2.astype(F32).reshape(fh, 1),
        w3.astype(F32).T, b3.astype(F32).reshape(fh, 1),
        w4.astype(F32),
        freq.astype(F32).reshape(fh, 1),
        deltas,
    )

    def full(a):
        return pl.BlockSpec(a.shape, lambda j: (0,) * a.ndim)

    return pl.pallas_call(
        functools.partial(_filter_kernel, L=L, tp=tp),
        grid=(L // tp,),
        in_specs=[full(a) for a in args],
        out_specs=pl.BlockSpec((tp, 2 * D_HYENA), lambda j: (j, 0)),
        out_shape=jax.ShapeDtypeStruct((L, 2 * D_HYENA), F32),
        compiler_params=_cparams(("parallel",)),
        name="implicit_filter",
    )(*args)


FFT_LANES = 128
FFT_T1 = 4


@functools.lru_cache(maxsize=None)
def _fft_tables(L):
    n = 2 * L
    n2 = FFT_N2
    n1 = n // n2
    n1h = n1 // 2
    odd = 2 * np.arange(n1h) + 1
    th = 2.0 * np.pi * (np.outer(odd, np.arange(n1h)) % (2 * n1)) / (2 * n1)
    f1 = np.concatenate([np.cos(th), -np.sin(th)], axis=0)
    tw = 2.0 * np.pi * (np.outer(odd, np.arange(n2)) % (2 * n)) / (2 * n)
    twr = np.cos(tw)[:, :, None]
    twi = -np.sin(tw)[:, :, None]
    t2 = 2.0 * np.pi * (np.outer(np.arange(n2), np.arange(n2)) % n2) / n2
    fr, fi = np.cos(t2), -np.sin(t2)
    f2 = np.block([[fr, -fi], [fi, fr]])
    f2inv = np.block([[fr, fi], [-fi, fr]])
    f1inv = np.concatenate([np.cos(th.T), -np.sin(th.T)], axis=1) * (2.0 / n)
    return dict(
        n1=n1, n1h=n1h,
        f1=np.asarray(f1, np.float32), f1inv=np.asarray(f1inv, np.float32),
        f2=np.asarray(f2, np.float32), f2inv=np.asarray(f2inv, np.float32),
        twr=np.asarray(twr, np.float32), twi=np.asarray(twi, np.float32),
    )


FFT_A = 16
SLOT_DTYPE = BF16


def _outer_dft(f, src_ref, dst_ref):
    n_in, n1h = src_ref.shape[0], src_ref.shape[1]
    n_out = dst_ref.shape[0]

    def body(i, carry):
        a0 = pl.multiple_of(i * FFT_A, FFT_A)
        ts = [jnp.swapaxes(src_ref[p, :, pl.ds(a0, FFT_A), :].astype(F32), 0, 1)
              for p in range(n_in)]
        outs = [[] for _ in range(n_out)]
        for j in range(0, FFT_A, 2):
            rhs = jnp.concatenate([jnp.concatenate([t[j], t[j + 1]], axis=1) for t in ts],
                                  axis=0).astype(BF16)
            res = jnp.dot(f, rhs, preferred_element_type=F32)
            for q in range(n_out):
                blk = res[q * n1h:(q + 1) * n1h]
                outs[q] += [blk[:, 0:FFT_LANES], blk[:, FFT_LANES:2 * FFT_LANES]]
        for q in range(n_out):
            dst_ref[q, :, pl.ds(a0, FFT_A), :] = jnp.swapaxes(
                jnp.stack(outs[q], axis=0), 0, 1).astype(dst_ref.dtype)
        return carry

    lax.fori_loop(0, FFT_N2 // FFT_A, body, 0, unroll=2)


def _fft_stage1_kernel(f_ref, u_ref, o_ref):
    _outer_dft(f_ref[...], u_ref, o_ref)


def _fft_stage1(u3, f1):
    bsz, L, c = u3.shape
    n1h = L // FFT_N2
    return pl.pallas_call(
        _fft_stage1_kernel,
        grid=(bsz, c // FFT_LANES),
        in_specs=[
            pl.BlockSpec(f1.shape, lambda b, j: (0, 0)),
            pl.BlockSpec((1, n1h, FFT_N2, FFT_LANES), lambda b, j: (b, 0, 0, j)),
        ],
        out_specs=pl.BlockSpec((pl.Squeezed(), 2, n1h, FFT_N2, FFT_LANES),
                               lambda b, j: (b, 0, 0, 0, j)),
        out_shape=jax.ShapeDtypeStruct((bsz, 2, n1h, FFT_N2, c), SLOT_DTYPE),
        compiler_params=_cparams(("parallel", "parallel")),
        name="fft_stage1",
    )(f1, u3.reshape(bsz, n1h, FFT_N2, c))


def _twiddle(ar, ai, tr, ti):
    return ar * tr - ai * ti, ar * ti + ai * tr


def _stage2_inputs(a_ref, twr_ref, twi_ref):
    xs = []
    for j in range(FFT_T1):
        xr, xi = _twiddle(a_ref[0, j].astype(F32), a_ref[1, j].astype(F32),
                          twr_ref[j], twi_ref[j])
        xs.append(jnp.concatenate([xr, xi], axis=0).astype(BF16))
    return xs


def _filter_spectrum_kernel(a_ref, twr_ref, twi_ref, f2_ref, o_ref):
    n2, c = FFT_N2, D_HYENA
    f2 = f2_ref[...]
    bigs = [jnp.dot(f2, x, preferred_element_type=F32)
            for x in _stage2_inputs(a_ref, twr_ref, twi_ref)]
    for j, big in enumerate(bigs):
        o_ref[j, 0] = big[0:n2, 0:c] + big[0:n2, c:2 * c]
        o_ref[j, 1] = big[n2:2 * n2, 0:c] - big[n2:2 * n2, c:2 * c]


def _filter_spectrum(hfilt, tb):
    L, c2 = hfilt.shape
    n1h = tb["n1h"]
    a5 = _fft_stage1(hfilt.reshape(1, L, c2), tb["f1"])
    return pl.pallas_call(
        _filter_spectrum_kernel,
        grid=(n1h // FFT_T1,),
        in_specs=[
            pl.BlockSpec((pl.Squeezed(), 2, FFT_T1, FFT_N2, c2), lambda k: (0, 0, k, 0, 0)),
            pl.BlockSpec((FFT_T1, FFT_N2, 1), lambda k: (k, 0, 0)),
            pl.BlockSpec((FFT_T1, FFT_N2, 1), lambda k: (k, 0, 0)),
            pl.BlockSpec((2 * FFT_N2, 2 * FFT_N2), lambda k: (0, 0)),
        ],
        out_specs=pl.BlockSpec((FFT_T1, 2, FFT_N2, D_HYENA), lambda k: (k, 0, 0, 0)),
        out_shape=jax.ShapeDtypeStruct((n1h, 2, FFT_N2, D_HYENA), F32),
        compiler_params=_cparams(("parallel",)),
        name="filter_spectrum",
    )(a5, tb["twr"], tb["twi"], tb["f2"])


def _fft_mid_kernel(a_ref, kf_ref, twr_ref, twi_ref, f2_ref, f2i_ref, o_ref):
    n2 = FFT_N2
    f2, f2i = f2_ref[...], f2i_ref[...]
    bigs = [jnp.dot(f2, x, preferred_element_type=F32)
            for x in _stage2_inputs(a_ref, twr_ref, twi_ref)]
    ys = []
    for j, big in enumerate(bigs):
        sr, si = big[0:n2], big[n2:2 * n2]
        kr, ki = kf_ref[j, 0], kf_ref[j, 1]
        ys.append(jnp.concatenate([sr * kr - si * ki, sr * ki + si * kr], axis=0).astype(BF16))
    backs = [jnp.dot(f2i, y, preferred_element_type=F32) for y in ys]
    for j, back in enumerate(backs):
        br, bi = back[0:n2], back[n2:2 * n2]
        tr, ti = twr_ref[j], twi_ref[j]
        o_ref[0, j] = (br * tr + bi * ti).astype(o_ref.dtype)
        o_ref[1, j] = (bi * tr - br * ti).astype(o_ref.dtype)


def _fft_mid(a5, kf, tb):
    bsz, _, n1h, n2, c = a5.shape
    slot = pl.BlockSpec((pl.Squeezed(), 2, FFT_T1, n2, c), lambda k, b: (b, 0, k, 0, 0))
    return pl.pallas_call(
        _fft_mid_kernel,
        grid=(n1h // FFT_T1, bsz),
        in_specs=[
            slot,
            pl.BlockSpec((FFT_T1, 2, n2, c), lambda k, b: (k, 0, 0, 0)),
            pl.BlockSpec((FFT_T1, n2, 1), lambda k, b: (k, 0, 0)),
            pl.BlockSpec((FFT_T1, n2, 1), lambda k, b: (k, 0, 0)),
            pl.BlockSpec((2 * n2, 2 * n2), lambda k, b: (0, 0)),
            pl.BlockSpec((2 * n2, 2 * n2), lambda k, b: (0, 0)),
        ],
        out_specs=slot,
        out_shape=jax.ShapeDtypeStruct(a5.shape, SLOT_DTYPE),
        compiler_params=_cparams(("parallel", "parallel")),
        name="fft_mid",
    )(a5, kf, tb["twr"], tb["twi"], tb["f2"], tb["f2inv"])


def _fft_out_kernel(f_ref, b_ref, o_ref):
    _outer_dft(f_ref[...], b_ref, o_ref)


def _fft_out(b5, f1inv):
    bsz, _, n1h, n2, c = b5.shape
    return pl.pallas_call(
        _fft_out_kernel,
        grid=(bsz, c // FFT_LANES),
        in_specs=[
            pl.BlockSpec(f1inv.shape, lambda b, j: (0, 0)),
            pl.BlockSpec((pl.Squeezed(), 2, n1h, n2, FFT_LANES), lambda b, j: (b, 0, 0, 0, j)),
        ],
        out_specs=pl.BlockSpec((1, n1h, n2, FFT_LANES), lambda b, j: (b, 0, 0, j)),
        out_shape=jax.ShapeDtypeStruct((bsz, n1h, n2, c), BF16),
        compiler_params=_cparams(("parallel", "parallel")),
        name="fft_out",
    )(f1inv, b5).reshape(bsz, n1h * n2, c)


def _hyena_conv(u, kf, tb):
    a5 = _fft_stage1(u, tb["f1"])
    b5 = _fft_mid(a5, kf, tb)
    return _fft_out(b5, tb["f1inv"])


def _merge_kernel(ya_ref, yc_ref, u_ref, x0_ref, dk_ref, ga_ref, gh_ref, x_ref, wa_ref, wh_ref,
                  wo_ref, o_ref):
    yh = x0_ref[...].astype(F32) * (yc_ref[...].astype(F32)
                                    + u_ref[...].astype(F32) * dk_ref[...])
    pa = jnp.dot(ya_ref[...], wa_ref[...], preferred_element_type=F32)
    ph = jnp.dot(yh.astype(BF16), wh_ref[...], preferred_element_type=F32)
    merged = (jax.nn.sigmoid(ga_ref[...].astype(F32)) * pa
              + jax.nn.sigmoid(gh_ref[...].astype(F32)) * ph)
    o_ref[...] = x_ref[...] + jnp.dot(merged.astype(BF16), wo_ref[...],
                                      preferred_element_type=F32)


def _merge(ya, yc, u, x0, d_skip, z, x2, wa, wh, wo):
    n, d = x2.shape
    tm = min(512, n)
    ga_blk = (D_ATTN * 3 + D_HYENA * 3) // d
    tok = pl.BlockSpec((tm, D_HYENA), lambda i: (i, 0))
    return pl.pallas_call(
        _merge_kernel,
        grid=(n // tm,),
        in_specs=[
            pl.BlockSpec((tm, D_ATTN), lambda i: (i, 0)),
            tok, tok, tok,
            pl.BlockSpec((1, D_HYENA), lambda i: (0, 0)),
            pl.BlockSpec((tm, d), lambda i: (i, ga_blk)),
            pl.BlockSpec((tm, d), lambda i: (i, ga_blk + 1)),
            pl.BlockSpec((tm, d), lambda i: (i, 0)),
            pl.BlockSpec(wa.shape, lambda i: (0, 0)),
            pl.BlockSpec(wh.shape, lambda i: (0, 0)),
            pl.BlockSpec(wo.shape, lambda i: (0, 0)),
        ],
        out_specs=pl.BlockSpec((tm, d), lambda i: (i, 0)),
        out_shape=jax.ShapeDtypeStruct((n, d), F32),
        compiler_params=_cparams(("parallel",)),
        name="merge_outproj",
    )(ya, yc, u, x0, d_skip.astype(F32).reshape(1, D_HYENA), z, z, x2, wa, wh, wo)


def _rms(x, g):
    inv = lax.rsqrt(jnp.mean(x * x, axis=-1, keepdims=True) + EPS)
    return (x * inv) * g


FFN_CHUNK = 256


def _ffn_kernel(x_ref, g_ref, wg_ref, wu_ref, wd_ref, gf_ref, o_ref):
    x = x_ref[...]
    h = _rms(x, g_ref[...]).astype(BF16)
    acc = x
    for c0 in range(0, wg_ref.shape[1], FFN_CHUNK):
        gate = jnp.dot(h, wg_ref[:, c0:c0 + FFN_CHUNK], preferred_element_type=F32)
        up = jnp.dot(h, wu_ref[:, c0:c0 + FFN_CHUNK], preferred_element_type=F32)
        act = (gate * jax.nn.sigmoid(gate) * up).astype(BF16)
        acc = acc + jnp.dot(act, wd_ref[c0:c0 + FFN_CHUNK, :], preferred_element_type=F32)
    o_ref[...] = _rms(acc, gf_ref[...])


def _ffn(x1, g, wg, wu, wd, gf):
    n, d = x1.shape
    dff = wg.shape[1]
    assert dff % FFN_CHUNK == 0
    tm = min(512, n)
    resident = pl.Buffered(1)
    return pl.pallas_call(
        _ffn_kernel,
        grid=(n // tm,),
        in_specs=[
            pl.BlockSpec((tm, d), lambda i: (i, 0)),
            pl.BlockSpec((1, d), lambda i: (0, 0)),
            pl.BlockSpec((d, dff), lambda i: (0, 0), pipeline_mode=resident),
            pl.BlockSpec((d, dff), lambda i: (0, 0), pipeline_mode=resident),
            pl.BlockSpec((dff, d), lambda i: (0, 0), pipeline_mode=resident),
            pl.BlockSpec((1, d), lambda i: (0, 0)),
        ],
        out_specs=pl.BlockSpec((tm, d), lambda i: (i, 0)),
        out_shape=jax.ShapeDtypeStruct((n, d), F32),
        compiler_params=_cparams(("parallel",)),
        name="ffn_final",
    )(x1, g.reshape(1, d), wg, wu, wd, gf.reshape(1, d))


def _layer(x2, bsz, L, p):
    n, d = x2.shape
    z = _norm_inproj(x2, p["norm_mix"], p["w_in"])
    z3 = z.reshape(bsz, L, z.shape[1])
    ya = _attention(z3, p["rpb"])
    u, x0 = _conv_gate(z3, p["conv_w"], p["conv_b"])
    tb = dict(_fft_tables(L))
    for name in ("f1", "f1inv", "f2", "f2inv"):
        tb[name] = jnp.asarray(tb[name]).astype(BF16)
    hfilt = _implicit_filters(L, *p["filt"])
    kf = _filter_spectrum(hfilt, tb)
    yc = _hyena_conv(u, kf, tb)
    x1 = _merge(ya.reshape(n, D_ATTN), yc.reshape(n, D_HYENA), u.reshape(n, D_HYENA),
                x0.reshape(n, D_HYENA), p["hyena_d"], z, x2,
                p["w_br_attn"], p["w_br_hyena"], p["w_out"])
    return x1


def _trunk(x, layers, norm_final):
    bsz, L, d = x.shape
    x2 = x.reshape(bsz * L, d)
    depth = len(layers)
    for li, p in enumerate(layers):
        x1 = _layer(x2, bsz, L, p)
        gf = norm_final if li == depth - 1 else None
        assert gf is not None, "only the last layer fuses the final norm"
        x2 = _ffn(x1, p["norm_ffn"], p["w_gate"], p["w_up"], p["w_down"], gf)
    return x2.reshape(bsz, L, d)


def kernel(x_prompt, x_sample, norm_mix, w_in, rpb, conv_w, conv_b, filt_w1, filt_b1, filt_w2,
           filt_b2, filt_w3, filt_b3, filt_w4, filt_freq, hyena_d, w_br_attn, w_br_hyena, w_out,
           norm_ffn, w_gate, w_up, w_down, norm_final):
    depth = w_in.shape[0]
    assert depth == 1
    layers = []
    col = lax.broadcasted_iota(jnp.int32, (1, w_in.shape[2]), 1)
    col_scale = jnp.where(col < D_ATTN, Q_SCALE, 1.0).astype(F32)
    for l in range(depth):
        layers.append(dict(
            norm_mix=norm_mix[l], w_in=(w_in[l] * col_scale).astype(BF16), rpb=rpb[l].reshape(-1),
            conv_w=conv_w[l], conv_b=conv_b[l],
            filt=(filt_w1[l], filt_b1[l], filt_w2[l], filt_b2[l], filt_w3[l], filt_b3[l],
                  filt_w4[l], filt_freq[l]),
            hyena_d=hyena_d[l],
            w_br_attn=w_br_attn[l].astype(BF16), w_br_hyena=w_br_hyena[l].astype(BF16),
            w_out=w_out[l].astype(BF16), norm_ffn=norm_ffn[l],
            w_gate=w_gate[l].astype(BF16), w_up=w_up[l].astype(BF16),
            w_down=w_down[l].astype(BF16),
        ))
    y_prompt = _trunk(x_prompt, layers, norm_final)
    y_sample = _trunk(x_sample, layers, norm_final)
    return (y_prompt, y_sample)
```

```python
import functools
import math

import numpy as np
import jax
import jax.numpy as jnp
from jax import lax
from jax.experimental import pallas as pl
from jax.experimental.pallas import tpu as pltpu

F32 = jnp.float32
BF16 = jnp.bfloat16

GRID_W = 64
N_HEADS = 8
HEAD_DIM = 64
D_ATTN = N_HEADS * HEAD_DIM
WIN_R = 8
WIN_C = 16
D_HYENA = 512
SHORT_K = 3
FILTER_EMB = 33
FILTER_HIDDEN = 64
DECAY_TARGET = 1e-2
FAST_DECAY_PCT = 0.3
SLOW_DECAY_PCT = 1.5
EPS = 1e-6

FFT_N2 = 128
ATT_ROWS = 8
NEG_BIG = -1e30
VMEM_LIMIT = 56 * 1024 * 1024


def _cparams(sem):
    return pltpu.CompilerParams(dimension_semantics=sem, vmem_limit_bytes=VMEM_LIMIT)


def _norm_inproj_kernel(x_ref, g_ref, w_ref, o_ref, h_ref):
    @pl.when(pl.program_id(1) == 0)
    def _():
        x = x_ref[...]
        inv = lax.rsqrt(jnp.mean(x * x, axis=-1, keepdims=True) + EPS)
        h_ref[...] = ((x * inv) * g_ref[...]).astype(BF16)

    o_ref[...] = jnp.dot(h_ref[...], w_ref[...], preferred_element_type=F32).astype(o_ref.dtype)


def _norm_inproj(x2, g, w_bf16):
    n, d = x2.shape
    d_in = w_bf16.shape[1]
    tm = min(1024, n)
    tn = 2560
    return pl.pallas_call(
        _norm_inproj_kernel,
        grid=(n // tm, d_in // tn),
        in_specs=[
            pl.BlockSpec((tm, d), lambda i, j: (i, 0)),
            pl.BlockSpec((1, d), lambda i, j: (0, 0)),
            pl.BlockSpec((d, tn), lambda i, j: (0, j)),
        ],
        out_specs=pl.BlockSpec((tm, tn), lambda i, j: (i, j)),
        out_shape=jax.ShapeDtypeStruct((n, d_in), BF16),
        scratch_shapes=[pltpu.VMEM((tm, d), BF16)],
        compiler_params=_cparams(("parallel", "arbitrary")),
        name="norm_inproj",
    )(x2, g.reshape(1, d), w_bf16)


N_DR = 2 * WIN_R - 1
N_DC = 2 * WIN_C - 1
LOG2_E = 1.4426950408889634
Q_SCALE = (HEAD_DIM ** -0.5) * LOG2_E
KV_ROWS = 2 * ATT_ROWS


def _build_bias_tiles(rpb_ref, pt_ref):
    k = lax.broadcasted_iota(jnp.int32, (GRID_W, 128), 0)
    lane = lax.broadcasted_iota(jnp.int32, (GRID_W, 128), 1)
    q = lane & (GRID_W - 1)
    first_half = lane < GRID_W
    cs = jnp.clip(q - WIN_C // 2, 0, GRID_W - WIN_C)
    valid = (k >= cs) & (k < cs + WIN_C)
    dc = jnp.clip(k - q, -(WIN_C - 1), WIN_C - 1) + (WIN_C - 1)

    def body(idx, carry):
        p = idx // N_DR
        d = idx % N_DR
        base0 = ((2 * p) * N_DR + d) * N_DC
        base1 = ((2 * p + 1) * N_DR + d) * N_DC
        acc = jnp.zeros((GRID_W, 128), F32)
        for off in range(N_DC):
            val = jnp.where(first_half, rpb_ref[base0 + off], rpb_ref[base1 + off])
            acc = jnp.where(dc == off, val, acc)
        pt_ref[idx] = jnp.where(valid, acc * LOG2_E, NEG_BIG)
        return carry

    lax.fori_loop(0, (N_HEADS // 2) * N_DR, body, 0)


def _kv_window_start(i, rows):
    return jnp.clip(i * ATT_ROWS - WIN_R // 2, 0, rows - KV_ROWS)


def _attn_kernel(rpb_ref, q_ref, k_ref, v_ref, o_ref, pt_ref, *, rows):
    b = pl.program_id(0)
    i = pl.program_id(1)

    @pl.when((b == 0) & (i == 0))
    def _():
        _build_bias_tiles(rpb_ref, pt_ref)

    lane = lax.broadcasted_iota(jnp.int32, (GRID_W, 128), 1)
    first_half = lane < HEAD_DIM
    ones = jnp.ones((WIN_R * GRID_W, 128), BF16)
    nkeys = WIN_R * GRID_W
    wstart = _kv_window_start(i, rows)
    nt_dims = (((1,), (1,)), ((), ()))

    def row_body(rr, carry):
        r = i * ATT_ROWS + rr
        rs = jnp.clip(r - WIN_R // 2, 0, rows - WIN_R)
        s = rs - r + (WIN_R - 1)
        koff = pl.multiple_of((rs - wstart) * GRID_W, GRID_W)
        qoff = pl.multiple_of(rr * GRID_W, GRID_W)
        q = q_ref[pl.ds(qoff, GRID_W), :]
        pairs = range(N_HEADS // 2)
        sts = []
        for p in pairs:
            qp = q[:, 128 * p:128 * (p + 1)]
            zero = jnp.zeros_like(qp)
            wt = jnp.concatenate([jnp.where(first_half, qp, zero),
                                  jnp.where(first_half, zero, qp)], axis=0)
            kp = k_ref[pl.ds(koff, nkeys), 128 * p:128 * (p + 1)]
            sts.append(lax.dot_general(kp, wt, nt_dims, preferred_element_type=F32))
        pms = []
        for p in pairs:
            st = sts[p]
            ch = [st[GRID_W * c:GRID_W * (c + 1)] + pt_ref[p * N_DR + s + c] for c in range(WIN_R)]
            m = ch[0]
            for c in range(1, WIN_R):
                m = jnp.maximum(m, ch[c])
            m = jnp.max(m, axis=0, keepdims=True)
            pt = jnp.concatenate([jnp.exp2(c - m) for c in ch], axis=0).astype(BF16)
            pms.append(pt.T)
        outs = []
        for p in pairs:
            vp = v_ref[pl.ds(koff, nkeys), 128 * p:128 * (p + 1)]
            ov = jnp.dot(pms[p], jnp.concatenate([vp, ones], axis=1), preferred_element_type=F32)
            o = ov[:, 0:128] / ov[:, 128:256]
            outs.append(jnp.where(first_half, o[0:GRID_W], o[GRID_W:2 * GRID_W]))
        o_ref[pl.ds(qoff, GRID_W), :] = jnp.concatenate(outs, axis=1).astype(o_ref.dtype)
        return carry

    lax.fori_loop(0, ATT_ROWS, row_body, 0, unroll=4)


def _attention(z3, rpb_flat):
    bsz, L, _ = z3.shape
    rows = L // GRID_W
    assert rows % ATT_ROWS == 0 and rows >= KV_ROWS
    nblk = rows // ATT_ROWS
    blk = ATT_ROWS * GRID_W

    def window(col):
        return pl.BlockSpec(
            (pl.Squeezed(), pl.Element(KV_ROWS * GRID_W), pl.Element(D_ATTN)),
            lambda b, i: (b, _kv_window_start(i, rows) * GRID_W, col * D_ATTN))

    return pl.pallas_call(
        functools.partial(_attn_kernel, rows=rows),
        grid=(bsz, nblk),
        in_specs=[
            pl.BlockSpec(memory_space=pltpu.SMEM),
            pl.BlockSpec((pl.Squeezed(), blk, D_ATTN), lambda b, i: (b, i, 0)),
            window(1),
            window(2),
        ],
        out_specs=pl.BlockSpec((pl.Squeezed(), blk, D_ATTN), lambda b, i: (b, i, 0)),
        out_shape=jax.ShapeDtypeStruct((bsz, L, D_ATTN), BF16),
        scratch_shapes=[pltpu.VMEM(((N_HEADS // 2) * N_DR, GRID_W, 128), F32)],
        compiler_params=_cparams(("arbitrary", "arbitrary")),
        name="nbr_attention",
    )(rpb_flat, z3, z3, z3)


def _conv_gate_kernel(xm_ref, xp_ref, xn_ref, w_ref, b_ref, u_ref, x0_ref, *, nt):
    i = pl.program_id(1)
    x = xm_ref[0].astype(F32)
    tm = x.shape[0]
    row = lax.broadcasted_iota(jnp.int32, x.shape, 0)
    halo = xp_ref.shape[1]
    prev_row = jnp.where(i == 0, 0.0, xp_ref[0].astype(F32)[halo - 1:halo, :])
    next_row = jnp.where(i == nt - 1, 0.0, xn_ref[0].astype(F32)[0:1, :])
    xm1 = jnp.where(row == 0, prev_row, pltpu.roll(x, 1, 0))
    xp1 = jnp.where(row == tm - 1, next_row, pltpu.roll(x, tm - 1, 0))
    y = xm1 * w_ref[0:1, :] + x * w_ref[1:2, :] + xp1 * w_ref[2:3, :] + b_ref[...]
    c = D_HYENA
    x0_ref[0] = y[:, 0:c].astype(x0_ref.dtype)
    u_ref[0] = (y[:, c:2 * c] * y[:, 2 * c:3 * c]).astype(u_ref.dtype)


def _conv_gate(z3, conv_w, conv_b):
    bsz, L, _ = z3.shape
    c3 = 3 * D_HYENA
    tm = min(512, L)
    nt = L // tm
    halo = 16
    hb = tm // halo
    nh = L // halo
    return pl.pallas_call(
        functools.partial(_conv_gate_kernel, nt=nt),
        grid=(bsz, nt),
        in_specs=[
            pl.BlockSpec((1, tm, c3), lambda b, i: (b, i, 1)),
            pl.BlockSpec((1, halo, c3), lambda b, i: (b, jnp.maximum(i * hb - 1, 0), 1)),
            pl.BlockSpec((1, halo, c3), lambda b, i: (b, jnp.minimum((i + 1) * hb, nh - 1), 1)),
            pl.BlockSpec((SHORT_K, c3), lambda b, i: (0, 0)),
            pl.BlockSpec((1, c3), lambda b, i: (0, 0)),
        ],
        out_specs=[
            pl.BlockSpec((1, tm, D_HYENA), lambda b, i: (b, i, 0)),
            pl.BlockSpec((1, tm, D_HYENA), lambda b, i: (b, i, 0)),
        ],
        out_shape=[
            jax.ShapeDtypeStruct((bsz, L, D_HYENA), BF16),
            jax.ShapeDtypeStruct((bsz, L, D_HYENA), BF16),
        ],
        compiler_params=_cparams(("parallel", "parallel")),
        name="conv_gate",
    )(z3, z3, z3, conv_w, conv_b.reshape(1, c3))


TWO_OVER_PI = 0.6366197723675814
PIO2_1 = 1.5703125
PIO2_2 = 4.837512969970703125e-4
PIO2_3 = 7.54978995489188216e-8
TRIG_FAST_LIMIT = 4096.0


def _quadrant_value(x, shift):
    kf = jnp.floor(x * TWO_OVER_PI + 0.5)
    r = ((x - kf * PIO2_1) - kf * PIO2_2) - kf * PIO2_3
    z = r * r
    s = r + r * z * (-1.6666654611e-1 + z * (8.3321608736e-3 + z * -1.9515295891e-4))
    c = (1.0 - 0.5 * z) + z * z * (4.166664568298827e-2
                                   + z * (-1.388731625493765e-3 + z * 2.443315711809948e-5))
    k = kf + shift
    half = jnp.floor(k * 0.5)
    odd = k - 2.0 * half
    flip = half - 2.0 * jnp.floor(half * 0.5)
    return (s + odd * (c - s)) * (1.0 - 2.0 * flip)


def _sin_small(x):
    return _quadrant_value(x, 0.0)


def _cos_small(x):
    return _quadrant_value(x, 1.0)


def _sin(x):
    return lax.cond(jnp.max(jnp.abs(x)) < TRIG_FAST_LIMIT, _sin_small, jnp.sin, x)


def _cos(x):
    return lax.cond(jnp.max(jnp.abs(x)) < TRIG_FAST_LIMIT, _cos_small, jnp.cos, x)


def _dot_split(a, b):
    a_hi = a.astype(BF16)
    b_hi = b.astype(BF16)
    a_lo = (a - a_hi.astype(F32)).astype(BF16)
    b_lo = (b - b_hi.astype(F32)).astype(BF16)
    return (jnp.dot(a_hi, b_hi, preferred_element_type=F32)
            + (jnp.dot(a_hi, b_lo, preferred_element_type=F32)
               + jnp.dot(a_lo, b_hi, preferred_element_type=F32)))


def _filter_kernel(fb_ref, w1t_ref, w1c_ref, w1s_ref, b1_ref, w2_ref, b2_ref, w3_ref, b3_ref,
                   w4_ref, fr_ref, dl_ref, o_ref, *, L, tp):
    j = pl.program_id(0)
    pos = (lax.broadcasted_iota(jnp.int32, (1, tp), 1) + j * tp).astype(F32)
    t = pos / (L - 1.0)
    omega = (2.0 * math.pi) * pos / float(L)
    ang = fb_ref[...] * omega
    fr = fr_ref[...]
    pre = (w1t_ref[...] * t
           + _dot_split(w1c_ref[...], _cos(ang)) - _dot_split(w1s_ref[...], _sin(ang)))
    h = _sin(fr * (pre + b1_ref[...]))
    h = _sin(fr * (_dot_split(w2_ref[...], h) + b2_ref[...]))
    h = _sin(fr * (_dot_split(w3_ref[...], h) + b3_ref[...]))
    out = _dot_split(h.T, w4_ref[...])
    tcol = (lax.broadcasted_iota(jnp.int32, (tp, 1), 0) + j * tp).astype(F32) / (L - 1.0)
    decay = jnp.exp(-tcol * dl_ref[...])
    c = D_HYENA
    o_ref[:, 0:c] = out[:, 0:c] * decay
    o_ref[:, c:2 * c] = out[:, c:2 * c] * decay


def _implicit_filters(L, w1, b1, w2, b2, w3, b3, w4, freq):
    bands = (FILTER_EMB - 1) // 2
    fh = FILTER_HIDDEN
    tp = min(512, L)
    fb = jnp.linspace(1e-4, bands - 1, bands, dtype=F32).reshape(bands, 1)
    max_decay = math.log(DECAY_TARGET) / FAST_DECAY_PCT
    min_decay = math.log(DECAY_TARGET) / SLOW_DECAY_PCT
    deltas = jnp.abs(jnp.linspace(min_decay, max_decay, D_HYENA, dtype=F32)).reshape(1, D_HYENA)
    w1 = w1.astype(F32)
    args = (
        fb,
        w1[0:1, :].T,
        w1[1:1 + bands, :].T,
        w1[1 + bands:, :].T,
        b1.astype(F32).reshape(fh, 1),
        w2.astype(F32).T, b2.astype(F32).reshape(fh, 1),
        w3.astype(F32).T, b3.astype(F32).reshape(fh, 1),
        w4.astype(F32),
        freq.astype(F32).reshape(fh, 1),
        deltas,
    )

    def full(a):
        return pl.BlockSpec(a.shape, lambda j: (0,) * a.ndim)

    return pl.pallas_call(
        functools.partial(_filter_kernel, L=L, tp=tp),
        grid=(L // tp,),
        in_specs=[full(a) for a in args],
        out_specs=pl.BlockSpec((tp, 2 * D_HYENA), lambda j: (j, 0)),
        out_shape=jax.ShapeDtypeStruct((L, 2 * D_HYENA), F32),
        compiler_params=_cparams(("parallel",)),
        name="implicit_filter",
    )(*args)


FFT_LANES = 128
FFT_T1 = 4


@functools.lru_cache(maxsize=None)
def _fft_tables(L):
    n = 2 * L
    n2 = FFT_N2
    n1 = n // n2
    n1h = n1 // 2
    odd = 2 * np.arange(n1h) + 1
    th = 2.0 * np.pi * (np.outer(odd, np.arange(n1h)) % (2 * n1)) / (2 * n1)
    f1 = np.concatenate([np.cos(th), -np.sin(th)], axis=0)
    tw = 2.0 * np.pi * (np.outer(odd, np.arange(n2)) % (2 * n)) / (2 * n)
    twr = np.cos(tw)[:, :, None]
    twi = -np.sin(tw)[:, :, None]
    t2 = 2.0 * np.pi * (np.outer(np.arange(n2), np.arange(n2)) % n2) / n2
    fr, fi = np.cos(t2), -np.sin(t2)
    f2 = np.block([[fr, -fi], [fi, fr]])
    f2inv = np.block([[fr, fi], [-fi, fr]])
    f1inv = np.concatenate([np.cos(th.T), -np.sin(th.T)], axis=1) * (2.0 / n)
    return dict(
        n1=n1, n1h=n1h,
        f1=np.asarray(f1, np.float32), f1inv=np.asarray(f1inv, np.float32),
        f2=np.asarray(f2, np.float32), f2inv=np.asarray(f2inv, np.float32),
        twr=np.asarray(twr, np.float32), twi=np.asarray(twi, np.float32),
    )


FFT_A = 16
SLOT_DTYPE = BF16


def _outer_dft(f, src_ref, dst_ref):
    n_in, n1h = src_ref.shape[0], src_ref.shape[1]
    n_out = dst_ref.shape[0]

    def body(i, carry):
        a0 = pl.multiple_of(i * FFT_A, FFT_A)
        ts = [jnp.swapaxes(src_ref[p, :, pl.ds(a0, FFT_A), :].astype(F32), 0, 1)
              for p in range(n_in)]
        outs = [[] for _ in range(n_out)]
        for j in range(0, FFT_A, 2):
            rhs = jnp.concatenate([jnp.concatenate([t[j], t[j + 1]], axis=1) for t in ts],
                                  axis=0).astype(BF16)
            res = jnp.dot(f, rhs, preferred_element_type=F32)
            for q in range(n_out):
                blk = res[q * n1h:(q + 1) * n1h]
                outs[q] += [blk[:, 0:FFT_LANES], blk[:, FFT_LANES:2 * FFT_LANES]]
        for q in range(n_out):
            dst_ref[q, :, pl.ds(a0, FFT_A), :] = jnp.swapaxes(
                jnp.stack(outs[q], axis=0), 0, 1).astype(dst_ref.dtype)
        return carry

    lax.fori_loop(0, FFT_N2 // FFT_A, body, 0, unroll=2)


def _fft_stage1_kernel(f_ref, u_ref, o_ref):
    _outer_dft(f_ref[...], u_ref, o_ref)


def _fft_stage1(u3, f1):
    bsz, L, c = u3.shape
    n1h = L // FFT_N2
    return pl.pallas_call(
        _fft_stage1_kernel,
        grid=(bsz, c // FFT_LANES),
        in_specs=[
            pl.BlockSpec(f1.shape, lambda b, j: (0, 0)),
            pl.BlockSpec((1, n1h, FFT_N2, FFT_LANES), lambda b, j: (b, 0, 0, j)),
        ],
        out_specs=pl.BlockSpec((pl.Squeezed(), 2, n1h, FFT_N2, FFT_LANES),
                               lambda b, j: (b, 0, 0, 0, j)),
        out_shape=jax.ShapeDtypeStruct((bsz, 2, n1h, FFT_N2, c), SLOT_DTYPE),
        compiler_params=_cparams(("parallel", "parallel")),
        name="fft_stage1",
    )(f1, u3.reshape(bsz, n1h, FFT_N2, c))


def _twiddle(ar, ai, tr, ti):
    return ar * tr - ai * ti, ar * ti + ai * tr


def _stage2_inputs(a_ref, twr_ref, twi_ref):
    xs = []
    for j in range(FFT_T1):
        xr, xi = _twiddle(a_ref[0, j].astype(F32), a_ref[1, j].astype(F32),
                          twr_ref[j], twi_ref[j])
        xs.append(jnp.concatenate([xr, xi], axis=0).astype(BF16))
    return xs


def _filter_spectrum_kernel(a_ref, twr_ref, twi_ref, f2_ref, o_ref):
    n2, c = FFT_N2, D_HYENA
    f2 = f2_ref[...]
    bigs = [jnp.dot(f2, x, preferred_element_type=F32)
            for x in _stage2_inputs(a_ref, twr_ref, twi_ref)]
    for j, big in enumerate(bigs):
        o_ref[j, 0] = big[0:n2, 0:c] + big[0:n2, c:2 * c]
        o_ref[j, 1] = big[n2:2 * n2, 0:c] - big[n2:2 * n2, c:2 * c]


def _filter_spectrum(hfilt, tb):
    L, c2 = hfilt.shape
    n1h = tb["n1h"]
    a5 = _fft_stage1(hfilt.reshape(1, L, c2), tb["f1"])
    return pl.pallas_call(
        _filter_spectrum_kernel,
        grid=(n1h // FFT_T1,),
        in_specs=[
            pl.BlockSpec((pl.Squeezed(), 2, FFT_T1, FFT_N2, c2), lambda k: (0, 0, k, 0, 0)),
            pl.BlockSpec((FFT_T1, FFT_N2, 1), lambda k: (k, 0, 0)),
            pl.BlockSpec((FFT_T1, FFT_N2, 1), lambda k: (k, 0, 0)),
            pl.BlockSpec((2 * FFT_N2, 2 * FFT_N2), lambda k: (0, 0)),
        ],
        out_specs=pl.BlockSpec((FFT_T1, 2, FFT_N2, D_HYENA), lambda k: (k, 0, 0, 0)),
        out_shape=jax.ShapeDtypeStruct((n1h, 2, FFT_N2, D_HYENA), F32),
        compiler_params=_cparams(("parallel",)),
        name="filter_spectrum",
    )(a5, tb["twr"], tb["twi"], tb["f2"])


def _fft_mid_kernel(a_ref, kf_ref, twr_ref, twi_ref, f2_ref, f2i_ref, o_ref):
    n2 = FFT_N2
    f2, f2i = f2_ref[...], f2i_ref[...]
    bigs = [jnp.dot(f2, x, preferred_element_type=F32)
            for x in _stage2_inputs(a_ref, twr_ref, twi_ref)]
    ys = []
    for j, big in enumerate(bigs):
        sr, si = big[0:n2], big[n2:2 * n2]
        kr, ki = kf_ref[j, 0], kf_ref[j, 1]
        ys.append(jnp.concatenate([sr * kr - si * ki, sr * ki + si * kr], axis=0).astype(BF16))
    backs = [jnp.dot(f2i, y, preferred_element_type=F32) for y in ys]
    for j, back in enumerate(backs):
        br, bi = back[0:n2], back[n2:2 * n2]
        tr, ti = twr_ref[j], twi_ref[j]
        o_ref[0, j] = (br * tr + bi * ti).astype(o_ref.dtype)
        o_ref[1, j] = (bi * tr - br * ti).astype(o_ref.dtype)


def _fft_mid(a5, kf, tb):
    bsz, _, n1h, n2, c = a5.shape
    slot = pl.BlockSpec((pl.Squeezed(), 2, FFT_T1, n2, c), lambda k, b: (b, 0, k, 0, 0))
    return pl.pallas_call(
        _fft_mid_kernel,
        grid=(n1h // FFT_T1, bsz),
        in_specs=[
            slot,
            pl.BlockSpec((FFT_T1, 2, n2, c), lambda k, b: (k, 0, 0, 0)),
            pl.BlockSpec((FFT_T1, n2, 1), lambda k, b: (k, 0, 0)),
            pl.BlockSpec((FFT_T1, n2, 1), lambda k, b: (k, 0, 0)),
            pl.BlockSpec((2 * n2, 2 * n2), lambda k, b: (0, 0)),
            pl.BlockSpec((2 * n2, 2 * n2), lambda k, b: (0, 0)),
        ],
        out_specs=slot,
        out_shape=jax.ShapeDtypeStruct(a5.shape, SLOT_DTYPE),
        compiler_params=_cparams(("parallel", "parallel")),
        name="fft_mid",
    )(a5, kf, tb["twr"], tb["twi"], tb["f2"], tb["f2inv"])


def _fft_out_kernel(f_ref, b_ref, o_ref):
    _outer_dft(f_ref[...], b_ref, o_ref)


def _fft_out(b5, f1inv):
    bsz, _, n1h, n2, c = b5.shape
    return pl.pallas_call(
        _fft_out_kernel,
        grid=(bsz, c // FFT_LANES),
        in_specs=[
            pl.BlockSpec(f1inv.shape, lambda b, j: (0, 0)),
            pl.BlockSpec((pl.Squeezed(), 2, n1h, n2, FFT_LANES), lambda b, j: (b, 0, 0, 0, j)),
        ],
        out_specs=pl.BlockSpec((1, n1h, n2, FFT_LANES), lambda b, j: (b, 0, 0, j)),
        out_shape=jax.ShapeDtypeStruct((bsz, n1h, n2, c), BF16),
        compiler_params=_cparams(("parallel", "parallel")),
        name="fft_out",
    )(f1inv, b5).reshape(bsz, n1h * n2, c)


def _hyena_conv(u, kf, tb):
    a5 = _fft_stage1(u, tb["f1"])
    b5 = _fft_mid(a5, kf, tb)
    return _fft_out(b5, tb["f1inv"])


def _rms(x, g):
    inv = lax.rsqrt(jnp.mean(x * x, axis=-1, keepdims=True) + EPS)
    return (x * inv) * g


FFN_CHUNK = 256


def _merge_ffn_kernel(ya_ref, yc_ref, u_ref, x0_ref, dk_ref, ga_ref, gh_ref, x_ref, wa_ref, wh_ref,
                      wo_ref, g_ref, wg_ref, wu_ref, wd_ref, gf_ref, o_ref):
    yh = x0_ref[...].astype(F32) * (yc_ref[...].astype(F32)
                                    + u_ref[...].astype(F32) * dk_ref[...])
    pa = jnp.dot(ya_ref[...], wa_ref[...], preferred_element_type=F32)
    ph = jnp.dot(yh.astype(BF16), wh_ref[...], preferred_element_type=F32)
    merged = (jax.nn.sigmoid(ga_ref[...].astype(F32)) * pa
              + jax.nn.sigmoid(gh_ref[...].astype(F32)) * ph)
    x1 = x_ref[...] + jnp.dot(merged.astype(BF16), wo_ref[...], preferred_element_type=F32)
    h = _rms(x1, g_ref[...]).astype(BF16)
    acc = x1
    for c0 in range(0, wg_ref.shape[1], FFN_CHUNK):
        gate = jnp.dot(h, wg_ref[:, c0:c0 + FFN_CHUNK], preferred_element_type=F32)
        up = jnp.dot(h, wu_ref[:, c0:c0 + FFN_CHUNK], preferred_element_type=F32)
        act = (gate * jax.nn.sigmoid(gate) * up).astype(BF16)
        acc = acc + jnp.dot(act, wd_ref[c0:c0 + FFN_CHUNK, :], preferred_element_type=F32)
    o_ref[...] = _rms(acc, gf_ref[...])


def _merge_ffn(ya, yc, u, x0, d_skip, z, x2, wa, wh, wo, g, wg, wu, wd, gf):
    n, d = x2.shape
    dff = wg.shape[1]
    assert dff % FFN_CHUNK == 0
    tm = min(512, n)
    ga_blk = (D_ATTN * 3 + D_HYENA * 3) // d
    tok = pl.BlockSpec((tm, D_HYENA), lambda i: (i, 0))

    def resident(a):
        return pl.BlockSpec(a.shape, lambda i: (0, 0), pipeline_mode=pl.Buffered(1))

    def row(width):
        return pl.BlockSpec((1, width), lambda i: (0, 0))

    return pl.pallas_call(
        _merge_ffn_kernel,
        grid=(n // tm,),
        in_specs=[
            pl.BlockSpec((tm, D_ATTN), lambda i: (i, 0)),
            tok, tok, tok,
            row(D_HYENA),
            pl.BlockSpec((tm, d), lambda i: (i, ga_blk)),
            pl.BlockSpec((tm, d), lambda i: (i, ga_blk + 1)),
            pl.BlockSpec((tm, d), lambda i: (i, 0)),
            resident(wa), resident(wh), resident(wo),
            row(d),
            resident(wg), resident(wu), resident(wd),
            row(d),
        ],
        out_specs=pl.BlockSpec((tm, d), lambda i: (i, 0)),
        out_shape=jax.ShapeDtypeStruct((n, d), F32),
        compiler_params=_cparams(("parallel",)),
        name="merge_ffn",
    )(ya, yc, u, x0, d_skip.astype(F32).reshape(1, D_HYENA), z, z, x2, wa, wh, wo,
      g.reshape(1, d), wg, wu, wd, gf.reshape(1, d))


def _trunk(x, p, norm_final):
    bsz, L, d = x.shape
    n = bsz * L
    x2 = x.reshape(n, d)
    z = _norm_inproj(x2, p["norm_mix"], p["w_in"])
    z3 = z.reshape(bsz, L, z.shape[1])
    ya = _attention(z3, p["rpb"])
    u, x0 = _conv_gate(z3, p["conv_w"], p["conv_b"])
    tb = dict(_fft_tables(L))
    for name in ("f1", "f1inv", "f2", "f2inv"):
        tb[name] = jnp.asarray(tb[name]).astype(BF16)
    hfilt = _implicit_filters(L, *p["filt"])
    kf = _filter_spectrum(hfilt, tb)
    yc = _hyena_conv(u, kf, tb)
    out = _merge_ffn(ya.reshape(n, D_ATTN), yc.reshape(n, D_HYENA), u.reshape(n, D_HYENA),
                     x0.reshape(n, D_HYENA), p["hyena_d"], z, x2,
                     p["w_br_attn"], p["w_br_hyena"], p["w_out"],
                     p["norm_ffn"], p["w_gate"], p["w_up"], p["w_down"], norm_final)
    return out.reshape(bsz, L, d)


def kernel(x_prompt, x_sample, norm_mix, w_in, rpb, conv_w, conv_b, filt_w1, filt_b1, filt_w2,
           filt_b2, filt_w3, filt_b3, filt_w4, filt_freq, hyena_d, w_br_attn, w_br_hyena, w_out,
           norm_ffn, w_gate, w_up, w_down, norm_final):
    assert w_in.shape[0] == 1, "the layer definition has depth 1"
    col = lax.broadcasted_iota(jnp.int32, (1, w_in.shape[2]), 1)
    col_scale = jnp.where(col < D_ATTN, Q_SCALE, 1.0).astype(F32)
    p = dict(
        norm_mix=norm_mix[0], w_in=(w_in[0] * col_scale).astype(BF16), rpb=rpb[0].reshape(-1),
        conv_w=conv_w[0], conv_b=conv_b[0],
        filt=(filt_w1[0], filt_b1[0], filt_w2[0], filt_b2[0], filt_w3[0], filt_b3[0],
              filt_w4[0], filt_freq[0]),
        hyena_d=hyena_d[0],
        w_br_attn=w_br_attn[0].astype(BF16), w_br_hyena=w_br_hyena[0].astype(BF16),
        w_out=w_out[0].astype(BF16), norm_ffn=norm_ffn[0],
        w_gate=w_gate[0].astype(BF16), w_up=w_up[0].astype(BF16),
        w_down=w_down[0].astype(BF16),
    )
    return (_trunk(x_prompt, p, norm_final), _trunk(x_sample, p, norm_final))
```

```python
import functools
import math

import numpy as np
import jax
import jax.numpy as jnp
from jax import lax
from jax.experimental import pallas as pl
from jax.experimental.pallas import tpu as pltpu

F32 = jnp.float32
BF16 = jnp.bfloat16

GRID_W = 64
N_HEADS = 8
HEAD_DIM = 64
D_ATTN = N_HEADS * HEAD_DIM
WIN_R = 8
WIN_C = 16
D_HYENA = 512
SHORT_K = 3
FILTER_EMB = 33
FILTER_HIDDEN = 64
DECAY_TARGET = 1e-2
FAST_DECAY_PCT = 0.3
SLOW_DECAY_PCT = 1.5
EPS = 1e-6

FFT_N2 = 128
ATT_ROWS = 8
NEG_BIG = -1e30
VMEM_LIMIT = 56 * 1024 * 1024


def _cparams(sem):
    return pltpu.CompilerParams(dimension_semantics=sem, vmem_limit_bytes=VMEM_LIMIT)


IN_CHUNK = 512
HALO = 8


def _inproj_conv_kernel(x_ref, xp_ref, xn_ref, g_ref, wqkv_ref, why_ref, wgt_ref, cw_ref, cb_ref,
                        qkv_ref, u_ref, x0_ref, gt_ref, *, tiles_per_seq):
    i = pl.program_id(0)
    g = g_ref[...]
    h = _rms(x_ref[...], g).astype(BF16)
    hh = _rms(jnp.concatenate([xp_ref[...], xn_ref[...]], axis=0), g).astype(BF16)
    for w_ref, o_ref in ((wqkv_ref, qkv_ref), (wgt_ref, gt_ref)):
        for c0 in range(0, w_ref.shape[1], IN_CHUNK):
            o_ref[:, c0:c0 + IN_CHUNK] = jnp.dot(
                h, w_ref[:, c0:c0 + IN_CHUNK], preferred_element_type=F32).astype(o_ref.dtype)
    tm = h.shape[0]
    c = D_HYENA
    row = lax.broadcasted_iota(jnp.int32, (tm, c), 0)
    first = (i % tiles_per_seq) == 0
    last = (i % tiles_per_seq) == tiles_per_seq - 1
    ys = []
    for k in range(3):
        w = why_ref[:, k * c:(k + 1) * c]
        hy = jnp.dot(h, w, preferred_element_type=F32)
        hyh = jnp.dot(hh, w, preferred_element_type=F32)
        prev_row = jnp.where(first, 0.0, hyh[HALO - 1:HALO])
        next_row = jnp.where(last, 0.0, hyh[HALO:HALO + 1])
        xm1 = jnp.where(row == 0, prev_row, pltpu.roll(hy, 1, 0))
        xp1 = jnp.where(row == tm - 1, next_row, pltpu.roll(hy, tm - 1, 0))
        cw = cw_ref[:, k * c:(k + 1) * c]
        ys.append(xm1 * cw[0:1] + hy * cw[1:2] + xp1 * cw[2:3] + cb_ref[:, k * c:(k + 1) * c])
    x0_ref[...] = ys[0].astype(x0_ref.dtype)
    u_ref[...] = (ys[1] * ys[2]).astype(u_ref.dtype)


def _inproj_conv(x2, L, g, w_bf16, conv_w, conv_b):
    n, d = x2.shape
    n_qkv, n_hy = 3 * D_ATTN, 3 * D_HYENA
    tm = min(1024, L)
    nh = n // HALO
    hb = tm // HALO

    def resident(a):
        return pl.BlockSpec(a.shape, lambda i: (0, 0), pipeline_mode=pl.Buffered(1))

    wqkv, why, wgt = w_bf16[:, :n_qkv], w_bf16[:, n_qkv:n_qkv + n_hy], w_bf16[:, n_qkv + n_hy:]
    widths = (n_qkv, D_HYENA, D_HYENA, wgt.shape[1])
    return pl.pallas_call(
        functools.partial(_inproj_conv_kernel, tiles_per_seq=L // tm),
        grid=(n // tm,),
        in_specs=[
            pl.BlockSpec((tm, d), lambda i: (i, 0)),
            pl.BlockSpec((HALO, d), lambda i: (jnp.maximum(i * hb - 1, 0), 0)),
            pl.BlockSpec((HALO, d), lambda i: (jnp.minimum((i + 1) * hb, nh - 1), 0)),
            pl.BlockSpec((1, d), lambda i: (0, 0)),
            resident(wqkv), resident(why), resident(wgt),
            pl.BlockSpec((SHORT_K, n_hy), lambda i: (0, 0)),
            pl.BlockSpec((1, n_hy), lambda i: (0, 0)),
        ],
        out_specs=[pl.BlockSpec((tm, w), lambda i: (i, 0)) for w in widths],
        out_shape=[jax.ShapeDtypeStruct((n, w), BF16) for w in widths],
        compiler_params=_cparams(("parallel",)),
        name="inproj_conv",
    )(x2, x2, x2, g.reshape(1, d), wqkv, why, wgt, conv_w, conv_b.reshape(1, n_hy))


N_DR = 2 * WIN_R - 1
N_DC = 2 * WIN_C - 1
LOG2_E = 1.4426950408889634
Q_SCALE = (HEAD_DIM ** -0.5) * LOG2_E
KV_ROWS = 2 * ATT_ROWS


def _build_bias_tiles(rpb_ref, pt_ref):
    k = lax.broadcasted_iota(jnp.int32, (GRID_W, 128), 0)
    lane = lax.broadcasted_iota(jnp.int32, (GRID_W, 128), 1)
    q = lane & (GRID_W - 1)
    first_half = lane < GRID_W
    cs = jnp.clip(q - WIN_C // 2, 0, GRID_W - WIN_C)
    valid = (k >= cs) & (k < cs + WIN_C)
    dc = jnp.clip(k - q, -(WIN_C - 1), WIN_C - 1) + (WIN_C - 1)

    def body(idx, carry):
        p = idx // N_DR
        d = idx % N_DR
        base0 = ((2 * p) * N_DR + d) * N_DC
        base1 = ((2 * p + 1) * N_DR + d) * N_DC
        acc = jnp.zeros((GRID_W, 128), F32)
        for off in range(N_DC):
            val = jnp.where(first_half, rpb_ref[base0 + off], rpb_ref[base1 + off])
            acc = jnp.where(dc == off, val, acc)
        pt_ref[idx] = jnp.where(valid, acc * LOG2_E, NEG_BIG)
        return carry

    lax.fori_loop(0, (N_HEADS // 2) * N_DR, body, 0)


def _kv_window_start(i, rows):
    return jnp.clip(i * ATT_ROWS - WIN_R // 2, 0, rows - KV_ROWS)


def _attn_kernel(rpb_ref, q_ref, k_ref, v_ref, o_ref, pt_ref, *, rows):
    b = pl.program_id(0)
    i = pl.program_id(1)

    @pl.when((b == 0) & (i == 0))
    def _():
        _build_bias_tiles(rpb_ref, pt_ref)

    lane = lax.broadcasted_iota(jnp.int32, (GRID_W, 128), 1)
    first_half = lane < HEAD_DIM
    ones = jnp.ones((WIN_R * GRID_W, 128), BF16)
    nkeys = WIN_R * GRID_W
    wstart = _kv_window_start(i, rows)
    nt_dims = (((1,), (1,)), ((), ()))

    def row_body(rr, carry):
        r = i * ATT_ROWS + rr
        rs = jnp.clip(r - WIN_R // 2, 0, rows - WIN_R)
        s = rs - r + (WIN_R - 1)
        koff = pl.multiple_of((rs - wstart) * GRID_W, GRID_W)
        qoff = pl.multiple_of(rr * GRID_W, GRID_W)
        q = q_ref[pl.ds(qoff, GRID_W), :]
        pairs = range(N_HEADS // 2)
        sts = []
        for p in pairs:
            qp = q[:, 128 * p:128 * (p + 1)]
            zero = jnp.zeros_like(qp)
            wt = jnp.concatenate([jnp.where(first_half, qp, zero),
                                  jnp.where(first_half, zero, qp)], axis=0)
            kp = k_ref[pl.ds(koff, nkeys), 128 * p:128 * (p + 1)]
            sts.append(lax.dot_general(kp, wt, nt_dims, preferred_element_type=F32))
        pms = []
        for p in pairs:
            st = sts[p]
            ch = [st[GRID_W * c:GRID_W * (c + 1)] + pt_ref[p * N_DR + s + c] for c in range(WIN_R)]
            m = ch[0]
            for c in range(1, WIN_R):
                m = jnp.maximum(m, ch[c])
            m = jnp.max(m, axis=0, keepdims=True)
            pt = jnp.concatenate([jnp.exp2(c - m) for c in ch], axis=0).astype(BF16)
            pms.append(pt.T)
        outs = []
        for p in pairs:
            vp = v_ref[pl.ds(koff, nkeys), 128 * p:128 * (p + 1)]
            ov = jnp.dot(pms[p], jnp.concatenate([vp, ones], axis=1), preferred_element_type=F32)
            o = ov[:, 0:128] / ov[:, 128:256]
            outs.append(jnp.where(first_half, o[0:GRID_W], o[GRID_W:2 * GRID_W]))
        o_ref[pl.ds(qoff, GRID_W), :] = jnp.concatenate(outs, axis=1).astype(o_ref.dtype)
        return carry

    lax.fori_loop(0, ATT_ROWS, row_body, 0, unroll=4)


def _attention(z3, rpb_flat):
    bsz, L, _ = z3.shape
    rows = L // GRID_W
    assert rows % ATT_ROWS == 0 and rows >= KV_ROWS
    nblk = rows // ATT_ROWS
    blk = ATT_ROWS * GRID_W

    def window(col):
        return pl.BlockSpec(
            (pl.Squeezed(), pl.Element(KV_ROWS * GRID_W), pl.Element(D_ATTN)),
            lambda b, i: (b, _kv_window_start(i, rows) * GRID_W, col * D_ATTN))

    return pl.pallas_call(
        functools.partial(_attn_kernel, rows=rows),
        grid=(bsz, nblk),
        in_specs=[
            pl.BlockSpec(memory_space=pltpu.SMEM),
            pl.BlockSpec((pl.Squeezed(), blk, D_ATTN), lambda b, i: (b, i, 0)),
            window(1),
            window(2),
        ],
        out_specs=pl.BlockSpec((pl.Squeezed(), blk, D_ATTN), lambda b, i: (b, i, 0)),
        out_shape=jax.ShapeDtypeStruct((bsz, L, D_ATTN), BF16),
        scratch_shapes=[pltpu.VMEM(((N_HEADS // 2) * N_DR, GRID_W, 128), F32)],
        compiler_params=_cparams(("arbitrary", "arbitrary")),
        name="nbr_attention",
    )(rpb_flat, z3, z3, z3)


TWO_OVER_PI = 0.6366197723675814
PIO2_1 = 1.5703125
PIO2_2 = 4.837512969970703125e-4
PIO2_3 = 7.54978995489188216e-8
TRIG_FAST_LIMIT = 4096.0


def _quadrant_value(x, shift):
    kf = jnp.floor(x * TWO_OVER_PI + 0.5)
    r = ((x - kf * PIO2_1) - kf * PIO2_2) - kf * PIO2_3
    z = r * r
    s = r + r * z * (-1.6666654611e-1 + z * (8.3321608736e-3 + z * -1.9515295891e-4))
    c = (1.0 - 0.5 * z) + z * z * (4.166664568298827e-2
                                   + z * (-1.388731625493765e-3 + z * 2.443315711809948e-5))
    k = kf + shift
    half = jnp.floor(k * 0.5)
    odd = k - 2.0 * half
    flip = half - 2.0 * jnp.floor(half * 0.5)
    return (s + odd * (c - s)) * (1.0 - 2.0 * flip)


def _sin_small(x):
    return _quadrant_value(x, 0.0)


def _cos_small(x):
    return _quadrant_value(x, 1.0)


def _sin(x):
    return lax.cond(jnp.max(jnp.abs(x)) < TRIG_FAST_LIMIT, _sin_small, jnp.sin, x)


def _cos(x):
    return lax.cond(jnp.max(jnp.abs(x)) < TRIG_FAST_LIMIT, _cos_small, jnp.cos, x)


def _dot_split(a, b):
    a_hi = a.astype(BF16)
    b_hi = b.astype(BF16)
    a_lo = (a - a_hi.astype(F32)).astype(BF16)
    b_lo = (b - b_hi.astype(F32)).astype(BF16)
    return (jnp.dot(a_hi, b_hi, preferred_element_type=F32)
            + (jnp.dot(a_hi, b_lo, preferred_element_type=F32)
               + jnp.dot(a_lo, b_hi, preferred_element_type=F32)))


def _filter_kernel(fb_ref, w1t_ref, w1c_ref, w1s_ref, b1_ref, w2_ref, b2_ref, w3_ref, b3_ref,
                   w4_ref, fr_ref, dl_ref, o_ref, *, L, tp):
    j = pl.program_id(0)
    pos = (lax.broadcasted_iota(jnp.int32, (1, tp), 1) + j * tp).astype(F32)
    t = pos / (L - 1.0)
    omega = (2.0 * math.pi) * pos / float(L)
    ang = fb_ref[...] * omega
    fr = fr_ref[...]
    pre = (w1t_ref[...] * t
           + _dot_split(w1c_ref[...], _cos(ang)) - _dot_split(w1s_ref[...], _sin(ang)))
    h = _sin(fr * (pre + b1_ref[...]))
    h = _sin(fr * (_dot_split(w2_ref[...], h) + b2_ref[...]))
    h = _sin(fr * (_dot_split(w3_ref[...], h) + b3_ref[...]))
    out = _dot_split(h.T, w4_ref[...])
    tcol = (lax.broadcasted_iota(jnp.int32, (tp, 1), 0) + j * tp).astype(F32) / (L - 1.0)
    decay = jnp.exp(-tcol * dl_ref[...])
    c = D_HYENA
    o_ref[:, 0:c] = out[:, 0:c] * decay
    o_ref[:, c:2 * c] = out[:, c:2 * c] * decay


def _implicit_filters(L, w1, b1, w2, b2, w3, b3, w4, freq):
    bands = (FILTER_EMB - 1) // 2
    fh = FILTER_HIDDEN
    tp = min(512, L)
    fb = jnp.linspace(1e-4, bands - 1, bands, dtype=F32).reshape(bands, 1)
    max_decay = math.log(DECAY_TARGET) / FAST_DECAY_PCT
    min_decay = math.log(DECAY_TARGET) / SLOW_DECAY_PCT
    deltas = jnp.abs(jnp.linspace(min_decay, max_decay, D_HYENA, dtype=F32)).reshape(1, D_HYENA)
    w1 = w1.astype(F32)
    args = (
        fb,
        w1[0:1, :].T,
        w1[1:1 + bands, :].T,
        w1[1 + bands:, :].T,
        b1.astype(F32).reshape(fh, 1),
        w2.astype(F32).T, b2.astype(F32).reshape(fh, 1),
        w3.astype(F32).T, b3.astype(F32).reshape(fh, 1),
        w4.astype(F32),
        freq.astype(F32).reshape(fh, 1),
        deltas,
    )

    def full(a):
        return pl.BlockSpec(a.shape, lambda j: (0,) * a.ndim)

    return pl.pallas_call(
        functools.partial(_filter_kernel, L=L, tp=tp),
        grid=(L // tp,),
        in_specs=[full(a) for a in args],
        out_specs=pl.BlockSpec((tp, 2 * D_HYENA), lambda j: (j, 0)),
        out_shape=jax.ShapeDtypeStruct((L, 2 * D_HYENA), F32),
        compiler_params=_cparams(("parallel",)),
        name="implicit_filter",
    )(*args)


FFT_LANES = 128
FFT_T1 = 4


@functools.lru_cache(maxsize=None)
def _fft_tables(L):
    n = 2 * L
    n2 = FFT_N2
    n1 = n // n2
    n1h = n1 // 2
    odd = 2 * np.arange(n1h) + 1
    th = 2.0 * np.pi * (np.outer(odd, np.arange(n1h)) % (2 * n1)) / (2 * n1)
    f1 = np.concatenate([np.cos(th), -np.sin(th)], axis=0)
    tw = 2.0 * np.pi * (np.outer(odd, np.arange(n2)) % (2 * n)) / (2 * n)
    twr = np.cos(tw)[:, :, None]
    twi = -np.sin(tw)[:, :, None]
    t2 = 2.0 * np.pi * (np.outer(np.arange(n2), np.arange(n2)) % n2) / n2
    fr, fi = np.cos(t2), -np.sin(t2)
    f2 = np.block([[fr, -fi], [fi, fr]])
    f2inv = np.block([[fr, fi], [-fi, fr]])
    f1inv = np.concatenate([np.cos(th.T), -np.sin(th.T)], axis=1) * (2.0 / n)
    return dict(
        n1=n1, n1h=n1h,
        f1=np.asarray(f1, np.float32), f1inv=np.asarray(f1inv, np.float32),
        f2=np.asarray(f2, np.float32), f2inv=np.asarray(f2inv, np.float32),
        twr=np.asarray(twr, np.float32), twi=np.asarray(twi, np.float32),
    )


FFT_A = 16
SLOT_DTYPE = BF16


def _outer_dft(f, src_ref, dst_ref):
    n_in, n1h = src_ref.shape[0], src_ref.shape[1]
    n_out = dst_ref.shape[0]

    def body(i, carry):
        a0 = pl.multiple_of(i * FFT_A, FFT_A)
        ts = [jnp.swapaxes(src_ref[p, :, pl.ds(a0, FFT_A), :].astype(F32), 0, 1)
              for p in range(n_in)]
        outs = [[] for _ in range(n_out)]
        for j in range(0, FFT_A, 2):
            rhs = jnp.concatenate([jnp.concatenate([t[j], t[j + 1]], axis=1) for t in ts],
                                  axis=0).astype(BF16)
            res = jnp.dot(f, rhs, preferred_element_type=F32)
            for q in range(n_out):
                blk = res[q * n1h:(q + 1) * n1h]
                outs[q] += [blk[:, 0:FFT_LANES], blk[:, FFT_LANES:2 * FFT_LANES]]
        for q in range(n_out):
            dst_ref[q, :, pl.ds(a0, FFT_A), :] = jnp.swapaxes(
                jnp.stack(outs[q], axis=0), 0, 1).astype(dst_ref.dtype)
        return carry

    lax.fori_loop(0, FFT_N2 // FFT_A, body, 0, unroll=2)


def _fft_stage1_kernel(f_ref, u_ref, o_ref):
    _outer_dft(f_ref[...], u_ref, o_ref)


def _fft_stage1(u3, f1):
    bsz, L, c = u3.shape
    n1h = L // FFT_N2
    return pl.pallas_call(
        _fft_stage1_kernel,
        grid=(bsz, c // FFT_LANES),
        in_specs=[
            pl.BlockSpec(f1.shape, lambda b, j: (0, 0)),
            pl.BlockSpec((1, n1h, FFT_N2, FFT_LANES), lambda b, j: (b, 0, 0, j)),
        ],
        out_specs=pl.BlockSpec((pl.Squeezed(), 2, n1h, FFT_N2, FFT_LANES),
                               lambda b, j: (b, 0, 0, 0, j)),
        out_shape=jax.ShapeDtypeStruct((bsz, 2, n1h, FFT_N2, c), SLOT_DTYPE),
        compiler_params=_cparams(("parallel", "parallel")),
        name="fft_stage1",
    )(f1, u3.reshape(bsz, n1h, FFT_N2, c))


def _twiddle(ar, ai, tr, ti):
    return ar * tr - ai * ti, ar * ti + ai * tr


def _stage2_inputs(a_ref, twr_ref, twi_ref):
    xs = []
    for j in range(FFT_T1):
        xr, xi = _twiddle(a_ref[0, j].astype(F32), a_ref[1, j].astype(F32),
                          twr_ref[j], twi_ref[j])
        xs.append(jnp.concatenate([xr, xi], axis=0).astype(BF16))
    return xs


def _filter_spectrum_kernel(a_ref, twr_ref, twi_ref, f2_ref, o_ref):
    n2, c = FFT_N2, D_HYENA
    f2 = f2_ref[...]
    bigs = [jnp.dot(f2, x, preferred_element_type=F32)
            for x in _stage2_inputs(a_ref, twr_ref, twi_ref)]
    for j, big in enumerate(bigs):
        o_ref[j, 0] = big[0:n2, 0:c] + big[0:n2, c:2 * c]
        o_ref[j, 1] = big[n2:2 * n2, 0:c] - big[n2:2 * n2, c:2 * c]


def _filter_spectrum(hfilt, tb):
    L, c2 = hfilt.shape
    n1h = tb["n1h"]
    a5 = _fft_stage1(hfilt.reshape(1, L, c2), tb["f1"])
    return pl.pallas_call(
        _filter_spectrum_kernel,
        grid=(n1h // FFT_T1,),
        in_specs=[
            pl.BlockSpec((pl.Squeezed(), 2, FFT_T1, FFT_N2, c2), lambda k: (0, 0, k, 0, 0)),
            pl.BlockSpec((FFT_T1, FFT_N2, 1), lambda k: (k, 0, 0)),
            pl.BlockSpec((FFT_T1, FFT_N2, 1), lambda k: (k, 0, 0)),
            pl.BlockSpec((2 * FFT_N2, 2 * FFT_N2), lambda k: (0, 0)),
        ],
        out_specs=pl.BlockSpec((FFT_T1, 2, FFT_N2, D_HYENA), lambda k: (k, 0, 0, 0)),
        out_shape=jax.ShapeDtypeStruct((n1h, 2, FFT_N2, D_HYENA), F32),
        compiler_params=_cparams(("parallel",)),
        name="filter_spectrum",
    )(a5, tb["twr"], tb["twi"], tb["f2"])


def _fft_mid_kernel(a_ref, kf_ref, twr_ref, twi_ref, f2_ref, f2i_ref, o_ref):
    n2 = FFT_N2
    f2, f2i = f2_ref[...], f2i_ref[...]
    bigs = [jnp.dot(f2, x, preferred_element_type=F32)
            for x in _stage2_inputs(a_ref, twr_ref, twi_ref)]
    ys = []
    for j, big in enumerate(bigs):
        sr, si = big[0:n2], big[n2:2 * n2]
        kr, ki = kf_ref[j, 0], kf_ref[j, 1]
        ys.append(jnp.concatenate([sr * kr - si * ki, sr * ki + si * kr], axis=0).astype(BF16))
    backs = [jnp.dot(f2i, y, preferred_element_type=F32) for y in ys]
    for j, back in enumerate(backs):
        br, bi = back[0:n2], back[n2:2 * n2]
        tr, ti = twr_ref[j], twi_ref[j]
        o_ref[0, j] = (br * tr + bi * ti).astype(o_ref.dtype)
        o_ref[1, j] = (bi * tr - br * ti).astype(o_ref.dtype)


def _fft_mid(a5, kf, tb):
    bsz, _, n1h, n2, c = a5.shape
    slot = pl.BlockSpec((pl.Squeezed(), 2, FFT_T1, n2, c), lambda k, b: (b, 0, k, 0, 0))
    return pl.pallas_call(
        _fft_mid_kernel,
        grid=(n1h // FFT_T1, bsz),
        in_specs=[
            slot,
            pl.BlockSpec((FFT_T1, 2, n2, c), lambda k, b: (k, 0, 0, 0)),
            pl.BlockSpec((FFT_T1, n2, 1), lambda k, b: (k, 0, 0)),
            pl.BlockSpec((FFT_T1, n2, 1), lambda k, b: (k, 0, 0)),
            pl.BlockSpec((2 * n2, 2 * n2), lambda k, b: (0, 0)),
            pl.BlockSpec((2 * n2, 2 * n2), lambda k, b: (0, 0)),
        ],
        out_specs=slot,
        out_shape=jax.ShapeDtypeStruct(a5.shape, SLOT_DTYPE),
        compiler_params=_cparams(("parallel", "parallel")),
        name="fft_mid",
    )(a5, kf, tb["twr"], tb["twi"], tb["f2"], tb["f2inv"])


def _fft_out_kernel(f_ref, b_ref, o_ref):
    _outer_dft(f_ref[...], b_ref, o_ref)


def _fft_out(b5, f1inv):
    bsz, _, n1h, n2, c = b5.shape
    return pl.pallas_call(
        _fft_out_kernel,
        grid=(bsz, c // FFT_LANES),
        in_specs=[
            pl.BlockSpec(f1inv.shape, lambda b, j: (0, 0)),
            pl.BlockSpec((pl.Squeezed(), 2, n1h, n2, FFT_LANES), lambda b, j: (b, 0, 0, 0, j)),
        ],
        out_specs=pl.BlockSpec((1, n1h, n2, FFT_LANES), lambda b, j: (b, 0, 0, j)),
        out_shape=jax.ShapeDtypeStruct((bsz, n1h, n2, c), BF16),
        compiler_params=_cparams(("parallel", "parallel")),
        name="fft_out",
    )(f1inv, b5).reshape(bsz, n1h * n2, c)


def _hyena_conv(u, kf, tb):
    a5 = _fft_stage1(u, tb["f1"])
    b5 = _fft_mid(a5, kf, tb)
    return _fft_out(b5, tb["f1inv"])


def _rms(x, g):
    inv = lax.rsqrt(jnp.mean(x * x, axis=-1, keepdims=True) + EPS)
    return (x * inv) * g


FFN_CHUNK = 256


def _merge_ffn_kernel(ya_ref, yc_ref, u_ref, x0_ref, dk_ref, ga_ref, gh_ref, x_ref, wa_ref, wh_ref,
                      wo_ref, g_ref, wg_ref, wu_ref, wd_ref, gf_ref, o_ref):
    yh = x0_ref[...].astype(F32) * (yc_ref[...].astype(F32)
                                    + u_ref[...].astype(F32) * dk_ref[...])
    pa = jnp.dot(ya_ref[...], wa_ref[...], preferred_element_type=F32)
    ph = jnp.dot(yh.astype(BF16), wh_ref[...], preferred_element_type=F32)
    merged = (jax.nn.sigmoid(ga_ref[...].astype(F32)) * pa
              + jax.nn.sigmoid(gh_ref[...].astype(F32)) * ph)
    x1 = x_ref[...] + jnp.dot(merged.astype(BF16), wo_ref[...], preferred_element_type=F32)
    h = _rms(x1, g_ref[...]).astype(BF16)
    acc = x1
    for c0 in range(0, wg_ref.shape[1], FFN_CHUNK):
        gate = jnp.dot(h, wg_ref[:, c0:c0 + FFN_CHUNK], preferred_element_type=F32)
        up = jnp.dot(h, wu_ref[:, c0:c0 + FFN_CHUNK], preferred_element_type=F32)
        act = (gate * jax.nn.sigmoid(gate) * up).astype(BF16)
        acc = acc + jnp.dot(act, wd_ref[c0:c0 + FFN_CHUNK, :], preferred_element_type=F32)
    o_ref[...] = _rms(acc, gf_ref[...])


def _merge_ffn(ya, yc, u, x0, d_skip, gates, x2, wa, wh, wo, g, wg, wu, wd, gf):
    n, d = x2.shape
    dff = wg.shape[1]
    assert dff % FFN_CHUNK == 0
    tm = min(512, n)
    tok = pl.BlockSpec((tm, D_HYENA), lambda i: (i, 0))

    def resident(a):
        return pl.BlockSpec(a.shape, lambda i: (0, 0), pipeline_mode=pl.Buffered(1))

    def row(width):
        return pl.BlockSpec((1, width), lambda i: (0, 0))

    return pl.pallas_call(
        _merge_ffn_kernel,
        grid=(n // tm,),
        in_specs=[
            pl.BlockSpec((tm, D_ATTN), lambda i: (i, 0)),
            tok, tok, tok,
            row(D_HYENA),
            pl.BlockSpec((tm, d), lambda i: (i, 0)),
            pl.BlockSpec((tm, d), lambda i: (i, 1)),
            pl.BlockSpec((tm, d), lambda i: (i, 0)),
            resident(wa), resident(wh), resident(wo),
            row(d),
            resident(wg), resident(wu), resident(wd),
            row(d),
        ],
        out_specs=pl.BlockSpec((tm, d), lambda i: (i, 0)),
        out_shape=jax.ShapeDtypeStruct((n, d), F32),
        compiler_params=_cparams(("parallel",)),
        name="merge_ffn",
    )(ya, yc, u, x0, d_skip.astype(F32).reshape(1, D_HYENA), gates, gates, x2, wa, wh, wo,
      g.reshape(1, d), wg, wu, wd, gf.reshape(1, d))


def _trunk(x, p, norm_final):
    bsz, L, d = x.shape
    n = bsz * L
    x2 = x.reshape(n, d)
    qkv, u, x0, gates = _inproj_conv(x2, L, p["norm_mix"], p["w_in"], p["conv_w"], p["conv_b"])
    ya = _attention(qkv.reshape(bsz, L, qkv.shape[1]), p["rpb"])
    u = u.reshape(bsz, L, D_HYENA)
    tb = dict(_fft_tables(L))
    for name in ("f1", "f1inv", "f2", "f2inv"):
        tb[name] = jnp.asarray(tb[name]).astype(BF16)
    hfilt = _implicit_filters(L, *p["filt"])
    kf = _filter_spectrum(hfilt, tb)
    yc = _hyena_conv(u, kf, tb)
    out = _merge_ffn(ya.reshape(n, D_ATTN), yc.reshape(n, D_HYENA), u.reshape(n, D_HYENA),
                     x0, p["hyena_d"], gates, x2,
                     p["w_br_attn"], p["w_br_hyena"], p["w_out"],
                     p["norm_ffn"], p["w_gate"], p["w_up"], p["w_down"], norm_final)
    return out.reshape(bsz, L, d)


def kernel(x_prompt, x_sample, norm_mix, w_in, rpb, conv_w, conv_b, filt_w1, filt_b1, filt_w2,
           filt_b2, filt_w3, filt_b3, filt_w4, filt_freq, hyena_d, w_br_attn, w_br_hyena, w_out,
           norm_ffn, w_gate, w_up, w_down, norm_final):
    assert w_in.shape[0] == 1, "the layer definition has depth 1"
    col = lax.broadcasted_iota(jnp.int32, (1, w_in.shape[2]), 1)
    col_scale = jnp.where(col < D_ATTN, Q_SCALE, 1.0).astype(F32)
    p = dict(
        norm_mix=norm_mix[0], w_in=(w_in[0] * col_scale).astype(BF16), rpb=rpb[0].reshape(-1),
        conv_w=conv_w[0], conv_b=conv_b[0],
        filt=(filt_w1[0], filt_b1[0], filt_w2[0], filt_b2[0], filt_w3[0], filt_b3[0],
              filt_w4[0], filt_freq[0]),
        hyena_d=hyena_d[0],
        w_br_attn=w_br_attn[0].astype(BF16), w_br_hyena=w_br_hyena[0].astype(BF16),
        w_out=w_out[0].astype(BF16), norm_ffn=norm_ffn[0],
        w_gate=w_gate[0].astype(BF16), w_up=w_up[0].astype(BF16),
        w_down=w_down[0].astype(BF16),
    )
    return (_trunk(x_prompt, p, norm_final), _trunk(x_sample, p, norm_final))
```

```python
import functools
import math

import numpy as np
import jax
import jax.numpy as jnp
from jax import lax
from jax.experimental import pallas as pl
from jax.experimental.pallas import tpu as pltpu

F32 = jnp.float32
BF16 = jnp.bfloat16

GRID_W = 64
N_HEADS = 8
HEAD_DIM = 64
D_ATTN = N_HEADS * HEAD_DIM
WIN_R = 8
WIN_C = 16
D_HYENA = 512
SHORT_K = 3
FILTER_EMB = 33
FILTER_HIDDEN = 64
DECAY_TARGET = 1e-2
FAST_DECAY_PCT = 0.3
SLOW_DECAY_PCT = 1.5
EPS = 1e-6

FFT_N2 = 128
ATT_ROWS = 8
NEG_BIG = -1e30
VMEM_LIMIT = 56 * 1024 * 1024


def _cparams(sem):
    return pltpu.CompilerParams(dimension_semantics=sem, vmem_limit_bytes=VMEM_LIMIT)


IN_CHUNK = 512
HALO = 8


def _inproj_conv_kernel(x_ref, xp_ref, xn_ref, g_ref, wqkv_ref, why_ref, wgt_ref, cw_ref, cb_ref,
                        qkv_ref, u_ref, x0_ref, gt_ref, *, tiles_per_seq):
    i = pl.program_id(0)
    g = g_ref[...]
    h = _rms(x_ref[...], g).astype(BF16)
    hh = _rms(jnp.concatenate([xp_ref[...], xn_ref[...]], axis=0), g).astype(BF16)
    for w_ref, o_ref in ((wqkv_ref, qkv_ref), (wgt_ref, gt_ref)):
        for c0 in range(0, w_ref.shape[1], IN_CHUNK):
            o_ref[:, c0:c0 + IN_CHUNK] = jnp.dot(
                h, w_ref[:, c0:c0 + IN_CHUNK], preferred_element_type=F32).astype(o_ref.dtype)
    tm = h.shape[0]
    c = D_HYENA
    row = lax.broadcasted_iota(jnp.int32, (tm, c), 0)
    first = (i % tiles_per_seq) == 0
    last = (i % tiles_per_seq) == tiles_per_seq - 1
    ys = []
    for k in range(3):
        w = why_ref[:, k * c:(k + 1) * c]
        hy = jnp.dot(h, w, preferred_element_type=F32)
        hyh = jnp.dot(hh, w, preferred_element_type=F32)
        prev_row = jnp.where(first, 0.0, hyh[HALO - 1:HALO])
        next_row = jnp.where(last, 0.0, hyh[HALO:HALO + 1])
        xm1 = jnp.where(row == 0, prev_row, pltpu.roll(hy, 1, 0))
        xp1 = jnp.where(row == tm - 1, next_row, pltpu.roll(hy, tm - 1, 0))
        cw = cw_ref[:, k * c:(k + 1) * c]
        ys.append(xm1 * cw[0:1] + hy * cw[1:2] + xp1 * cw[2:3] + cb_ref[:, k * c:(k + 1) * c])
    x0_ref[...] = ys[0].astype(x0_ref.dtype)
    u_ref[...] = (ys[1] * ys[2]).astype(u_ref.dtype)


def _inproj_conv(x2, L, g, w_bf16, conv_w, conv_b):
    n, d = x2.shape
    n_qkv, n_hy = 3 * D_ATTN, 3 * D_HYENA
    tm = min(1024, L)
    nh = n // HALO
    hb = tm // HALO

    def resident(a):
        return pl.BlockSpec(a.shape, lambda i: (0, 0), pipeline_mode=pl.Buffered(1))

    wqkv, why, wgt = w_bf16[:, :n_qkv], w_bf16[:, n_qkv:n_qkv + n_hy], w_bf16[:, n_qkv + n_hy:]
    widths = (n_qkv, D_HYENA, D_HYENA, wgt.shape[1])
    return pl.pallas_call(
        functools.partial(_inproj_conv_kernel, tiles_per_seq=L // tm),
        grid=(n // tm,),
        in_specs=[
            pl.BlockSpec((tm, d), lambda i: (i, 0)),
            pl.BlockSpec((HALO, d), lambda i: (jnp.maximum(i * hb - 1, 0), 0)),
            pl.BlockSpec((HALO, d), lambda i: (jnp.minimum((i + 1) * hb, nh - 1), 0)),
            pl.BlockSpec((1, d), lambda i: (0, 0)),
            resident(wqkv), resident(why), resident(wgt),
            pl.BlockSpec((SHORT_K, n_hy), lambda i: (0, 0)),
            pl.BlockSpec((1, n_hy), lambda i: (0, 0)),
        ],
        out_specs=[pl.BlockSpec((tm, w), lambda i: (i, 0)) for w in widths],
        out_shape=[jax.ShapeDtypeStruct((n, w), BF16) for w in widths],
        compiler_params=_cparams(("parallel",)),
        name="inproj_conv",
    )(x2, x2, x2, g.reshape(1, d), wqkv, why, wgt, conv_w, conv_b.reshape(1, n_hy))


N_DR = 2 * WIN_R - 1
N_DC = 2 * WIN_C - 1
LOG2_E = 1.4426950408889634
Q_SCALE = (HEAD_DIM ** -0.5) * LOG2_E
KV_ROWS = 2 * ATT_ROWS


def _build_bias_tiles(rpb_ref, pt_ref):
    k = lax.broadcasted_iota(jnp.int32, (GRID_W, 128), 0)
    lane = lax.broadcasted_iota(jnp.int32, (GRID_W, 128), 1)
    q = lane & (GRID_W - 1)
    first_half = lane < GRID_W
    cs = jnp.clip(q - WIN_C // 2, 0, GRID_W - WIN_C)
    valid = (k >= cs) & (k < cs + WIN_C)
    dc = jnp.clip(k - q, -(WIN_C - 1), WIN_C - 1) + (WIN_C - 1)

    def body(idx, carry):
        p = idx // N_DR
        d = idx % N_DR
        base0 = ((2 * p) * N_DR + d) * N_DC
        base1 = ((2 * p + 1) * N_DR + d) * N_DC
        acc = jnp.zeros((GRID_W, 128), F32)
        for off in range(N_DC):
            val = jnp.where(first_half, rpb_ref[base0 + off], rpb_ref[base1 + off])
            acc = jnp.where(dc == off, val, acc)
        pt_ref[idx] = jnp.where(valid, acc * LOG2_E, NEG_BIG)
        return carry

    lax.fori_loop(0, (N_HEADS // 2) * N_DR, body, 0)


def _kv_window_start(i, rows):
    return jnp.clip(i * ATT_ROWS - WIN_R // 2, 0, rows - KV_ROWS)


def _attn_kernel(rpb_ref, q_ref, k_ref, v_ref, o_ref, pt_ref, *, rows):
    b = pl.program_id(0)
    i = pl.program_id(1)

    @pl.when((b == 0) & (i == 0))
    def _():
        _build_bias_tiles(rpb_ref, pt_ref)

    lane = lax.broadcasted_iota(jnp.int32, (GRID_W, 128), 1)
    first_half = lane < HEAD_DIM
    ones = jnp.ones((WIN_R * GRID_W, 128), BF16)
    nkeys = WIN_R * GRID_W
    wstart = _kv_window_start(i, rows)
    nt_dims = (((1,), (1,)), ((), ()))

    def row_body(rr, carry):
        r = i * ATT_ROWS + rr
        rs = jnp.clip(r - WIN_R // 2, 0, rows - WIN_R)
        s = rs - r + (WIN_R - 1)
        koff = pl.multiple_of((rs - wstart) * GRID_W, GRID_W)
        qoff = pl.multiple_of(rr * GRID_W, GRID_W)
        q = q_ref[pl.ds(qoff, GRID_W), :]
        pairs = range(N_HEADS // 2)
        sts = []
        for p in pairs:
            qp = q[:, 128 * p:128 * (p + 1)]
            zero = jnp.zeros_like(qp)
            wt = jnp.concatenate([jnp.where(first_half, qp, zero),
                                  jnp.where(first_half, zero, qp)], axis=0)
            kp = k_ref[pl.ds(koff, nkeys), 128 * p:128 * (p + 1)]
            sts.append(lax.dot_general(kp, wt, nt_dims, preferred_element_type=F32))
        pms = []
        for p in pairs:
            st = sts[p]
            ch = [st[GRID_W * c:GRID_W * (c + 1)] + pt_ref[p * N_DR + s + c] for c in range(WIN_R)]
            m = ch[0]
            for c in range(1, WIN_R):
                m = jnp.maximum(m, ch[c])
            m = jnp.max(m, axis=0, keepdims=True)
            pt = jnp.concatenate([jnp.exp2(c - m) for c in ch], axis=0).astype(BF16)
            pms.append(pt.T)
        outs = []
        for p in pairs:
            vp = v_ref[pl.ds(koff, nkeys), 128 * p:128 * (p + 1)]
            ov = jnp.dot(pms[p], jnp.concatenate([vp, ones], axis=1), preferred_element_type=F32)
            o = ov[:, 0:128] / ov[:, 128:256]
            outs.append(jnp.where(first_half, o[0:GRID_W], o[GRID_W:2 * GRID_W]))
        o_ref[pl.ds(qoff, GRID_W), :] = jnp.concatenate(outs, axis=1).astype(o_ref.dtype)
        return carry

    lax.fori_loop(0, ATT_ROWS, row_body, 0, unroll=4)


def _attention(z3, rpb_flat):
    bsz, L, _ = z3.shape
    rows = L // GRID_W
    assert rows % ATT_ROWS == 0 and rows >= KV_ROWS
    nblk = rows // ATT_ROWS
    blk = ATT_ROWS * GRID_W

    def window(col):
        return pl.BlockSpec(
            (pl.Squeezed(), pl.Element(KV_ROWS * GRID_W), pl.Element(D_ATTN)),
            lambda b, i: (b, _kv_window_start(i, rows) * GRID_W, col * D_ATTN))

    return pl.pallas_call(
        functools.partial(_attn_kernel, rows=rows),
        grid=(bsz, nblk),
        in_specs=[
            pl.BlockSpec(memory_space=pltpu.SMEM),
            pl.BlockSpec((pl.Squeezed(), blk, D_ATTN), lambda b, i: (b, i, 0)),
            window(1),
            window(2),
        ],
        out_specs=pl.BlockSpec((pl.Squeezed(), blk, D_ATTN), lambda b, i: (b, i, 0)),
        out_shape=jax.ShapeDtypeStruct((bsz, L, D_ATTN), BF16),
        scratch_shapes=[pltpu.VMEM(((N_HEADS // 2) * N_DR, GRID_W, 128), F32)],
        compiler_params=_cparams(("arbitrary", "arbitrary")),
        name="nbr_attention",
    )(rpb_flat, z3, z3, z3)


TWO_OVER_PI = 0.6366197723675814
PIO2_1 = 1.5703125
PIO2_2 = 4.837512969970703125e-4
PIO2_3 = 7.54978995489188216e-8
TRIG_FAST_LIMIT = 4096.0


def _quadrant_value(x, shift):
    kf = jnp.floor(x * TWO_OVER_PI + 0.5)
    r = ((x - kf * PIO2_1) - kf * PIO2_2) - kf * PIO2_3
    z = r * r
    s = r + r * z * (-1.6666654611e-1 + z * (8.3321608736e-3 + z * -1.9515295891e-4))
    c = (1.0 - 0.5 * z) + z * z * (4.166664568298827e-2
                                   + z * (-1.388731625493765e-3 + z * 2.443315711809948e-5))
    k = kf + shift
    half = jnp.floor(k * 0.5)
    odd = k - 2.0 * half
    flip = half - 2.0 * jnp.floor(half * 0.5)
    return (s + odd * (c - s)) * (1.0 - 2.0 * flip)


def _sin_small(x):
    return _quadrant_value(x, 0.0)


def _cos_small(x):
    return _quadrant_value(x, 1.0)


def _sin(x):
    return lax.cond(jnp.max(jnp.abs(x)) < TRIG_FAST_LIMIT, _sin_small, jnp.sin, x)


def _cos(x):
    return lax.cond(jnp.max(jnp.abs(x)) < TRIG_FAST_LIMIT, _cos_small, jnp.cos, x)


def _hi_lo(x):
    hi = x.astype(BF16)
    return hi, (x - hi.astype(F32)).astype(BF16)


def _stack_cols(w):
    hi, lo = _hi_lo(w)
    return jnp.concatenate([hi, hi, lo], axis=1)


def _stack_rows(h):
    hi, lo = _hi_lo(h)
    return jnp.concatenate([hi, lo, hi], axis=0)


def _dot_stacked(w_stacked, h):
    return jnp.dot(w_stacked, _stack_rows(h), preferred_element_type=F32)


def _filter_kernel(fb_ref, w1t_ref, w1c_ref, w1s_ref, b1_ref, w2_ref, b2_ref, w3_ref, b3_ref,
                   w4_ref, fr_ref, dl_ref, o_ref, *, L, tp):
    j = pl.program_id(0)
    fh = FILTER_HIDDEN
    pos = (lax.broadcasted_iota(jnp.int32, (1, tp), 1) + j * tp).astype(F32)
    t = pos / (L - 1.0)
    omega = (2.0 * math.pi) * pos / float(L)
    ang = fb_ref[...] * omega
    fr = fr_ref[...]
    pre = (w1t_ref[...] * t
           + _dot_stacked(w1c_ref[...], _cos(ang)) - _dot_stacked(w1s_ref[...], _sin(ang)))
    h = _sin(fr * (pre + b1_ref[...]))
    h = _sin(fr * (_dot_stacked(w2_ref[...], h) + b2_ref[...]))
    h = _sin(fr * (_dot_stacked(w3_ref[...], h) + b3_ref[...]))
    h_hi = h.astype(BF16).astype(F32)
    tr = jnp.concatenate([h_hi, h - h_hi], axis=0).T.astype(BF16)
    lane = lax.broadcasted_iota(jnp.int32, tr.shape, 1)
    lhs = jnp.concatenate([tr, jnp.where(lane < fh, tr, jnp.zeros_like(tr))], axis=1)
    out = jnp.dot(lhs, w4_ref[...], preferred_element_type=F32)
    tcol = (lax.broadcasted_iota(jnp.int32, (tp, 1), 0) + j * tp).astype(F32) / (L - 1.0)
    decay = jnp.exp(-tcol * dl_ref[...])
    c = D_HYENA
    o_ref[:, 0:c] = out[:, 0:c] * decay
    o_ref[:, c:2 * c] = out[:, c:2 * c] * decay


def _implicit_filters(L, w1, b1, w2, b2, w3, b3, w4, freq):
    bands = (FILTER_EMB - 1) // 2
    fh = FILTER_HIDDEN
    tp = min(512, L)
    fb = jnp.linspace(1e-4, bands - 1, bands, dtype=F32).reshape(bands, 1)
    max_decay = math.log(DECAY_TARGET) / FAST_DECAY_PCT
    min_decay = math.log(DECAY_TARGET) / SLOW_DECAY_PCT
    deltas = jnp.abs(jnp.linspace(min_decay, max_decay, D_HYENA, dtype=F32)).reshape(1, D_HYENA)
    w1 = w1.astype(F32)
    w4_hi, w4_lo = _hi_lo(w4.astype(F32))
    args = (
        fb,
        w1[0:1, :].T,
        _stack_cols(w1[1:1 + bands, :].T),
        _stack_cols(w1[1 + bands:, :].T),
        b1.astype(F32).reshape(fh, 1),
        _stack_cols(w2.astype(F32).T), b2.astype(F32).reshape(fh, 1),
        _stack_cols(w3.astype(F32).T), b3.astype(F32).reshape(fh, 1),
        jnp.concatenate([w4_hi, w4_hi, w4_lo, jnp.zeros_like(w4_hi)], axis=0),
        freq.astype(F32).reshape(fh, 1),
        deltas,
    )

    def full(a):
        return pl.BlockSpec(a.shape, lambda j: (0,) * a.ndim)

    return pl.pallas_call(
        functools.partial(_filter_kernel, L=L, tp=tp),
        grid=(L // tp,),
        in_specs=[full(a) for a in args],
        out_specs=pl.BlockSpec((tp, 2 * D_HYENA), lambda j: (j, 0)),
        out_shape=jax.ShapeDtypeStruct((L, 2 * D_HYENA), F32),
        compiler_params=_cparams(("parallel",)),
        name="implicit_filter",
    )(*args)


FFT_LANES = 128
FFT_T1 = 4


@functools.lru_cache(maxsize=None)
def _fft_tables(L):
    n = 2 * L
    n2 = FFT_N2
    n1 = n // n2
    n1h = n1 // 2
    odd = 2 * np.arange(n1h) + 1
    th = 2.0 * np.pi * (np.outer(odd, np.arange(n1h)) % (2 * n1)) / (2 * n1)
    f1 = np.concatenate([np.cos(th), -np.sin(th)], axis=0)
    tw = 2.0 * np.pi * (np.outer(odd, np.arange(n2)) % (2 * n)) / (2 * n)
    twr = np.cos(tw)[:, :, None]
    twi = -np.sin(tw)[:, :, None]
    t2 = 2.0 * np.pi * (np.outer(np.arange(n2), np.arange(n2)) % n2) / n2
    fr, fi = np.cos(t2), -np.sin(t2)
    f2 = np.block([[fr, -fi], [fi, fr]])
    f2inv = np.block([[fr, fi], [-fi, fr]])
    f1inv = np.concatenate([np.cos(th.T), -np.sin(th.T)], axis=1) * (2.0 / n)
    return dict(
        n1=n1, n1h=n1h,
        f1=np.asarray(f1, np.float32), f1inv=np.asarray(f1inv, np.float32),
        f2=np.asarray(f2, np.float32), f2inv=np.asarray(f2inv, np.float32),
        twr=np.asarray(twr, np.float32), twi=np.asarray(twi, np.float32),
    )


FFT_A = 16
SLOT_DTYPE = BF16


def _outer_dft(f, src_ref, dst_ref):
    n_in, n1h = src_ref.shape[0], src_ref.shape[1]
    n_out = dst_ref.shape[0]

    def body(i, carry):
        a0 = pl.multiple_of(i * FFT_A, FFT_A)
        ts = [jnp.swapaxes(src_ref[p, :, pl.ds(a0, FFT_A), :].astype(F32), 0, 1)
              for p in range(n_in)]
        outs = [[] for _ in range(n_out)]
        for j in range(0, FFT_A, 2):
            rhs = jnp.concatenate([jnp.concatenate([t[j], t[j + 1]], axis=1) for t in ts],
                                  axis=0).astype(BF16)
            res = jnp.dot(f, rhs, preferred_element_type=F32)
            for q in range(n_out):
                blk = res[q * n1h:(q + 1) * n1h]
                outs[q] += [blk[:, 0:FFT_LANES], blk[:, FFT_LANES:2 * FFT_LANES]]
        for q in range(n_out):
            dst_ref[q, :, pl.ds(a0, FFT_A), :] = jnp.swapaxes(
                jnp.stack(outs[q], axis=0), 0, 1).astype(dst_ref.dtype)
        return carry

    lax.fori_loop(0, FFT_N2 // FFT_A, body, 0, unroll=2)


def _fft_stage1_kernel(f_ref, u_ref, o_ref):
    _outer_dft(f_ref[...], u_ref, o_ref)


def _fft_stage1(u3, f1):
    bsz, L, c = u3.shape
    n1h = L // FFT_N2
    return pl.pallas_call(
        _fft_stage1_kernel,
        grid=(bsz, c // FFT_LANES),
        in_specs=[
            pl.BlockSpec(f1.shape, lambda b, j: (0, 0)),
            pl.BlockSpec((1, n1h, FFT_N2, FFT_LANES), lambda b, j: (b, 0, 0, j)),
        ],
        out_specs=pl.BlockSpec((pl.Squeezed(), 2, n1h, FFT_N2, FFT_LANES),
                               lambda b, j: (b, 0, 0, 0, j)),
        out_shape=jax.ShapeDtypeStruct((bsz, 2, n1h, FFT_N2, c), SLOT_DTYPE),
        compiler_params=_cparams(("parallel", "parallel")),
        name="fft_stage1",
    )(f1, u3.reshape(bsz, n1h, FFT_N2, c))


def _twiddle(ar, ai, tr, ti):
    return ar * tr - ai * ti, ar * ti + ai * tr


def _stage2_inputs(a_ref, twr_ref, twi_ref):
    xs = []
    for j in range(FFT_T1):
        xr, xi = _twiddle(a_ref[0, j].astype(F32), a_ref[1, j].astype(F32),
                          twr_ref[j], twi_ref[j])
        xs.append(jnp.concatenate([xr, xi], axis=0).astype(BF16))
    return xs


def _filter_spectrum_kernel(a_ref, twr_ref, twi_ref, f2_ref, o_ref):
    n2, c = FFT_N2, D_HYENA
    f2 = f2_ref[...]
    bigs = [jnp.dot(f2, x, preferred_element_type=F32)
            for x in _stage2_inputs(a_ref, twr_ref, twi_ref)]
    for j, big in enumerate(bigs):
        o_ref[j, 0] = (big[0:n2, 0:c] + big[0:n2, c:2 * c]).astype(o_ref.dtype)
        o_ref[j, 1] = (big[n2:2 * n2, 0:c] - big[n2:2 * n2, c:2 * c]).astype(o_ref.dtype)


def _filter_spectrum(hfilt, tb):
    L, c2 = hfilt.shape
    n1h = tb["n1h"]
    a5 = _fft_stage1(hfilt.reshape(1, L, c2), tb["f1"])
    return pl.pallas_call(
        _filter_spectrum_kernel,
        grid=(n1h // FFT_T1,),
        in_specs=[
            pl.BlockSpec((pl.Squeezed(), 2, FFT_T1, FFT_N2, c2), lambda k: (0, 0, k, 0, 0)),
            pl.BlockSpec((FFT_T1, FFT_N2, 1), lambda k: (k, 0, 0)),
            pl.BlockSpec((FFT_T1, FFT_N2, 1), lambda k: (k, 0, 0)),
            pl.BlockSpec((2 * FFT_N2, 2 * FFT_N2), lambda k: (0, 0)),
        ],
        out_specs=pl.BlockSpec((FFT_T1, 2, FFT_N2, D_HYENA), lambda k: (k, 0, 0, 0)),
        out_shape=jax.ShapeDtypeStruct((n1h, 2, FFT_N2, D_HYENA), SLOT_DTYPE),
        compiler_params=_cparams(("parallel",)),
        name="filter_spectrum",
    )(a5, tb["twr"], tb["twi"], tb["f2"])


def _fft_mid_kernel(a_ref, kf_ref, twr_ref, twi_ref, f2_ref, f2i_ref, o_ref):
    n2 = FFT_N2
    f2, f2i = f2_ref[...], f2i_ref[...]
    bigs = [jnp.dot(f2, x, preferred_element_type=F32)
            for x in _stage2_inputs(a_ref, twr_ref, twi_ref)]
    ys = []
    for j, big in enumerate(bigs):
        sr, si = big[0:n2], big[n2:2 * n2]
        kr, ki = kf_ref[j, 0].astype(F32), kf_ref[j, 1].astype(F32)
        ys.append(jnp.concatenate([sr * kr - si * ki, sr * ki + si * kr], axis=0).astype(BF16))
    backs = [jnp.dot(f2i, y, preferred_element_type=F32) for y in ys]
    for j, back in enumerate(backs):
        br, bi = back[0:n2], back[n2:2 * n2]
        tr, ti = twr_ref[j], twi_ref[j]
        o_ref[0, j] = (br * tr + bi * ti).astype(o_ref.dtype)
        o_ref[1, j] = (bi * tr - br * ti).astype(o_ref.dtype)


def _fft_mid(a5, kf, tb):
    bsz, _, n1h, n2, c = a5.shape
    slot = pl.BlockSpec((pl.Squeezed(), 2, FFT_T1, n2, c), lambda k, b: (b, 0, k, 0, 0))
    return pl.pallas_call(
        _fft_mid_kernel,
        grid=(n1h // FFT_T1, bsz),
        in_specs=[
            slot,
            pl.BlockSpec((FFT_T1, 2, n2, c), lambda k, b: (k, 0, 0, 0)),
            pl.BlockSpec((FFT_T1, n2, 1), lambda k, b: (k, 0, 0)),
            pl.BlockSpec((FFT_T1, n2, 1), lambda k, b: (k, 0, 0)),
            pl.BlockSpec((2 * n2, 2 * n2), lambda k, b: (0, 0)),
            pl.BlockSpec((2 * n2, 2 * n2), lambda k, b: (0, 0)),
        ],
        out_specs=slot,
        out_shape=jax.ShapeDtypeStruct(a5.shape, SLOT_DTYPE),
        compiler_params=_cparams(("parallel", "parallel")),
        name="fft_mid",
    )(a5, kf, tb["twr"], tb["twi"], tb["f2"], tb["f2inv"])


def _fft_out_kernel(f_ref, b_ref, o_ref):
    _outer_dft(f_ref[...], b_ref, o_ref)


def _fft_out(b5, f1inv):
    bsz, _, n1h, n2, c = b5.shape
    return pl.pallas_call(
        _fft_out_kernel,
        grid=(bsz, c // FFT_LANES),
        in_specs=[
            pl.BlockSpec(f1inv.shape, lambda b, j: (0, 0)),
            pl.BlockSpec((pl.Squeezed(), 2, n1h, n2, FFT_LANES), lambda b, j: (b, 0, 0, 0, j)),
        ],
        out_specs=pl.BlockSpec((1, n1h, n2, FFT_LANES), lambda b, j: (b, 0, 0, j)),
        out_shape=jax.ShapeDtypeStruct((bsz, n1h, n2, c), BF16),
        compiler_params=_cparams(("parallel", "parallel")),
        name="fft_out",
    )(f1inv, b5).reshape(bsz, n1h * n2, c)


def _hyena_conv(u, kf, tb):
    a5 = _fft_stage1(u, tb["f1"])
    b5 = _fft_mid(a5, kf, tb)
    return _fft_out(b5, tb["f1inv"])


def _rms(x, g):
    inv = lax.rsqrt(jnp.mean(x * x, axis=-1, keepdims=True) + EPS)
    return (x * inv) * g


FFN_CHUNK = 256


def _merge_ffn_kernel(ya_ref, yc_ref, u_ref, x0_ref, dk_ref, ga_ref, gh_ref, x_ref, wa_ref, wh_ref,
                      wo_ref, g_ref, wg_ref, wu_ref, wd_ref, gf_ref, o_ref):
    yh = x0_ref[...].astype(F32) * (yc_ref[...].astype(F32)
                                    + u_ref[...].astype(F32) * dk_ref[...])
    pa = jnp.dot(ya_ref[...], wa_ref[...], preferred_element_type=F32)
    ph = jnp.dot(yh.astype(BF16), wh_ref[...], preferred_element_type=F32)
    merged = (jax.nn.sigmoid(ga_ref[...].astype(F32)) * pa
              + jax.nn.sigmoid(gh_ref[...].astype(F32)) * ph)
    x1 = x_ref[...] + jnp.dot(merged.astype(BF16), wo_ref[...], preferred_element_type=F32)
    h = _rms(x1, g_ref[...]).astype(BF16)
    acc = x1
    for c0 in range(0, wg_ref.shape[1], FFN_CHUNK):
        gate = jnp.dot(h, wg_ref[:, c0:c0 + FFN_CHUNK], preferred_element_type=F32)
        up = jnp.dot(h, wu_ref[:, c0:c0 + FFN_CHUNK], preferred_element_type=F32)
        act = (gate * jax.nn.sigmoid(gate) * up).astype(BF16)
        acc = acc + jnp.dot(act, wd_ref[c0:c0 + FFN_CHUNK, :], preferred_element_type=F32)
    o_ref[...] = _rms(acc, gf_ref[...])


def _merge_ffn(ya, yc, u, x0, d_skip, gates, x2, wa, wh, wo, g, wg, wu, wd, gf):
    n, d = x2.shape
    dff = wg.shape[1]
    assert dff % FFN_CHUNK == 0
    tm = min(512, n)
    tok = pl.BlockSpec((tm, D_HYENA), lambda i: (i, 0))

    def resident(a):
        return pl.BlockSpec(a.shape, lambda i: (0, 0), pipeline_mode=pl.Buffered(1))

    def row(width):
        return pl.BlockSpec((1, width), lambda i: (0, 0))

    return pl.pallas_call(
        _merge_ffn_kernel,
        grid=(n // tm,),
        in_specs=[
            pl.BlockSpec((tm, D_ATTN), lambda i: (i, 0)),
            tok, tok, tok,
            row(D_HYENA),
            pl.BlockSpec((tm, d), lambda i: (i, 0)),
            pl.BlockSpec((tm, d), lambda i: (i, 1)),
            pl.BlockSpec((tm, d), lambda i: (i, 0)),
            resident(wa), resident(wh), resident(wo),
            row(d),
            resident(wg), resident(wu), resident(wd),
            row(d),
        ],
        out_specs=pl.BlockSpec((tm, d), lambda i: (i, 0)),
        out_shape=jax.ShapeDtypeStruct((n, d), F32),
        compiler_params=_cparams(("parallel",)),
        name="merge_ffn",
    )(ya, yc, u, x0, d_skip.astype(F32).reshape(1, D_HYENA), gates, gates, x2, wa, wh, wo,
      g.reshape(1, d), wg, wu, wd, gf.reshape(1, d))


def _trunk(x, p, norm_final):
    bsz, L, d = x.shape
    n = bsz * L
    x2 = x.reshape(n, d)
    qkv, u, x0, gates = _inproj_conv(x2, L, p["norm_mix"], p["w_in"], p["conv_w"], p["conv_b"])
    ya = _attention(qkv.reshape(bsz, L, qkv.shape[1]), p["rpb"])
    u = u.reshape(bsz, L, D_HYENA)
    tb = dict(_fft_tables(L))
    for name in ("f1", "f1inv", "f2", "f2inv"):
        tb[name] = jnp.asarray(tb[name]).astype(BF16)
    hfilt = _implicit_filters(L, *p["filt"])
    kf = _filter_spectrum(hfilt, tb)
    yc = _hyena_conv(u, kf, tb)
    out = _merge_ffn(ya.reshape(n, D_ATTN), yc.reshape(n, D_HYENA), u.reshape(n, D_HYENA),
                     x0, p["hyena_d"], gates, x2,
                     p["w_br_attn"], p["w_br_hyena"], p["w_out"],
                     p["norm_ffn"], p["w_gate"], p["w_up"], p["w_down"], norm_final)
    return out.reshape(bsz, L, d)


def kernel(x_prompt, x_sample, norm_mix, w_in, rpb, conv_w, conv_b, filt_w1, filt_b1, filt_w2,
           filt_b2, filt_w3, filt_b3, filt_w4, filt_freq, hyena_d, w_br_attn, w_br_hyena, w_out,
           norm_ffn, w_gate, w_up, w_down, norm_final):
    assert w_in.shape[0] == 1, "the layer definition has depth 1"
    col = lax.broadcasted_iota(jnp.int32, (1, w_in.shape[2]), 1)
    col_scale = jnp.where(col < D_ATTN, Q_SCALE, 1.0).astype(F32)
    p = dict(
        norm_mix=norm_mix[0], w_in=(w_in[0] * col_scale).astype(BF16), rpb=rpb[0].reshape(-1),
        conv_w=conv_w[0], conv_b=conv_b[0],
        filt=(filt_w1[0], filt_b1[0], filt_w2[0], filt_b2[0], filt_w3[0], filt_b3[0],
              filt_w4[0], filt_freq[0]),
        hyena_d=hyena_d[0],
        w_br_attn=w_br_attn[0].astype(BF16), w_br_hyena=w_br_hyena[0].astype(BF16),
        w_out=w_out[0].astype(BF16), norm_ffn=norm_ffn[0],
        w_gate=w_gate[0].astype(BF16), w_up=w_up[0].astype(BF16),
        w_down=w_down[0].astype(BF16),
    )
    return (_trunk(x_prompt, p, norm_final), _trunk(x_sample, p, norm_final))
```

```python
import functools
import math

import numpy as np
import jax
import jax.numpy as jnp
from jax import lax
from jax.experimental import pallas as pl
from jax.experimental.pallas import tpu as pltpu

F32 = jnp.float32
BF16 = jnp.bfloat16

GRID_W = 64
N_HEADS = 8
HEAD_DIM = 64
D_ATTN = N_HEADS * HEAD_DIM
WIN_R = 8
WIN_C = 16
D_HYENA = 512
SHORT_K = 3
FILTER_EMB = 33
FILTER_HIDDEN = 64
DECAY_TARGET = 1e-2
FAST_DECAY_PCT = 0.3
SLOW_DECAY_PCT = 1.5
EPS = 1e-6

FFT_N2 = 128
ATT_ROWS = 16
NEG_BIG = -1e30
VMEM_LIMIT = 56 * 1024 * 1024


def _cparams(sem):
    return pltpu.CompilerParams(dimension_semantics=sem, vmem_limit_bytes=VMEM_LIMIT)


IN_CHUNK = 512
HALO = 8


def _inproj_conv_kernel(x_ref, xp_ref, xn_ref, g_ref, wqkv_ref, why_ref, wgt_ref, cw_ref, cb_ref,
                        qkv_ref, u_ref, x0_ref, gt_ref, *, tiles_per_seq):
    i = pl.program_id(0)
    g = g_ref[...]
    h = _rms(x_ref[...], g).astype(BF16)
    hh = _rms(jnp.concatenate([xp_ref[...], xn_ref[...]], axis=0), g).astype(BF16)
    for w_ref, o_ref in ((wqkv_ref, qkv_ref), (wgt_ref, gt_ref)):
        for c0 in range(0, w_ref.shape[1], IN_CHUNK):
            o_ref[:, c0:c0 + IN_CHUNK] = jnp.dot(
                h, w_ref[:, c0:c0 + IN_CHUNK], preferred_element_type=F32).astype(o_ref.dtype)
    tm = h.shape[0]
    c = D_HYENA
    row = lax.broadcasted_iota(jnp.int32, (tm, c), 0)
    first = (i % tiles_per_seq) == 0
    last = (i % tiles_per_seq) == tiles_per_seq - 1
    ys = []
    for k in range(3):
        w = why_ref[:, k * c:(k + 1) * c]
        hy = jnp.dot(h, w, preferred_element_type=F32)
        hyh = jnp.dot(hh, w, preferred_element_type=F32)
        prev_row = jnp.where(first, 0.0, hyh[HALO - 1:HALO])
        next_row = jnp.where(last, 0.0, hyh[HALO:HALO + 1])
        xm1 = jnp.where(row == 0, prev_row, pltpu.roll(hy, 1, 0))
        xp1 = jnp.where(row == tm - 1, next_row, pltpu.roll(hy, tm - 1, 0))
        cw = cw_ref[:, k * c:(k + 1) * c]
        ys.append(xm1 * cw[0:1] + hy * cw[1:2] + xp1 * cw[2:3] + cb_ref[:, k * c:(k + 1) * c])
    x0_ref[...] = ys[0].astype(x0_ref.dtype)
    u_ref[...] = (ys[1] * ys[2]).astype(u_ref.dtype)


def _inproj_conv(x2, L, g, w_bf16, conv_w, conv_b):
    n, d = x2.shape
    n_qkv, n_hy = 3 * D_ATTN, 3 * D_HYENA
    tm = min(1024, L)
    nh = n // HALO
    hb = tm // HALO

    def resident(a):
        return pl.BlockSpec(a.shape, lambda i: (0, 0), pipeline_mode=pl.Buffered(1))

    wqkv, why, wgt = w_bf16[:, :n_qkv], w_bf16[:, n_qkv:n_qkv + n_hy], w_bf16[:, n_qkv + n_hy:]
    widths = (n_qkv, D_HYENA, D_HYENA, wgt.shape[1])
    return pl.pallas_call(
        functools.partial(_inproj_conv_kernel, tiles_per_seq=L // tm),
        grid=(n // tm,),
        in_specs=[
            pl.BlockSpec((tm, d), lambda i: (i, 0)),
            pl.BlockSpec((HALO, d), lambda i: (jnp.maximum(i * hb - 1, 0), 0)),
            pl.BlockSpec((HALO, d), lambda i: (jnp.minimum((i + 1) * hb, nh - 1), 0)),
            pl.BlockSpec((1, d), lambda i: (0, 0)),
            resident(wqkv), resident(why), resident(wgt),
            pl.BlockSpec((SHORT_K, n_hy), lambda i: (0, 0)),
            pl.BlockSpec((1, n_hy), lambda i: (0, 0)),
        ],
        out_specs=[pl.BlockSpec((tm, w), lambda i: (i, 0)) for w in widths],
        out_shape=[jax.ShapeDtypeStruct((n, w), BF16) for w in widths],
        compiler_params=_cparams(("parallel",)),
        name="inproj_conv",
    )(x2, x2, x2, g.reshape(1, d), wqkv, why, wgt, conv_w, conv_b.reshape(1, n_hy))


N_DR = 2 * WIN_R - 1
N_DC = 2 * WIN_C - 1
LOG2_E = 1.4426950408889634
Q_SCALE = (HEAD_DIM ** -0.5) * LOG2_E
KV_ROWS = ATT_ROWS + WIN_R


def _build_bias_tiles(rpb_ref, pt_ref):
    k = lax.broadcasted_iota(jnp.int32, (GRID_W, 128), 0)
    lane = lax.broadcasted_iota(jnp.int32, (GRID_W, 128), 1)
    q = lane & (GRID_W - 1)
    first_half = lane < GRID_W
    cs = jnp.clip(q - WIN_C // 2, 0, GRID_W - WIN_C)
    valid = (k >= cs) & (k < cs + WIN_C)
    dc = jnp.clip(k - q, -(WIN_C - 1), WIN_C - 1) + (WIN_C - 1)

    def body(idx, carry):
        p = idx // N_DR
        d = idx % N_DR
        base0 = ((2 * p) * N_DR + d) * N_DC
        base1 = ((2 * p + 1) * N_DR + d) * N_DC
        acc = jnp.zeros((GRID_W, 128), F32)
        for off in range(N_DC):
            val = jnp.where(first_half, rpb_ref[base0 + off], rpb_ref[base1 + off])
            acc = jnp.where(dc == off, val, acc)
        pt_ref[idx] = jnp.where(valid, acc * LOG2_E, NEG_BIG)
        return carry

    lax.fori_loop(0, (N_HEADS // 2) * N_DR, body, 0)


def _kv_window_start(i, rows):
    return jnp.clip(i * ATT_ROWS - WIN_R // 2, 0, rows - KV_ROWS)


def _attn_kernel(rpb_ref, q_ref, k_ref, v_ref, o_ref, pt_ref, *, rows):
    b = pl.program_id(0)
    i = pl.program_id(1)

    @pl.when((b == 0) & (i == 0))
    def _():
        _build_bias_tiles(rpb_ref, pt_ref)

    lane = lax.broadcasted_iota(jnp.int32, (GRID_W, 128), 1)
    first_half = lane < HEAD_DIM
    ones = jnp.ones((WIN_R * GRID_W, 128), BF16)
    nkeys = WIN_R * GRID_W
    wstart = _kv_window_start(i, rows)
    nt_dims = (((1,), (1,)), ((), ()))

    def row_body(rr, carry):
        r = i * ATT_ROWS + rr
        rs = jnp.clip(r - WIN_R // 2, 0, rows - WIN_R)
        s = rs - r + (WIN_R - 1)
        koff = pl.multiple_of((rs - wstart) * GRID_W, GRID_W)
        qoff = pl.multiple_of(rr * GRID_W, GRID_W)
        q = q_ref[pl.ds(qoff, GRID_W), :]
        pairs = range(N_HEADS // 2)
        sts = []
        for p in pairs:
            qp = q[:, 128 * p:128 * (p + 1)]
            zero = jnp.zeros_like(qp)
            wt = jnp.concatenate([jnp.where(first_half, qp, zero),
                                  jnp.where(first_half, zero, qp)], axis=0)
            kp = k_ref[pl.ds(koff, nkeys), 128 * p:128 * (p + 1)]
            sts.append(lax.dot_general(kp, wt, nt_dims, preferred_element_type=F32))
        pms = []
        for p in pairs:
            st = sts[p]
            ch = [st[GRID_W * c:GRID_W * (c + 1)] + pt_ref[p * N_DR + s + c] for c in range(WIN_R)]
            m = ch[0]
            for c in range(1, WIN_R):
                m = jnp.maximum(m, ch[c])
            m = jnp.max(m, axis=0, keepdims=True)
            pt = jnp.concatenate([jnp.exp2(c - m) for c in ch], axis=0).astype(BF16)
            pms.append(pt.T)
        outs = []
        for p in pairs:
            vp = v_ref[pl.ds(koff, nkeys), 128 * p:128 * (p + 1)]
            ov = jnp.dot(pms[p], jnp.concatenate([vp, ones], axis=1), preferred_element_type=F32)
            o = ov[:, 0:128] / ov[:, 128:256]
            outs.append(jnp.where(first_half, o[0:GRID_W], o[GRID_W:2 * GRID_W]))
        o_ref[pl.ds(qoff, GRID_W), :] = jnp.concatenate(outs, axis=1).astype(o_ref.dtype)
        return carry

    lax.fori_loop(0, ATT_ROWS, row_body, 0, unroll=8)


def _attention(z3, rpb_flat):
    bsz, L, _ = z3.shape
    rows = L // GRID_W
    assert rows % ATT_ROWS == 0 and rows >= KV_ROWS
    nblk = rows // ATT_ROWS
    blk = ATT_ROWS * GRID_W

    def window(col):
        return pl.BlockSpec(
            (pl.Squeezed(), pl.Element(KV_ROWS * GRID_W), pl.Element(D_ATTN)),
            lambda b, i: (b, _kv_window_start(i, rows) * GRID_W, col * D_ATTN))

    return pl.pallas_call(
        functools.partial(_attn_kernel, rows=rows),
        grid=(bsz, nblk),
        in_specs=[
            pl.BlockSpec(memory_space=pltpu.SMEM),
            pl.BlockSpec((pl.Squeezed(), blk, D_ATTN), lambda b, i: (b, i, 0)),
            window(1),
            window(2),
        ],
        out_specs=pl.BlockSpec((pl.Squeezed(), blk, D_ATTN), lambda b, i: (b, i, 0)),
        out_shape=jax.ShapeDtypeStruct((bsz, L, D_ATTN), BF16),
        scratch_shapes=[pltpu.VMEM(((N_HEADS // 2) * N_DR, GRID_W, 128), F32)],
        compiler_params=_cparams(("arbitrary", "arbitrary")),
        name="nbr_attention",
    )(rpb_flat, z3, z3, z3)


TWO_OVER_PI = 0.6366197723675814
PIO2_1 = 1.5703125
PIO2_2 = 4.837512969970703125e-4
PIO2_3 = 7.54978995489188216e-8
TRIG_FAST_LIMIT = 4096.0


def _quadrant_value(x, shift):
    kf = jnp.floor(x * TWO_OVER_PI + 0.5)
    r = ((x - kf * PIO2_1) - kf * PIO2_2) - kf * PIO2_3
    z = r * r
    s = r + r * z * (-1.6666654611e-1 + z * (8.3321608736e-3 + z * -1.9515295891e-4))
    c = (1.0 - 0.5 * z) + z * z * (4.166664568298827e-2
                                   + z * (-1.388731625493765e-3 + z * 2.443315711809948e-5))
    k = kf + shift
    half = jnp.floor(k * 0.5)
    odd = k - 2.0 * half
    flip = half - 2.0 * jnp.floor(half * 0.5)
    return (s + odd * (c - s)) * (1.0 - 2.0 * flip)


def _sin_small(x):
    return _quadrant_value(x, 0.0)


def _cos_small(x):
    return _quadrant_value(x, 1.0)


def _sin(x):
    return lax.cond(jnp.max(jnp.abs(x)) < TRIG_FAST_LIMIT, _sin_small, jnp.sin, x)


def _cos(x):
    return lax.cond(jnp.max(jnp.abs(x)) < TRIG_FAST_LIMIT, _cos_small, jnp.cos, x)


def _hi_lo(x):
    hi = x.astype(BF16)
    return hi, (x - hi.astype(F32)).astype(BF16)


def _stack_cols(w):
    hi, lo = _hi_lo(w)
    return jnp.concatenate([hi, hi, lo], axis=1)


def _stack_rows(h):
    hi, lo = _hi_lo(h)
    return jnp.concatenate([hi, lo, hi], axis=0)


def _dot_stacked(w_stacked, h):
    return jnp.dot(w_stacked, _stack_rows(h), preferred_element_type=F32)


def _filter_kernel(fb_ref, w1t_ref, w1c_ref, w1s_ref, b1_ref, w2_ref, b2_ref, w3_ref, b3_ref,
                   w4_ref, fr_ref, dl_ref, o_ref, *, L, tp):
    j = pl.program_id(0)
    fh = FILTER_HIDDEN
    pos = (lax.broadcasted_iota(jnp.int32, (1, tp), 1) + j * tp).astype(F32)
    t = pos / (L - 1.0)
    omega = (2.0 * math.pi) * pos / float(L)
    ang = fb_ref[...] * omega
    fr = fr_ref[...]
    pre = (w1t_ref[...] * t
           + _dot_stacked(w1c_ref[...], _cos(ang)) - _dot_stacked(w1s_ref[...], _sin(ang)))
    h = _sin(fr * (pre + b1_ref[...]))
    h = _sin(fr * (_dot_stacked(w2_ref[...], h) + b2_ref[...]))
    h = _sin(fr * (_dot_stacked(w3_ref[...], h) + b3_ref[...]))
    h_hi = h.astype(BF16).astype(F32)
    tr = jnp.concatenate([h_hi, h - h_hi], axis=0).T.astype(BF16)
    lane = lax.broadcasted_iota(jnp.int32, tr.shape, 1)
    lhs = jnp.concatenate([tr, jnp.where(lane < fh, tr, jnp.zeros_like(tr))], axis=1)
    out = jnp.dot(lhs, w4_ref[...], preferred_element_type=F32)
    tcol = (lax.broadcasted_iota(jnp.int32, (tp, 1), 0) + j * tp).astype(F32) / (L - 1.0)
    decay = jnp.exp(-tcol * dl_ref[...])
    c = D_HYENA
    o_ref[:, 0:c] = out[:, 0:c] * decay
    o_ref[:, c:2 * c] = out[:, c:2 * c] * decay


def _implicit_filters(L, w1, b1, w2, b2, w3, b3, w4, freq):
    bands = (FILTER_EMB - 1) // 2
    fh = FILTER_HIDDEN
    tp = min(512, L)
    fb = jnp.linspace(1e-4, bands - 1, bands, dtype=F32).reshape(bands, 1)
    max_decay = math.log(DECAY_TARGET) / FAST_DECAY_PCT
    min_decay = math.log(DECAY_TARGET) / SLOW_DECAY_PCT
    deltas = jnp.abs(jnp.linspace(min_decay, max_decay, D_HYENA, dtype=F32)).reshape(1, D_HYENA)
    w1 = w1.astype(F32)
    w4_hi, w4_lo = _hi_lo(w4.astype(F32))
    args = (
        fb,
        w1[0:1, :].T,
        _stack_cols(w1[1:1 + bands, :].T),
        _stack_cols(w1[1 + bands:, :].T),
        b1.astype(F32).reshape(fh, 1),
        _stack_cols(w2.astype(F32).T), b2.astype(F32).reshape(fh, 1),
        _stack_cols(w3.astype(F32).T), b3.astype(F32).reshape(fh, 1),
        jnp.concatenate([w4_hi, w4_hi, w4_lo, jnp.zeros_like(w4_hi)], axis=0),
        freq.astype(F32).reshape(fh, 1),
        deltas,
    )

    def full(a):
        return pl.BlockSpec(a.shape, lambda j: (0,) * a.ndim)

    return pl.pallas_call(
        functools.partial(_filter_kernel, L=L, tp=tp),
        grid=(L // tp,),
        in_specs=[full(a) for a in args],
        out_specs=pl.BlockSpec((tp, 2 * D_HYENA), lambda j: (j, 0)),
        out_shape=jax.ShapeDtypeStruct((L, 2 * D_HYENA), F32),
        compiler_params=_cparams(("parallel",)),
        name="implicit_filter",
    )(*args)


FFT_LANES = 128
FFT_T1 = 4


@functools.lru_cache(maxsize=None)
def _fft_tables(L):
    n = 2 * L
    n2 = FFT_N2
    n1 = n // n2
    n1h = n1 // 2
    odd = 2 * np.arange(n1h) + 1
    th = 2.0 * np.pi * (np.outer(odd, np.arange(n1h)) % (2 * n1)) / (2 * n1)
    f1 = np.concatenate([np.cos(th), -np.sin(th)], axis=0)
    tw = 2.0 * np.pi * (np.outer(odd, np.arange(n2)) % (2 * n)) / (2 * n)
    twr = np.cos(tw)[:, :, None]
    twi = -np.sin(tw)[:, :, None]
    t2 = 2.0 * np.pi * (np.outer(np.arange(n2), np.arange(n2)) % n2) / n2
    fr, fi = np.cos(t2), -np.sin(t2)
    f2 = np.block([[fr, -fi], [fi, fr]])
    f2inv = np.block([[fr, fi], [-fi, fr]])
    f1inv = np.concatenate([np.cos(th.T), -np.sin(th.T)], axis=1) * (2.0 / n)
    return dict(
        n1=n1, n1h=n1h,
        f1=np.asarray(f1, np.float32), f1inv=np.asarray(f1inv, np.float32),
        f2=np.asarray(f2, np.float32), f2inv=np.asarray(f2inv, np.float32),
        twr=np.asarray(twr, np.float32), twi=np.asarray(twi, np.float32),
    )


FFT_A = 16
SLOT_DTYPE = BF16


def _outer_dft(f, src_ref, dst_ref):
    n_in, n1h = src_ref.shape[0], src_ref.shape[1]
    n_out = dst_ref.shape[0]

    def body(i, carry):
        a0 = pl.multiple_of(i * FFT_A, FFT_A)
        ts = [jnp.swapaxes(src_ref[p, :, pl.ds(a0, FFT_A), :].astype(F32), 0, 1)
              for p in range(n_in)]
        outs = [[] for _ in range(n_out)]
        for j in range(0, FFT_A, 2):
            rhs = jnp.concatenate([jnp.concatenate([t[j], t[j + 1]], axis=1) for t in ts],
                                  axis=0).astype(BF16)
            res = jnp.dot(f, rhs, preferred_element_type=F32)
            for q in range(n_out):
                blk = res[q * n1h:(q + 1) * n1h]
                outs[q] += [blk[:, 0:FFT_LANES], blk[:, FFT_LANES:2 * FFT_LANES]]
        for q in range(n_out):
            dst_ref[q, :, pl.ds(a0, FFT_A), :] = jnp.swapaxes(
                jnp.stack(outs[q], axis=0), 0, 1).astype(dst_ref.dtype)
        return carry

    lax.fori_loop(0, FFT_N2 // FFT_A, body, 0, unroll=2)


def _fft_stage1_kernel(f_ref, u_ref, o_ref):
    _outer_dft(f_ref[...], u_ref, o_ref)


def _fft_stage1(u3, f1):
    bsz, L, c = u3.shape
    n1h = L // FFT_N2
    return pl.pallas_call(
        _fft_stage1_kernel,
        grid=(bsz, c // FFT_LANES),
        in_specs=[
            pl.BlockSpec(f1.shape, lambda b, j: (0, 0)),
            pl.BlockSpec((1, n1h, FFT_N2, FFT_LANES), lambda b, j: (b, 0, 0, j)),
        ],
        out_specs=pl.BlockSpec((pl.Squeezed(), 2, n1h, FFT_N2, FFT_LANES),
                               lambda b, j: (b, 0, 0, 0, j)),
        out_shape=jax.ShapeDtypeStruct((bsz, 2, n1h, FFT_N2, c), SLOT_DTYPE),
        compiler_params=_cparams(("parallel", "parallel")),
        name="fft_stage1",
    )(f1, u3.reshape(bsz, n1h, FFT_N2, c))


def _twiddle(ar, ai, tr, ti):
    return ar * tr - ai * ti, ar * ti + ai * tr


def _stage2_inputs(a_ref, twr_ref, twi_ref):
    xs = []
    for j in range(FFT_T1):
        xr, xi = _twiddle(a_ref[0, j].astype(F32), a_ref[1, j].astype(F32),
                          twr_ref[j], twi_ref[j])
        xs.append(jnp.concatenate([xr, xi], axis=0).astype(BF16))
    return xs


def _filter_spectrum_kernel(a_ref, twr_ref, twi_ref, f2_ref, o_ref):
    n2, c = FFT_N2, D_HYENA
    f2 = f2_ref[...]
    bigs = [jnp.dot(f2, x, preferred_element_type=F32)
            for x in _stage2_inputs(a_ref, twr_ref, twi_ref)]
    for j, big in enumerate(bigs):
        o_ref[j, 0] = (big[0:n2, 0:c] + big[0:n2, c:2 * c]).astype(o_ref.dtype)
        o_ref[j, 1] = (big[n2:2 * n2, 0:c] - big[n2:2 * n2, c:2 * c]).astype(o_ref.dtype)


def _filter_spectrum(hfilt, tb):
    L, c2 = hfilt.shape
    n1h = tb["n1h"]
    a5 = _fft_stage1(hfilt.reshape(1, L, c2), tb["f1"])
    return pl.pallas_call(
        _filter_spectrum_kernel,
        grid=(n1h // FFT_T1,),
        in_specs=[
            pl.BlockSpec((pl.Squeezed(), 2, FFT_T1, FFT_N2, c2), lambda k: (0, 0, k, 0, 0)),
            pl.BlockSpec((FFT_T1, FFT_N2, 1), lambda k: (k, 0, 0)),
            pl.BlockSpec((FFT_T1, FFT_N2, 1), lambda k: (k, 0, 0)),
            pl.BlockSpec((2 * FFT_N2, 2 * FFT_N2), lambda k: (0, 0)),
        ],
        out_specs=pl.BlockSpec((FFT_T1, 2, FFT_N2, D_HYENA), lambda k: (k, 0, 0, 0)),
        out_shape=jax.ShapeDtypeStruct((n1h, 2, FFT_N2, D_HYENA), SLOT_DTYPE),
        compiler_params=_cparams(("parallel",)),
        name="filter_spectrum",
    )(a5, tb["twr"], tb["twi"], tb["f2"])


def _fft_mid_kernel(a_ref, kf_ref, twr_ref, twi_ref, f2_ref, f2i_ref, o_ref):
    n2 = FFT_N2
    f2, f2i = f2_ref[...], f2i_ref[...]
    bigs = [jnp.dot(f2, x, preferred_element_type=F32)
            for x in _stage2_inputs(a_ref, twr_ref, twi_ref)]
    ys = []
    for j, big in enumerate(bigs):
        sr, si = big[0:n2], big[n2:2 * n2]
        kr, ki = kf_ref[j, 0].astype(F32), kf_ref[j, 1].astype(F32)
        ys.append(jnp.concatenate([sr * kr - si * ki, sr * ki + si * kr], axis=0).astype(BF16))
    backs = [jnp.dot(f2i, y, preferred_element_type=F32) for y in ys]
    for j, back in enumerate(backs):
        br, bi = back[0:n2], back[n2:2 * n2]
        tr, ti = twr_ref[j], twi_ref[j]
        o_ref[0, j] = (br * tr + bi * ti).astype(o_ref.dtype)
        o_ref[1, j] = (bi * tr - br * ti).astype(o_ref.dtype)


def _fft_mid(a5, kf, tb):
    bsz, _, n1h, n2, c = a5.shape
    slot = pl.BlockSpec((pl.Squeezed(), 2, FFT_T1, n2, c), lambda k, b: (b, 0, k, 0, 0))
    return pl.pallas_call(
        _fft_mid_kernel,
        grid=(n1h // FFT_T1, bsz),
        in_specs=[
            slot,
            pl.BlockSpec((FFT_T1, 2, n2, c), lambda k, b: (k, 0, 0, 0)),
            pl.BlockSpec((FFT_T1, n2, 1), lambda k, b: (k, 0, 0)),
            pl.BlockSpec((FFT_T1, n2, 1), lambda k, b: (k, 0, 0)),
            pl.BlockSpec((2 * n2, 2 * n2), lambda k, b: (0, 0)),
            pl.BlockSpec((2 * n2, 2 * n2), lambda k, b: (0, 0)),
        ],
        out_specs=slot,
        out_shape=jax.ShapeDtypeStruct(a5.shape, SLOT_DTYPE),
        compiler_params=_cparams(("parallel", "parallel")),
        name="fft_mid",
    )(a5, kf, tb["twr"], tb["twi"], tb["f2"], tb["f2inv"])


def _fft_out_kernel(f_ref, b_ref, o_ref):
    _outer_dft(f_ref[...], b_ref, o_ref)


def _fft_out(b5, f1inv):
    bsz, _, n1h, n2, c = b5.shape
    return pl.pallas_call(
        _fft_out_kernel,
        grid=(bsz, c // FFT_LANES),
        in_specs=[
            pl.BlockSpec(f1inv.shape, lambda b, j: (0, 0)),
            pl.BlockSpec((pl.Squeezed(), 2, n1h, n2, FFT_LANES), lambda b, j: (b, 0, 0, 0, j)),
        ],
        out_specs=pl.BlockSpec((1, n1h, n2, FFT_LANES), lambda b, j: (b, 0, 0, j)),
        out_shape=jax.ShapeDtypeStruct((bsz, n1h, n2, c), BF16),
        compiler_params=_cparams(("parallel", "parallel")),
        name="fft_out",
    )(f1inv, b5).reshape(bsz, n1h * n2, c)


def _hyena_conv(u, kf, tb):
    a5 = _fft_stage1(u, tb["f1"])
    b5 = _fft_mid(a5, kf, tb)
    return _fft_out(b5, tb["f1inv"])


def _rms(x, g):
    inv = lax.rsqrt(jnp.mean(x * x, axis=-1, keepdims=True) + EPS)
    return (x * inv) * g


FFN_CHUNK = 256


def _merge_ffn_kernel(ya_ref, yc_ref, u_ref, x0_ref, dk_ref, ga_ref, gh_ref, x_ref, wa_ref, wh_ref,
                      wo_ref, g_ref, wg_ref, wu_ref, wd_ref, gf_ref, o_ref):
    yh = x0_ref[...].astype(F32) * (yc_ref[...].astype(F32)
                                    + u_ref[...].astype(F32) * dk_ref[...])
    pa = jnp.dot(ya_ref[...], wa_ref[...], preferred_element_type=F32)
    ph = jnp.dot(yh.astype(BF16), wh_ref[...], preferred_element_type=F32)
    merged = (jax.nn.sigmoid(ga_ref[...].astype(F32)) * pa
              + jax.nn.sigmoid(gh_ref[...].astype(F32)) * ph)
    x1 = x_ref[...] + jnp.dot(merged.astype(BF16), wo_ref[...], preferred_element_type=F32)
    h = _rms(x1, g_ref[...]).astype(BF16)
    acc = x1
    for c0 in range(0, wg_ref.shape[1], FFN_CHUNK):
        gate = jnp.dot(h, wg_ref[:, c0:c0 + FFN_CHUNK], preferred_element_type=F32)
        up = jnp.dot(h, wu_ref[:, c0:c0 + FFN_CHUNK], preferred_element_type=F32)
        act = (gate * jax.nn.sigmoid(gate) * up).astype(BF16)
        acc = acc + jnp.dot(act, wd_ref[c0:c0 + FFN_CHUNK, :], preferred_element_type=F32)
    o_ref[...] = _rms(acc, gf_ref[...])


def _merge_ffn(ya, yc, u, x0, d_skip, gates, x2, wa, wh, wo, g, wg, wu, wd, gf):
    n, d = x2.shape
    dff = wg.shape[1]
    assert dff % FFN_CHUNK == 0
    tm = min(512, n)
    tok = pl.BlockSpec((tm, D_HYENA), lambda i: (i, 0))

    def resident(a):
        return pl.BlockSpec(a.shape, lambda i: (0, 0), pipeline_mode=pl.Buffered(1))

    def row(width):
        return pl.BlockSpec((1, width), lambda i: (0, 0))

    return pl.pallas_call(
        _merge_ffn_kernel,
        grid=(n // tm,),
        in_specs=[
            pl.BlockSpec((tm, D_ATTN), lambda i: (i, 0)),
            tok, tok, tok,
            row(D_HYENA),
            pl.BlockSpec((tm, d), lambda i: (i, 0)),
            pl.BlockSpec((tm, d), lambda i: (i, 1)),
            pl.BlockSpec((tm, d), lambda i: (i, 0)),
            resident(wa), resident(wh), resident(wo),
            row(d),
            resident(wg), resident(wu), resident(wd),
            row(d),
        ],
        out_specs=pl.BlockSpec((tm, d), lambda i: (i, 0)),
        out_shape=jax.ShapeDtypeStruct((n, d), F32),
        compiler_params=_cparams(("parallel",)),
        name="merge_ffn",
    )(ya, yc, u, x0, d_skip.astype(F32).reshape(1, D_HYENA), gates, gates, x2, wa, wh, wo,
      g.reshape(1, d), wg, wu, wd, gf.reshape(1, d))


def _trunk(x, p, norm_final):
    bsz, L, d = x.shape
    n = bsz * L
    x2 = x.reshape(n, d)
    qkv, u, x0, gates = _inproj_conv(x2, L, p["norm_mix"], p["w_in"], p["conv_w"], p["conv_b"])
    ya = _attention(qkv.reshape(bsz, L, qkv.shape[1]), p["rpb"])
    u = u.reshape(bsz, L, D_HYENA)
    tb = dict(_fft_tables(L))
    for name in ("f1", "f1inv", "f2", "f2inv"):
        tb[name] = jnp.asarray(tb[name]).astype(BF16)
    hfilt = _implicit_filters(L, *p["filt"])
    kf = _filter_spectrum(hfilt, tb)
    yc = _hyena_conv(u, kf, tb)
    out = _merge_ffn(ya.reshape(n, D_ATTN), yc.reshape(n, D_HYENA), u.reshape(n, D_HYENA),
                     x0, p["hyena_d"], gates, x2,
                     p["w_br_attn"], p["w_br_hyena"], p["w_out"],
                     p["norm_ffn"], p["w_gate"], p["w_up"], p["w_down"], norm_final)
    return out.reshape(bsz, L, d)


def kernel(x_prompt, x_sample, norm_mix, w_in, rpb, conv_w, conv_b, filt_w1, filt_b1, filt_w2,
           filt_b2, filt_w3, filt_b3, filt_w4, filt_freq, hyena_d, w_br_attn, w_br_hyena, w_out,
           norm_ffn, w_gate, w_up, w_down, norm_final):
    assert w_in.shape[0] == 1, "the layer definition has depth 1"
    col = lax.broadcasted_iota(jnp.int32, (1, w_in.shape[2]), 1)
    col_scale = jnp.where(col < D_ATTN, Q_SCALE, 1.0).astype(F32)
    p = dict(
        norm_mix=norm_mix[0], w_in=(w_in[0] * col_scale).astype(BF16), rpb=rpb[0].reshape(-1),
        conv_w=conv_w[0], conv_b=conv_b[0],
        filt=(filt_w1[0], filt_b1[0], filt_w2[0], filt_b2[0], filt_w3[0], filt_b3[0],
              filt_w4[0], filt_freq[0]),
        hyena_d=hyena_d[0],
        w_br_attn=w_br_attn[0].astype(BF16), w_br_hyena=w_br_hyena[0].astype(BF16),
        w_out=w_out[0].astype(BF16), norm_ffn=norm_ffn[0],
        w_gate=w_gate[0].astype(BF16), w_up=w_up[0].astype(BF16),
        w_down=w_down[0].astype(BF16),
    )
    return (_trunk(x_prompt, p, norm_final), _trunk(x_sample, p, norm_final))
```

```python
import functools
import math

import numpy as np
import jax
import jax.numpy as jnp
from jax import lax
from jax.experimental import pallas as pl
from jax.experimental.pallas import tpu as pltpu

F32 = jnp.float32
BF16 = jnp.bfloat16

GRID_W = 64
N_HEADS = 8
HEAD_DIM = 64
D_ATTN = N_HEADS * HEAD_DIM
WIN_R = 8
WIN_C = 16
D_HYENA = 512
SHORT_K = 3
FILTER_EMB = 33
FILTER_HIDDEN = 64
DECAY_TARGET = 1e-2
FAST_DECAY_PCT = 0.3
SLOW_DECAY_PCT = 1.5
EPS = 1e-6

FFT_N2 = 128
ATT_ROWS = 16
NEG_BIG = -1e30
VMEM_LIMIT = 56 * 1024 * 1024


def _cparams(sem):
    return pltpu.CompilerParams(dimension_semantics=sem, vmem_limit_bytes=VMEM_LIMIT)


IN_CHUNK = 512
HALO = 8


def _inproj_conv_kernel(x_ref, xp_ref, xn_ref, g_ref, wqkv_ref, why_ref, wgt_ref, cw_ref, cb_ref,
                        qkv_ref, u_ref, x0_ref, gt_ref, *, tiles_per_seq):
    i = pl.program_id(0)
    g = g_ref[...]
    h = _rms(x_ref[...], g).astype(BF16)
    hh = _rms(jnp.concatenate([xp_ref[...], xn_ref[...]], axis=0), g).astype(BF16)
    for w_ref, o_ref in ((wqkv_ref, qkv_ref), (wgt_ref, gt_ref)):
        for c0 in range(0, w_ref.shape[1], IN_CHUNK):
            o_ref[:, c0:c0 + IN_CHUNK] = jnp.dot(
                h, w_ref[:, c0:c0 + IN_CHUNK], preferred_element_type=F32).astype(o_ref.dtype)
    tm = h.shape[0]
    c = D_HYENA
    row = lax.broadcasted_iota(jnp.int32, (tm, c), 0)
    first = (i % tiles_per_seq) == 0
    last = (i % tiles_per_seq) == tiles_per_seq - 1
    ys = []
    for k in range(3):
        w = why_ref[:, k * c:(k + 1) * c]
        hy = jnp.dot(h, w, preferred_element_type=F32)
        hyh = jnp.dot(hh, w, preferred_element_type=F32)
        prev_row = jnp.where(first, 0.0, hyh[HALO - 1:HALO])
        next_row = jnp.where(last, 0.0, hyh[HALO:HALO + 1])
        xm1 = jnp.where(row == 0, prev_row, pltpu.roll(hy, 1, 0))
        xp1 = jnp.where(row == tm - 1, next_row, pltpu.roll(hy, tm - 1, 0))
        cw = cw_ref[:, k * c:(k + 1) * c]
        ys.append(xm1 * cw[0:1] + hy * cw[1:2] + xp1 * cw[2:3] + cb_ref[:, k * c:(k + 1) * c])
    x0_ref[...] = ys[0].astype(x0_ref.dtype)
    u_ref[...] = (ys[1] * ys[2]).astype(u_ref.dtype)


def _inproj_conv(x2, L, g, w_bf16, conv_w, conv_b):
    n, d = x2.shape
    n_qkv, n_hy = 3 * D_ATTN, 3 * D_HYENA
    tm = min(1024, L)
    nh = n // HALO
    hb = tm // HALO

    def resident(a):
        return pl.BlockSpec(a.shape, lambda i: (0, 0), pipeline_mode=pl.Buffered(1))

    wqkv, why, wgt = w_bf16[:, :n_qkv], w_bf16[:, n_qkv:n_qkv + n_hy], w_bf16[:, n_qkv + n_hy:]
    widths = (n_qkv, D_HYENA, D_HYENA, wgt.shape[1])
    return pl.pallas_call(
        functools.partial(_inproj_conv_kernel, tiles_per_seq=L // tm),
        grid=(n // tm,),
        in_specs=[
            pl.BlockSpec((tm, d), lambda i: (i, 0)),
            pl.BlockSpec((HALO, d), lambda i: (jnp.maximum(i * hb - 1, 0), 0)),
            pl.BlockSpec((HALO, d), lambda i: (jnp.minimum((i + 1) * hb, nh - 1), 0)),
            pl.BlockSpec((1, d), lambda i: (0, 0)),
            resident(wqkv), resident(why), resident(wgt),
            pl.BlockSpec((SHORT_K, n_hy), lambda i: (0, 0)),
            pl.BlockSpec((1, n_hy), lambda i: (0, 0)),
        ],
        out_specs=[pl.BlockSpec((tm, w), lambda i: (i, 0)) for w in widths],
        out_shape=[jax.ShapeDtypeStruct((n, w), BF16) for w in widths],
        compiler_params=_cparams(("parallel",)),
        name="inproj_conv",
    )(x2, x2, x2, g.reshape(1, d), wqkv, why, wgt, conv_w, conv_b.reshape(1, n_hy))


N_DR = 2 * WIN_R - 1
N_DC = 2 * WIN_C - 1
LOG2_E = 1.4426950408889634
Q_SCALE = (HEAD_DIM ** -0.5) * LOG2_E
KV_ROWS = ATT_ROWS + WIN_R


def _build_bias_tiles(rpb_ref, pt_ref):
    k = lax.broadcasted_iota(jnp.int32, (GRID_W, 128), 0)
    lane = lax.broadcasted_iota(jnp.int32, (GRID_W, 128), 1)
    q = lane & (GRID_W - 1)
    first_half = lane < GRID_W
    cs = jnp.clip(q - WIN_C // 2, 0, GRID_W - WIN_C)
    valid = (k >= cs) & (k < cs + WIN_C)
    dc = jnp.clip(k - q, -(WIN_C - 1), WIN_C - 1) + (WIN_C - 1)

    def body(idx, carry):
        p = idx // N_DR
        d = idx % N_DR
        base0 = ((2 * p) * N_DR + d) * N_DC
        base1 = ((2 * p + 1) * N_DR + d) * N_DC
        acc = jnp.zeros((GRID_W, 128), F32)
        for off in range(N_DC):
            val = jnp.where(first_half, rpb_ref[base0 + off], rpb_ref[base1 + off])
            acc = jnp.where(dc == off, val, acc)
        pt_ref[idx] = jnp.where(valid, acc * LOG2_E, NEG_BIG)
        return carry

    lax.fori_loop(0, (N_HEADS // 2) * N_DR, body, 0)


def _kv_window_start(i, rows):
    return jnp.clip(i * ATT_ROWS - WIN_R // 2, 0, rows - KV_ROWS)


def _attn_kernel(rpb_ref, q_ref, k_ref, v_ref, o_ref, pt_ref, *, rows):
    b = pl.program_id(0)
    i = pl.program_id(1)

    @pl.when((b == 0) & (i == 0))
    def _():
        _build_bias_tiles(rpb_ref, pt_ref)

    lane = lax.broadcasted_iota(jnp.int32, (GRID_W, 128), 1)
    first_half = lane < HEAD_DIM
    ones = jnp.ones((WIN_R * GRID_W, 128), BF16)
    nkeys = WIN_R * GRID_W
    wstart = _kv_window_start(i, rows)
    nt_dims = (((1,), (1,)), ((), ()))

    def row_body(rr, carry):
        r = i * ATT_ROWS + rr
        rs = jnp.clip(r - WIN_R // 2, 0, rows - WIN_R)
        s = rs - r + (WIN_R - 1)
        koff = pl.multiple_of((rs - wstart) * GRID_W, GRID_W)
        qoff = pl.multiple_of(rr * GRID_W, GRID_W)
        q = q_ref[pl.ds(qoff, GRID_W), :]
        pairs = range(N_HEADS // 2)
        sts = []
        for p in pairs:
            qp = q[:, 128 * p:128 * (p + 1)]
            zero = jnp.zeros_like(qp)
            wt = jnp.concatenate([jnp.where(first_half, qp, zero),
                                  jnp.where(first_half, zero, qp)], axis=0)
            kp = k_ref[pl.ds(koff, nkeys), 128 * p:128 * (p + 1)]
            sts.append(lax.dot_general(kp, wt, nt_dims, preferred_element_type=F32))
        pms = []
        for p in pairs:
            st = sts[p]
            ch = [st[GRID_W * c:GRID_W * (c + 1)] + pt_ref[p * N_DR + s + c] for c in range(WIN_R)]
            m = ch[0]
            for c in range(1, WIN_R):
                m = jnp.maximum(m, ch[c])
            m = jnp.max(m, axis=0, keepdims=True)
            pt = jnp.concatenate([jnp.exp2(c - m) for c in ch], axis=0).astype(BF16)
            pms.append(pt.T)
        outs = []
        for p in pairs:
            vp = v_ref[pl.ds(koff, nkeys), 128 * p:128 * (p + 1)]
            ov = jnp.dot(pms[p], jnp.concatenate([vp, ones], axis=1), preferred_element_type=F32)
            o = ov[:, 0:128] / ov[:, 128:256]
            outs.append(jnp.where(first_half, o[0:GRID_W], o[GRID_W:2 * GRID_W]))
        o_ref[pl.ds(qoff, GRID_W), :] = jnp.concatenate(outs, axis=1).astype(o_ref.dtype)
        return carry

    lax.fori_loop(0, ATT_ROWS, row_body, 0, unroll=8)


def _attention(z3, rpb_flat):
    bsz, L, _ = z3.shape
    rows = L // GRID_W
    assert rows % ATT_ROWS == 0 and rows >= KV_ROWS
    nblk = rows // ATT_ROWS
    blk = ATT_ROWS * GRID_W

    def window(col):
        return pl.BlockSpec(
            (pl.Squeezed(), pl.Element(KV_ROWS * GRID_W), pl.Element(D_ATTN)),
            lambda b, i: (b, _kv_window_start(i, rows) * GRID_W, col * D_ATTN))

    return pl.pallas_call(
        functools.partial(_attn_kernel, rows=rows),
        grid=(bsz, nblk),
        in_specs=[
            pl.BlockSpec(memory_space=pltpu.SMEM),
            pl.BlockSpec((pl.Squeezed(), blk, D_ATTN), lambda b, i: (b, i, 0)),
            window(1),
            window(2),
        ],
        out_specs=pl.BlockSpec((pl.Squeezed(), blk, D_ATTN), lambda b, i: (b, i, 0)),
        out_shape=jax.ShapeDtypeStruct((bsz, L, D_ATTN), BF16),
        scratch_shapes=[pltpu.VMEM(((N_HEADS // 2) * N_DR, GRID_W, 128), F32)],
        compiler_params=_cparams(("arbitrary", "arbitrary")),
        name="nbr_attention",
    )(rpb_flat, z3, z3, z3)


TWO_OVER_PI = 0.6366197723675814
PIO2_1 = 1.5703125
PIO2_2 = 4.837512969970703125e-4
PIO2_3 = 7.54978995489188216e-8
TRIG_FAST_LIMIT = 4096.0


def _quadrant_value(x, shift):
    kf = jnp.floor(x * TWO_OVER_PI + 0.5)
    r = ((x - kf * PIO2_1) - kf * PIO2_2) - kf * PIO2_3
    z = r * r
    s = r + r * z * (-1.6666654611e-1 + z * (8.3321608736e-3 + z * -1.9515295891e-4))
    c = (1.0 - 0.5 * z) + z * z * (4.166664568298827e-2
                                   + z * (-1.388731625493765e-3 + z * 2.443315711809948e-5))
    k = kf + shift
    half = jnp.floor(k * 0.5)
    odd = k - 2.0 * half
    flip = half - 2.0 * jnp.floor(half * 0.5)
    return (s + odd * (c - s)) * (1.0 - 2.0 * flip)


def _sin_small(x):
    return _quadrant_value(x, 0.0)


def _cos_small(x):
    return _quadrant_value(x, 1.0)


def _sin(x):
    return lax.cond(jnp.max(jnp.abs(x)) < TRIG_FAST_LIMIT, _sin_small, jnp.sin, x)


def _cos(x):
    return lax.cond(jnp.max(jnp.abs(x)) < TRIG_FAST_LIMIT, _cos_small, jnp.cos, x)


def _hi_lo(x):
    hi = x.astype(BF16)
    return hi, (x - hi.astype(F32)).astype(BF16)


def _stack_cols(w):
    hi, lo = _hi_lo(w)
    return jnp.concatenate([hi, hi, lo], axis=1)


def _stack_rows(h):
    hi, lo = _hi_lo(h)
    return jnp.concatenate([hi, lo, hi], axis=0)


def _dot_stacked(w_stacked, h):
    return jnp.dot(w_stacked, _stack_rows(h), preferred_element_type=F32)


def _filter_kernel(fb_ref, w1t_ref, w1c_ref, w1s_ref, b1_ref, w2_ref, b2_ref, w3_ref, b3_ref,
                   w4_ref, fr_ref, dl_ref, o_ref, *, L, tp):
    j = pl.program_id(0)
    fh = FILTER_HIDDEN
    pos = (lax.broadcasted_iota(jnp.int32, (1, tp), 1) + j * tp).astype(F32)
    t = pos / (L - 1.0)
    omega = (2.0 * math.pi) * pos / float(L)
    ang = fb_ref[...] * omega
    fr = fr_ref[...]
    pre = (w1t_ref[...] * t
           + _dot_stacked(w1c_ref[...], _cos(ang)) - _dot_stacked(w1s_ref[...], _sin(ang)))
    h = _sin(fr * (pre + b1_ref[...]))
    h = _sin(fr * (_dot_stacked(w2_ref[...], h) + b2_ref[...]))
    h = _sin(fr * (_dot_stacked(w3_ref[...], h) + b3_ref[...]))
    h_hi = h.astype(BF16).astype(F32)
    tr = jnp.concatenate([h_hi, h - h_hi], axis=0).T.astype(BF16)
    lane = lax.broadcasted_iota(jnp.int32, tr.shape, 1)
    lhs = jnp.concatenate([tr, jnp.where(lane < fh, tr, jnp.zeros_like(tr))], axis=1)
    out = jnp.dot(lhs, w4_ref[...], preferred_element_type=F32)
    tcol = (lax.broadcasted_iota(jnp.int32, (tp, 1), 0) + j * tp).astype(F32) / (L - 1.0)
    decay = jnp.exp(-tcol * dl_ref[...])
    c = D_HYENA
    o_ref[:, 0:c] = out[:, 0:c] * decay
    o_ref[:, c:2 * c] = out[:, c:2 * c] * decay


def _implicit_filters(L, w1, b1, w2, b2, w3, b3, w4, freq):
    bands = (FILTER_EMB - 1) // 2
    fh = FILTER_HIDDEN
    tp = min(512, L)
    fb = jnp.linspace(1e-4, bands - 1, bands, dtype=F32).reshape(bands, 1)
    max_decay = math.log(DECAY_TARGET) / FAST_DECAY_PCT
    min_decay = math.log(DECAY_TARGET) / SLOW_DECAY_PCT
    deltas = jnp.abs(jnp.linspace(min_decay, max_decay, D_HYENA, dtype=F32)).reshape(1, D_HYENA)
    w1 = w1.astype(F32)
    w4_hi, w4_lo = _hi_lo(w4.astype(F32))
    args = (
        fb,
        w1[0:1, :].T,
        _stack_cols(w1[1:1 + bands, :].T),
        _stack_cols(w1[1 + bands:, :].T),
        b1.astype(F32).reshape(fh, 1),
        _stack_cols(w2.astype(F32).T), b2.astype(F32).reshape(fh, 1),
        _stack_cols(w3.astype(F32).T), b3.astype(F32).reshape(fh, 1),
        jnp.concatenate([w4_hi, w4_hi, w4_lo, jnp.zeros_like(w4_hi)], axis=0),
        freq.astype(F32).reshape(fh, 1),
        deltas,
    )

    def full(a):
        return pl.BlockSpec(a.shape, lambda j: (0,) * a.ndim)

    return pl.pallas_call(
        functools.partial(_filter_kernel, L=L, tp=tp),
        grid=(L // tp,),
        in_specs=[full(a) for a in args],
        out_specs=pl.BlockSpec((tp, 2 * D_HYENA), lambda j: (j, 0)),
        out_shape=jax.ShapeDtypeStruct((L, 2 * D_HYENA), F32),
        compiler_params=_cparams(("parallel",)),
        name="implicit_filter",
    )(*args)


FFT_LANES = 128
FFT_T1 = 8


@functools.lru_cache(maxsize=None)
def _fft_tables(L):
    n = 2 * L
    n2 = FFT_N2
    n1 = n // n2
    n1h = n1 // 2
    odd = 2 * np.arange(n1h) + 1
    th = 2.0 * np.pi * (np.outer(odd, np.arange(n1h)) % (2 * n1)) / (2 * n1)
    f1 = np.concatenate([np.cos(th), -np.sin(th)], axis=0)
    tw = 2.0 * np.pi * (np.outer(odd, np.arange(n2)) % (2 * n)) / (2 * n)
    twr = np.cos(tw)[:, :, None]
    twi = -np.sin(tw)[:, :, None]
    t2 = 2.0 * np.pi * (np.outer(np.arange(n2), np.arange(n2)) % n2) / n2
    fr, fi = np.cos(t2), -np.sin(t2)
    f2 = np.block([[fr, -fi], [fi, fr]])
    f2inv = np.block([[fr, fi], [-fi, fr]])
    f1inv = np.concatenate([np.cos(th.T), -np.sin(th.T)], axis=1) * (2.0 / n)
    return dict(
        n1=n1, n1h=n1h,
        f1=np.asarray(f1, np.float32), f1inv=np.asarray(f1inv, np.float32),
        f2=np.asarray(f2, np.float32), f2inv=np.asarray(f2inv, np.float32),
        twr=np.asarray(twr, np.float32), twi=np.asarray(twi, np.float32),
    )


FFT_A = 16
SLOT_DTYPE = BF16


def _outer_dft(f, src_ref, dst_ref):
    n_in, n1h = src_ref.shape[0], src_ref.shape[1]
    n_out = dst_ref.shape[0]

    def body(i, carry):
        a0 = pl.multiple_of(i * FFT_A, FFT_A)
        ts = [jnp.swapaxes(src_ref[p, :, pl.ds(a0, FFT_A), :].astype(F32), 0, 1)
              for p in range(n_in)]
        outs = [[] for _ in range(n_out)]
        for j in range(0, FFT_A, 2):
            rhs = jnp.concatenate([jnp.concatenate([t[j], t[j + 1]], axis=1) for t in ts],
                                  axis=0).astype(BF16)
            res = jnp.dot(f, rhs, preferred_element_type=F32)
            for q in range(n_out):
                blk = res[q * n1h:(q + 1) * n1h]
                outs[q] += [blk[:, 0:FFT_LANES], blk[:, FFT_LANES:2 * FFT_LANES]]
        for q in range(n_out):
            dst_ref[q, :, pl.ds(a0, FFT_A), :] = jnp.swapaxes(
                jnp.stack(outs[q], axis=0), 0, 1).astype(dst_ref.dtype)
        return carry

    lax.fori_loop(0, FFT_N2 // FFT_A, body, 0, unroll=2)


def _fft_stage1_kernel(f_ref, u_ref, o_ref):
    _outer_dft(f_ref[...], u_ref, o_ref)


def _fft_stage1(u3, f1):
    bsz, L, c = u3.shape
    n1h = L // FFT_N2
    return pl.pallas_call(
        _fft_stage1_kernel,
        grid=(bsz, c // FFT_LANES),
        in_specs=[
            pl.BlockSpec(f1.shape, lambda b, j: (0, 0)),
            pl.BlockSpec((1, n1h, FFT_N2, FFT_LANES), lambda b, j: (b, 0, 0, j)),
        ],
        out_specs=pl.BlockSpec((pl.Squeezed(), 2, n1h, FFT_N2, FFT_LANES),
                               lambda b, j: (b, 0, 0, 0, j)),
        out_shape=jax.ShapeDtypeStruct((bsz, 2, n1h, FFT_N2, c), SLOT_DTYPE),
        compiler_params=_cparams(("parallel", "parallel")),
        name="fft_stage1",
    )(f1, u3.reshape(bsz, n1h, FFT_N2, c))


def _twiddle(ar, ai, tr, ti):
    return ar * tr - ai * ti, ar * ti + ai * tr


def _stage2_inputs(a_ref, twr_ref, twi_ref):
    xs = []
    for j in range(FFT_T1):
        xr, xi = _twiddle(a_ref[0, j].astype(F32), a_ref[1, j].astype(F32),
                          twr_ref[j], twi_ref[j])
        xs.append(jnp.concatenate([xr, xi], axis=0).astype(BF16))
    return xs


def _filter_spectrum_kernel(a_ref, twr_ref, twi_ref, f2_ref, o_ref):
    n2, c = FFT_N2, D_HYENA
    f2 = f2_ref[...]
    bigs = [jnp.dot(f2, x, preferred_element_type=F32)
            for x in _stage2_inputs(a_ref, twr_ref, twi_ref)]
    for j, big in enumerate(bigs):
        o_ref[j, 0] = (big[0:n2, 0:c] + big[0:n2, c:2 * c]).astype(o_ref.dtype)
        o_ref[j, 1] = (big[n2:2 * n2, 0:c] - big[n2:2 * n2, c:2 * c]).astype(o_ref.dtype)


def _filter_spectrum(hfilt, tb):
    L, c2 = hfilt.shape
    n1h = tb["n1h"]
    a5 = _fft_stage1(hfilt.reshape(1, L, c2), tb["f1"])
    return pl.pallas_call(
        _filter_spectrum_kernel,
        grid=(n1h // FFT_T1,),
        in_specs=[
            pl.BlockSpec((pl.Squeezed(), 2, FFT_T1, FFT_N2, c2), lambda k: (0, 0, k, 0, 0)),
            pl.BlockSpec((FFT_T1, FFT_N2, 1), lambda k: (k, 0, 0)),
            pl.BlockSpec((FFT_T1, FFT_N2, 1), lambda k: (k, 0, 0)),
            pl.BlockSpec((2 * FFT_N2, 2 * FFT_N2), lambda k: (0, 0)),
        ],
        out_specs=pl.BlockSpec((FFT_T1, 2, FFT_N2, D_HYENA), lambda k: (k, 0, 0, 0)),
        out_shape=jax.ShapeDtypeStruct((n1h, 2, FFT_N2, D_HYENA), SLOT_DTYPE),
        compiler_params=_cparams(("parallel",)),
        name="filter_spectrum",
    )(a5, tb["twr"], tb["twi"], tb["f2"])


def _fft_mid_kernel(a_ref, kf_ref, twr_ref, twi_ref, f2_ref, f2i_ref, o_ref):
    n2 = FFT_N2
    f2, f2i = f2_ref[...], f2i_ref[...]
    bigs = [jnp.dot(f2, x, preferred_element_type=F32)
            for x in _stage2_inputs(a_ref, twr_ref, twi_ref)]
    ys = []
    for j, big in enumerate(bigs):
        sr, si = big[0:n2], big[n2:2 * n2]
        kr, ki = kf_ref[j, 0].astype(F32), kf_ref[j, 1].astype(F32)
        ys.append(jnp.concatenate([sr * kr - si * ki, sr * ki + si * kr], axis=0).astype(BF16))
    backs = [jnp.dot(f2i, y, preferred_element_type=F32) for y in ys]
    for j, back in enumerate(backs):
        br, bi = back[0:n2], back[n2:2 * n2]
        tr, ti = twr_ref[j], twi_ref[j]
        o_ref[0, j] = (br * tr + bi * ti).astype(o_ref.dtype)
        o_ref[1, j] = (bi * tr - br * ti).astype(o_ref.dtype)


def _fft_mid(a5, kf, tb):
    bsz, _, n1h, n2, c = a5.shape
    slot = pl.BlockSpec((pl.Squeezed(), 2, FFT_T1, n2, c), lambda k, b: (b, 0, k, 0, 0))
    return pl.pallas_call(
        _fft_mid_kernel,
        grid=(n1h // FFT_T1, bsz),
        in_specs=[
            slot,
            pl.BlockSpec((FFT_T1, 2, n2, c), lambda k, b: (k, 0, 0, 0)),
            pl.BlockSpec((FFT_T1, n2, 1), lambda k, b: (k, 0, 0)),
            pl.BlockSpec((FFT_T1, n2, 1), lambda k, b: (k, 0, 0)),
            pl.BlockSpec((2 * n2, 2 * n2), lambda k, b: (0, 0)),
            pl.BlockSpec((2 * n2, 2 * n2), lambda k, b: (0, 0)),
        ],
        out_specs=slot,
        out_shape=jax.ShapeDtypeStruct(a5.shape, SLOT_DTYPE),
        compiler_params=_cparams(("parallel", "parallel")),
        name="fft_mid",
    )(a5, kf, tb["twr"], tb["twi"], tb["f2"], tb["f2inv"])


def _fft_out_kernel(f_ref, b_ref, o_ref):
    _outer_dft(f_ref[...], b_ref, o_ref)


def _fft_out(b5, f1inv):
    bsz, _, n1h, n2, c = b5.shape
    return pl.pallas_call(
        _fft_out_kernel,
        grid=(bsz, c // FFT_LANES),
        in_specs=[
            pl.BlockSpec(f1inv.shape, lambda b, j: (0, 0)),
            pl.BlockSpec((pl.Squeezed(), 2, n1h, n2, FFT_LANES), lambda b, j: (b, 0, 0, 0, j)),
        ],
        out_specs=pl.BlockSpec((1, n1h, n2, FFT_LANES), lambda b, j: (b, 0, 0, j)),
        out_shape=jax.ShapeDtypeStruct((bsz, n1h, n2, c), BF16),
        compiler_params=_cparams(("parallel", "parallel")),
        name="fft_out",
    )(f1inv, b5).reshape(bsz, n1h * n2, c)


def _hyena_conv(u, kf, tb):
    a5 = _fft_stage1(u, tb["f1"])
    b5 = _fft_mid(a5, kf, tb)
    return _fft_out(b5, tb["f1inv"])


def _rms(x, g):
    inv = lax.rsqrt(jnp.mean(x * x, axis=-1, keepdims=True) + EPS)
    return (x * inv) * g


FFN_CHUNK = 256


def _merge_ffn_kernel(ya_ref, yc_ref, u_ref, x0_ref, dk_ref, ga_ref, gh_ref, x_ref, wa_ref, wh_ref,
                      wo_ref, g_ref, wg_ref, wu_ref, wd_ref, gf_ref, o_ref):
    yh = x0_ref[...].astype(F32) * (yc_ref[...].astype(F32)
                                    + u_ref[...].astype(F32) * dk_ref[...])
    pa = jnp.dot(ya_ref[...], wa_ref[...], preferred_element_type=F32)
    ph = jnp.dot(yh.astype(BF16), wh_ref[...], preferred_element_type=F32)
    merged = (jax.nn.sigmoid(ga_ref[...].astype(F32)) * pa
              + jax.nn.sigmoid(gh_ref[...].astype(F32)) * ph)
    x1 = x_ref[...] + jnp.dot(merged.astype(BF16), wo_ref[...], preferred_element_type=F32)
    h = _rms(x1, g_ref[...]).astype(BF16)
    acc = x1
    for c0 in range(0, wg_ref.shape[1], FFN_CHUNK):
        gate = jnp.dot(h, wg_ref[:, c0:c0 + FFN_CHUNK], preferred_element_type=F32)
        up = jnp.dot(h, wu_ref[:, c0:c0 + FFN_CHUNK], preferred_element_type=F32)
        act = (gate * jax.nn.sigmoid(gate) * up).astype(BF16)
        acc = acc + jnp.dot(act, wd_ref[c0:c0 + FFN_CHUNK, :], preferred_element_type=F32)
    o_ref[...] = _rms(acc, gf_ref[...])


def _merge_ffn(ya, yc, u, x0, d_skip, gates, x2, wa, wh, wo, g, wg, wu, wd, gf):
    n, d = x2.shape
    dff = wg.shape[1]
    assert dff % FFN_CHUNK == 0
    tm = min(512, n)
    tok = pl.BlockSpec((tm, D_HYENA), lambda i: (i, 0))

    def resident(a):
        return pl.BlockSpec(a.shape, lambda i: (0, 0), pipeline_mode=pl.Buffered(1))

    def row(width):
        return pl.BlockSpec((1, width), lambda i: (0, 0))

    return pl.pallas_call(
        _merge_ffn_kernel,
        grid=(n // tm,),
        in_specs=[
            pl.BlockSpec((tm, D_ATTN), lambda i: (i, 0)),
            tok, tok, tok,
            row(D_HYENA),
            pl.BlockSpec((tm, d), lambda i: (i, 0)),
            pl.BlockSpec((tm, d), lambda i: (i, 1)),
            pl.BlockSpec((tm, d), lambda i: (i, 0)),
            resident(wa), resident(wh), resident(wo),
            row(d),
            resident(wg), resident(wu), resident(wd),
            row(d),
        ],
        out_specs=pl.BlockSpec((tm, d), lambda i: (i, 0)),
        out_shape=jax.ShapeDtypeStruct((n, d), F32),
        compiler_params=_cparams(("parallel",)),
        name="merge_ffn",
    )(ya, yc, u, x0, d_skip.astype(F32).reshape(1, D_HYENA), gates, gates, x2, wa, wh, wo,
      g.reshape(1, d), wg, wu, wd, gf.reshape(1, d))


def _trunk(x, p, norm_final):
    bsz, L, d = x.shape
    n = bsz * L
    x2 = x.reshape(n, d)
    qkv, u, x0, gates = _inproj_conv(x2, L, p["norm_mix"], p["w_in"], p["conv_w"], p["conv_b"])
    ya = _attention(qkv.reshape(bsz, L, qkv.shape[1]), p["rpb"])
    u = u.reshape(bsz, L, D_HYENA)
    tb = dict(_fft_tables(L))
    for name in ("f1", "f1inv", "f2", "f2inv"):
        tb[name] = jnp.asarray(tb[name]).astype(BF16)
    hfilt = _implicit_filters(L, *p["filt"])
    kf = _filter_spectrum(hfilt, tb)
    yc = _hyena_conv(u, kf, tb)
    out = _merge_ffn(ya.reshape(n, D_ATTN), yc.reshape(n, D_HYENA), u.reshape(n, D_HYENA),
                     x0, p["hyena_d"], gates, x2,
                     p["w_br_attn"], p["w_br_hyena"], p["w_out"],
                     p["norm_ffn"], p["w_gate"], p["w_up"], p["w_down"], norm_final)
    return out.reshape(bsz, L, d)


def kernel(x_prompt, x_sample, norm_mix, w_in, rpb, conv_w, conv_b, filt_w1, filt_b1, filt_w2,
           filt_b2, filt_w3, filt_b3, filt_w4, filt_freq, hyena_d, w_br_attn, w_br_hyena, w_out,
           norm_ffn, w_gate, w_up, w_down, norm_final):
    assert w_in.shape[0] == 1, "the layer definition has depth 1"
    col = lax.broadcasted_iota(jnp.int32, (1, w_in.shape[2]), 1)
    col_scale = jnp.where(col < D_ATTN, Q_SCALE, 1.0).astype(F32)
    p = dict(
        norm_mix=norm_mix[0], w_in=(w_in[0] * col_scale).astype(BF16), rpb=rpb[0].reshape(-1),
        conv_w=conv_w[0], conv_b=conv_b[0],
        filt=(filt_w1[0], filt_b1[0], filt_w2[0], filt_b2[0], filt_w3[0], filt_b3[0],
              filt_w4[0], filt_freq[0]),
        hyena_d=hyena_d[0],
        w_br_attn=w_br_attn[0].astype(BF16), w_br_hyena=w_br_hyena[0].astype(BF16),
        w_out=w_out[0].astype(BF16), norm_ffn=norm_ffn[0],
        w_gate=w_gate[0].astype(BF16), w_up=w_up[0].astype(BF16),
        w_down=w_down[0].astype(BF16),
    )
    return (_trunk(x_prompt, p, norm_final), _trunk(x_sample, p, norm_final))
```

```python
import functools
import math

import numpy as np
import jax
import jax.numpy as jnp
from jax import lax
from jax.experimental import pallas as pl
from jax.experimental.pallas import tpu as pltpu

F32 = jnp.float32
BF16 = jnp.bfloat16

GRID_W = 64
N_HEADS = 8
HEAD_DIM = 64
D_ATTN = N_HEADS * HEAD_DIM
WIN_R = 8
WIN_C = 16
D_HYENA = 512
SHORT_K = 3
FILTER_EMB = 33
FILTER_HIDDEN = 64
DECAY_TARGET = 1e-2
FAST_DECAY_PCT = 0.3
SLOW_DECAY_PCT = 1.5
EPS = 1e-6

FFT_N2 = 128
ATT_ROWS = 16
NEG_BIG = -1e30
VMEM_LIMIT = 56 * 1024 * 1024


def _cparams(sem):
    return pltpu.CompilerParams(dimension_semantics=sem, vmem_limit_bytes=VMEM_LIMIT)


IN_CHUNK = 512
HALO = 8


def _inproj_conv_kernel(x_ref, xp_ref, xn_ref, g_ref, wqkv_ref, why_ref, wgt_ref, cw_ref, cb_ref,
                        qkv_ref, u_ref, x0_ref, gt_ref, *, tiles_per_seq):
    i = pl.program_id(0)
    g = g_ref[...]
    h = _rms(x_ref[...], g).astype(BF16)
    hh = _rms(jnp.concatenate([xp_ref[...], xn_ref[...]], axis=0), g).astype(BF16)
    for w_ref, o_ref in ((wqkv_ref, qkv_ref), (wgt_ref, gt_ref)):
        for c0 in range(0, w_ref.shape[1], IN_CHUNK):
            o_ref[:, c0:c0 + IN_CHUNK] = jnp.dot(
                h, w_ref[:, c0:c0 + IN_CHUNK], preferred_element_type=F32).astype(o_ref.dtype)
    tm = h.shape[0]
    c = D_HYENA
    row = lax.broadcasted_iota(jnp.int32, (tm, c), 0)
    first = (i % tiles_per_seq) == 0
    last = (i % tiles_per_seq) == tiles_per_seq - 1
    ys = []
    for k in range(3):
        w = why_ref[:, k * c:(k + 1) * c]
        hy = jnp.dot(h, w, preferred_element_type=F32)
        hyh = jnp.dot(hh, w, preferred_element_type=F32)
        prev_row = jnp.where(first, 0.0, hyh[HALO - 1:HALO])
        next_row = jnp.where(last, 0.0, hyh[HALO:HALO + 1])
        xm1 = jnp.where(row == 0, prev_row, pltpu.roll(hy, 1, 0))
        xp1 = jnp.where(row == tm - 1, next_row, pltpu.roll(hy, tm - 1, 0))
        cw = cw_ref[:, k * c:(k + 1) * c]
        ys.append(xm1 * cw[0:1] + hy * cw[1:2] + xp1 * cw[2:3] + cb_ref[:, k * c:(k + 1) * c])
    x0_ref[...] = ys[0].astype(x0_ref.dtype)
    u_ref[...] = (ys[1] * ys[2]).astype(u_ref.dtype)


def _inproj_conv(x2, L, g, w_bf16, conv_w, conv_b):
    n, d = x2.shape
    n_qkv, n_hy = 3 * D_ATTN, 3 * D_HYENA
    tm = min(1024, L)
    nh = n // HALO
    hb = tm // HALO

    def resident(a):
        return pl.BlockSpec(a.shape, lambda i: (0, 0), pipeline_mode=pl.Buffered(1))

    wqkv, why, wgt = w_bf16[:, :n_qkv], w_bf16[:, n_qkv:n_qkv + n_hy], w_bf16[:, n_qkv + n_hy:]
    widths = (n_qkv, D_HYENA, D_HYENA, wgt.shape[1])
    return pl.pallas_call(
        functools.partial(_inproj_conv_kernel, tiles_per_seq=L // tm),
        grid=(n // tm,),
        in_specs=[
            pl.BlockSpec((tm, d), lambda i: (i, 0)),
            pl.BlockSpec((HALO, d), lambda i: (jnp.maximum(i * hb - 1, 0), 0)),
            pl.BlockSpec((HALO, d), lambda i: (jnp.minimum((i + 1) * hb, nh - 1), 0)),
            pl.BlockSpec((1, d), lambda i: (0, 0)),
            resident(wqkv), resident(why), resident(wgt),
            pl.BlockSpec((SHORT_K, n_hy), lambda i: (0, 0)),
            pl.BlockSpec((1, n_hy), lambda i: (0, 0)),
        ],
        out_specs=[pl.BlockSpec((tm, w), lambda i: (i, 0)) for w in widths],
        out_shape=[jax.ShapeDtypeStruct((n, w), BF16) for w in widths],
        compiler_params=_cparams(("parallel",)),
        name="inproj_conv",
    )(x2, x2, x2, g.reshape(1, d), wqkv, why, wgt, conv_w, conv_b.reshape(1, n_hy))


N_DR = 2 * WIN_R - 1
N_DC = 2 * WIN_C - 1
LOG2_E = 1.4426950408889634
Q_SCALE = (HEAD_DIM ** -0.5) * LOG2_E
KV_ROWS = ATT_ROWS + WIN_R


def _build_bias_tiles(rpb_ref, pt_ref):
    k = lax.broadcasted_iota(jnp.int32, (GRID_W, 128), 0)
    lane = lax.broadcasted_iota(jnp.int32, (GRID_W, 128), 1)
    q = lane & (GRID_W - 1)
    first_half = lane < GRID_W
    cs = jnp.clip(q - WIN_C // 2, 0, GRID_W - WIN_C)
    valid = (k >= cs) & (k < cs + WIN_C)
    dc = jnp.clip(k - q, -(WIN_C - 1), WIN_C - 1) + (WIN_C - 1)

    def body(idx, carry):
        p = idx // N_DR
        d = idx % N_DR
        base0 = ((2 * p) * N_DR + d) * N_DC
        base1 = ((2 * p + 1) * N_DR + d) * N_DC
        acc = jnp.zeros((GRID_W, 128), F32)
        for off in range(N_DC):
            val = jnp.where(first_half, rpb_ref[base0 + off], rpb_ref[base1 + off])
            acc = jnp.where(dc == off, val, acc)
        pt_ref[idx] = jnp.where(valid, acc * LOG2_E, NEG_BIG)
        return carry

    lax.fori_loop(0, (N_HEADS // 2) * N_DR, body, 0)


def _kv_window_start(i, rows):
    return jnp.clip(i * ATT_ROWS - WIN_R // 2, 0, rows - KV_ROWS)


def _attn_kernel(rpb_ref, q_ref, k_ref, v_ref, o_ref, pt_ref, *, rows):
    b = pl.program_id(0)
    i = pl.program_id(1)

    @pl.when((b == 0) & (i == 0))
    def _():
        _build_bias_tiles(rpb_ref, pt_ref)

    lane = lax.broadcasted_iota(jnp.int32, (GRID_W, 128), 1)
    first_half = lane < HEAD_DIM
    ones = jnp.ones((WIN_R * GRID_W, 128), BF16)
    nkeys = WIN_R * GRID_W
    wstart = _kv_window_start(i, rows)
    nt_dims = (((1,), (1,)), ((), ()))

    def row_body(rr, carry):
        r = i * ATT_ROWS + rr
        rs = jnp.clip(r - WIN_R // 2, 0, rows - WIN_R)
        s = rs - r + (WIN_R - 1)
        koff = pl.multiple_of((rs - wstart) * GRID_W, GRID_W)
        qoff = pl.multiple_of(rr * GRID_W, GRID_W)
        q = q_ref[pl.ds(qoff, GRID_W), :]
        pairs = range(N_HEADS // 2)
        sts = []
        for p in pairs:
            qp = q[:, 128 * p:128 * (p + 1)]
            zero = jnp.zeros_like(qp)
            wt = jnp.concatenate([jnp.where(first_half, qp, zero),
                                  jnp.where(first_half, zero, qp)], axis=0)
            kp = k_ref[pl.ds(koff, nkeys), 128 * p:128 * (p + 1)]
            sts.append(lax.dot_general(kp, wt, nt_dims, preferred_element_type=F32))
        pms = []
        for p in pairs:
            st = sts[p]
            ch = [st[GRID_W * c:GRID_W * (c + 1)] + pt_ref[p * N_DR + s + c] for c in range(WIN_R)]
            m = ch[0]
            for c in range(1, WIN_R):
                m = jnp.maximum(m, ch[c])
            m = jnp.max(m, axis=0, keepdims=True)
            pt = jnp.concatenate([jnp.exp2(c - m) for c in ch], axis=0).astype(BF16)
            pms.append(pt.T)
        outs = []
        for p in pairs:
            vp = v_ref[pl.ds(koff, nkeys), 128 * p:128 * (p + 1)]
            ov = jnp.dot(pms[p], jnp.concatenate([vp, ones], axis=1), preferred_element_type=F32)
            o = ov[:, 0:128] / ov[:, 128:256]
            outs.append(jnp.where(first_half, o[0:GRID_W], o[GRID_W:2 * GRID_W]))
        o_ref[pl.ds(qoff, GRID_W), :] = jnp.concatenate(outs, axis=1).astype(o_ref.dtype)
        return carry

    lax.fori_loop(0, ATT_ROWS, row_body, 0, unroll=8)


def _attention(z3, rpb_flat):
    bsz, L, _ = z3.shape
    rows = L // GRID_W
    assert rows % ATT_ROWS == 0 and rows >= KV_ROWS
    nblk = rows // ATT_ROWS
    blk = ATT_ROWS * GRID_W

    def window(col):
        return pl.BlockSpec(
            (pl.Squeezed(), pl.Element(KV_ROWS * GRID_W), pl.Element(D_ATTN)),
            lambda b, i: (b, _kv_window_start(i, rows) * GRID_W, col * D_ATTN))

    return pl.pallas_call(
        functools.partial(_attn_kernel, rows=rows),
        grid=(bsz, nblk),
        in_specs=[
            pl.BlockSpec(memory_space=pltpu.SMEM),
            pl.BlockSpec((pl.Squeezed(), blk, D_ATTN), lambda b, i: (b, i, 0)),
            window(1),
            window(2),
        ],
        out_specs=pl.BlockSpec((pl.Squeezed(), blk, D_ATTN), lambda b, i: (b, i, 0)),
        out_shape=jax.ShapeDtypeStruct((bsz, L, D_ATTN), BF16),
        scratch_shapes=[pltpu.VMEM(((N_HEADS // 2) * N_DR, GRID_W, 128), F32)],
        compiler_params=_cparams(("arbitrary", "arbitrary")),
        name="nbr_attention",
    )(rpb_flat, z3, z3, z3)


TWO_OVER_PI = 0.6366197723675814
PIO2_1 = 1.5703125
PIO2_2 = 4.837512969970703125e-4
PIO2_3 = 7.54978995489188216e-8
TRIG_FAST_LIMIT = 4096.0


def _quadrant_value(x, shift):
    kf = jnp.floor(x * TWO_OVER_PI + 0.5)
    r = ((x - kf * PIO2_1) - kf * PIO2_2) - kf * PIO2_3
    z = r * r
    s = r + r * z * (-1.6666654611e-1 + z * (8.3321608736e-3 + z * -1.9515295891e-4))
    c = (1.0 - 0.5 * z) + z * z * (4.166664568298827e-2
                                   + z * (-1.388731625493765e-3 + z * 2.443315711809948e-5))
    k = kf + shift
    half = jnp.floor(k * 0.5)
    odd = k - 2.0 * half
    flip = half - 2.0 * jnp.floor(half * 0.5)
    return (s + odd * (c - s)) * (1.0 - 2.0 * flip)


def _sin_small(x):
    return _quadrant_value(x, 0.0)


def _cos_small(x):
    return _quadrant_value(x, 1.0)


def _sin(x):
    return lax.cond(jnp.max(jnp.abs(x)) < TRIG_FAST_LIMIT, _sin_small, jnp.sin, x)


def _cos(x):
    return lax.cond(jnp.max(jnp.abs(x)) < TRIG_FAST_LIMIT, _cos_small, jnp.cos, x)


def _hi_lo(x):
    hi = x.astype(BF16)
    return hi, (x - hi.astype(F32)).astype(BF16)


def _stack_cols(w):
    hi, lo = _hi_lo(w)
    return jnp.concatenate([hi, hi, lo], axis=1)


def _stack_rows(h):
    hi, lo = _hi_lo(h)
    return jnp.concatenate([hi, lo, hi], axis=0)


def _dot_stacked(w_stacked, h):
    return jnp.dot(w_stacked, _stack_rows(h), preferred_element_type=F32)


def _filter_kernel(fb_ref, w1t_ref, w1c_ref, w1s_ref, b1_ref, w2_ref, b2_ref, w3_ref, b3_ref,
                   w4_ref, fr_ref, dl_ref, o_ref, *, L, tp):
    j = pl.program_id(0)
    fh = FILTER_HIDDEN
    pos = (lax.broadcasted_iota(jnp.int32, (1, tp), 1) + j * tp).astype(F32)
    t = pos / (L - 1.0)
    omega = (2.0 * math.pi) * pos / float(L)
    ang = fb_ref[...] * omega
    fr = fr_ref[...]
    pre = (w1t_ref[...] * t
           + _dot_stacked(w1c_ref[...], _cos(ang)) - _dot_stacked(w1s_ref[...], _sin(ang)))
    h = _sin(fr * (pre + b1_ref[...]))
    h = _sin(fr * (_dot_stacked(w2_ref[...], h) + b2_ref[...]))
    h = _sin(fr * (_dot_stacked(w3_ref[...], h) + b3_ref[...]))
    h_hi = h.astype(BF16).astype(F32)
    tr = jnp.concatenate([h_hi, h - h_hi], axis=0).T.astype(BF16)
    lane = lax.broadcasted_iota(jnp.int32, tr.shape, 1)
    lhs = jnp.concatenate([tr, jnp.where(lane < fh, tr, jnp.zeros_like(tr))], axis=1)
    out = jnp.dot(lhs, w4_ref[...], preferred_element_type=F32)
    tcol = (lax.broadcasted_iota(jnp.int32, (tp, 1), 0) + j * tp).astype(F32) / (L - 1.0)
    decay = jnp.exp(-tcol * dl_ref[...])
    c = D_HYENA
    o_ref[:, 0:c] = out[:, 0:c] * decay
    o_ref[:, c:2 * c] = out[:, c:2 * c] * decay


def _implicit_filters(L, w1, b1, w2, b2, w3, b3, w4, freq):
    bands = (FILTER_EMB - 1) // 2
    fh = FILTER_HIDDEN
    tp = min(1024, L)
    fb = jnp.linspace(1e-4, bands - 1, bands, dtype=F32).reshape(bands, 1)
    max_decay = math.log(DECAY_TARGET) / FAST_DECAY_PCT
    min_decay = math.log(DECAY_TARGET) / SLOW_DECAY_PCT
    deltas = jnp.abs(jnp.linspace(min_decay, max_decay, D_HYENA, dtype=F32)).reshape(1, D_HYENA)
    w1 = w1.astype(F32)
    w4_hi, w4_lo = _hi_lo(w4.astype(F32))
    args = (
        fb,
        w1[0:1, :].T,
        _stack_cols(w1[1:1 + bands, :].T),
        _stack_cols(w1[1 + bands:, :].T),
        b1.astype(F32).reshape(fh, 1),
        _stack_cols(w2.astype(F32).T), b2.astype(F32).reshape(fh, 1),
        _stack_cols(w3.astype(F32).T), b3.astype(F32).reshape(fh, 1),
        jnp.concatenate([w4_hi, w4_hi, w4_lo, jnp.zeros_like(w4_hi)], axis=0),
        freq.astype(F32).reshape(fh, 1),
        deltas,
    )

    def full(a):
        return pl.BlockSpec(a.shape, lambda j: (0,) * a.ndim)

    return pl.pallas_call(
        functools.partial(_filter_kernel, L=L, tp=tp),
        grid=(L // tp,),
        in_specs=[full(a) for a in args],
        out_specs=pl.BlockSpec((tp, 2 * D_HYENA), lambda j: (j, 0)),
        out_shape=jax.ShapeDtypeStruct((L, 2 * D_HYENA), F32),
        compiler_params=_cparams(("parallel",)),
        name="implicit_filter",
    )(*args)


FFT_LANES = 128
FFT_T1 = 16


@functools.lru_cache(maxsize=None)
def _fft_tables(L):
    n = 2 * L
    n2 = FFT_N2
    n1 = n // n2
    n1h = n1 // 2
    odd = 2 * np.arange(n1h) + 1
    th = 2.0 * np.pi * (np.outer(odd, np.arange(n1h)) % (2 * n1)) / (2 * n1)
    f1 = np.concatenate([np.cos(th), -np.sin(th)], axis=0)
    tw = 2.0 * np.pi * (np.outer(odd, np.arange(n2)) % (2 * n)) / (2 * n)
    twr = np.cos(tw)[:, :, None]
    twi = -np.sin(tw)[:, :, None]
    t2 = 2.0 * np.pi * (np.outer(np.arange(n2), np.arange(n2)) % n2) / n2
    fr, fi = np.cos(t2), -np.sin(t2)
    f2 = np.block([[fr, -fi], [fi, fr]])
    f2inv = np.block([[fr, fi], [-fi, fr]])
    f1inv = np.concatenate([np.cos(th.T), -np.sin(th.T)], axis=1) * (2.0 / n)
    return dict(
        n1=n1, n1h=n1h,
        f1=np.asarray(f1, np.float32), f1inv=np.asarray(f1inv, np.float32),
        f2=np.asarray(f2, np.float32), f2inv=np.asarray(f2inv, np.float32),
        twr=np.asarray(twr, np.float32), twi=np.asarray(twi, np.float32),
    )


FFT_A = 16
SLOT_DTYPE = BF16


def _outer_dft(f, src_ref, dst_ref):
    n_in, n1h = src_ref.shape[0], src_ref.shape[1]
    n_out = dst_ref.shape[0]

    def body(i, carry):
        a0 = pl.multiple_of(i * FFT_A, FFT_A)
        ts = [jnp.swapaxes(src_ref[p, :, pl.ds(a0, FFT_A), :].astype(F32), 0, 1)
              for p in range(n_in)]
        outs = [[] for _ in range(n_out)]
        for j in range(0, FFT_A, 2):
            rhs = jnp.concatenate([jnp.concatenate([t[j], t[j + 1]], axis=1) for t in ts],
                                  axis=0).astype(BF16)
            res = jnp.dot(f, rhs, preferred_element_type=F32)
            for q in range(n_out):
                blk = res[q * n1h:(q + 1) * n1h]
                outs[q] += [blk[:, 0:FFT_LANES], blk[:, FFT_LANES:2 * FFT_LANES]]
        for q in range(n_out):
            dst_ref[q, :, pl.ds(a0, FFT_A), :] = jnp.swapaxes(
                jnp.stack(outs[q], axis=0), 0, 1).astype(dst_ref.dtype)
        return carry

    lax.fori_loop(0, FFT_N2 // FFT_A, body, 0, unroll=2)


def _fft_stage1_kernel(f_ref, u_ref, o_ref):
    _outer_dft(f_ref[...], u_ref, o_ref)


def _fft_stage1(u3, f1):
    bsz, L, c = u3.shape
    n1h = L // FFT_N2
    return pl.pallas_call(
        _fft_stage1_kernel,
        grid=(bsz, c // FFT_LANES),
        in_specs=[
            pl.BlockSpec(f1.shape, lambda b, j: (0, 0)),
            pl.BlockSpec((1, n1h, FFT_N2, FFT_LANES), lambda b, j: (b, 0, 0, j)),
        ],
        out_specs=pl.BlockSpec((pl.Squeezed(), 2, n1h, FFT_N2, FFT_LANES),
                               lambda b, j: (b, 0, 0, 0, j)),
        out_shape=jax.ShapeDtypeStruct((bsz, 2, n1h, FFT_N2, c), SLOT_DTYPE),
        compiler_params=_cparams(("parallel", "parallel")),
        name="fft_stage1",
    )(f1, u3.reshape(bsz, n1h, FFT_N2, c))


def _twiddle(ar, ai, tr, ti):
    return ar * tr - ai * ti, ar * ti + ai * tr


def _stage2_inputs(a_ref, twr_ref, twi_ref):
    xs = []
    for j in range(FFT_T1):
        xr, xi = _twiddle(a_ref[0, j].astype(F32), a_ref[1, j].astype(F32),
                          twr_ref[j], twi_ref[j])
        xs.append(jnp.concatenate([xr, xi], axis=0).astype(BF16))
    return xs


def _filter_spectrum_kernel(a_ref, twr_ref, twi_ref, f2_ref, o_ref):
    n2, c = FFT_N2, D_HYENA
    f2 = f2_ref[...]
    bigs = [jnp.dot(f2, x, preferred_element_type=F32)
            for x in _stage2_inputs(a_ref, twr_ref, twi_ref)]
    for j, big in enumerate(bigs):
        o_ref[j, 0] = (big[0:n2, 0:c] + big[0:n2, c:2 * c]).astype(o_ref.dtype)
        o_ref[j, 1] = (big[n2:2 * n2, 0:c] - big[n2:2 * n2, c:2 * c]).astype(o_ref.dtype)


def _filter_spectrum(hfilt, tb):
    L, c2 = hfilt.shape
    n1h = tb["n1h"]
    a5 = _fft_stage1(hfilt.reshape(1, L, c2), tb["f1"])
    return pl.pallas_call(
        _filter_spectrum_kernel,
        grid=(n1h // FFT_T1,),
        in_specs=[
            pl.BlockSpec((pl.Squeezed(), 2, FFT_T1, FFT_N2, c2), lambda k: (0, 0, k, 0, 0)),
            pl.BlockSpec((FFT_T1, FFT_N2, 1), lambda k: (k, 0, 0)),
            pl.BlockSpec((FFT_T1, FFT_N2, 1), lambda k: (k, 0, 0)),
            pl.BlockSpec((2 * FFT_N2, 2 * FFT_N2), lambda k: (0, 0)),
        ],
        out_specs=pl.BlockSpec((FFT_T1, 2, FFT_N2, D_HYENA), lambda k: (k, 0, 0, 0)),
        out_shape=jax.ShapeDtypeStruct((n1h, 2, FFT_N2, D_HYENA), SLOT_DTYPE),
        compiler_params=_cparams(("parallel",)),
        name="filter_spectrum",
    )(a5, tb["twr"], tb["twi"], tb["f2"])


def _fft_mid_kernel(a_ref, kf_ref, twr_ref, twi_ref, f2_ref, f2i_ref, o_ref):
    n2 = FFT_N2
    f2, f2i = f2_ref[...], f2i_ref[...]
    bigs = [jnp.dot(f2, x, preferred_element_type=F32)
            for x in _stage2_inputs(a_ref, twr_ref, twi_ref)]
    ys = []
    for j, big in enumerate(bigs):
        sr, si = big[0:n2], big[n2:2 * n2]
        kr, ki = kf_ref[j, 0].astype(F32), kf_ref[j, 1].astype(F32)
        ys.append(jnp.concatenate([sr * kr - si * ki, sr * ki + si * kr], axis=0).astype(BF16))
    backs = [jnp.dot(f2i, y, preferred_element_type=F32) for y in ys]
    for j, back in enumerate(backs):
        br, bi = back[0:n2], back[n2:2 * n2]
        tr, ti = twr_ref[j], twi_ref[j]
        o_ref[0, j] = (br * tr + bi * ti).astype(o_ref.dtype)
        o_ref[1, j] = (bi * tr - br * ti).astype(o_ref.dtype)


def _fft_mid(a5, kf, tb):
    bsz, _, n1h, n2, c = a5.shape
    slot = pl.BlockSpec((pl.Squeezed(), 2, FFT_T1, n2, c), lambda k, b: (b, 0, k, 0, 0))
    return pl.pallas_call(
        _fft_mid_kernel,
        grid=(n1h // FFT_T1, bsz),
        in_specs=[
            slot,
            pl.BlockSpec((FFT_T1, 2, n2, c), lambda k, b: (k, 0, 0, 0)),
            pl.BlockSpec((FFT_T1, n2, 1), lambda k, b: (k, 0, 0)),
            pl.BlockSpec((FFT_T1, n2, 1), lambda k, b: (k, 0, 0)),
            pl.BlockSpec((2 * n2, 2 * n2), lambda k, b: (0, 0)),
            pl.BlockSpec((2 * n2, 2 * n2), lambda k, b: (0, 0)),
        ],
        out_specs=slot,
        out_shape=jax.ShapeDtypeStruct(a5.shape, SLOT_DTYPE),
        compiler_params=_cparams(("parallel", "parallel")),
        name="fft_mid",
    )(a5, kf, tb["twr"], tb["twi"], tb["f2"], tb["f2inv"])


def _fft_out_kernel(f_ref, b_ref, o_ref):
    _outer_dft(f_ref[...], b_ref, o_ref)


def _fft_out(b5, f1inv):
    bsz, _, n1h, n2, c = b5.shape
    return pl.pallas_call(
        _fft_out_kernel,
        grid=(bsz, c // FFT_LANES),
        in_specs=[
            pl.BlockSpec(f1inv.shape, lambda b, j: (0, 0)),
            pl.BlockSpec((pl.Squeezed(), 2, n1h, n2, FFT_LANES), lambda b, j: (b, 0, 0, 0, j)),
        ],
        out_specs=pl.BlockSpec((1, n1h, n2, FFT_LANES), lambda b, j: (b, 0, 0, j)),
        out_shape=jax.ShapeDtypeStruct((bsz, n1h, n2, c), BF16),
        compiler_params=_cparams(("parallel", "parallel")),
        name="fft_out",
    )(f1inv, b5).reshape(bsz, n1h * n2, c)


def _hyena_conv(u, kf, tb):
    a5 = _fft_stage1(u, tb["f1"])
    b5 = _fft_mid(a5, kf, tb)
    return _fft_out(b5, tb["f1inv"])


def _rms(x, g):
    inv = lax.rsqrt(jnp.mean(x * x, axis=-1, keepdims=True) + EPS)
    return (x * inv) * g


FFN_CHUNK = 256


def _merge_ffn_kernel(ya_ref, yc_ref, u_ref, x0_ref, dk_ref, ga_ref, gh_ref, x_ref, wa_ref, wh_ref,
                      wo_ref, g_ref, wg_ref, wu_ref, wd_ref, gf_ref, o_ref):
    yh = x0_ref[...].astype(F32) * (yc_ref[...].astype(F32)
                                    + u_ref[...].astype(F32) * dk_ref[...])
    pa = jnp.dot(ya_ref[...], wa_ref[...], preferred_element_type=F32)
    ph = jnp.dot(yh.astype(BF16), wh_ref[...], preferred_element_type=F32)
    merged = (jax.nn.sigmoid(ga_ref[...].astype(F32)) * pa
              + jax.nn.sigmoid(gh_ref[...].astype(F32)) * ph)
    x1 = x_ref[...] + jnp.dot(merged.astype(BF16), wo_ref[...], preferred_element_type=F32)
    h = _rms(x1, g_ref[...]).astype(BF16)
    acc = x1
    for c0 in range(0, wg_ref.shape[1], FFN_CHUNK):
        gate = jnp.dot(h, wg_ref[:, c0:c0 + FFN_CHUNK], preferred_element_type=F32)
        up = jnp.dot(h, wu_ref[:, c0:c0 + FFN_CHUNK], preferred_element_type=F32)
        act = (gate * jax.nn.sigmoid(gate) * up).astype(BF16)
        acc = acc + jnp.dot(act, wd_ref[c0:c0 + FFN_CHUNK, :], preferred_element_type=F32)
    o_ref[...] = _rms(acc, gf_ref[...])


def _merge_ffn(ya, yc, u, x0, d_skip, gates, x2, wa, wh, wo, g, wg, wu, wd, gf):
    n, d = x2.shape
    dff = wg.shape[1]
    assert dff % FFN_CHUNK == 0
    tm = min(512, n)
    tok = pl.BlockSpec((tm, D_HYENA), lambda i: (i, 0))

    def resident(a):
        return pl.BlockSpec(a.shape, lambda i: (0, 0), pipeline_mode=pl.Buffered(1))

    def row(width):
        return pl.BlockSpec((1, width), lambda i: (0, 0))

    return pl.pallas_call(
        _merge_ffn_kernel,
        grid=(n // tm,),
        in_specs=[
            pl.BlockSpec((tm, D_ATTN), lambda i: (i, 0)),
            tok, tok, tok,
            row(D_HYENA),
            pl.BlockSpec((tm, d), lambda i: (i, 0)),
            pl.BlockSpec((tm, d), lambda i: (i, 1)),
            pl.BlockSpec((tm, d), lambda i: (i, 0)),
            resident(wa), resident(wh), resident(wo),
            row(d),
            resident(wg), resident(wu), resident(wd),
            row(d),
        ],
        out_specs=pl.BlockSpec((tm, d), lambda i: (i, 0)),
        out_shape=jax.ShapeDtypeStruct((n, d), F32),
        compiler_params=_cparams(("parallel",)),
        name="merge_ffn",
    )(ya, yc, u, x0, d_skip.astype(F32).reshape(1, D_HYENA), gates, gates, x2, wa, wh, wo,
      g.reshape(1, d), wg, wu, wd, gf.reshape(1, d))


def _trunk(x, p, norm_final):
    bsz, L, d = x.shape
    n = bsz * L
    x2 = x.reshape(n, d)
    qkv, u, x0, gates = _inproj_conv(x2, L, p["norm_mix"], p["w_in"], p["conv_w"], p["conv_b"])
    ya = _attention(qkv.reshape(bsz, L, qkv.shape[1]), p["rpb"])
    u = u.reshape(bsz, L, D_HYENA)
    tb = dict(_fft_tables(L))
    for name in ("f1", "f1inv", "f2", "f2inv"):
        tb[name] = jnp.asarray(tb[name]).astype(BF16)
    hfilt = _implicit_filters(L, *p["filt"])
    kf = _filter_spectrum(hfilt, tb)
    yc = _hyena_conv(u, kf, tb)
    out = _merge_ffn(ya.reshape(n, D_ATTN), yc.reshape(n, D_HYENA), u.reshape(n, D_HYENA),
                     x0, p["hyena_d"], gates, x2,
                     p["w_br_attn"], p["w_br_hyena"], p["w_out"],
                     p["norm_ffn"], p["w_gate"], p["w_up"], p["w_down"], norm_final)
    return out.reshape(bsz, L, d)


def kernel(x_prompt, x_sample, norm_mix, w_in, rpb, conv_w, conv_b, filt_w1, filt_b1, filt_w2,
           filt_b2, filt_w3, filt_b3, filt_w4, filt_freq, hyena_d, w_br_attn, w_br_hyena, w_out,
           norm_ffn, w_gate, w_up, w_down, norm_final):
    assert w_in.shape[0] == 1, "the layer definition has depth 1"
    col = lax.broadcasted_iota(jnp.int32, (1, w_in.shape[2]), 1)
    col_scale = jnp.where(col < D_ATTN, Q_SCALE, 1.0).astype(F32)
    p = dict(
        norm_mix=norm_mix[0], w_in=(w_in[0] * col_scale).astype(BF16), rpb=rpb[0].reshape(-1),
        conv_w=conv_w[0], conv_b=conv_b[0],
        filt=(filt_w1[0], filt_b1[0], filt_w2[0], filt_b2[0], filt_w3[0], filt_b3[0],
              filt_w4[0], filt_freq[0]),
        hyena_d=hyena_d[0],
        w_br_attn=w_br_attn[0].astype(BF16), w_br_hyena=w_br_hyena[0].astype(BF16),
        w_out=w_out[0].astype(BF16), norm_ffn=norm_ffn[0],
        w_gate=w_gate[0].astype(BF16), w_up=w_up[0].astype(BF16),
        w_down=w_down[0].astype(BF16),
    )
    return (_trunk(x_prompt, p, norm_final), _trunk(x_sample, p, norm_final))
```

```python
import functools
import math

import numpy as np
import jax
import jax.numpy as jnp
from jax import lax
from jax.experimental import pallas as pl
from jax.experimental.pallas import tpu as pltpu

F32 = jnp.float32
BF16 = jnp.bfloat16

GRID_W = 64
N_HEADS = 8
HEAD_DIM = 64
D_ATTN = N_HEADS * HEAD_DIM
WIN_R = 8
WIN_C = 16
D_HYENA = 512
SHORT_K = 3
FILTER_EMB = 33
FILTER_HIDDEN = 64
DECAY_TARGET = 1e-2
FAST_DECAY_PCT = 0.3
SLOW_DECAY_PCT = 1.5
EPS = 1e-6

FFT_N2 = 128
ATT_ROWS = 16
NEG_BIG = -1e30
VMEM_LIMIT = 56 * 1024 * 1024


def _cparams(sem):
    return pltpu.CompilerParams(dimension_semantics=sem, vmem_limit_bytes=VMEM_LIMIT)


IN_CHUNK = 512
HALO = 8


def _inproj_conv_kernel(x_ref, xp_ref, xn_ref, g_ref, wqkv_ref, why_ref, wgt_ref, cw_ref, cb_ref,
                        qkv_ref, u_ref, x0_ref, gt_ref, *, tiles_per_seq):
    i = pl.program_id(0)
    g = g_ref[...]
    h = _rms(x_ref[...], g).astype(BF16)
    hh = _rms(jnp.concatenate([xp_ref[...], xn_ref[...]], axis=0), g).astype(BF16)
    for w_ref, o_ref in ((wqkv_ref, qkv_ref), (wgt_ref, gt_ref)):
        for c0 in range(0, w_ref.shape[1], IN_CHUNK):
            o_ref[:, c0:c0 + IN_CHUNK] = jnp.dot(
                h, w_ref[:, c0:c0 + IN_CHUNK], preferred_element_type=F32).astype(o_ref.dtype)
    tm = h.shape[0]
    c = D_HYENA
    row = lax.broadcasted_iota(jnp.int32, (tm, c), 0)
    first = (i % tiles_per_seq) == 0
    last = (i % tiles_per_seq) == tiles_per_seq - 1
    ys = []
    for k in range(3):
        w = why_ref[:, k * c:(k + 1) * c]
        hy = jnp.dot(h, w, preferred_element_type=F32)
        hyh = jnp.dot(hh, w, preferred_element_type=F32)
        prev_row = jnp.where(first, 0.0, hyh[HALO - 1:HALO])
        next_row = jnp.where(last, 0.0, hyh[HALO:HALO + 1])
        xm1 = jnp.where(row == 0, prev_row, pltpu.roll(hy, 1, 0))
        xp1 = jnp.where(row == tm - 1, next_row, pltpu.roll(hy, tm - 1, 0))
        cw = cw_ref[:, k * c:(k + 1) * c]
        ys.append(xm1 * cw[0:1] + hy * cw[1:2] + xp1 * cw[2:3] + cb_ref[:, k * c:(k + 1) * c])
    x0_ref[...] = ys[0].astype(x0_ref.dtype)
    u_ref[...] = (ys[1] * ys[2]).astype(u_ref.dtype)


def _inproj_conv(x2, L, g, w_bf16, conv_w, conv_b):
    n, d = x2.shape
    n_qkv, n_hy = 3 * D_ATTN, 3 * D_HYENA
    tm = min(1024, L)
    nh = n // HALO
    hb = tm // HALO

    def resident(a):
        return pl.BlockSpec(a.shape, lambda i: (0, 0), pipeline_mode=pl.Buffered(1))

    wqkv, why, wgt = w_bf16[:, :n_qkv], w_bf16[:, n_qkv:n_qkv + n_hy], w_bf16[:, n_qkv + n_hy:]
    widths = (n_qkv, D_HYENA, D_HYENA, wgt.shape[1])
    return pl.pallas_call(
        functools.partial(_inproj_conv_kernel, tiles_per_seq=L // tm),
        grid=(n // tm,),
        in_specs=[
            pl.BlockSpec((tm, d), lambda i: (i, 0)),
            pl.BlockSpec((HALO, d), lambda i: (jnp.maximum(i * hb - 1, 0), 0)),
            pl.BlockSpec((HALO, d), lambda i: (jnp.minimum((i + 1) * hb, nh - 1), 0)),
            pl.BlockSpec((1, d), lambda i: (0, 0)),
            resident(wqkv), resident(why), resident(wgt),
            pl.BlockSpec((SHORT_K, n_hy), lambda i: (0, 0)),
            pl.BlockSpec((1, n_hy), lambda i: (0, 0)),
        ],
        out_specs=[pl.BlockSpec((tm, w), lambda i: (i, 0)) for w in widths],
        out_shape=[jax.ShapeDtypeStruct((n, w), BF16) for w in widths],
        compiler_params=_cparams(("parallel",)),
        name="inproj_conv",
    )(x2, x2, x2, g.reshape(1, d), wqkv, why, wgt, conv_w, conv_b.reshape(1, n_hy))


N_DR = 2 * WIN_R - 1
N_DC = 2 * WIN_C - 1
LOG2_E = 1.4426950408889634
Q_SCALE = (HEAD_DIM ** -0.5) * LOG2_E
KV_ROWS = ATT_ROWS + WIN_R


def _build_bias_tiles(rpb_ref, pt_ref):
    k = lax.broadcasted_iota(jnp.int32, (GRID_W, 128), 0)
    lane = lax.broadcasted_iota(jnp.int32, (GRID_W, 128), 1)
    q = lane & (GRID_W - 1)
    first_half = lane < GRID_W
    cs = jnp.clip(q - WIN_C // 2, 0, GRID_W - WIN_C)
    valid = (k >= cs) & (k < cs + WIN_C)
    dc = jnp.clip(k - q, -(WIN_C - 1), WIN_C - 1) + (WIN_C - 1)

    def body(idx, carry):
        p = idx // N_DR
        d = idx % N_DR
        base0 = ((2 * p) * N_DR + d) * N_DC
        base1 = ((2 * p + 1) * N_DR + d) * N_DC
        acc = jnp.zeros((GRID_W, 128), F32)
        for off in range(N_DC):
            val = jnp.where(first_half, rpb_ref[base0 + off], rpb_ref[base1 + off])
            acc = jnp.where(dc == off, val, acc)
        pt_ref[idx] = jnp.where(valid, acc * LOG2_E, NEG_BIG)
        return carry

    lax.fori_loop(0, (N_HEADS // 2) * N_DR, body, 0)


def _kv_window_start(i, rows):
    return jnp.clip(i * ATT_ROWS - WIN_R // 2, 0, rows - KV_ROWS)


def _attn_kernel(rpb_ref, q_ref, k_ref, v_ref, o_ref, pt_ref, *, rows):
    b = pl.program_id(0)
    i = pl.program_id(1)

    @pl.when((b == 0) & (i == 0))
    def _():
        _build_bias_tiles(rpb_ref, pt_ref)

    lane = lax.broadcasted_iota(jnp.int32, (GRID_W, 128), 1)
    first_half = lane < HEAD_DIM
    ones = jnp.ones((WIN_R * GRID_W, 128), BF16)
    nkeys = WIN_R * GRID_W
    wstart = _kv_window_start(i, rows)
    nt_dims = (((1,), (1,)), ((), ()))

    def row_body(rr, carry):
        r = i * ATT_ROWS + rr
        rs = jnp.clip(r - WIN_R // 2, 0, rows - WIN_R)
        s = rs - r + (WIN_R - 1)
        koff = pl.multiple_of((rs - wstart) * GRID_W, GRID_W)
        qoff = pl.multiple_of(rr * GRID_W, GRID_W)
        q = q_ref[pl.ds(qoff, GRID_W), :]
        pairs = range(N_HEADS // 2)
        sts = []
        for p in pairs:
            qp = q[:, 128 * p:128 * (p + 1)]
            zero = jnp.zeros_like(qp)
            wt = jnp.concatenate([jnp.where(first_half, qp, zero),
                                  jnp.where(first_half, zero, qp)], axis=0)
            kp = k_ref[pl.ds(koff, nkeys), 128 * p:128 * (p + 1)]
            sts.append(lax.dot_general(kp, wt, nt_dims, preferred_element_type=F32))
        pms = []
        for p in pairs:
            st = sts[p]
            ch = [st[GRID_W * c:GRID_W * (c + 1)] + pt_ref[p * N_DR + s + c] for c in range(WIN_R)]
            m = ch[0]
            for c in range(1, WIN_R):
                m = jnp.maximum(m, ch[c])
            m = jnp.max(m, axis=0, keepdims=True)
            pt = jnp.concatenate([jnp.exp2(c - m) for c in ch], axis=0).astype(BF16)
            pms.append(pt.T)
        outs = []
        for p in pairs:
            vp = v_ref[pl.ds(koff, nkeys), 128 * p:128 * (p + 1)]
            ov = jnp.dot(pms[p], jnp.concatenate([vp, ones], axis=1), preferred_element_type=F32)
            o = ov[:, 0:128] / ov[:, 128:256]
            outs.append(jnp.where(first_half, o[0:GRID_W], o[GRID_W:2 * GRID_W]))
        o_ref[pl.ds(qoff, GRID_W), :] = jnp.concatenate(outs, axis=1).astype(o_ref.dtype)
        return carry

    lax.fori_loop(0, ATT_ROWS, row_body, 0, unroll=8)


def _attention(z3, rpb_flat):
    bsz, L, _ = z3.shape
    rows = L // GRID_W
    assert rows % ATT_ROWS == 0 and rows >= KV_ROWS
    nblk = rows // ATT_ROWS
    blk = ATT_ROWS * GRID_W

    def window(col):
        return pl.BlockSpec(
            (pl.Squeezed(), pl.Element(KV_ROWS * GRID_W), pl.Element(D_ATTN)),
            lambda b, i: (b, _kv_window_start(i, rows) * GRID_W, col * D_ATTN))

    return pl.pallas_call(
        functools.partial(_attn_kernel, rows=rows),
        grid=(bsz, nblk),
        in_specs=[
            pl.BlockSpec(memory_space=pltpu.SMEM),
            pl.BlockSpec((pl.Squeezed(), blk, D_ATTN), lambda b, i: (b, i, 0)),
            window(1),
            window(2),
        ],
        out_specs=pl.BlockSpec((pl.Squeezed(), blk, D_ATTN), lambda b, i: (b, i, 0)),
        out_shape=jax.ShapeDtypeStruct((bsz, L, D_ATTN), BF16),
        scratch_shapes=[pltpu.VMEM(((N_HEADS // 2) * N_DR, GRID_W, 128), F32)],
        compiler_params=_cparams(("arbitrary", "arbitrary")),
        name="nbr_attention",
    )(rpb_flat, z3, z3, z3)


TWO_OVER_PI = 0.6366197723675814
PIO2_1 = 1.5703125
PIO2_2 = 4.837512969970703125e-4
PIO2_3 = 7.54978995489188216e-8
TRIG_FAST_LIMIT = 4096.0


def _quadrant_value(x, shift):
    kf = jnp.floor(x * TWO_OVER_PI + 0.5)
    r = ((x - kf * PIO2_1) - kf * PIO2_2) - kf * PIO2_3
    z = r * r
    s = r + r * z * (-1.6666654611e-1 + z * (8.3321608736e-3 + z * -1.9515295891e-4))
    c = (1.0 - 0.5 * z) + z * z * (4.166664568298827e-2
                                   + z * (-1.388731625493765e-3 + z * 2.443315711809948e-5))
    k = kf + shift
    half = jnp.floor(k * 0.5)
    odd = k - 2.0 * half
    flip = half - 2.0 * jnp.floor(half * 0.5)
    return (s + odd * (c - s)) * (1.0 - 2.0 * flip)


def _sin_small(x):
    return _quadrant_value(x, 0.0)


def _cos_small(x):
    return _quadrant_value(x, 1.0)


def _sin(x):
    return lax.cond(jnp.max(jnp.abs(x)) < TRIG_FAST_LIMIT, _sin_small, jnp.sin, x)


def _cos(x):
    return lax.cond(jnp.max(jnp.abs(x)) < TRIG_FAST_LIMIT, _cos_small, jnp.cos, x)


def _hi_lo(x):
    hi = x.astype(BF16)
    return hi, (x - hi.astype(F32)).astype(BF16)


def _stack_cols(w):
    hi, lo = _hi_lo(w)
    return jnp.concatenate([hi, hi, lo], axis=1)


def _stack_rows(h):
    hi, lo = _hi_lo(h)
    return jnp.concatenate([hi, lo, hi], axis=0)


def _dot_stacked(w_stacked, h):
    return jnp.dot(w_stacked, _stack_rows(h), preferred_element_type=F32)


def _filter_kernel(fb_ref, w1t_ref, w1c_ref, w1s_ref, b1_ref, w2_ref, b2_ref, w3_ref, b3_ref,
                   w4_ref, fr_ref, dl_ref, o_ref, *, L, tp):
    j = pl.program_id(0)
    fh = FILTER_HIDDEN
    pos = (lax.broadcasted_iota(jnp.int32, (1, tp), 1) + j * tp).astype(F32)
    t = pos / (L - 1.0)
    omega = (2.0 * math.pi) * pos / float(L)
    ang = fb_ref[...] * omega
    fr = fr_ref[...]
    pre = (w1t_ref[...] * t
           + _dot_stacked(w1c_ref[...], _cos(ang)) - _dot_stacked(w1s_ref[...], _sin(ang)))
    h = _sin(fr * (pre + b1_ref[...]))
    h = _sin(fr * (_dot_stacked(w2_ref[...], h) + b2_ref[...]))
    h = _sin(fr * (_dot_stacked(w3_ref[...], h) + b3_ref[...]))
    h_hi = h.astype(BF16).astype(F32)
    tr = jnp.concatenate([h_hi, h - h_hi], axis=0).T.astype(BF16)
    lane = lax.broadcasted_iota(jnp.int32, tr.shape, 1)
    lhs = jnp.concatenate([tr, jnp.where(lane < fh, tr, jnp.zeros_like(tr))], axis=1)
    out = jnp.dot(lhs, w4_ref[...], preferred_element_type=F32)
    tcol = (lax.broadcasted_iota(jnp.int32, (tp, 1), 0) + j * tp).astype(F32) / (L - 1.0)
    decay = jnp.exp(-tcol * dl_ref[...])
    c = D_HYENA
    o_ref[:, 0:c] = out[:, 0:c] * decay
    o_ref[:, c:2 * c] = out[:, c:2 * c] * decay


def _implicit_filters(L, w1, b1, w2, b2, w3, b3, w4, freq):
    bands = (FILTER_EMB - 1) // 2
    fh = FILTER_HIDDEN
    tp = min(1024, L)
    fb = jnp.linspace(1e-4, bands - 1, bands, dtype=F32).reshape(bands, 1)
    max_decay = math.log(DECAY_TARGET) / FAST_DECAY_PCT
    min_decay = math.log(DECAY_TARGET) / SLOW_DECAY_PCT
    deltas = jnp.abs(jnp.linspace(min_decay, max_decay, D_HYENA, dtype=F32)).reshape(1, D_HYENA)
    w1 = w1.astype(F32)
    w4_hi, w4_lo = _hi_lo(w4.astype(F32))
    args = (
        fb,
        w1[0:1, :].T,
        _stack_cols(w1[1:1 + bands, :].T),
        _stack_cols(w1[1 + bands:, :].T),
        b1.astype(F32).reshape(fh, 1),
        _stack_cols(w2.astype(F32).T), b2.astype(F32).reshape(fh, 1),
        _stack_cols(w3.astype(F32).T), b3.astype(F32).reshape(fh, 1),
        jnp.concatenate([w4_hi, w4_hi, w4_lo, jnp.zeros_like(w4_hi)], axis=0),
        freq.astype(F32).reshape(fh, 1),
        deltas,
    )

    def full(a):
        return pl.BlockSpec(a.shape, lambda j: (0,) * a.ndim)

    return pl.pallas_call(
        functools.partial(_filter_kernel, L=L, tp=tp),
        grid=(L // tp,),
        in_specs=[full(a) for a in args],
        out_specs=pl.BlockSpec((tp, 2 * D_HYENA), lambda j: (j, 0)),
        out_shape=jax.ShapeDtypeStruct((L, 2 * D_HYENA), F32),
        compiler_params=_cparams(("parallel",)),
        name="implicit_filter",
    )(*args)


FFT_LANES = 128
FFT_T1 = 16


@functools.lru_cache(maxsize=None)
def _fft_tables(L):
    n = 2 * L
    n2 = FFT_N2
    n1 = n // n2
    n1h = n1 // 2
    odd = 2 * np.arange(n1h) + 1
    th = 2.0 * np.pi * (np.outer(odd, np.arange(n1h)) % (2 * n1)) / (2 * n1)
    f1 = np.concatenate([np.cos(th), -np.sin(th)], axis=0)
    tw = 2.0 * np.pi * (np.outer(odd, np.arange(n2)) % (2 * n)) / (2 * n)
    twr = np.cos(tw)
    twi = -np.sin(tw)
    t2 = 2.0 * np.pi * (np.outer(np.arange(n2), np.arange(n2)) % n2) / n2
    fr, fi = np.cos(t2), -np.sin(t2)
    f2 = np.block([[fr, -fi], [fi, fr]])
    f2inv = np.block([[fr, fi], [-fi, fr]])
    f1inv = np.concatenate([np.cos(th.T), -np.sin(th.T)], axis=1) * (2.0 / n)
    return dict(
        n1=n1, n1h=n1h,
        f1=np.asarray(f1, np.float32), f1inv=np.asarray(f1inv, np.float32),
        f2=np.asarray(f2, np.float32), f2inv=np.asarray(f2inv, np.float32),
        twr=np.asarray(twr, np.float32), twi=np.asarray(twi, np.float32),
    )


FFT_A = 16
SLOT_DTYPE = BF16


def _outer_dft(f, src_ref, dst_ref):
    n_in, n1h = src_ref.shape[0], src_ref.shape[1]
    n_out = dst_ref.shape[0]

    def body(i, carry):
        a0 = pl.multiple_of(i * FFT_A, FFT_A)
        ts = [jnp.swapaxes(src_ref[p, :, pl.ds(a0, FFT_A), :].astype(F32), 0, 1)
              for p in range(n_in)]
        outs = [[] for _ in range(n_out)]
        for j in range(0, FFT_A, 2):
            rhs = jnp.concatenate([jnp.concatenate([t[j], t[j + 1]], axis=1) for t in ts],
                                  axis=0).astype(BF16)
            res = jnp.dot(f, rhs, preferred_element_type=F32)
            for q in range(n_out):
                blk = res[q * n1h:(q + 1) * n1h]
                outs[q] += [blk[:, 0:FFT_LANES], blk[:, FFT_LANES:2 * FFT_LANES]]
        for q in range(n_out):
            dst_ref[q, :, pl.ds(a0, FFT_A), :] = jnp.swapaxes(
                jnp.stack(outs[q], axis=0), 0, 1).astype(dst_ref.dtype)
        return carry

    lax.fori_loop(0, FFT_N2 // FFT_A, body, 0, unroll=2)


def _fft_stage1_kernel(f_ref, u_ref, o_ref):
    _outer_dft(f_ref[...], u_ref, o_ref)


def _fft_stage1(u3, f1):
    bsz, L, c = u3.shape
    n1h = L // FFT_N2
    return pl.pallas_call(
        _fft_stage1_kernel,
        grid=(bsz, c // FFT_LANES),
        in_specs=[
            pl.BlockSpec(f1.shape, lambda b, j: (0, 0)),
            pl.BlockSpec((1, n1h, FFT_N2, FFT_LANES), lambda b, j: (b, 0, 0, j)),
        ],
        out_specs=pl.BlockSpec((pl.Squeezed(), 2, n1h, FFT_N2, FFT_LANES),
                               lambda b, j: (b, 0, 0, 0, j)),
        out_shape=jax.ShapeDtypeStruct((bsz, 2, n1h, FFT_N2, c), SLOT_DTYPE),
        compiler_params=_cparams(("parallel", "parallel")),
        name="fft_stage1",
    )(f1, u3.reshape(bsz, n1h, FFT_N2, c))


def _twiddle(ar, ai, tr, ti):
    return ar * tr - ai * ti, ar * ti + ai * tr


def _tw_columns(tw_ref):
    t = tw_ref[...].T
    return [t[:, j:j + 1] for j in range(FFT_T1)]


def _stage2_inputs(a_ref, trs, tis):
    xs = []
    for j in range(FFT_T1):
        xr, xi = _twiddle(a_ref[0, j].astype(F32), a_ref[1, j].astype(F32), trs[j], tis[j])
        xs.append(jnp.concatenate([xr, xi], axis=0).astype(BF16))
    return xs


def _filter_spectrum_kernel(a_ref, twr_ref, twi_ref, f2_ref, o_ref):
    n2, c = FFT_N2, D_HYENA
    f2 = f2_ref[...]
    bigs = [jnp.dot(f2, x, preferred_element_type=F32)
            for x in _stage2_inputs(a_ref, _tw_columns(twr_ref), _tw_columns(twi_ref))]
    for j, big in enumerate(bigs):
        o_ref[j, 0] = (big[0:n2, 0:c] + big[0:n2, c:2 * c]).astype(o_ref.dtype)
        o_ref[j, 1] = (big[n2:2 * n2, 0:c] - big[n2:2 * n2, c:2 * c]).astype(o_ref.dtype)


def _filter_spectrum(hfilt, tb):
    L, c2 = hfilt.shape
    n1h = tb["n1h"]
    a5 = _fft_stage1(hfilt.reshape(1, L, c2), tb["f1"])
    return pl.pallas_call(
        _filter_spectrum_kernel,
        grid=(n1h // FFT_T1,),
        in_specs=[
            pl.BlockSpec((pl.Squeezed(), 2, FFT_T1, FFT_N2, c2), lambda k: (0, 0, k, 0, 0)),
            pl.BlockSpec((FFT_T1, FFT_N2), lambda k: (k, 0)),
            pl.BlockSpec((FFT_T1, FFT_N2), lambda k: (k, 0)),
            pl.BlockSpec((2 * FFT_N2, 2 * FFT_N2), lambda k: (0, 0)),
        ],
        out_specs=pl.BlockSpec((FFT_T1, 2, FFT_N2, D_HYENA), lambda k: (k, 0, 0, 0)),
        out_shape=jax.ShapeDtypeStruct((n1h, 2, FFT_N2, D_HYENA), SLOT_DTYPE),
        compiler_params=_cparams(("parallel",)),
        name="filter_spectrum",
    )(a5, tb["twr"], tb["twi"], tb["f2"])


def _fft_mid_kernel(a_ref, kf_ref, twr_ref, twi_ref, f2_ref, f2i_ref, o_ref):
    n2 = FFT_N2
    f2, f2i = f2_ref[...], f2i_ref[...]
    trs, tis = _tw_columns(twr_ref), _tw_columns(twi_ref)
    bigs = [jnp.dot(f2, x, preferred_element_type=F32)
            for x in _stage2_inputs(a_ref, trs, tis)]
    ys = []
    for j, big in enumerate(bigs):
        sr, si = big[0:n2], big[n2:2 * n2]
        kr, ki = kf_ref[j, 0].astype(F32), kf_ref[j, 1].astype(F32)
        ys.append(jnp.concatenate([sr * kr - si * ki, sr * ki + si * kr], axis=0).astype(BF16))
    backs = [jnp.dot(f2i, y, preferred_element_type=F32) for y in ys]
    for j, back in enumerate(backs):
        br, bi = back[0:n2], back[n2:2 * n2]
        tr, ti = trs[j], tis[j]
        o_ref[0, j] = (br * tr + bi * ti).astype(o_ref.dtype)
        o_ref[1, j] = (bi * tr - br * ti).astype(o_ref.dtype)


def _fft_mid(a5, kf, tb):
    bsz, _, n1h, n2, c = a5.shape
    slot = pl.BlockSpec((pl.Squeezed(), 2, FFT_T1, n2, c), lambda k, b: (b, 0, k, 0, 0))
    return pl.pallas_call(
        _fft_mid_kernel,
        grid=(n1h // FFT_T1, bsz),
        in_specs=[
            slot,
            pl.BlockSpec((FFT_T1, 2, n2, c), lambda k, b: (k, 0, 0, 0)),
            pl.BlockSpec((FFT_T1, n2), lambda k, b: (k, 0)),
            pl.BlockSpec((FFT_T1, n2), lambda k, b: (k, 0)),
            pl.BlockSpec((2 * n2, 2 * n2), lambda k, b: (0, 0)),
            pl.BlockSpec((2 * n2, 2 * n2), lambda k, b: (0, 0)),
        ],
        out_specs=slot,
        out_shape=jax.ShapeDtypeStruct(a5.shape, SLOT_DTYPE),
        compiler_params=_cparams(("parallel", "parallel")),
        name="fft_mid",
    )(a5, kf, tb["twr"], tb["twi"], tb["f2"], tb["f2inv"])


def _fft_out_kernel(f_ref, b_ref, o_ref):
    _outer_dft(f_ref[...], b_ref, o_ref)


def _fft_out(b5, f1inv):
    bsz, _, n1h, n2, c = b5.shape
    return pl.pallas_call(
        _fft_out_kernel,
        grid=(bsz, c // FFT_LANES),
        in_specs=[
            pl.BlockSpec(f1inv.shape, lambda b, j: (0, 0)),
            pl.BlockSpec((pl.Squeezed(), 2, n1h, n2, FFT_LANES), lambda b, j: (b, 0, 0, 0, j)),
        ],
        out_specs=pl.BlockSpec((1, n1h, n2, FFT_LANES), lambda b, j: (b, 0, 0, j)),
        out_shape=jax.ShapeDtypeStruct((bsz, n1h, n2, c), BF16),
        compiler_params=_cparams(("parallel", "parallel")),
        name="fft_out",
    )(f1inv, b5).reshape(bsz, n1h * n2, c)


def _hyena_conv(u, kf, tb):
    a5 = _fft_stage1(u, tb["f1"])
    b5 = _fft_mid(a5, kf, tb)
    return _fft_out(b5, tb["f1inv"])


def _rms(x, g):
    inv = lax.rsqrt(jnp.mean(x * x, axis=-1, keepdims=True) + EPS)
    return (x * inv) * g


FFN_CHUNK = 256


def _merge_ffn_kernel(ya_ref, yc_ref, u_ref, x0_ref, dk_ref, ga_ref, gh_ref, x_ref, wa_ref, wh_ref,
                      wo_ref, g_ref, wg_ref, wu_ref, wd_ref, gf_ref, o_ref):
    yh = x0_ref[...].astype(F32) * (yc_ref[...].astype(F32)
                                    + u_ref[...].astype(F32) * dk_ref[...])
    pa = jnp.dot(ya_ref[...], wa_ref[...], preferred_element_type=F32)
    ph = jnp.dot(yh.astype(BF16), wh_ref[...], preferred_element_type=F32)
    merged = (jax.nn.sigmoid(ga_ref[...].astype(F32)) * pa
              + jax.nn.sigmoid(gh_ref[...].astype(F32)) * ph)
    x1 = x_ref[...] + jnp.dot(merged.astype(BF16), wo_ref[...], preferred_element_type=F32)
    h = _rms(x1, g_ref[...]).astype(BF16)
    acc = x1
    for c0 in range(0, wg_ref.shape[1], FFN_CHUNK):
        gate = jnp.dot(h, wg_ref[:, c0:c0 + FFN_CHUNK], preferred_element_type=F32)
        up = jnp.dot(h, wu_ref[:, c0:c0 + FFN_CHUNK], preferred_element_type=F32)
        act = (gate * jax.nn.sigmoid(gate) * up).astype(BF16)
        acc = acc + jnp.dot(act, wd_ref[c0:c0 + FFN_CHUNK, :], preferred_element_type=F32)
    o_ref[...] = _rms(acc, gf_ref[...])


def _merge_ffn(ya, yc, u, x0, d_skip, gates, x2, wa, wh, wo, g, wg, wu, wd, gf):
    n, d = x2.shape
    dff = wg.shape[1]
    assert dff % FFN_CHUNK == 0
    tm = min(512, n)
    tok = pl.BlockSpec((tm, D_HYENA), lambda i: (i, 0))

    def resident(a):
        return pl.BlockSpec(a.shape, lambda i: (0, 0), pipeline_mode=pl.Buffered(1))

    def row(width):
        return pl.BlockSpec((1, width), lambda i: (0, 0))

    return pl.pallas_call(
        _merge_ffn_kernel,
        grid=(n // tm,),
        in_specs=[
            pl.BlockSpec((tm, D_ATTN), lambda i: (i, 0)),
            tok, tok, tok,
            row(D_HYENA),
            pl.BlockSpec((tm, d), lambda i: (i, 0)),
            pl.BlockSpec((tm, d), lambda i: (i, 1)),
            pl.BlockSpec((tm, d), lambda i: (i, 0)),
            resident(wa), resident(wh), resident(wo),
            row(d),
            resident(wg), resident(wu), resident(wd),
            row(d),
        ],
        out_specs=pl.BlockSpec((tm, d), lambda i: (i, 0)),
        out_shape=jax.ShapeDtypeStruct((n, d), F32),
        compiler_params=_cparams(("parallel",)),
        name="merge_ffn",
    )(ya, yc, u, x0, d_skip.astype(F32).reshape(1, D_HYENA), gates, gates, x2, wa, wh, wo,
      g.reshape(1, d), wg, wu, wd, gf.reshape(1, d))


def _trunk(x, p, norm_final):
    bsz, L, d = x.shape
    n = bsz * L
    x2 = x.reshape(n, d)
    qkv, u, x0, gates = _inproj_conv(x2, L, p["norm_mix"], p["w_in"], p["conv_w"], p["conv_b"])
    ya = _attention(qkv.reshape(bsz, L, qkv.shape[1]), p["rpb"])
    u = u.reshape(bsz, L, D_HYENA)
    tb = dict(_fft_tables(L))
    for name in ("f1", "f1inv", "f2", "f2inv"):
        tb[name] = jnp.asarray(tb[name]).astype(BF16)
    hfilt = _implicit_filters(L, *p["filt"])
    kf = _filter_spectrum(hfilt, tb)
    yc = _hyena_conv(u, kf, tb)
    out = _merge_ffn(ya.reshape(n, D_ATTN), yc.reshape(n, D_HYENA), u.reshape(n, D_HYENA),
                     x0, p["hyena_d"], gates, x2,
                     p["w_br_attn"], p["w_br_hyena"], p["w_out"],
                     p["norm_ffn"], p["w_gate"], p["w_up"], p["w_down"], norm_final)
    return out.reshape(bsz, L, d)


def kernel(x_prompt, x_sample, norm_mix, w_in, rpb, conv_w, conv_b, filt_w1, filt_b1, filt_w2,
           filt_b2, filt_w3, filt_b3, filt_w4, filt_freq, hyena_d, w_br_attn, w_br_hyena, w_out,
           norm_ffn, w_gate, w_up, w_down, norm_final):
    assert w_in.shape[0] == 1, "the layer definition has depth 1"
    col = lax.broadcasted_iota(jnp.int32, (1, w_in.shape[2]), 1)
    col_scale = jnp.where(col < D_ATTN, Q_SCALE, 1.0).astype(F32)
    p = dict(
        norm_mix=norm_mix[0], w_in=(w_in[0] * col_scale).astype(BF16), rpb=rpb[0].reshape(-1),
        conv_w=conv_w[0], conv_b=conv_b[0],
        filt=(filt_w1[0], filt_b1[0], filt_w2[0], filt_b2[0], filt_w3[0], filt_b3[0],
              filt_w4[0], filt_freq[0]),
        hyena_d=hyena_d[0],
        w_br_attn=w_br_attn[0].astype(BF16), w_br_hyena=w_br_hyena[0].astype(BF16),
        w_out=w_out[0].astype(BF16), norm_ffn=norm_ffn[0],
        w_gate=w_gate[0].astype(BF16), w_up=w_up[0].astype(BF16),
        w_down=w_down[0].astype(BF16),
    )
    return (_trunk(x_prompt, p, norm_final), _trunk(x_sample, p, norm_final))
```

```python
import functools
import math

import numpy as np
import jax
import jax.numpy as jnp
from jax import lax
from jax.experimental import pallas as pl
from jax.experimental.pallas import tpu as pltpu

F32 = jnp.float32
BF16 = jnp.bfloat16

GRID_W = 64
N_HEADS = 8
HEAD_DIM = 64
D_ATTN = N_HEADS * HEAD_DIM
WIN_R = 8
WIN_C = 16
D_HYENA = 512
SHORT_K = 3
FILTER_EMB = 33
FILTER_HIDDEN = 64
DECAY_TARGET = 1e-2
FAST_DECAY_PCT = 0.3
SLOW_DECAY_PCT = 1.5
EPS = 1e-6

FFT_N2 = 128
ATT_ROWS = 16
ATT_UNROLL = 8
NEG_BIG = -1e30
VMEM_LIMIT = 56 * 1024 * 1024


def _cparams(sem):
    return pltpu.CompilerParams(dimension_semantics=sem, vmem_limit_bytes=VMEM_LIMIT)


IN_CHUNK = 512
HALO = 8


def _inproj_conv_kernel(x_ref, xp_ref, xn_ref, g_ref, wqkv_ref, why_ref, wgt_ref, cw_ref, cb_ref,
                        qkv_ref, u_ref, x0_ref, gt_ref, *, tiles_per_seq):
    i = pl.program_id(0)
    g = g_ref[...]
    h = _rms(x_ref[...], g).astype(BF16)
    hh = _rms(jnp.concatenate([xp_ref[...], xn_ref[...]], axis=0), g).astype(BF16)
    for w_ref, o_ref in ((wqkv_ref, qkv_ref), (wgt_ref, gt_ref)):
        for c0 in range(0, w_ref.shape[1], IN_CHUNK):
            o_ref[:, c0:c0 + IN_CHUNK] = jnp.dot(
                h, w_ref[:, c0:c0 + IN_CHUNK], preferred_element_type=F32).astype(o_ref.dtype)
    tm = h.shape[0]
    c = D_HYENA
    row = lax.broadcasted_iota(jnp.int32, (tm, c), 0)
    first = (i % tiles_per_seq) == 0
    last = (i % tiles_per_seq) == tiles_per_seq - 1
    ys = []
    for k in range(3):
        w = why_ref[:, k * c:(k + 1) * c]
        hy = jnp.dot(h, w, preferred_element_type=F32)
        hyh = jnp.dot(hh, w, preferred_element_type=F32)
        prev_row = jnp.where(first, 0.0, hyh[HALO - 1:HALO])
        next_row = jnp.where(last, 0.0, hyh[HALO:HALO + 1])
        xm1 = jnp.where(row == 0, prev_row, pltpu.roll(hy, 1, 0))
        xp1 = jnp.where(row == tm - 1, next_row, pltpu.roll(hy, tm - 1, 0))
        cw = cw_ref[:, k * c:(k + 1) * c]
        ys.append(xm1 * cw[0:1] + hy * cw[1:2] + xp1 * cw[2:3] + cb_ref[:, k * c:(k + 1) * c])
    x0_ref[...] = ys[0].astype(x0_ref.dtype)
    u_ref[...] = (ys[1] * ys[2]).astype(u_ref.dtype)


def _inproj_conv(x2, L, g, w_bf16, conv_w, conv_b):
    n, d = x2.shape
    n_qkv, n_hy = 3 * D_ATTN, 3 * D_HYENA
    tm = min(1024, L)
    nh = n // HALO
    hb = tm // HALO

    def resident(a):
        return pl.BlockSpec(a.shape, lambda i: (0, 0), pipeline_mode=pl.Buffered(1))

    wqkv, why, wgt = w_bf16[:, :n_qkv], w_bf16[:, n_qkv:n_qkv + n_hy], w_bf16[:, n_qkv + n_hy:]
    widths = (n_qkv, D_HYENA, D_HYENA, wgt.shape[1])
    return pl.pallas_call(
        functools.partial(_inproj_conv_kernel, tiles_per_seq=L // tm),
        grid=(n // tm,),
        in_specs=[
            pl.BlockSpec((tm, d), lambda i: (i, 0)),
            pl.BlockSpec((HALO, d), lambda i: (jnp.maximum(i * hb - 1, 0), 0)),
            pl.BlockSpec((HALO, d), lambda i: (jnp.minimum((i + 1) * hb, nh - 1), 0)),
            pl.BlockSpec((1, d), lambda i: (0, 0)),
            resident(wqkv), resident(why), resident(wgt),
            pl.BlockSpec((SHORT_K, n_hy), lambda i: (0, 0)),
            pl.BlockSpec((1, n_hy), lambda i: (0, 0)),
        ],
        out_specs=[pl.BlockSpec((tm, w), lambda i: (i, 0)) for w in widths],
        out_shape=[jax.ShapeDtypeStruct((n, w), BF16) for w in widths],
        compiler_params=_cparams(("parallel",)),
        name="inproj_conv",
    )(x2, x2, x2, g.reshape(1, d), wqkv, why, wgt, conv_w, conv_b.reshape(1, n_hy))


N_DR = 2 * WIN_R - 1
N_DC = 2 * WIN_C - 1
LOG2_E = 1.4426950408889634
Q_SCALE = (HEAD_DIM ** -0.5) * LOG2_E
KV_ROWS = ATT_ROWS + WIN_R


def _build_bias_tiles(rpb_ref, pt_ref):
    k = lax.broadcasted_iota(jnp.int32, (GRID_W, 128), 0)
    lane = lax.broadcasted_iota(jnp.int32, (GRID_W, 128), 1)
    q = lane & (GRID_W - 1)
    first_half = lane < GRID_W
    cs = jnp.clip(q - WIN_C // 2, 0, GRID_W - WIN_C)
    valid = (k >= cs) & (k < cs + WIN_C)
    dc = jnp.clip(k - q, -(WIN_C - 1), WIN_C - 1) + (WIN_C - 1)

    def body(idx, carry):
        p = idx // N_DR
        d = idx % N_DR
        base0 = ((2 * p) * N_DR + d) * N_DC
        base1 = ((2 * p + 1) * N_DR + d) * N_DC
        acc = jnp.zeros((GRID_W, 128), F32)
        for off in range(N_DC):
            val = jnp.where(first_half, rpb_ref[base0 + off], rpb_ref[base1 + off])
            acc = jnp.where(dc == off, val, acc)
        pt_ref[idx] = jnp.where(valid, acc * LOG2_E, NEG_BIG)
        return carry

    lax.fori_loop(0, (N_HEADS // 2) * N_DR, body, 0)


def _kv_window_start(i, rows):
    return jnp.clip(i * ATT_ROWS - WIN_R // 2, 0, rows - KV_ROWS)


def _attn_kernel(rpb_ref, q_ref, k_ref, v_ref, o_ref, pt_ref, *, rows):
    b = pl.program_id(0)
    i = pl.program_id(1)

    @pl.when((b == 0) & (i == 0))
    def _():
        _build_bias_tiles(rpb_ref, pt_ref)

    lane = lax.broadcasted_iota(jnp.int32, (GRID_W, 128), 1)
    first_half = lane < HEAD_DIM
    ones = jnp.ones((WIN_R * GRID_W, 128), BF16)
    nkeys = WIN_R * GRID_W
    wstart = _kv_window_start(i, rows)
    nt_dims = (((1,), (1,)), ((), ()))

    pairs = range(N_HEADS // 2)

    def scores(rr):
        r = i * ATT_ROWS + rr
        rs = jnp.clip(r - WIN_R // 2, 0, rows - WIN_R)
        s = rs - r + (WIN_R - 1)
        koff = pl.multiple_of((rs - wstart) * GRID_W, GRID_W)
        q = q_ref[pl.ds(pl.multiple_of(rr * GRID_W, GRID_W), GRID_W), :]
        sts = []
        for p in pairs:
            qp = q[:, 128 * p:128 * (p + 1)]
            zero = jnp.zeros_like(qp)
            wt = jnp.concatenate([jnp.where(first_half, qp, zero),
                                  jnp.where(first_half, zero, qp)], axis=0)
            kp = k_ref[pl.ds(koff, nkeys), 128 * p:128 * (p + 1)]
            sts.append(lax.dot_general(kp, wt, nt_dims, preferred_element_type=F32))
        return sts, s, koff

    def probabilities(sts, s):
        pms = []
        for p in pairs:
            st = sts[p]
            ch = [st[GRID_W * c:GRID_W * (c + 1)] + pt_ref[p * N_DR + s + c] for c in range(WIN_R)]
            m = ch[0]
            for c in range(1, WIN_R):
                m = jnp.maximum(m, ch[c])
            m = jnp.max(m, axis=0, keepdims=True)
            pt = jnp.concatenate([jnp.exp2(c - m) for c in ch], axis=0).astype(BF16)
            pms.append(pt.T)
        return pms

    def weighted_values(rr, pms, koff):
        outs = []
        for p in pairs:
            vp = v_ref[pl.ds(koff, nkeys), 128 * p:128 * (p + 1)]
            ov = jnp.dot(pms[p], jnp.concatenate([vp, ones], axis=1), preferred_element_type=F32)
            o = ov[:, 0:128] / ov[:, 128:256]
            outs.append(jnp.where(first_half, o[0:GRID_W], o[GRID_W:2 * GRID_W]))
        qoff = pl.multiple_of(rr * GRID_W, GRID_W)
        o_ref[pl.ds(qoff, GRID_W), :] = jnp.concatenate(outs, axis=1).astype(o_ref.dtype)

    def trip(t, carry):
        r0 = t * ATT_UNROLL
        nxt = scores(r0)
        for j in range(ATT_UNROLL):
            sts, s, koff = nxt
            if j + 1 < ATT_UNROLL:
                nxt = scores(r0 + j + 1)
            weighted_values(r0 + j, probabilities(sts, s), koff)
        return carry

    lax.fori_loop(0, ATT_ROWS // ATT_UNROLL, trip, 0)


def _attention(z3, rpb_flat):
    bsz, L, _ = z3.shape
    rows = L // GRID_W
    assert rows % ATT_ROWS == 0 and rows >= KV_ROWS
    nblk = rows // ATT_ROWS
    blk = ATT_ROWS * GRID_W

    def window(col):
        return pl.BlockSpec(
            (pl.Squeezed(), pl.Element(KV_ROWS * GRID_W), pl.Element(D_ATTN)),
            lambda b, i: (b, _kv_window_start(i, rows) * GRID_W, col * D_ATTN))

    return pl.pallas_call(
        functools.partial(_attn_kernel, rows=rows),
        grid=(bsz, nblk),
        in_specs=[
            pl.BlockSpec(memory_space=pltpu.SMEM),
            pl.BlockSpec((pl.Squeezed(), blk, D_ATTN), lambda b, i: (b, i, 0)),
            window(1),
            window(2),
        ],
        out_specs=pl.BlockSpec((pl.Squeezed(), blk, D_ATTN), lambda b, i: (b, i, 0)),
        out_shape=jax.ShapeDtypeStruct((bsz, L, D_ATTN), BF16),
        scratch_shapes=[pltpu.VMEM(((N_HEADS // 2) * N_DR, GRID_W, 128), F32)],
        compiler_params=_cparams(("arbitrary", "arbitrary")),
        name="nbr_attention",
    )(rpb_flat, z3, z3, z3)


TWO_OVER_PI = 0.6366197723675814
PIO2_1 = 1.5703125
PIO2_2 = 4.837512969970703125e-4
PIO2_3 = 7.54978995489188216e-8
TRIG_FAST_LIMIT = 4096.0


def _quadrant_value(x, shift):
    kf = jnp.floor(x * TWO_OVER_PI + 0.5)
    r = ((x - kf * PIO2_1) - kf * PIO2_2) - kf * PIO2_3
    z = r * r
    s = r + r * z * (-1.6666654611e-1 + z * (8.3321608736e-3 + z * -1.9515295891e-4))
    c = (1.0 - 0.5 * z) + z * z * (4.166664568298827e-2
                                   + z * (-1.388731625493765e-3 + z * 2.443315711809948e-5))
    k = kf + shift
    half = jnp.floor(k * 0.5)
    odd = k - 2.0 * half
    flip = half - 2.0 * jnp.floor(half * 0.5)
    return (s + odd * (c - s)) * (1.0 - 2.0 * flip)


def _sin_small(x):
    return _quadrant_value(x, 0.0)


def _cos_small(x):
    return _quadrant_value(x, 1.0)


def _sin(x):
    return lax.cond(jnp.max(jnp.abs(x)) < TRIG_FAST_LIMIT, _sin_small, jnp.sin, x)


def _cos(x):
    return lax.cond(jnp.max(jnp.abs(x)) < TRIG_FAST_LIMIT, _cos_small, jnp.cos, x)


def _hi_lo(x):
    hi = x.astype(BF16)
    return hi, (x - hi.astype(F32)).astype(BF16)


def _stack_cols(w):
    hi, lo = _hi_lo(w)
    return jnp.concatenate([hi, hi, lo], axis=1)


def _stack_rows(h):
    hi, lo = _hi_lo(h)
    return jnp.concatenate([hi, lo, hi], axis=0)


def _dot_stacked(w_stacked, h):
    return jnp.dot(w_stacked, _stack_rows(h), preferred_element_type=F32)


def _filter_kernel(fb_ref, w1t_ref, w1c_ref, w1s_ref, b1_ref, w2_ref, b2_ref, w3_ref, b3_ref,
                   w4_ref, fr_ref, dl_ref, o_ref, *, L, tp):
    j = pl.program_id(0)
    fh = FILTER_HIDDEN
    pos = (lax.broadcasted_iota(jnp.int32, (1, tp), 1) + j * tp).astype(F32)
    t = pos / (L - 1.0)
    omega = (2.0 * math.pi) * pos / float(L)
    ang = fb_ref[...] * omega
    fr = fr_ref[...]
    pre = (w1t_ref[...] * t
           + _dot_stacked(w1c_ref[...], _cos(ang)) - _dot_stacked(w1s_ref[...], _sin(ang)))
    h = _sin(fr * (pre + b1_ref[...]))
    h = _sin(fr * (_dot_stacked(w2_ref[...], h) + b2_ref[...]))
    h = _sin(fr * (_dot_stacked(w3_ref[...], h) + b3_ref[...]))
    h_hi = h.astype(BF16).astype(F32)
    tr = jnp.concatenate([h_hi, h - h_hi], axis=0).T.astype(BF16)
    lane = lax.broadcasted_iota(jnp.int32, tr.shape, 1)
    lhs = jnp.concatenate([tr, jnp.where(lane < fh, tr, jnp.zeros_like(tr))], axis=1)
    out = jnp.dot(lhs, w4_ref[...], preferred_element_type=F32)
    tcol = (lax.broadcasted_iota(jnp.int32, (tp, 1), 0) + j * tp).astype(F32) / (L - 1.0)
    decay = jnp.exp(-tcol * dl_ref[...])
    c = D_HYENA
    o_ref[:, 0:c] = out[:, 0:c] * decay
    o_ref[:, c:2 * c] = out[:, c:2 * c] * decay


def _implicit_filters(L, w1, b1, w2, b2, w3, b3, w4, freq):
    bands = (FILTER_EMB - 1) // 2
    fh = FILTER_HIDDEN
    tp = min(1024, L)
    fb = jnp.linspace(1e-4, bands - 1, bands, dtype=F32).reshape(bands, 1)
    max_decay = math.log(DECAY_TARGET) / FAST_DECAY_PCT
    min_decay = math.log(DECAY_TARGET) / SLOW_DECAY_PCT
    deltas = jnp.abs(jnp.linspace(min_decay, max_decay, D_HYENA, dtype=F32)).reshape(1, D_HYENA)
    w1 = w1.astype(F32)
    w4_hi, w4_lo = _hi_lo(w4.astype(F32))
    args = (
        fb,
        w1[0:1, :].T,
        _stack_cols(w1[1:1 + bands, :].T),
        _stack_cols(w1[1 + bands:, :].T),
        b1.astype(F32).reshape(fh, 1),
        _stack_cols(w2.astype(F32).T), b2.astype(F32).reshape(fh, 1),
        _stack_cols(w3.astype(F32).T), b3.astype(F32).reshape(fh, 1),
        jnp.concatenate([w4_hi, w4_hi, w4_lo, jnp.zeros_like(w4_hi)], axis=0),
        freq.astype(F32).reshape(fh, 1),
        deltas,
    )

    def full(a):
        return pl.BlockSpec(a.shape, lambda j: (0,) * a.ndim)

    return pl.pallas_call(
        functools.partial(_filter_kernel, L=L, tp=tp),
        grid=(L // tp,),
        in_specs=[full(a) for a in args],
        out_specs=pl.BlockSpec((tp, 2 * D_HYENA), lambda j: (j, 0)),
        out_shape=jax.ShapeDtypeStruct((L, 2 * D_HYENA), F32),
        compiler_params=_cparams(("parallel",)),
        name="implicit_filter",
    )(*args)


FFT_LANES = 128
FFT_T1 = 16


@functools.lru_cache(maxsize=None)
def _fft_tables(L):
    n = 2 * L
    n2 = FFT_N2
    n1 = n // n2
    n1h = n1 // 2
    odd = 2 * np.arange(n1h) + 1
    th = 2.0 * np.pi * (np.outer(odd, np.arange(n1h)) % (2 * n1)) / (2 * n1)
    f1 = np.concatenate([np.cos(th), -np.sin(th)], axis=0)
    tw = 2.0 * np.pi * (np.outer(odd, np.arange(n2)) % (2 * n)) / (2 * n)
    twr = np.cos(tw)
    twi = -np.sin(tw)
    t2 = 2.0 * np.pi * (np.outer(np.arange(n2), np.arange(n2)) % n2) / n2
    fr, fi = np.cos(t2), -np.sin(t2)
    f2 = np.block([[fr, -fi], [fi, fr]])
    f2inv = np.block([[fr, fi], [-fi, fr]])
    f1inv = np.concatenate([np.cos(th.T), -np.sin(th.T)], axis=1) * (2.0 / n)
    return dict(
        n1=n1, n1h=n1h,
        f1=np.asarray(f1, np.float32), f1inv=np.asarray(f1inv, np.float32),
        f2=np.asarray(f2, np.float32), f2inv=np.asarray(f2inv, np.float32),
        twr=np.asarray(twr, np.float32), twi=np.asarray(twi, np.float32),
    )


FFT_A = 16
SLOT_DTYPE = BF16


def _outer_dft(f, src_ref, dst_ref):
    n_in, n1h = src_ref.shape[0], src_ref.shape[1]
    n_out = dst_ref.shape[0]

    def body(i, carry):
        a0 = pl.multiple_of(i * FFT_A, FFT_A)
        ts = [jnp.swapaxes(src_ref[p, :, pl.ds(a0, FFT_A), :].astype(F32), 0, 1)
              for p in range(n_in)]
        outs = [[] for _ in range(n_out)]
        for j in range(0, FFT_A, 2):
            rhs = jnp.concatenate([jnp.concatenate([t[j], t[j + 1]], axis=1) for t in ts],
                                  axis=0).astype(BF16)
            res = jnp.dot(f, rhs, preferred_element_type=F32)
            for q in range(n_out):
                blk = res[q * n1h:(q + 1) * n1h]
                outs[q] += [blk[:, 0:FFT_LANES], blk[:, FFT_LANES:2 * FFT_LANES]]
        for q in range(n_out):
            dst_ref[q, :, pl.ds(a0, FFT_A), :] = jnp.swapaxes(
                jnp.stack(outs[q], axis=0), 0, 1).astype(dst_ref.dtype)
        return carry

    lax.fori_loop(0, FFT_N2 // FFT_A, body, 0, unroll=2)


def _fft_stage1_kernel(f_ref, u_ref, o_ref):
    _outer_dft(f_ref[...], u_ref, o_ref)


def _fft_stage1(u3, f1):
    bsz, L, c = u3.shape
    n1h = L // FFT_N2
    return pl.pallas_call(
        _fft_stage1_kernel,
        grid=(bsz, c // FFT_LANES),
        in_specs=[
            pl.BlockSpec(f1.shape, lambda b, j: (0, 0)),
            pl.BlockSpec((1, n1h, FFT_N2, FFT_LANES), lambda b, j: (b, 0, 0, j)),
        ],
        out_specs=pl.BlockSpec((pl.Squeezed(), 2, n1h, FFT_N2, FFT_LANES),
                               lambda b, j: (b, 0, 0, 0, j)),
        out_shape=jax.ShapeDtypeStruct((bsz, 2, n1h, FFT_N2, c), SLOT_DTYPE),
        compiler_params=_cparams(("parallel", "parallel")),
        name="fft_stage1",
    )(f1, u3.reshape(bsz, n1h, FFT_N2, c))


def _twiddle(ar, ai, tr, ti):
    return ar * tr - ai * ti, ar * ti + ai * tr


def _tw_columns(tw_ref):
    t = tw_ref[...].T
    return [t[:, j:j + 1] for j in range(FFT_T1)]


def _stage2_inputs(a_ref, trs, tis):
    xs = []
    for j in range(FFT_T1):
        xr, xi = _twiddle(a_ref[0, j].astype(F32), a_ref[1, j].astype(F32), trs[j], tis[j])
        xs.append(jnp.concatenate([xr, xi], axis=0).astype(BF16))
    return xs


def _filter_spectrum_kernel(a_ref, twr_ref, twi_ref, f2_ref, o_ref):
    n2, c = FFT_N2, D_HYENA
    f2 = f2_ref[...]
    bigs = [jnp.dot(f2, x, preferred_element_type=F32)
            for x in _stage2_inputs(a_ref, _tw_columns(twr_ref), _tw_columns(twi_ref))]
    for j, big in enumerate(bigs):
        o_ref[j, 0] = (big[0:n2, 0:c] + big[0:n2, c:2 * c]).astype(o_ref.dtype)
        o_ref[j, 1] = (big[n2:2 * n2, 0:c] - big[n2:2 * n2, c:2 * c]).astype(o_ref.dtype)


def _filter_spectrum(hfilt, tb):
    L, c2 = hfilt.shape
    n1h = tb["n1h"]
    a5 = _fft_stage1(hfilt.reshape(1, L, c2), tb["f1"])
    return pl.pallas_call(
        _filter_spectrum_kernel,
        grid=(n1h // FFT_T1,),
        in_specs=[
            pl.BlockSpec((pl.Squeezed(), 2, FFT_T1, FFT_N2, c2), lambda k: (0, 0, k, 0, 0)),
            pl.BlockSpec((FFT_T1, FFT_N2), lambda k: (k, 0)),
            pl.BlockSpec((FFT_T1, FFT_N2), lambda k: (k, 0)),
            pl.BlockSpec((2 * FFT_N2, 2 * FFT_N2), lambda k: (0, 0)),
        ],
        out_specs=pl.BlockSpec((FFT_T1, 2, FFT_N2, D_HYENA), lambda k: (k, 0, 0, 0)),
        out_shape=jax.ShapeDtypeStruct((n1h, 2, FFT_N2, D_HYENA), SLOT_DTYPE),
        compiler_params=_cparams(("parallel",)),
        name="filter_spectrum",
    )(a5, tb["twr"], tb["twi"], tb["f2"])


def _fft_mid_kernel(a_ref, kf_ref, twr_ref, twi_ref, f2_ref, f2i_ref, o_ref):
    n2 = FFT_N2
    f2, f2i = f2_ref[...], f2i_ref[...]
    trs, tis = _tw_columns(twr_ref), _tw_columns(twi_ref)
    bigs = [jnp.dot(f2, x, preferred_element_type=F32)
            for x in _stage2_inputs(a_ref, trs, tis)]
    ys = []
    for j, big in enumerate(bigs):
        sr, si = big[0:n2], big[n2:2 * n2]
        kr, ki = kf_ref[j, 0].astype(F32), kf_ref[j, 1].astype(F32)
        ys.append(jnp.concatenate([sr * kr - si * ki, sr * ki + si * kr], axis=0).astype(BF16))
    backs = [jnp.dot(f2i, y, preferred_element_type=F32) for y in ys]
    for j, back in enumerate(backs):
        br, bi = back[0:n2], back[n2:2 * n2]
        tr, ti = trs[j], tis[j]
        o_ref[0, j] = (br * tr + bi * ti).astype(o_ref.dtype)
        o_ref[1, j] = (bi * tr - br * ti).astype(o_ref.dtype)


def _fft_mid(a5, kf, tb):
    bsz, _, n1h, n2, c = a5.shape
    slot = pl.BlockSpec((pl.Squeezed(), 2, FFT_T1, n2, c), lambda k, b: (b, 0, k, 0, 0))
    return pl.pallas_call(
        _fft_mid_kernel,
        grid=(n1h // FFT_T1, bsz),
        in_specs=[
            slot,
            pl.BlockSpec((FFT_T1, 2, n2, c), lambda k, b: (k, 0, 0, 0)),
            pl.BlockSpec((FFT_T1, n2), lambda k, b: (k, 0)),
            pl.BlockSpec((FFT_T1, n2), lambda k, b: (k, 0)),
            pl.BlockSpec((2 * n2, 2 * n2), lambda k, b: (0, 0)),
            pl.BlockSpec((2 * n2, 2 * n2), lambda k, b: (0, 0)),
        ],
        out_specs=slot,
        out_shape=jax.ShapeDtypeStruct(a5.shape, SLOT_DTYPE),
        compiler_params=_cparams(("parallel", "parallel")),
        name="fft_mid",
    )(a5, kf, tb["twr"], tb["twi"], tb["f2"], tb["f2inv"])


def _fft_out_kernel(f_ref, b_ref, o_ref):
    _outer_dft(f_ref[...], b_ref, o_ref)


def _fft_out(b5, f1inv):
    bsz, _, n1h, n2, c = b5.shape
    return pl.pallas_call(
        _fft_out_kernel,
        grid=(bsz, c // FFT_LANES),
        in_specs=[
            pl.BlockSpec(f1inv.shape, lambda b, j: (0, 0)),
            pl.BlockSpec((pl.Squeezed(), 2, n1h, n2, FFT_LANES), lambda b, j: (b, 0, 0, 0, j)),
        ],
        out_specs=pl.BlockSpec((1, n1h, n2, FFT_LANES), lambda b, j: (b, 0, 0, j)),
        out_shape=jax.ShapeDtypeStruct((bsz, n1h, n2, c), BF16),
        compiler_params=_cparams(("parallel", "parallel")),
        name="fft_out",
    )(f1inv, b5).reshape(bsz, n1h * n2, c)


def _hyena_conv(u, kf, tb):
    a5 = _fft_stage1(u, tb["f1"])
    b5 = _fft_mid(a5, kf, tb)
    return _fft_out(b5, tb["f1inv"])


def _rms(x, g):
    inv = lax.rsqrt(jnp.mean(x * x, axis=-1, keepdims=True) + EPS)
    return (x * inv) * g


FFN_CHUNK = 256


def _merge_ffn_kernel(ya_ref, yc_ref, u_ref, x0_ref, dk_ref, ga_ref, gh_ref, x_ref, wa_ref, wh_ref,
                      wo_ref, g_ref, wg_ref, wu_ref, wd_ref, gf_ref, o_ref):
    yh = x0_ref[...].astype(F32) * (yc_ref[...].astype(F32)
                                    + u_ref[...].astype(F32) * dk_ref[...])
    pa = jnp.dot(ya_ref[...], wa_ref[...], preferred_element_type=F32)
    ph = jnp.dot(yh.astype(BF16), wh_ref[...], preferred_element_type=F32)
    merged = (jax.nn.sigmoid(ga_ref[...].astype(F32)) * pa
              + jax.nn.sigmoid(gh_ref[...].astype(F32)) * ph)
    x1 = x_ref[...] + jnp.dot(merged.astype(BF16), wo_ref[...], preferred_element_type=F32)
    h = _rms(x1, g_ref[...]).astype(BF16)
    acc = x1
    for c0 in range(0, wg_ref.shape[1], FFN_CHUNK):
        gate = jnp.dot(h, wg_ref[:, c0:c0 + FFN_CHUNK], preferred_element_type=F32)
        up = jnp.dot(h, wu_ref[:, c0:c0 + FFN_CHUNK], preferred_element_type=F32)
        act = (gate * jax.nn.sigmoid(gate) * up).astype(BF16)
        acc = acc + jnp.dot(act, wd_ref[c0:c0 + FFN_CHUNK, :], preferred_element_type=F32)
    o_ref[...] = _rms(acc, gf_ref[...])


def _merge_ffn(ya, yc, u, x0, d_skip, gates, x2, wa, wh, wo, g, wg, wu, wd, gf):
    n, d = x2.shape
    dff = wg.shape[1]
    assert dff % FFN_CHUNK == 0
    tm = min(512, n)
    tok = pl.BlockSpec((tm, D_HYENA), lambda i: (i, 0))

    def resident(a):
        return pl.BlockSpec(a.shape, lambda i: (0, 0), pipeline_mode=pl.Buffered(1))

    def row(width):
        return pl.BlockSpec((1, width), lambda i: (0, 0))

    return pl.pallas_call(
        _merge_ffn_kernel,
        grid=(n // tm,),
        in_specs=[
            pl.BlockSpec((tm, D_ATTN), lambda i: (i, 0)),
            tok, tok, tok,
            row(D_HYENA),
            pl.BlockSpec((tm, d), lambda i: (i, 0)),
            pl.BlockSpec((tm, d), lambda i: (i, 1)),
            pl.BlockSpec((tm, d), lambda i: (i, 0)),
            resident(wa), resident(wh), resident(wo),
            row(d),
            resident(wg), resident(wu), resident(wd),
            row(d),
        ],
        out_specs=pl.BlockSpec((tm, d), lambda i: (i, 0)),
        out_shape=jax.ShapeDtypeStruct((n, d), F32),
        compiler_params=_cparams(("parallel",)),
        name="merge_ffn",
    )(ya, yc, u, x0, d_skip.astype(F32).reshape(1, D_HYENA), gates, gates, x2, wa, wh, wo,
      g.reshape(1, d), wg, wu, wd, gf.reshape(1, d))


def _trunk(x, p, norm_final):
    bsz, L, d = x.shape
    n = bsz * L
    x2 = x.reshape(n, d)
    qkv, u, x0, gates = _inproj_conv(x2, L, p["norm_mix"], p["w_in"], p["conv_w"], p["conv_b"])
    ya = _attention(qkv.reshape(bsz, L, qkv.shape[1]), p["rpb"])
    u = u.reshape(bsz, L, D_HYENA)
    tb = dict(_fft_tables(L))
    for name in ("f1", "f1inv", "f2", "f2inv"):
        tb[name] = jnp.asarray(tb[name]).astype(BF16)
    hfilt = _implicit_filters(L, *p["filt"])
    kf = _filter_spectrum(hfilt, tb)
    yc = _hyena_conv(u, kf, tb)
    out = _merge_ffn(ya.reshape(n, D_ATTN), yc.reshape(n, D_HYENA), u.reshape(n, D_HYENA),
                     x0, p["hyena_d"], gates, x2,
                     p["w_br_attn"], p["w_br_hyena"], p["w_out"],
                     p["norm_ffn"], p["w_gate"], p["w_up"], p["w_down"], norm_final)
    return out.reshape(bsz, L, d)


def kernel(x_prompt, x_sample, norm_mix, w_in, rpb, conv_w, conv_b, filt_w1, filt_b1, filt_w2,
           filt_b2, filt_w3, filt_b3, filt_w4, filt_freq, hyena_d, w_br_attn, w_br_hyena, w_out,
           norm_ffn, w_gate, w_up, w_down, norm_final):
    assert w_in.shape[0] == 1, "the layer definition has depth 1"
    col = lax.broadcasted_iota(jnp.int32, (1, w_in.shape[2]), 1)
    col_scale = jnp.where(col < D_ATTN, Q_SCALE, 1.0).astype(F32)
    p = dict(
        norm_mix=norm_mix[0], w_in=(w_in[0] * col_scale).astype(BF16), rpb=rpb[0].reshape(-1),
        conv_w=conv_w[0], conv_b=conv_b[0],
        filt=(filt_w1[0], filt_b1[0], filt_w2[0], filt_b2[0], filt_w3[0], filt_b3[0],
              filt_w4[0], filt_freq[0]),
        hyena_d=hyena_d[0],
        w_br_attn=w_br_attn[0].astype(BF16), w_br_hyena=w_br_hyena[0].astype(BF16),
        w_out=w_out[0].astype(BF16), norm_ffn=norm_ffn[0],
        w_gate=w_gate[0].astype(BF16), w_up=w_up[0].astype(BF16),
        w_down=w_down[0].astype(BF16),
    )
    return (_trunk(x_prompt, p, norm_final), _trunk(x_sample, p, norm_final))
```

```python
import functools
import math

import numpy as np
import jax
import jax.numpy as jnp
from jax import lax
from jax.experimental import pallas as pl
from jax.experimental.pallas import tpu as pltpu

F32 = jnp.float32
BF16 = jnp.bfloat16

GRID_W = 64
N_HEADS = 8
HEAD_DIM = 64
D_ATTN = N_HEADS * HEAD_DIM
WIN_R = 8
WIN_C = 16
D_HYENA = 512
SHORT_K = 3
FILTER_EMB = 33
FILTER_HIDDEN = 64
DECAY_TARGET = 1e-2
FAST_DECAY_PCT = 0.3
SLOW_DECAY_PCT = 1.5
EPS = 1e-6

FFT_N2 = 128
ATT_ROWS = 16
ATT_UNROLL = 8
NEG_BIG = -1e30
VMEM_LIMIT = 56 * 1024 * 1024


def _cparams(sem):
    return pltpu.CompilerParams(dimension_semantics=sem, vmem_limit_bytes=VMEM_LIMIT)


IN_CHUNK = 512
HALO = 8


def _inproj_conv_kernel(x_ref, xp_ref, xn_ref, g_ref, wqkv_ref, why_ref, wgt_ref, cw_ref, cb_ref,
                        qkv_ref, u_ref, x0_ref, gt_ref, *, tiles_per_seq):
    i = pl.program_id(0)
    g = g_ref[...]
    h = _rms(x_ref[...], g).astype(BF16)
    hh = _rms(jnp.concatenate([xp_ref[...], xn_ref[...]], axis=0), g).astype(BF16)
    for w_ref, o_ref in ((wqkv_ref, qkv_ref), (wgt_ref, gt_ref)):
        for c0 in range(0, w_ref.shape[1], IN_CHUNK):
            o_ref[:, c0:c0 + IN_CHUNK] = jnp.dot(
                h, w_ref[:, c0:c0 + IN_CHUNK], preferred_element_type=F32).astype(o_ref.dtype)
    tm = h.shape[0]
    c = D_HYENA
    row = lax.broadcasted_iota(jnp.int32, (tm, c), 0)
    first = (i % tiles_per_seq) == 0
    last = (i % tiles_per_seq) == tiles_per_seq - 1
    ys = []
    for k in range(3):
        w = why_ref[:, k * c:(k + 1) * c]
        hy = jnp.dot(h, w, preferred_element_type=F32)
        hyh = jnp.dot(hh, w, preferred_element_type=F32)
        prev_row = jnp.where(first, 0.0, hyh[HALO - 1:HALO])
        next_row = jnp.where(last, 0.0, hyh[HALO:HALO + 1])
        xm1 = jnp.where(row == 0, prev_row, pltpu.roll(hy, 1, 0))
        xp1 = jnp.where(row == tm - 1, next_row, pltpu.roll(hy, tm - 1, 0))
        cw = cw_ref[:, k * c:(k + 1) * c]
        ys.append(xm1 * cw[0:1] + hy * cw[1:2] + xp1 * cw[2:3] + cb_ref[:, k * c:(k + 1) * c])
    x0_ref[...] = ys[0].astype(x0_ref.dtype)
    u_ref[...] = (ys[1] * ys[2]).astype(u_ref.dtype)


def _inproj_conv(x2, L, g, w_bf16, conv_w, conv_b):
    n, d = x2.shape
    n_qkv, n_hy = 3 * D_ATTN, 3 * D_HYENA
    tm = min(1024, L)
    nh = n // HALO
    hb = tm // HALO

    def resident(a):
        return pl.BlockSpec(a.shape, lambda i: (0, 0), pipeline_mode=pl.Buffered(1))

    wqkv, why, wgt = w_bf16[:, :n_qkv], w_bf16[:, n_qkv:n_qkv + n_hy], w_bf16[:, n_qkv + n_hy:]
    widths = (n_qkv, D_HYENA, D_HYENA, wgt.shape[1])
    return pl.pallas_call(
        functools.partial(_inproj_conv_kernel, tiles_per_seq=L // tm),
        grid=(n // tm,),
        in_specs=[
            pl.BlockSpec((tm, d), lambda i: (i, 0)),
            pl.BlockSpec((HALO, d), lambda i: (jnp.maximum(i * hb - 1, 0), 0)),
            pl.BlockSpec((HALO, d), lambda i: (jnp.minimum((i + 1) * hb, nh - 1), 0)),
            pl.BlockSpec((1, d), lambda i: (0, 0)),
            resident(wqkv), resident(why), resident(wgt),
            pl.BlockSpec((SHORT_K, n_hy), lambda i: (0, 0)),
            pl.BlockSpec((1, n_hy), lambda i: (0, 0)),
        ],
        out_specs=[pl.BlockSpec((tm, w), lambda i: (i, 0)) for w in widths],
        out_shape=[jax.ShapeDtypeStruct((n, w), BF16) for w in widths],
        compiler_params=_cparams(("parallel",)),
        name="inproj_conv",
    )(x2, x2, x2, g.reshape(1, d), wqkv, why, wgt, conv_w, conv_b.reshape(1, n_hy))


N_DR = 2 * WIN_R - 1
N_DC = 2 * WIN_C - 1
LOG2_E = 1.4426950408889634
Q_SCALE = (HEAD_DIM ** -0.5) * LOG2_E
KV_ROWS = ATT_ROWS + WIN_R


def _build_bias_tiles(rpb_ref, pt_ref):
    k = lax.broadcasted_iota(jnp.int32, (GRID_W, 128), 0)
    lane = lax.broadcasted_iota(jnp.int32, (GRID_W, 128), 1)
    q = lane & (GRID_W - 1)
    first_half = lane < GRID_W
    cs = jnp.clip(q - WIN_C // 2, 0, GRID_W - WIN_C)
    valid = (k >= cs) & (k < cs + WIN_C)
    dc = jnp.clip(k - q, -(WIN_C - 1), WIN_C - 1) + (WIN_C - 1)

    def body(idx, carry):
        p = idx // N_DR
        d = idx % N_DR
        base0 = ((2 * p) * N_DR + d) * N_DC
        base1 = ((2 * p + 1) * N_DR + d) * N_DC
        acc = jnp.zeros((GRID_W, 128), F32)
        for off in range(N_DC):
            val = jnp.where(first_half, rpb_ref[base0 + off], rpb_ref[base1 + off])
            acc = jnp.where(dc == off, val, acc)
        pt_ref[idx] = jnp.where(valid, acc * LOG2_E, NEG_BIG)
        return carry

    lax.fori_loop(0, (N_HEADS // 2) * N_DR, body, 0)


def _kv_window_start(i, rows):
    return jnp.clip(i * ATT_ROWS - WIN_R // 2, 0, rows - KV_ROWS)


def _attn_kernel(rpb_ref, q_ref, k_ref, v_ref, o_ref, pt_ref, *, rows):
    b = pl.program_id(0)
    i = pl.program_id(1)

    @pl.when((b == 0) & (i == 0))
    def _():
        _build_bias_tiles(rpb_ref, pt_ref)

    lane = lax.broadcasted_iota(jnp.int32, (GRID_W, 128), 1)
    first_half = lane < HEAD_DIM
    ones = jnp.ones((WIN_R * GRID_W, 128), BF16)
    nkeys = WIN_R * GRID_W
    wstart = _kv_window_start(i, rows)
    nt_dims = (((1,), (1,)), ((), ()))

    pairs = range(N_HEADS // 2)

    def scores(rr):
        r = i * ATT_ROWS + rr
        rs = jnp.clip(r - WIN_R // 2, 0, rows - WIN_R)
        s = rs - r + (WIN_R - 1)
        koff = pl.multiple_of((rs - wstart) * GRID_W, GRID_W)
        q = q_ref[pl.ds(pl.multiple_of(rr * GRID_W, GRID_W), GRID_W), :]
        sts = []
        for p in pairs:
            qp = q[:, 128 * p:128 * (p + 1)]
            zero = jnp.zeros_like(qp)
            wt = jnp.concatenate([jnp.where(first_half, qp, zero),
                                  jnp.where(first_half, zero, qp)], axis=0)
            kp = k_ref[pl.ds(koff, nkeys), 128 * p:128 * (p + 1)]
            sts.append(lax.dot_general(kp, wt, nt_dims, preferred_element_type=F32))
        return sts, s, koff

    def probabilities(sts, s):
        pms = []
        for p in pairs:
            st = sts[p]
            ch = [st[GRID_W * c:GRID_W * (c + 1)] + pt_ref[p * N_DR + s + c] for c in range(WIN_R)]
            m = ch[0]
            for c in range(1, WIN_R):
                m = jnp.maximum(m, ch[c])
            m = jnp.max(m, axis=0, keepdims=True)
            pt = jnp.concatenate([jnp.exp2(c - m) for c in ch], axis=0).astype(BF16)
            pms.append(pt.T)
        return pms

    def weighted_values(rr, pms, koff):
        outs = []
        for p in pairs:
            vp = v_ref[pl.ds(koff, nkeys), 128 * p:128 * (p + 1)]
            ov = jnp.dot(pms[p], jnp.concatenate([vp, ones], axis=1), preferred_element_type=F32)
            o = ov[:, 0:128] / ov[:, 128:256]
            outs.append(jnp.where(first_half, o[0:GRID_W], o[GRID_W:2 * GRID_W]))
        qoff = pl.multiple_of(rr * GRID_W, GRID_W)
        o_ref[pl.ds(qoff, GRID_W), :] = jnp.concatenate(outs, axis=1).astype(o_ref.dtype)

    def trip(t, carry):
        r0 = t * ATT_UNROLL
        nxt = scores(r0)
        for j in range(ATT_UNROLL):
            sts, s, koff = nxt
            if j + 1 < ATT_UNROLL:
                nxt = scores(r0 + j + 1)
            weighted_values(r0 + j, probabilities(sts, s), koff)
        return carry

    lax.fori_loop(0, ATT_ROWS // ATT_UNROLL, trip, 0)


def _attention(z3, rpb_flat):
    bsz, L, _ = z3.shape
    rows = L // GRID_W
    assert rows % ATT_ROWS == 0 and rows >= KV_ROWS
    nblk = rows // ATT_ROWS
    blk = ATT_ROWS * GRID_W

    def window(col):
        return pl.BlockSpec(
            (pl.Squeezed(), pl.Element(KV_ROWS * GRID_W), pl.Element(D_ATTN)),
            lambda b, i: (b, _kv_window_start(i, rows) * GRID_W, col * D_ATTN))

    return pl.pallas_call(
        functools.partial(_attn_kernel, rows=rows),
        grid=(bsz, nblk),
        in_specs=[
            pl.BlockSpec(memory_space=pltpu.SMEM),
            pl.BlockSpec((pl.Squeezed(), blk, D_ATTN), lambda b, i: (b, i, 0)),
            window(1),
            window(2),
        ],
        out_specs=pl.BlockSpec((pl.Squeezed(), blk, D_ATTN), lambda b, i: (b, i, 0)),
        out_shape=jax.ShapeDtypeStruct((bsz, L, D_ATTN), BF16),
        scratch_shapes=[pltpu.VMEM(((N_HEADS // 2) * N_DR, GRID_W, 128), F32)],
        compiler_params=_cparams(("arbitrary", "arbitrary")),
        name="nbr_attention",
    )(rpb_flat, z3, z3, z3)


TWO_OVER_PI = 0.6366197723675814
PIO2_1 = 1.5703125
PIO2_2 = 4.837512969970703125e-4
PIO2_3 = 7.54978995489188216e-8
TRIG_FAST_LIMIT = 4096.0


def _quadrant_value(x, shift):
    kf = jnp.floor(x * TWO_OVER_PI + 0.5)
    r = ((x - kf * PIO2_1) - kf * PIO2_2) - kf * PIO2_3
    z = r * r
    s = r + r * z * (-1.6666654611e-1 + z * (8.3321608736e-3 + z * -1.9515295891e-4))
    c = (1.0 - 0.5 * z) + z * z * (4.166664568298827e-2
                                   + z * (-1.388731625493765e-3 + z * 2.443315711809948e-5))
    k = kf + shift
    half = jnp.floor(k * 0.5)
    odd = k - 2.0 * half
    flip = half - 2.0 * jnp.floor(half * 0.5)
    return (s + odd * (c - s)) * (1.0 - 2.0 * flip)


def _sin_small(x):
    return _quadrant_value(x, 0.0)


def _cos_small(x):
    return _quadrant_value(x, 1.0)


def _sin(x):
    return lax.cond(jnp.max(jnp.abs(x)) < TRIG_FAST_LIMIT, _sin_small, jnp.sin, x)


def _cos(x):
    return lax.cond(jnp.max(jnp.abs(x)) < TRIG_FAST_LIMIT, _cos_small, jnp.cos, x)


def _hi_lo(x):
    hi = x.astype(BF16)
    return hi, (x - hi.astype(F32)).astype(BF16)


def _stack_cols(w):
    hi, lo = _hi_lo(w)
    return jnp.concatenate([hi, hi, lo], axis=1)


def _stack_rows(h):
    hi, lo = _hi_lo(h)
    return jnp.concatenate([hi, lo, hi], axis=0)


def _dot_stacked(w_stacked, h):
    return jnp.dot(w_stacked, _stack_rows(h), preferred_element_type=F32)


def _filter_kernel(fb_ref, w1t_ref, w1c_ref, w1s_ref, b1_ref, w2_ref, b2_ref, w3_ref, b3_ref,
                   w4_ref, fr_ref, dl_ref, o_ref, *, L, tp):
    j = pl.program_id(0)
    fh = FILTER_HIDDEN
    pos = (lax.broadcasted_iota(jnp.int32, (1, tp), 1) + j * tp).astype(F32)
    t = pos / (L - 1.0)
    omega = (2.0 * math.pi) * pos / float(L)
    ang = fb_ref[...] * omega
    fr = fr_ref[...]
    pre = (w1t_ref[...] * t
           + _dot_stacked(w1c_ref[...], _cos(ang)) - _dot_stacked(w1s_ref[...], _sin(ang)))
    h = _sin(fr * (pre + b1_ref[...]))
    h = _sin(fr * (_dot_stacked(w2_ref[...], h) + b2_ref[...]))
    h = _sin(fr * (_dot_stacked(w3_ref[...], h) + b3_ref[...]))
    h_hi = h.astype(BF16).astype(F32)
    tr = jnp.concatenate([h_hi, h - h_hi], axis=0).T.astype(BF16)
    lane = lax.broadcasted_iota(jnp.int32, tr.shape, 1)
    lhs = jnp.concatenate([tr, jnp.where(lane < fh, tr, jnp.zeros_like(tr))], axis=1)
    out = jnp.dot(lhs, w4_ref[...], preferred_element_type=F32)
    tcol = (lax.broadcasted_iota(jnp.int32, (tp, 1), 0) + j * tp).astype(F32) / (L - 1.0)
    decay = jnp.exp(-tcol * dl_ref[...])
    c = D_HYENA
    o_ref[:, 0:c] = (out[:, 0:c] * decay).astype(o_ref.dtype)
    o_ref[:, c:2 * c] = (out[:, c:2 * c] * decay).astype(o_ref.dtype)


def _implicit_filters(L, w1, b1, w2, b2, w3, b3, w4, freq):
    bands = (FILTER_EMB - 1) // 2
    fh = FILTER_HIDDEN
    tp = min(1024, L)
    fb = jnp.linspace(1e-4, bands - 1, bands, dtype=F32).reshape(bands, 1)
    max_decay = math.log(DECAY_TARGET) / FAST_DECAY_PCT
    min_decay = math.log(DECAY_TARGET) / SLOW_DECAY_PCT
    deltas = jnp.abs(jnp.linspace(min_decay, max_decay, D_HYENA, dtype=F32)).reshape(1, D_HYENA)
    w1 = w1.astype(F32)
    w4_hi, w4_lo = _hi_lo(w4.astype(F32))
    args = (
        fb,
        w1[0:1, :].T,
        _stack_cols(w1[1:1 + bands, :].T),
        _stack_cols(w1[1 + bands:, :].T),
        b1.astype(F32).reshape(fh, 1),
        _stack_cols(w2.astype(F32).T), b2.astype(F32).reshape(fh, 1),
        _stack_cols(w3.astype(F32).T), b3.astype(F32).reshape(fh, 1),
        jnp.concatenate([w4_hi, w4_hi, w4_lo, jnp.zeros_like(w4_hi)], axis=0),
        freq.astype(F32).reshape(fh, 1),
        deltas,
    )

    def full(a):
        return pl.BlockSpec(a.shape, lambda j: (0,) * a.ndim)

    return pl.pallas_call(
        functools.partial(_filter_kernel, L=L, tp=tp),
        grid=(L // tp,),
        in_specs=[full(a) for a in args],
        out_specs=pl.BlockSpec((tp, 2 * D_HYENA), lambda j: (j, 0)),
        out_shape=jax.ShapeDtypeStruct((L, 2 * D_HYENA), BF16),
        compiler_params=_cparams(("parallel",)),
        name="implicit_filter",
    )(*args)


FFT_LANES = 128
FFT_T1 = 16


@functools.lru_cache(maxsize=None)
def _fft_tables(L):
    n = 2 * L
    n2 = FFT_N2
    n1 = n // n2
    n1h = n1 // 2
    odd = 2 * np.arange(n1h) + 1
    th = 2.0 * np.pi * (np.outer(odd, np.arange(n1h)) % (2 * n1)) / (2 * n1)
    f1 = np.concatenate([np.cos(th), -np.sin(th)], axis=0)
    tw = 2.0 * np.pi * (np.outer(odd, np.arange(n2)) % (2 * n)) / (2 * n)
    twr = np.cos(tw)
    twi = -np.sin(tw)
    t2 = 2.0 * np.pi * (np.outer(np.arange(n2), np.arange(n2)) % n2) / n2
    fr, fi = np.cos(t2), -np.sin(t2)
    f2 = np.block([[fr, -fi], [fi, fr]])
    f2inv = np.block([[fr, fi], [-fi, fr]])
    f1inv = np.concatenate([np.cos(th.T), -np.sin(th.T)], axis=1) * (2.0 / n)
    return dict(
        n1=n1, n1h=n1h,
        f1=np.asarray(f1, np.float32), f1inv=np.asarray(f1inv, np.float32),
        f2=np.asarray(f2, np.float32), f2inv=np.asarray(f2inv, np.float32),
        twr=np.asarray(twr, np.float32), twi=np.asarray(twi, np.float32),
    )


FFT_A = 16
SLOT_DTYPE = BF16


def _outer_dft(f, src_ref, dst_ref):
    n_in, n1h = src_ref.shape[0], src_ref.shape[1]
    n_out = dst_ref.shape[0]

    def body(i, carry):
        a0 = pl.multiple_of(i * FFT_A, FFT_A)
        ts = [jnp.swapaxes(src_ref[p, :, pl.ds(a0, FFT_A), :].astype(F32), 0, 1)
              for p in range(n_in)]
        outs = [[] for _ in range(n_out)]
        for j in range(0, FFT_A, 2):
            rhs = jnp.concatenate([jnp.concatenate([t[j], t[j + 1]], axis=1) for t in ts],
                                  axis=0).astype(BF16)
            res = jnp.dot(f, rhs, preferred_element_type=F32)
            for q in range(n_out):
                blk = res[q * n1h:(q + 1) * n1h]
                outs[q] += [blk[:, 0:FFT_LANES], blk[:, FFT_LANES:2 * FFT_LANES]]
        for q in range(n_out):
            dst_ref[q, :, pl.ds(a0, FFT_A), :] = jnp.swapaxes(
                jnp.stack(outs[q], axis=0), 0, 1).astype(dst_ref.dtype)
        return carry

    lax.fori_loop(0, FFT_N2 // FFT_A, body, 0, unroll=2)


def _fft_stage1_kernel(f_ref, u_ref, o_ref):
    _outer_dft(f_ref[...], u_ref, o_ref)


def _fft_stage1(u3, f1):
    bsz, L, c = u3.shape
    n1h = L // FFT_N2
    return pl.pallas_call(
        _fft_stage1_kernel,
        grid=(bsz, c // FFT_LANES),
        in_specs=[
            pl.BlockSpec(f1.shape, lambda b, j: (0, 0)),
            pl.BlockSpec((1, n1h, FFT_N2, FFT_LANES), lambda b, j: (b, 0, 0, j)),
        ],
        out_specs=pl.BlockSpec((pl.Squeezed(), 2, n1h, FFT_N2, FFT_LANES),
                               lambda b, j: (b, 0, 0, 0, j)),
        out_shape=jax.ShapeDtypeStruct((bsz, 2, n1h, FFT_N2, c), SLOT_DTYPE),
        compiler_params=_cparams(("parallel", "parallel")),
        name="fft_stage1",
    )(f1, u3.reshape(bsz, n1h, FFT_N2, c))


def _twiddle(ar, ai, tr, ti):
    return ar * tr - ai * ti, ar * ti + ai * tr


def _tw_columns(tw_ref):
    t = tw_ref[...].T
    return [t[:, j:j + 1] for j in range(FFT_T1)]


def _stage2_inputs(a_ref, trs, tis):
    xs = []
    for j in range(FFT_T1):
        xr, xi = _twiddle(a_ref[0, j].astype(F32), a_ref[1, j].astype(F32), trs[j], tis[j])
        xs.append(jnp.concatenate([xr, xi], axis=0).astype(BF16))
    return xs


def _filter_spectrum_kernel(a_ref, twr_ref, twi_ref, f2_ref, o_ref):
    n2, c = FFT_N2, D_HYENA
    f2 = f2_ref[...]
    bigs = [jnp.dot(f2, x, preferred_element_type=F32)
            for x in _stage2_inputs(a_ref, _tw_columns(twr_ref), _tw_columns(twi_ref))]
    for j, big in enumerate(bigs):
        o_ref[j, 0] = (big[0:n2, 0:c] + big[0:n2, c:2 * c]).astype(o_ref.dtype)
        o_ref[j, 1] = (big[n2:2 * n2, 0:c] - big[n2:2 * n2, c:2 * c]).astype(o_ref.dtype)


def _filter_spectrum(hfilt, tb):
    L, c2 = hfilt.shape
    n1h = tb["n1h"]
    a5 = _fft_stage1(hfilt.reshape(1, L, c2), tb["f1"])
    return pl.pallas_call(
        _filter_spectrum_kernel,
        grid=(n1h // FFT_T1,),
        in_specs=[
            pl.BlockSpec((pl.Squeezed(), 2, FFT_T1, FFT_N2, c2), lambda k: (0, 0, k, 0, 0)),
            pl.BlockSpec((FFT_T1, FFT_N2), lambda k: (k, 0)),
            pl.BlockSpec((FFT_T1, FFT_N2), lambda k: (k, 0)),
            pl.BlockSpec((2 * FFT_N2, 2 * FFT_N2), lambda k: (0, 0)),
        ],
        out_specs=pl.BlockSpec((FFT_T1, 2, FFT_N2, D_HYENA), lambda k: (k, 0, 0, 0)),
        out_shape=jax.ShapeDtypeStruct((n1h, 2, FFT_N2, D_HYENA), SLOT_DTYPE),
        compiler_params=_cparams(("parallel",)),
        name="filter_spectrum",
    )(a5, tb["twr"], tb["twi"], tb["f2"])


def _fft_mid_kernel(a_ref, kf_ref, twr_ref, twi_ref, f2_ref, f2i_ref, o_ref):
    n2 = FFT_N2
    f2, f2i = f2_ref[...], f2i_ref[...]
    trs, tis = _tw_columns(twr_ref), _tw_columns(twi_ref)
    bigs = [jnp.dot(f2, x, preferred_element_type=F32)
            for x in _stage2_inputs(a_ref, trs, tis)]
    ys = []
    for j, big in enumerate(bigs):
        sr, si = big[0:n2], big[n2:2 * n2]
        kr, ki = kf_ref[j, 0].astype(F32), kf_ref[j, 1].astype(F32)
        ys.append(jnp.concatenate([sr * kr - si * ki, sr * ki + si * kr], axis=0).astype(BF16))
    backs = [jnp.dot(f2i, y, preferred_element_type=F32) for y in ys]
    for j, back in enumerate(backs):
        br, bi = back[0:n2], back[n2:2 * n2]
        tr, ti = trs[j], tis[j]
        o_ref[0, j] = (br * tr + bi * ti).astype(o_ref.dtype)
        o_ref[1, j] = (bi * tr - br * ti).astype(o_ref.dtype)


def _fft_mid(a5, kf, tb):
    bsz, _, n1h, n2, c = a5.shape
    slot = pl.BlockSpec((pl.Squeezed(), 2, FFT_T1, n2, c), lambda k, b: (b, 0, k, 0, 0))
    return pl.pallas_call(
        _fft_mid_kernel,
        grid=(n1h // FFT_T1, bsz),
        in_specs=[
            slot,
            pl.BlockSpec((FFT_T1, 2, n2, c), lambda k, b: (k, 0, 0, 0)),
            pl.BlockSpec((FFT_T1, n2), lambda k, b: (k, 0)),
            pl.BlockSpec((FFT_T1, n2), lambda k, b: (k, 0)),
            pl.BlockSpec((2 * n2, 2 * n2), lambda k, b: (0, 0)),
            pl.BlockSpec((2 * n2, 2 * n2), lambda k, b: (0, 0)),
        ],
        out_specs=slot,
        out_shape=jax.ShapeDtypeStruct(a5.shape, SLOT_DTYPE),
        compiler_params=_cparams(("parallel", "parallel")),
        name="fft_mid",
    )(a5, kf, tb["twr"], tb["twi"], tb["f2"], tb["f2inv"])


def _fft_out_kernel(f_ref, b_ref, o_ref):
    _outer_dft(f_ref[...], b_ref, o_ref)


def _fft_out(b5, f1inv):
    bsz, _, n1h, n2, c = b5.shape
    return pl.pallas_call(
        _fft_out_kernel,
        grid=(bsz, c // FFT_LANES),
        in_specs=[
            pl.BlockSpec(f1inv.shape, lambda b, j: (0, 0)),
            pl.BlockSpec((pl.Squeezed(), 2, n1h, n2, FFT_LANES), lambda b, j: (b, 0, 0, 0, j)),
        ],
        out_specs=pl.BlockSpec((1, n1h, n2, FFT_LANES), lambda b, j: (b, 0, 0, j)),
        out_shape=jax.ShapeDtypeStruct((bsz, n1h, n2, c), BF16),
        compiler_params=_cparams(("parallel", "parallel")),
        name="fft_out",
    )(f1inv, b5).reshape(bsz, n1h * n2, c)


def _hyena_conv(u, kf, tb):
    a5 = _fft_stage1(u, tb["f1"])
    b5 = _fft_mid(a5, kf, tb)
    return _fft_out(b5, tb["f1inv"])


def _rms(x, g):
    inv = lax.rsqrt(jnp.mean(x * x, axis=-1, keepdims=True) + EPS)
    return (x * inv) * g


FFN_CHUNK = 256


def _merge_ffn_kernel(ya_ref, yc_ref, u_ref, x0_ref, dk_ref, ga_ref, gh_ref, x_ref, wa_ref, wh_ref,
                      wo_ref, g_ref, wg_ref, wu_ref, wd_ref, gf_ref, o_ref):
    yh = x0_ref[...].astype(F32) * (yc_ref[...].astype(F32)
                                    + u_ref[...].astype(F32) * dk_ref[...])
    pa = jnp.dot(ya_ref[...], wa_ref[...], preferred_element_type=F32)
    ph = jnp.dot(yh.astype(BF16), wh_ref[...], preferred_element_type=F32)
    merged = (jax.nn.sigmoid(ga_ref[...].astype(F32)) * pa
              + jax.nn.sigmoid(gh_ref[...].astype(F32)) * ph)
    x1 = x_ref[...] + jnp.dot(merged.astype(BF16), wo_ref[...], preferred_element_type=F32)
    h = _rms(x1, g_ref[...]).astype(BF16)
    acc = x1
    for c0 in range(0, wg_ref.shape[1], FFN_CHUNK):
        gate = jnp.dot(h, wg_ref[:, c0:c0 + FFN_CHUNK], preferred_element_type=F32)
        up = jnp.dot(h, wu_ref[:, c0:c0 + FFN_CHUNK], preferred_element_type=F32)
        act = (gate * jax.nn.sigmoid(gate) * up).astype(BF16)
        acc = acc + jnp.dot(act, wd_ref[c0:c0 + FFN_CHUNK, :], preferred_element_type=F32)
    o_ref[...] = _rms(acc, gf_ref[...])


def _merge_ffn(ya, yc, u, x0, d_skip, gates, x2, wa, wh, wo, g, wg, wu, wd, gf):
    n, d = x2.shape
    dff = wg.shape[1]
    assert dff % FFN_CHUNK == 0
    tm = min(512, n)
    tok = pl.BlockSpec((tm, D_HYENA), lambda i: (i, 0))

    def resident(a):
        return pl.BlockSpec(a.shape, lambda i: (0, 0), pipeline_mode=pl.Buffered(1))

    def row(width):
        return pl.BlockSpec((1, width), lambda i: (0, 0))

    return pl.pallas_call(
        _merge_ffn_kernel,
        grid=(n // tm,),
        in_specs=[
            pl.BlockSpec((tm, D_ATTN), lambda i: (i, 0)),
            tok, tok, tok,
            row(D_HYENA),
            pl.BlockSpec((tm, d), lambda i: (i, 0)),
            pl.BlockSpec((tm, d), lambda i: (i, 1)),
            pl.BlockSpec((tm, d), lambda i: (i, 0)),
            resident(wa), resident(wh), resident(wo),
            row(d),
            resident(wg), resident(wu), resident(wd),
            row(d),
        ],
        out_specs=pl.BlockSpec((tm, d), lambda i: (i, 0)),
        out_shape=jax.ShapeDtypeStruct((n, d), F32),
        compiler_params=_cparams(("parallel",)),
        name="merge_ffn",
    )(ya, yc, u, x0, d_skip.astype(F32).reshape(1, D_HYENA), gates, gates, x2, wa, wh, wo,
      g.reshape(1, d), wg, wu, wd, gf.reshape(1, d))


def _trunk(x, p, norm_final):
    bsz, L, d = x.shape
    n = bsz * L
    x2 = x.reshape(n, d)
    qkv, u, x0, gates = _inproj_conv(x2, L, p["norm_mix"], p["w_in"], p["conv_w"], p["conv_b"])
    ya = _attention(qkv.reshape(bsz, L, qkv.shape[1]), p["rpb"])
    u = u.reshape(bsz, L, D_HYENA)
    tb = dict(_fft_tables(L))
    for name in ("f1", "f1inv", "f2", "f2inv"):
        tb[name] = jnp.asarray(tb[name]).astype(BF16)
    hfilt = _implicit_filters(L, *p["filt"])
    kf = _filter_spectrum(hfilt, tb)
    yc = _hyena_conv(u, kf, tb)
    out = _merge_ffn(ya.reshape(n, D_ATTN), yc.reshape(n, D_HYENA), u.reshape(n, D_HYENA),
                     x0, p["hyena_d"], gates, x2,
                     p["w_br_attn"], p["w_br_hyena"], p["w_out"],
                     p["norm_ffn"], p["w_gate"], p["w_up"], p["w_down"], norm_final)
    return out.reshape(bsz, L, d)


def kernel(x_prompt, x_sample, norm_mix, w_in, rpb, conv_w, conv_b, filt_w1, filt_b1, filt_w2,
           filt_b2, filt_w3, filt_b3, filt_w4, filt_freq, hyena_d, w_br_attn, w_br_hyena, w_out,
           norm_ffn, w_gate, w_up, w_down, norm_final):
    assert w_in.shape[0] == 1, "the layer definition has depth 1"
    col = lax.broadcasted_iota(jnp.int32, (1, w_in.shape[2]), 1)
    col_scale = jnp.where(col < D_ATTN, Q_SCALE, 1.0).astype(F32)
    p = dict(
        norm_mix=norm_mix[0], w_in=(w_in[0] * col_scale).astype(BF16), rpb=rpb[0].reshape(-1),
        conv_w=conv_w[0], conv_b=conv_b[0],
        filt=(filt_w1[0], filt_b1[0], filt_w2[0], filt_b2[0], filt_w3[0], filt_b3[0],
              filt_w4[0], filt_freq[0]),
        hyena_d=hyena_d[0],
        w_br_attn=w_br_attn[0].astype(BF16), w_br_hyena=w_br_hyena[0].astype(BF16),
        w_out=w_out[0].astype(BF16), norm_ffn=norm_ffn[0],
        w_gate=w_gate[0].astype(BF16), w_up=w_up[0].astype(BF16),
        w_down=w_down[0].astype(BF16),
    )
    return (_trunk(x_prompt, p, norm_final), _trunk(x_sample, p, norm_final))
```

```python
import functools
import math

import numpy as np
import jax
import jax.numpy as jnp
from jax import lax
from jax.experimental import pallas as pl
from jax.experimental.pallas import tpu as pltpu

F32 = jnp.float32
BF16 = jnp.bfloat16

GRID_W = 64
N_HEADS = 8
HEAD_DIM = 64
D_ATTN = N_HEADS * HEAD_DIM
WIN_R = 8
WIN_C = 16
D_HYENA = 512
SHORT_K = 3
FILTER_EMB = 33
FILTER_HIDDEN = 64
DECAY_TARGET = 1e-2
FAST_DECAY_PCT = 0.3
SLOW_DECAY_PCT = 1.5
EPS = 1e-6

FFT_N2 = 128
ATT_ROWS = 16
ATT_UNROLL = 8
NEG_BIG = -1e30
VMEM_LIMIT = 56 * 1024 * 1024


def _cparams(sem):
    return pltpu.CompilerParams(dimension_semantics=sem, vmem_limit_bytes=VMEM_LIMIT)


IN_CHUNK = 512
HALO = 8


def _inproj_conv_kernel(x_ref, xp_ref, xn_ref, g_ref, wqkv_ref, why_ref, wgt_ref, cw_ref, cb_ref,
                        qkv_ref, u_ref, x0_ref, gt_ref, *, tiles_per_seq):
    i = pl.program_id(0)
    g = g_ref[...]
    h = _rms(x_ref[...], g).astype(BF16)
    hh = _rms(jnp.concatenate([xp_ref[...], xn_ref[...]], axis=0), g).astype(BF16)
    for w_ref, o_ref in ((wqkv_ref, qkv_ref), (wgt_ref, gt_ref)):
        for c0 in range(0, w_ref.shape[1], IN_CHUNK):
            o_ref[:, c0:c0 + IN_CHUNK] = jnp.dot(
                h, w_ref[:, c0:c0 + IN_CHUNK], preferred_element_type=F32).astype(o_ref.dtype)
    tm = h.shape[0]
    c = D_HYENA
    row = lax.broadcasted_iota(jnp.int32, (tm, c), 0)
    first = (i % tiles_per_seq) == 0
    last = (i % tiles_per_seq) == tiles_per_seq - 1
    ys = []
    for k in range(3):
        w = why_ref[:, k * c:(k + 1) * c]
        hy = jnp.dot(h, w, preferred_element_type=F32)
        hyh = jnp.dot(hh, w, preferred_element_type=F32)
        prev_row = jnp.where(first, 0.0, hyh[HALO - 1:HALO])
        next_row = jnp.where(last, 0.0, hyh[HALO:HALO + 1])
        xm1 = jnp.where(row == 0, prev_row, pltpu.roll(hy, 1, 0))
        xp1 = jnp.where(row == tm - 1, next_row, pltpu.roll(hy, tm - 1, 0))
        cw = cw_ref[:, k * c:(k + 1) * c]
        ys.append(xm1 * cw[0:1] + hy * cw[1:2] + xp1 * cw[2:3] + cb_ref[:, k * c:(k + 1) * c])
    x0_ref[...] = ys[0].astype(x0_ref.dtype)
    u_ref[...] = (ys[1] * ys[2]).astype(u_ref.dtype)


def _inproj_conv(x2, L, g, w_bf16, conv_w, conv_b):
    n, d = x2.shape
    n_qkv, n_hy = 3 * D_ATTN, 3 * D_HYENA
    tm = min(1024, L)
    nh = n // HALO
    hb = tm // HALO

    def resident(a):
        return pl.BlockSpec(a.shape, lambda i: (0, 0), pipeline_mode=pl.Buffered(1))

    wqkv, why, wgt = w_bf16[:, :n_qkv], w_bf16[:, n_qkv:n_qkv + n_hy], w_bf16[:, n_qkv + n_hy:]
    widths = (n_qkv, D_HYENA, D_HYENA, wgt.shape[1])
    return pl.pallas_call(
        functools.partial(_inproj_conv_kernel, tiles_per_seq=L // tm),
        grid=(n // tm,),
        in_specs=[
            pl.BlockSpec((tm, d), lambda i: (i, 0)),
            pl.BlockSpec((HALO, d), lambda i: (jnp.maximum(i * hb - 1, 0), 0)),
            pl.BlockSpec((HALO, d), lambda i: (jnp.minimum((i + 1) * hb, nh - 1), 0)),
            pl.BlockSpec((1, d), lambda i: (0, 0)),
            resident(wqkv), resident(why), resident(wgt),
            pl.BlockSpec((SHORT_K, n_hy), lambda i: (0, 0)),
            pl.BlockSpec((1, n_hy), lambda i: (0, 0)),
        ],
        out_specs=[pl.BlockSpec((tm, w), lambda i: (i, 0)) for w in widths],
        out_shape=[jax.ShapeDtypeStruct((n, w), BF16) for w in widths],
        compiler_params=_cparams(("parallel",)),
        name="inproj_conv",
    )(x2, x2, x2, g.reshape(1, d), wqkv, why, wgt, conv_w, conv_b.reshape(1, n_hy))


N_DR = 2 * WIN_R - 1
N_DC = 2 * WIN_C - 1
LOG2_E = 1.4426950408889634
Q_SCALE = (HEAD_DIM ** -0.5) * LOG2_E
KV_ROWS = ATT_ROWS + WIN_R


def _build_bias_tiles(rpb_ref, pt_ref):
    k = lax.broadcasted_iota(jnp.int32, (GRID_W, 128), 0)
    lane = lax.broadcasted_iota(jnp.int32, (GRID_W, 128), 1)
    q = lane & (GRID_W - 1)
    first_half = lane < GRID_W
    cs = jnp.clip(q - WIN_C // 2, 0, GRID_W - WIN_C)
    valid = (k >= cs) & (k < cs + WIN_C)
    dc = jnp.clip(k - q, -(WIN_C - 1), WIN_C - 1) + (WIN_C - 1)

    def body(idx, carry):
        p = idx // N_DR
        d = idx % N_DR
        base0 = ((2 * p) * N_DR + d) * N_DC
        base1 = ((2 * p + 1) * N_DR + d) * N_DC
        acc = jnp.zeros((GRID_W, 128), F32)
        for off in range(N_DC):
            val = jnp.where(first_half, rpb_ref[base0 + off], rpb_ref[base1 + off])
            acc = jnp.where(dc == off, val, acc)
        pt_ref[idx] = jnp.where(valid, acc * LOG2_E, NEG_BIG)
        return carry

    lax.fori_loop(0, (N_HEADS // 2) * N_DR, body, 0)


def _kv_window_start(i, rows):
    return jnp.clip(i * ATT_ROWS - WIN_R // 2, 0, rows - KV_ROWS)


def _attn_kernel(rpb_ref, q_ref, k_ref, v_ref, o_ref, pt_ref, *, rows):
    b = pl.program_id(0)
    i = pl.program_id(1)

    @pl.when((b == 0) & (i == 0))
    def _():
        _build_bias_tiles(rpb_ref, pt_ref)

    lane = lax.broadcasted_iota(jnp.int32, (GRID_W, 128), 1)
    first_half = lane < HEAD_DIM
    ones = jnp.ones((WIN_R * GRID_W, 128), BF16)
    nkeys = WIN_R * GRID_W
    wstart = _kv_window_start(i, rows)
    nt_dims = (((1,), (1,)), ((), ()))

    pairs = range(N_HEADS // 2)

    def scores(rr):
        r = i * ATT_ROWS + rr
        rs = jnp.clip(r - WIN_R // 2, 0, rows - WIN_R)
        s = rs - r + (WIN_R - 1)
        koff = pl.multiple_of((rs - wstart) * GRID_W, GRID_W)
        q = q_ref[pl.ds(pl.multiple_of(rr * GRID_W, GRID_W), GRID_W), :]
        sts = []
        for p in pairs:
            qp = q[:, 128 * p:128 * (p + 1)]
            zero = jnp.zeros_like(qp)
            wt = jnp.concatenate([jnp.where(first_half, qp, zero),
                                  jnp.where(first_half, zero, qp)], axis=0)
            kp = k_ref[pl.ds(koff, nkeys), 128 * p:128 * (p + 1)]
            sts.append(lax.dot_general(kp, wt, nt_dims, preferred_element_type=F32))
        return sts, s, koff

    def probabilities(sts, s):
        pms = []
        for p in pairs:
            st = sts[p]
            ch = [st[GRID_W * c:GRID_W * (c + 1)] + pt_ref[p * N_DR + s + c] for c in range(WIN_R)]
            m = ch[0]
            for c in range(1, WIN_R):
                m = jnp.maximum(m, ch[c])
            m = jnp.max(m, axis=0, keepdims=True)
            pt = jnp.concatenate([jnp.exp2(c - m) for c in ch], axis=0).astype(BF16)
            pms.append(pt.T)
        return pms

    def weighted_values(rr, pms, koff):
        outs = []
        for p in pairs:
            vp = v_ref[pl.ds(koff, nkeys), 128 * p:128 * (p + 1)]
            ov = jnp.dot(pms[p], jnp.concatenate([vp, ones], axis=1), preferred_element_type=F32)
            o = ov[:, 0:128] / ov[:, 128:256]
            outs.append(jnp.where(first_half, o[0:GRID_W], o[GRID_W:2 * GRID_W]))
        qoff = pl.multiple_of(rr * GRID_W, GRID_W)
        o_ref[pl.ds(qoff, GRID_W), :] = jnp.concatenate(outs, axis=1).astype(o_ref.dtype)

    def trip(t, carry):
        r0 = t * ATT_UNROLL
        nxt = scores(r0)
        for j in range(ATT_UNROLL):
            sts, s, koff = nxt
            if j + 1 < ATT_UNROLL:
                nxt = scores(r0 + j + 1)
            weighted_values(r0 + j, probabilities(sts, s), koff)
        return carry

    lax.fori_loop(0, ATT_ROWS // ATT_UNROLL, trip, 0)


def _attention(z3, rpb_flat):
    bsz, L, _ = z3.shape
    rows = L // GRID_W
    assert rows % ATT_ROWS == 0 and rows >= KV_ROWS
    nblk = rows // ATT_ROWS
    blk = ATT_ROWS * GRID_W

    def window(col):
        return pl.BlockSpec(
            (pl.Squeezed(), pl.Element(KV_ROWS * GRID_W), pl.Element(D_ATTN)),
            lambda b, i: (b, _kv_window_start(i, rows) * GRID_W, col * D_ATTN))

    return pl.pallas_call(
        functools.partial(_attn_kernel, rows=rows),
        grid=(bsz, nblk),
        in_specs=[
            pl.BlockSpec(memory_space=pltpu.SMEM),
            pl.BlockSpec((pl.Squeezed(), blk, D_ATTN), lambda b, i: (b, i, 0)),
            window(1),
            window(2),
        ],
        out_specs=pl.BlockSpec((pl.Squeezed(), blk, D_ATTN), lambda b, i: (b, i, 0)),
        out_shape=jax.ShapeDtypeStruct((bsz, L, D_ATTN), BF16),
        scratch_shapes=[pltpu.VMEM(((N_HEADS // 2) * N_DR, GRID_W, 128), F32)],
        compiler_params=_cparams(("arbitrary", "arbitrary")),
        name="nbr_attention",
    )(rpb_flat, z3, z3, z3)


TWO_OVER_PI = 0.6366197723675814
PIO2_1 = 1.5703125
PIO2_2 = 4.837512969970703125e-4
PIO2_3 = 7.54978995489188216e-8
TRIG_FAST_LIMIT = 4096.0


def _quadrant_value(x, shift):
    kf = jnp.floor(x * TWO_OVER_PI + 0.5)
    r = ((x - kf * PIO2_1) - kf * PIO2_2) - kf * PIO2_3
    z = r * r
    s = r + r * z * (-1.6666654611e-1 + z * (8.3321608736e-3 + z * -1.9515295891e-4))
    c = (1.0 - 0.5 * z) + z * z * (4.166664568298827e-2
                                   + z * (-1.388731625493765e-3 + z * 2.443315711809948e-5))
    k = kf + shift
    half = jnp.floor(k * 0.5)
    odd = k - 2.0 * half
    flip = half - 2.0 * jnp.floor(half * 0.5)
    return (s + odd * (c - s)) * (1.0 - 2.0 * flip)


def _sin_small(x):
    return _quadrant_value(x, 0.0)


def _cos_small(x):
    return _quadrant_value(x, 1.0)


def _sin(x):
    return lax.cond(jnp.max(jnp.abs(x)) < TRIG_FAST_LIMIT, _sin_small, jnp.sin, x)


def _cos(x):
    return lax.cond(jnp.max(jnp.abs(x)) < TRIG_FAST_LIMIT, _cos_small, jnp.cos, x)


def _hi_lo(x):
    hi = x.astype(BF16)
    return hi, (x - hi.astype(F32)).astype(BF16)


def _stack_cols(w):
    hi, lo = _hi_lo(w)
    return jnp.concatenate([hi, hi, lo], axis=1)


def _stack_rows(h):
    hi, lo = _hi_lo(h)
    return jnp.concatenate([hi, lo, hi], axis=0)


def _dot_stacked(w_stacked, h):
    return jnp.dot(w_stacked, _stack_rows(h), preferred_element_type=F32)


def _filter_kernel(fb_ref, w1t_ref, w1c_ref, w1s_ref, b1_ref, w2_ref, b2_ref, w3_ref, b3_ref,
                   w4_ref, fr_ref, dl_ref, o_ref, *, L, tp):
    j = pl.program_id(0)
    fh = FILTER_HIDDEN
    pos = (lax.broadcasted_iota(jnp.int32, (1, tp), 1) + j * tp).astype(F32)
    t = pos / (L - 1.0)
    omega = (2.0 * math.pi) * pos / float(L)
    ang = fb_ref[...] * omega
    fr = fr_ref[...]
    pre = (w1t_ref[...] * t
           + _dot_stacked(w1c_ref[...], _cos(ang)) - _dot_stacked(w1s_ref[...], _sin(ang)))
    h = _sin(fr * (pre + b1_ref[...]))
    h = _sin(fr * (_dot_stacked(w2_ref[...], h) + b2_ref[...]))
    h = _sin(fr * (_dot_stacked(w3_ref[...], h) + b3_ref[...]))
    h_hi = h.astype(BF16).astype(F32)
    tr = jnp.concatenate([h_hi, h - h_hi], axis=0).T.astype(BF16)
    lane = lax.broadcasted_iota(jnp.int32, tr.shape, 1)
    lhs = jnp.concatenate([tr, jnp.where(lane < fh, tr, jnp.zeros_like(tr))], axis=1)
    out = jnp.dot(lhs, w4_ref[...], preferred_element_type=F32)
    tcol = (lax.broadcasted_iota(jnp.int32, (tp, 1), 0) + j * tp).astype(F32) / (L - 1.0)
    decay = jnp.exp(-tcol * dl_ref[...])
    c = D_HYENA
    o_ref[:, 0:c] = (out[:, 0:c] * decay).astype(o_ref.dtype)
    o_ref[:, c:2 * c] = (out[:, c:2 * c] * decay).astype(o_ref.dtype)


def _implicit_filters(L, w1, b1, w2, b2, w3, b3, w4, freq):
    bands = (FILTER_EMB - 1) // 2
    fh = FILTER_HIDDEN
    tp = min(1024, L)
    fb = jnp.linspace(1e-4, bands - 1, bands, dtype=F32).reshape(bands, 1)
    max_decay = math.log(DECAY_TARGET) / FAST_DECAY_PCT
    min_decay = math.log(DECAY_TARGET) / SLOW_DECAY_PCT
    deltas = jnp.abs(jnp.linspace(min_decay, max_decay, D_HYENA, dtype=F32)).reshape(1, D_HYENA)
    w1 = w1.astype(F32)
    w4_hi, w4_lo = _hi_lo(w4.astype(F32))
    args = (
        fb,
        w1[0:1, :].T,
        _stack_cols(w1[1:1 + bands, :].T),
        _stack_cols(w1[1 + bands:, :].T),
        b1.astype(F32).reshape(fh, 1),
        _stack_cols(w2.astype(F32).T), b2.astype(F32).reshape(fh, 1),
        _stack_cols(w3.astype(F32).T), b3.astype(F32).reshape(fh, 1),
        jnp.concatenate([w4_hi, w4_hi, w4_lo, jnp.zeros_like(w4_hi)], axis=0),
        freq.astype(F32).reshape(fh, 1),
        deltas,
    )

    def full(a):
        return pl.BlockSpec(a.shape, lambda j: (0,) * a.ndim)

    return pl.pallas_call(
        functools.partial(_filter_kernel, L=L, tp=tp),
        grid=(L // tp,),
        in_specs=[full(a) for a in args],
        out_specs=pl.BlockSpec((tp, 2 * D_HYENA), lambda j: (j, 0)),
        out_shape=jax.ShapeDtypeStruct((L, 2 * D_HYENA), BF16),
        compiler_params=_cparams(("parallel",)),
        name="implicit_filter",
    )(*args)


FFT_LANES = 128
FFT_T1 = 16


@functools.lru_cache(maxsize=None)
def _fft_tables(L):
    n = 2 * L
    n2 = FFT_N2
    n1 = n // n2
    n1h = n1 // 2
    odd = 2 * np.arange(n1h) + 1
    th = 2.0 * np.pi * (np.outer(odd, np.arange(n1h)) % (2 * n1)) / (2 * n1)
    f1 = np.concatenate([np.cos(th), -np.sin(th)], axis=0)
    tw = 2.0 * np.pi * (np.outer(odd, np.arange(n2)) % (2 * n)) / (2 * n)
    twr = np.cos(tw)
    twi = -np.sin(tw)
    t2 = 2.0 * np.pi * (np.outer(np.arange(n2), np.arange(n2)) % n2) / n2
    fr, fi = np.cos(t2), -np.sin(t2)
    f2 = np.block([[fr, -fi], [fi, fr]])
    f2inv = np.block([[fr, fi], [-fi, fr]])
    f1inv = np.concatenate([np.cos(th.T), -np.sin(th.T)], axis=1) * (2.0 / n)
    return dict(
        n1=n1, n1h=n1h,
        f1=np.asarray(f1, np.float32), f1inv=np.asarray(f1inv, np.float32),
        f2=np.asarray(f2, np.float32), f2inv=np.asarray(f2inv, np.float32),
        twr=np.asarray(twr, np.float32), twi=np.asarray(twi, np.float32),
    )


FFT_A = 16
SLOT_DTYPE = BF16


def _outer_dft(f, src_ref, dst_ref):
    n_in, n1h = src_ref.shape[0], src_ref.shape[1]
    n_out = dst_ref.shape[0]

    def body(i, carry):
        a0 = pl.multiple_of(i * FFT_A, FFT_A)
        ts = [jnp.swapaxes(src_ref[p, :, pl.ds(a0, FFT_A), :].astype(F32), 0, 1)
              for p in range(n_in)]
        outs = [[] for _ in range(n_out)]
        for j in range(0, FFT_A, 2):
            rhs = jnp.concatenate([jnp.concatenate([t[j], t[j + 1]], axis=1) for t in ts],
                                  axis=0).astype(BF16)
            res = jnp.dot(f, rhs, preferred_element_type=F32)
            for q in range(n_out):
                blk = res[q * n1h:(q + 1) * n1h]
                outs[q] += [blk[:, 0:FFT_LANES], blk[:, FFT_LANES:2 * FFT_LANES]]
        for q in range(n_out):
            dst_ref[q, :, pl.ds(a0, FFT_A), :] = jnp.swapaxes(
                jnp.stack(outs[q], axis=0), 0, 1).astype(dst_ref.dtype)
        return carry

    lax.fori_loop(0, FFT_N2 // FFT_A, body, 0, unroll=2)


def _fft_stage1_kernel(f_ref, u_ref, o_ref):
    _outer_dft(f_ref[...], u_ref, o_ref)


def _fft_stage1(u3, f1):
    bsz, L, c = u3.shape
    n1h = L // FFT_N2
    return pl.pallas_call(
        _fft_stage1_kernel,
        grid=(bsz, c // FFT_LANES),
        in_specs=[
            pl.BlockSpec(f1.shape, lambda b, j: (0, 0)),
            pl.BlockSpec((1, n1h, FFT_N2, FFT_LANES), lambda b, j: (b, 0, 0, j)),
        ],
        out_specs=pl.BlockSpec((pl.Squeezed(), 2, n1h, FFT_N2, FFT_LANES),
                               lambda b, j: (b, 0, 0, 0, j)),
        out_shape=jax.ShapeDtypeStruct((bsz, 2, n1h, FFT_N2, c), SLOT_DTYPE),
        compiler_params=_cparams(("parallel", "parallel")),
        name="fft_stage1",
    )(f1, u3.reshape(bsz, n1h, FFT_N2, c))


def _twiddle(ar, ai, tr, ti):
    return ar * tr - ai * ti, ar * ti + ai * tr


def _tw_columns(tw_ref):
    t = tw_ref[...].T
    return [t[:, j:j + 1] for j in range(FFT_T1)]


def _stage2_inputs(a_ref, trs, tis):
    xs = []
    for j in range(FFT_T1):
        xr, xi = _twiddle(a_ref[0, j].astype(F32), a_ref[1, j].astype(F32), trs[j], tis[j])
        xs.append(jnp.concatenate([xr, xi], axis=0).astype(BF16))
    return xs


def _filter_spectrum_kernel(a_ref, twr_ref, twi_ref, f2_ref, o_ref):
    n2, c = FFT_N2, D_HYENA
    f2 = f2_ref[...]
    bigs = [jnp.dot(f2, x, preferred_element_type=F32)
            for x in _stage2_inputs(a_ref, _tw_columns(twr_ref), _tw_columns(twi_ref))]
    for j, big in enumerate(bigs):
        o_ref[j, 0] = (big[0:n2, 0:c] + big[0:n2, c:2 * c]).astype(o_ref.dtype)
        o_ref[j, 1] = (big[n2:2 * n2, 0:c] - big[n2:2 * n2, c:2 * c]).astype(o_ref.dtype)


def _filter_spectrum(hfilt, tb):
    L, c2 = hfilt.shape
    n1h = tb["n1h"]
    a5 = _fft_stage1(hfilt.reshape(1, L, c2), tb["f1"])
    return pl.pallas_call(
        _filter_spectrum_kernel,
        grid=(n1h // FFT_T1,),
        in_specs=[
            pl.BlockSpec((pl.Squeezed(), 2, FFT_T1, FFT_N2, c2), lambda k: (0, 0, k, 0, 0)),
            pl.BlockSpec((FFT_T1, FFT_N2), lambda k: (k, 0)),
            pl.BlockSpec((FFT_T1, FFT_N2), lambda k: (k, 0)),
            pl.BlockSpec((2 * FFT_N2, 2 * FFT_N2), lambda k: (0, 0)),
        ],
        out_specs=pl.BlockSpec((FFT_T1, 2, FFT_N2, D_HYENA), lambda k: (k, 0, 0, 0)),
        out_shape=jax.ShapeDtypeStruct((n1h, 2, FFT_N2, D_HYENA), SLOT_DTYPE),
        compiler_params=_cparams(("parallel",)),
        name="filter_spectrum",
    )(a5, tb["twr"], tb["twi"], tb["f2"])


def _fft_mid_kernel(a_ref, kf_ref, twr_ref, twi_ref, f2_ref, f2i_ref, o_ref):
    n2 = FFT_N2
    f2, f2i = f2_ref[...], f2i_ref[...]
    trs, tis = _tw_columns(twr_ref), _tw_columns(twi_ref)

    def forward(j):
        xr, xi = _twiddle(a_ref[0, j].astype(F32), a_ref[1, j].astype(F32), trs[j], tis[j])
        x = jnp.concatenate([xr, xi], axis=0).astype(BF16)
        return jnp.dot(f2, x, preferred_element_type=F32)

    nxt = forward(0)
    for j in range(FFT_T1):
        big = nxt
        if j + 1 < FFT_T1:
            nxt = forward(j + 1)
        sr, si = big[0:n2], big[n2:2 * n2]
        kr, ki = kf_ref[j, 0].astype(F32), kf_ref[j, 1].astype(F32)
        y = jnp.concatenate([sr * kr - si * ki, sr * ki + si * kr], axis=0).astype(BF16)
        back = jnp.dot(f2i, y, preferred_element_type=F32)
        br, bi = back[0:n2], back[n2:2 * n2]
        tr, ti = trs[j], tis[j]
        o_ref[0, j] = (br * tr + bi * ti).astype(o_ref.dtype)
        o_ref[1, j] = (bi * tr - br * ti).astype(o_ref.dtype)


def _fft_mid(a5, kf, tb):
    bsz, _, n1h, n2, c = a5.shape
    slot = pl.BlockSpec((pl.Squeezed(), 2, FFT_T1, n2, c), lambda k, b: (b, 0, k, 0, 0))
    return pl.pallas_call(
        _fft_mid_kernel,
        grid=(n1h // FFT_T1, bsz),
        in_specs=[
            slot,
            pl.BlockSpec((FFT_T1, 2, n2, c), lambda k, b: (k, 0, 0, 0)),
            pl.BlockSpec((FFT_T1, n2), lambda k, b: (k, 0)),
            pl.BlockSpec((FFT_T1, n2), lambda k, b: (k, 0)),
            pl.BlockSpec((2 * n2, 2 * n2), lambda k, b: (0, 0)),
            pl.BlockSpec((2 * n2, 2 * n2), lambda k, b: (0, 0)),
        ],
        out_specs=slot,
        out_shape=jax.ShapeDtypeStruct(a5.shape, SLOT_DTYPE),
        compiler_params=_cparams(("parallel", "parallel")),
        name="fft_mid",
    )(a5, kf, tb["twr"], tb["twi"], tb["f2"], tb["f2inv"])


def _fft_out_kernel(f_ref, b_ref, o_ref):
    _outer_dft(f_ref[...], b_ref, o_ref)


def _fft_out(b5, f1inv):
    bsz, _, n1h, n2, c = b5.shape
    return pl.pallas_call(
        _fft_out_kernel,
        grid=(bsz, c // FFT_LANES),
        in_specs=[
            pl.BlockSpec(f1inv.shape, lambda b, j: (0, 0)),
            pl.BlockSpec((pl.Squeezed(), 2, n1h, n2, FFT_LANES), lambda b, j: (b, 0, 0, 0, j)),
        ],
        out_specs=pl.BlockSpec((1, n1h, n2, FFT_LANES), lambda b, j: (b, 0, 0, j)),
        out_shape=jax.ShapeDtypeStruct((bsz, n1h, n2, c), BF16),
        compiler_params=_cparams(("parallel", "parallel")),
        name="fft_out",
    )(f1inv, b5).reshape(bsz, n1h * n2, c)


def _hyena_conv(u, kf, tb):
    a5 = _fft_stage1(u, tb["f1"])
    b5 = _fft_mid(a5, kf, tb)
    return _fft_out(b5, tb["f1inv"])


def _rms(x, g):
    inv = lax.rsqrt(jnp.mean(x * x, axis=-1, keepdims=True) + EPS)
    return (x * inv) * g


FFN_CHUNK = 256


def _merge_ffn_kernel(ya_ref, yc_ref, u_ref, x0_ref, dk_ref, ga_ref, gh_ref, x_ref, wa_ref, wh_ref,
                      wo_ref, g_ref, wg_ref, wu_ref, wd_ref, gf_ref, o_ref):
    yh = x0_ref[...].astype(F32) * (yc_ref[...].astype(F32)
                                    + u_ref[...].astype(F32) * dk_ref[...])
    pa = jnp.dot(ya_ref[...], wa_ref[...], preferred_element_type=F32)
    ph = jnp.dot(yh.astype(BF16), wh_ref[...], preferred_element_type=F32)
    merged = (jax.nn.sigmoid(ga_ref[...].astype(F32)) * pa
              + jax.nn.sigmoid(gh_ref[...].astype(F32)) * ph)
    x1 = x_ref[...] + jnp.dot(merged.astype(BF16), wo_ref[...], preferred_element_type=F32)
    h = _rms(x1, g_ref[...]).astype(BF16)
    acc = x1
    for c0 in range(0, wg_ref.shape[1], FFN_CHUNK):
        gate = jnp.dot(h, wg_ref[:, c0:c0 + FFN_CHUNK], preferred_element_type=F32)
        up = jnp.dot(h, wu_ref[:, c0:c0 + FFN_CHUNK], preferred_element_type=F32)
        act = (gate * jax.nn.sigmoid(gate) * up).astype(BF16)
        acc = acc + jnp.dot(act, wd_ref[c0:c0 + FFN_CHUNK, :], preferred_element_type=F32)
    o_ref[...] = _rms(acc, gf_ref[...])


def _merge_ffn(ya, yc, u, x0, d_skip, gates, x2, wa, wh, wo, g, wg, wu, wd, gf):
    n, d = x2.shape
    dff = wg.shape[1]
    assert dff % FFN_CHUNK == 0
    tm = min(512, n)
    tok = pl.BlockSpec((tm, D_HYENA), lambda i: (i, 0))

    def resident(a):
        return pl.BlockSpec(a.shape, lambda i: (0, 0), pipeline_mode=pl.Buffered(1))

    def row(width):
        return pl.BlockSpec((1, width), lambda i: (0, 0))

    return pl.pallas_call(
        _merge_ffn_kernel,
        grid=(n // tm,),
        in_specs=[
            pl.BlockSpec((tm, D_ATTN), lambda i: (i, 0)),
            tok, tok, tok,
            row(D_HYENA),
            pl.BlockSpec((tm, d), lambda i: (i, 0)),
            pl.BlockSpec((tm, d), lambda i: (i, 1)),
            pl.BlockSpec((tm, d), lambda i: (i, 0)),
            resident(wa), resident(wh), resident(wo),
            row(d),
            resident(wg), resident(wu), resident(wd),
            row(d),
        ],
        out_specs=pl.BlockSpec((tm, d), lambda i: (i, 0)),
        out_shape=jax.ShapeDtypeStruct((n, d), F32),
        compiler_params=_cparams(("parallel",)),
        name="merge_ffn",
    )(ya, yc, u, x0, d_skip.astype(F32).reshape(1, D_HYENA), gates, gates, x2, wa, wh, wo,
      g.reshape(1, d), wg, wu, wd, gf.reshape(1, d))


def _trunk(x, p, norm_final):
    bsz, L, d = x.shape
    n = bsz * L
    x2 = x.reshape(n, d)
    qkv, u, x0, gates = _inproj_conv(x2, L, p["norm_mix"], p["w_in"], p["conv_w"], p["conv_b"])
    ya = _attention(qkv.reshape(bsz, L, qkv.shape[1]), p["rpb"])
    u = u.reshape(bsz, L, D_HYENA)
    tb = dict(_fft_tables(L))
    for name in ("f1", "f1inv", "f2", "f2inv"):
        tb[name] = jnp.asarray(tb[name]).astype(BF16)
    hfilt = _implicit_filters(L, *p["filt"])
    kf = _filter_spectrum(hfilt, tb)
    yc = _hyena_conv(u, kf, tb)
    out = _merge_ffn(ya.reshape(n, D_ATTN), yc.reshape(n, D_HYENA), u.reshape(n, D_HYENA),
                     x0, p["hyena_d"], gates, x2,
                     p["w_br_attn"], p["w_br_hyena"], p["w_out"],
                     p["norm_ffn"], p["w_gate"], p["w_up"], p["w_down"], norm_final)
    return out.reshape(bsz, L, d)


def kernel(x_prompt, x_sample, norm_mix, w_in, rpb, conv_w, conv_b, filt_w1, filt_b1, filt_w2,
           filt_b2, filt_w3, filt_b3, filt_w4, filt_freq, hyena_d, w_br_attn, w_br_hyena, w_out,
           norm_ffn, w_gate, w_up, w_down, norm_final):
    assert w_in.shape[0] == 1, "the layer definition has depth 1"
    col = lax.broadcasted_iota(jnp.int32, (1, w_in.shape[2]), 1)
    col_scale = jnp.where(col < D_ATTN, Q_SCALE, 1.0).astype(F32)
    p = dict(
        norm_mix=norm_mix[0], w_in=(w_in[0] * col_scale).astype(BF16), rpb=rpb[0].reshape(-1),
        conv_w=conv_w[0], conv_b=conv_b[0],
        filt=(filt_w1[0], filt_b1[0], filt_w2[0], filt_b2[0], filt_w3[0], filt_b3[0],
              filt_w4[0], filt_freq[0]),
        hyena_d=hyena_d[0],
        w_br_attn=w_br_attn[0].astype(BF16), w_br_hyena=w_br_hyena[0].astype(BF16),
        w_out=w_out[0].astype(BF16), norm_ffn=norm_ffn[0],
        w_gate=w_gate[0].astype(BF16), w_up=w_up[0].astype(BF16),
        w_down=w_down[0].astype(BF16),
    )
    return (_trunk(x_prompt, p, norm_final), _trunk(x_sample, p, norm_final))
```

```python
import functools
import math

import numpy as np
import jax
import jax.numpy as jnp
from jax import lax
from jax.experimental import pallas as pl
from jax.experimental.pallas import tpu as pltpu

F32 = jnp.float32
BF16 = jnp.bfloat16

GRID_W = 64
N_HEADS = 8
HEAD_DIM = 64
D_ATTN = N_HEADS * HEAD_DIM
WIN_R = 8
WIN_C = 16
D_HYENA = 512
SHORT_K = 3
FILTER_EMB = 33
FILTER_HIDDEN = 64
DECAY_TARGET = 1e-2
FAST_DECAY_PCT = 0.3
SLOW_DECAY_PCT = 1.5
EPS = 1e-6

FFT_N2 = 128
ATT_ROWS = 16
ATT_UNROLL = 8
NEG_BIG = -1e30
VMEM_LIMIT = 56 * 1024 * 1024


def _cparams(sem):
    return pltpu.CompilerParams(dimension_semantics=sem, vmem_limit_bytes=VMEM_LIMIT)


IN_CHUNK = 512
HALO = 8


def _inproj_conv_kernel(x_ref, xp_ref, xn_ref, g_ref, wqkv_ref, why_ref, wgt_ref, cw_ref, cb_ref,
                        qkv_ref, u_ref, x0_ref, gt_ref, *, tiles_per_seq):
    i = pl.program_id(0)
    g = g_ref[...]
    h = _rms(x_ref[...], g).astype(BF16)
    hh = _rms(jnp.concatenate([xp_ref[...], xn_ref[...]], axis=0), g).astype(BF16)
    for w_ref, o_ref in ((wqkv_ref, qkv_ref), (wgt_ref, gt_ref)):
        for c0 in range(0, w_ref.shape[1], IN_CHUNK):
            o_ref[:, c0:c0 + IN_CHUNK] = jnp.dot(
                h, w_ref[:, c0:c0 + IN_CHUNK], preferred_element_type=F32).astype(o_ref.dtype)
    tm = h.shape[0]
    c = D_HYENA
    row = lax.broadcasted_iota(jnp.int32, (tm, c), 0)
    first = (i % tiles_per_seq) == 0
    last = (i % tiles_per_seq) == tiles_per_seq - 1
    ys = []
    for k in range(3):
        w = why_ref[:, k * c:(k + 1) * c]
        hy = jnp.dot(h, w, preferred_element_type=F32)
        hyh = jnp.dot(hh, w, preferred_element_type=F32)
        prev_row = jnp.where(first, 0.0, hyh[HALO - 1:HALO])
        next_row = jnp.where(last, 0.0, hyh[HALO:HALO + 1])
        xm1 = jnp.where(row == 0, prev_row, pltpu.roll(hy, 1, 0))
        xp1 = jnp.where(row == tm - 1, next_row, pltpu.roll(hy, tm - 1, 0))
        cw = cw_ref[:, k * c:(k + 1) * c]
        ys.append(xm1 * cw[0:1] + hy * cw[1:2] + xp1 * cw[2:3] + cb_ref[:, k * c:(k + 1) * c])
    x0_ref[...] = ys[0].astype(x0_ref.dtype)
    u_ref[...] = (ys[1] * ys[2]).astype(u_ref.dtype)


def _inproj_conv(x2, L, g, w_bf16, conv_w, conv_b):
    n, d = x2.shape
    n_qkv, n_hy = 3 * D_ATTN, 3 * D_HYENA
    tm = min(1024, L)
    nh = n // HALO
    hb = tm // HALO

    def resident(a):
        return pl.BlockSpec(a.shape, lambda i: (0, 0), pipeline_mode=pl.Buffered(1))

    wqkv, why, wgt = w_bf16[:, :n_qkv], w_bf16[:, n_qkv:n_qkv + n_hy], w_bf16[:, n_qkv + n_hy:]
    widths = (n_qkv, D_HYENA, D_HYENA, wgt.shape[1])
    return pl.pallas_call(
        functools.partial(_inproj_conv_kernel, tiles_per_seq=L // tm),
        grid=(n // tm,),
        in_specs=[
            pl.BlockSpec((tm, d), lambda i: (i, 0)),
            pl.BlockSpec((HALO, d), lambda i: (jnp.maximum(i * hb - 1, 0), 0)),
            pl.BlockSpec((HALO, d), lambda i: (jnp.minimum((i + 1) * hb, nh - 1), 0)),
            pl.BlockSpec((1, d), lambda i: (0, 0)),
            resident(wqkv), resident(why), resident(wgt),
            pl.BlockSpec((SHORT_K, n_hy), lambda i: (0, 0)),
            pl.BlockSpec((1, n_hy), lambda i: (0, 0)),
        ],
        out_specs=[pl.BlockSpec((tm, w), lambda i: (i, 0)) for w in widths],
        out_shape=[jax.ShapeDtypeStruct((n, w), BF16) for w in widths],
        compiler_params=_cparams(("parallel",)),
        name="inproj_conv",
    )(x2, x2, x2, g.reshape(1, d), wqkv, why, wgt, conv_w, conv_b.reshape(1, n_hy))


N_DR = 2 * WIN_R - 1
N_DC = 2 * WIN_C - 1
LOG2_E = 1.4426950408889634
Q_SCALE = (HEAD_DIM ** -0.5) * LOG2_E
KV_ROWS = ATT_ROWS + WIN_R


def _build_bias_tiles(rpb_ref, pt_ref):
    k = lax.broadcasted_iota(jnp.int32, (GRID_W, 128), 0)
    lane = lax.broadcasted_iota(jnp.int32, (GRID_W, 128), 1)
    q = lane & (GRID_W - 1)
    first_half = lane < GRID_W
    cs = jnp.clip(q - WIN_C // 2, 0, GRID_W - WIN_C)
    valid = (k >= cs) & (k < cs + WIN_C)
    dc = jnp.clip(k - q, -(WIN_C - 1), WIN_C - 1) + (WIN_C - 1)

    def body(idx, carry):
        p = idx // N_DR
        d = idx % N_DR
        base0 = ((2 * p) * N_DR + d) * N_DC
        base1 = ((2 * p + 1) * N_DR + d) * N_DC
        acc = jnp.zeros((GRID_W, 128), F32)
        for off in range(N_DC):
            val = jnp.where(first_half, rpb_ref[base0 + off], rpb_ref[base1 + off])
            acc = jnp.where(dc == off, val, acc)
        pt_ref[idx] = jnp.where(valid, acc * LOG2_E, NEG_BIG)
        return carry

    lax.fori_loop(0, (N_HEADS // 2) * N_DR, body, 0)


def _kv_window_start(i, rows):
    return jnp.clip(i * ATT_ROWS - WIN_R // 2, 0, rows - KV_ROWS)


def _attn_kernel(rpb_ref, q_ref, k_ref, v_ref, o_ref, pt_ref, *, rows):
    b = pl.program_id(0)
    i = pl.program_id(1)

    @pl.when((b == 0) & (i == 0))
    def _():
        _build_bias_tiles(rpb_ref, pt_ref)

    lane = lax.broadcasted_iota(jnp.int32, (GRID_W, 128), 1)
    first_half = lane < HEAD_DIM
    ones = jnp.ones((WIN_R * GRID_W, 128), BF16)
    nkeys = WIN_R * GRID_W
    wstart = _kv_window_start(i, rows)
    nt_dims = (((1,), (1,)), ((), ()))

    pairs = range(N_HEADS // 2)

    def scores(rr):
        r = i * ATT_ROWS + rr
        rs = jnp.clip(r - WIN_R // 2, 0, rows - WIN_R)
        s = rs - r + (WIN_R - 1)
        koff = pl.multiple_of((rs - wstart) * GRID_W, GRID_W)
        q = q_ref[pl.ds(pl.multiple_of(rr * GRID_W, GRID_W), GRID_W), :]
        sts = []
        for p in pairs:
            qp = q[:, 128 * p:128 * (p + 1)]
            zero = jnp.zeros_like(qp)
            wt = jnp.concatenate([jnp.where(first_half, qp, zero),
                                  jnp.where(first_half, zero, qp)], axis=0)
            kp = k_ref[pl.ds(koff, nkeys), 128 * p:128 * (p + 1)]
            sts.append(lax.dot_general(kp, wt, nt_dims, preferred_element_type=F32))
        return sts, s, koff

    def probabilities(sts, s):
        pms = []
        for p in pairs:
            st = sts[p]
            ch = [st[GRID_W * c:GRID_W * (c + 1)] + pt_ref[p * N_DR + s + c] for c in range(WIN_R)]
            m = ch[0]
            for c in range(1, WIN_R):
                m = jnp.maximum(m, ch[c])
            m = jnp.max(m, axis=0, keepdims=True)
            pt = jnp.concatenate([jnp.exp2(c - m) for c in ch], axis=0).astype(BF16)
            pms.append(pt.T)
        return pms

    def weighted_values(rr, pms, koff):
        outs = []
        for p in pairs:
            vp = v_ref[pl.ds(koff, nkeys), 128 * p:128 * (p + 1)]
            ov = jnp.dot(pms[p], jnp.concatenate([vp, ones], axis=1), preferred_element_type=F32)
            o = ov[:, 0:128] / ov[:, 128:256]
            outs.append(jnp.where(first_half, o[0:GRID_W], o[GRID_W:2 * GRID_W]))
        qoff = pl.multiple_of(rr * GRID_W, GRID_W)
        o_ref[pl.ds(qoff, GRID_W), :] = jnp.concatenate(outs, axis=1).astype(o_ref.dtype)

    def trip(t, carry):
        r0 = t * ATT_UNROLL
        nxt = scores(r0)
        for j in range(ATT_UNROLL):
            sts, s, koff = nxt
            if j + 1 < ATT_UNROLL:
                nxt = scores(r0 + j + 1)
            weighted_values(r0 + j, probabilities(sts, s), koff)
        return carry

    lax.fori_loop(0, ATT_ROWS // ATT_UNROLL, trip, 0)


def _attention(z3, rpb_flat):
    bsz, L, _ = z3.shape
    rows = L // GRID_W
    assert rows % ATT_ROWS == 0 and rows >= KV_ROWS
    nblk = rows // ATT_ROWS
    blk = ATT_ROWS * GRID_W

    def window(col):
        return pl.BlockSpec(
            (pl.Squeezed(), pl.Element(KV_ROWS * GRID_W), pl.Element(D_ATTN)),
            lambda b, i: (b, _kv_window_start(i, rows) * GRID_W, col * D_ATTN))

    return pl.pallas_call(
        functools.partial(_attn_kernel, rows=rows),
        grid=(bsz, nblk),
        in_specs=[
            pl.BlockSpec(memory_space=pltpu.SMEM),
            pl.BlockSpec((pl.Squeezed(), blk, D_ATTN), lambda b, i: (b, i, 0)),
            window(1),
            window(2),
        ],
        out_specs=pl.BlockSpec((pl.Squeezed(), blk, D_ATTN), lambda b, i: (b, i, 0)),
        out_shape=jax.ShapeDtypeStruct((bsz, L, D_ATTN), BF16),
        scratch_shapes=[pltpu.VMEM(((N_HEADS // 2) * N_DR, GRID_W, 128), F32)],
        compiler_params=_cparams(("arbitrary", "arbitrary")),
        name="nbr_attention",
    )(rpb_flat, z3, z3, z3)


TWO_OVER_PI = 0.6366197723675814
PIO2_1 = 1.5703125
PIO2_2 = 4.837512969970703125e-4
PIO2_3 = 7.54978995489188216e-8
TRIG_FAST_LIMIT = 4096.0


def _quadrant_value(x, shift):
    kf = jnp.floor(x * TWO_OVER_PI + 0.5)
    r = ((x - kf * PIO2_1) - kf * PIO2_2) - kf * PIO2_3
    z = r * r
    s = r + r * z * (-1.6666654611e-1 + z * (8.3321608736e-3 + z * -1.9515295891e-4))
    c = (1.0 - 0.5 * z) + z * z * (4.166664568298827e-2
                                   + z * (-1.388731625493765e-3 + z * 2.443315711809948e-5))
    k = kf + shift
    half = jnp.floor(k * 0.5)
    odd = k - 2.0 * half
    flip = half - 2.0 * jnp.floor(half * 0.5)
    return (s + odd * (c - s)) * (1.0 - 2.0 * flip)


def _sin_small(x):
    return _quadrant_value(x, 0.0)


def _cos_small(x):
    return _quadrant_value(x, 1.0)


def _sin(x):
    return lax.cond(jnp.max(jnp.abs(x)) < TRIG_FAST_LIMIT, _sin_small, jnp.sin, x)


def _cos(x):
    return lax.cond(jnp.max(jnp.abs(x)) < TRIG_FAST_LIMIT, _cos_small, jnp.cos, x)


def _hi_lo(x):
    hi = x.astype(BF16)
    return hi, (x - hi.astype(F32)).astype(BF16)


def _stack_cols(w):
    hi, lo = _hi_lo(w)
    return jnp.concatenate([hi, hi, lo], axis=1)


def _stack_rows(h):
    hi, lo = _hi_lo(h)
    return jnp.concatenate([hi, lo, hi], axis=0)


def _dot_stacked(w_stacked, h):
    return jnp.dot(w_stacked, _stack_rows(h), preferred_element_type=F32)


def _filter_kernel(fb_ref, w1t_ref, w1c_ref, w1s_ref, b1_ref, w2_ref, b2_ref, w3_ref, b3_ref,
                   w4_ref, fr_ref, dl_ref, o_ref, *, L, tp):
    j = pl.program_id(0)
    fh = FILTER_HIDDEN
    pos = (lax.broadcasted_iota(jnp.int32, (1, tp), 1) + j * tp).astype(F32)
    t = pos / (L - 1.0)
    omega = (2.0 * math.pi) * pos / float(L)
    ang = fb_ref[...] * omega
    fr = fr_ref[...]
    pre = (w1t_ref[...] * t
           + _dot_stacked(w1c_ref[...], _cos(ang)) - _dot_stacked(w1s_ref[...], _sin(ang)))
    h = _sin(fr * (pre + b1_ref[...]))
    h = _sin(fr * (_dot_stacked(w2_ref[...], h) + b2_ref[...]))
    h = _sin(fr * (_dot_stacked(w3_ref[...], h) + b3_ref[...]))
    h_hi = h.astype(BF16).astype(F32)
    tr = jnp.concatenate([h_hi, h - h_hi], axis=0).T.astype(BF16)
    lane = lax.broadcasted_iota(jnp.int32, tr.shape, 1)
    lhs = jnp.concatenate([tr, jnp.where(lane < fh, tr, jnp.zeros_like(tr))], axis=1)
    out = jnp.dot(lhs, w4_ref[...], preferred_element_type=F32)
    tcol = (lax.broadcasted_iota(jnp.int32, (tp, 1), 0) + j * tp).astype(F32) / (L - 1.0)
    decay = jnp.exp(-tcol * dl_ref[...])
    c = D_HYENA
    o_ref[:, 0:c] = (out[:, 0:c] * decay).astype(o_ref.dtype)
    o_ref[:, c:2 * c] = (out[:, c:2 * c] * decay).astype(o_ref.dtype)


def _implicit_filters(L, w1, b1, w2, b2, w3, b3, w4, freq):
    bands = (FILTER_EMB - 1) // 2
    fh = FILTER_HIDDEN
    tp = min(1024, L)
    fb = jnp.linspace(1e-4, bands - 1, bands, dtype=F32).reshape(bands, 1)
    max_decay = math.log(DECAY_TARGET) / FAST_DECAY_PCT
    min_decay = math.log(DECAY_TARGET) / SLOW_DECAY_PCT
    deltas = jnp.abs(jnp.linspace(min_decay, max_decay, D_HYENA, dtype=F32)).reshape(1, D_HYENA)
    w1 = w1.astype(F32)
    w4_hi, w4_lo = _hi_lo(w4.astype(F32))
    args = (
        fb,
        w1[0:1, :].T,
        _stack_cols(w1[1:1 + bands, :].T),
        _stack_cols(w1[1 + bands:, :].T),
        b1.astype(F32).reshape(fh, 1),
        _stack_cols(w2.astype(F32).T), b2.astype(F32).reshape(fh, 1),
        _stack_cols(w3.astype(F32).T), b3.astype(F32).reshape(fh, 1),
        jnp.concatenate([w4_hi, w4_hi, w4_lo, jnp.zeros_like(w4_hi)], axis=0),
        freq.astype(F32).reshape(fh, 1),
        deltas,
    )

    def full(a):
        return pl.BlockSpec(a.shape, lambda j: (0,) * a.ndim)

    return pl.pallas_call(
        functools.partial(_filter_kernel, L=L, tp=tp),
        grid=(L // tp,),
        in_specs=[full(a) for a in args],
        out_specs=pl.BlockSpec((tp, 2 * D_HYENA), lambda j: (j, 0)),
        out_shape=jax.ShapeDtypeStruct((L, 2 * D_HYENA), BF16),
        compiler_params=_cparams(("parallel",)),
        name="implicit_filter",
    )(*args)


FFT_LANES = 128
FFT_T1 = 16


@functools.lru_cache(maxsize=None)
def _fft_tables(L):
    n = 2 * L
    n2 = FFT_N2
    n1 = n // n2
    n1h = n1 // 2
    odd = 2 * np.arange(n1h) + 1
    th = 2.0 * np.pi * (np.outer(odd, np.arange(n1h)) % (2 * n1)) / (2 * n1)
    f1 = np.concatenate([np.cos(th), -np.sin(th)], axis=0)
    tw = 2.0 * np.pi * (np.outer(odd, np.arange(n2)) % (2 * n)) / (2 * n)
    twr = np.cos(tw)
    twi = -np.sin(tw)
    t2 = 2.0 * np.pi * (np.outer(np.arange(n2), np.arange(n2)) % n2) / n2
    fr, fi = np.cos(t2), -np.sin(t2)
    f2 = np.block([[fr, -fi], [fi, fr]])
    f2inv = np.block([[fr, fi], [-fi, fr]])
    f1inv = np.concatenate([np.cos(th.T), -np.sin(th.T)], axis=1) * (2.0 / n)
    return dict(
        n1=n1, n1h=n1h,
        f1=np.asarray(f1, np.float32), f1inv=np.asarray(f1inv, np.float32),
        f2=np.asarray(f2, np.float32), f2inv=np.asarray(f2inv, np.float32),
        twr=np.asarray(twr, np.float32), twi=np.asarray(twi, np.float32),
    )


FFT_A = 16
SLOT_DTYPE = BF16


def _outer_dft(f, src_ref, dst_ref):
    n_in, n1h = src_ref.shape[0], src_ref.shape[1]
    n_out = dst_ref.shape[0]

    def body(i, carry):
        a0 = pl.multiple_of(i * FFT_A, FFT_A)
        ts = [jnp.swapaxes(src_ref[p, :, pl.ds(a0, FFT_A), :].astype(F32), 0, 1)
              for p in range(n_in)]
        outs = [[] for _ in range(n_out)]
        for j in range(0, FFT_A, 2):
            rhs = jnp.concatenate([jnp.concatenate([t[j], t[j + 1]], axis=1) for t in ts],
                                  axis=0).astype(BF16)
            res = jnp.dot(f, rhs, preferred_element_type=F32)
            for q in range(n_out):
                blk = res[q * n1h:(q + 1) * n1h]
                outs[q] += [blk[:, 0:FFT_LANES], blk[:, FFT_LANES:2 * FFT_LANES]]
        for q in range(n_out):
            dst_ref[q, :, pl.ds(a0, FFT_A), :] = jnp.swapaxes(
                jnp.stack(outs[q], axis=0), 0, 1).astype(dst_ref.dtype)
        return carry

    lax.fori_loop(0, FFT_N2 // FFT_A, body, 0, unroll=2)


def _fft_stage1_kernel(f_ref, u_ref, o_ref):
    _outer_dft(f_ref[...], u_ref, o_ref)


def _fft_stage1(u3, f1):
    bsz, L, c = u3.shape
    n1h = L // FFT_N2
    return pl.pallas_call(
        _fft_stage1_kernel,
        grid=(bsz, c // FFT_LANES),
        in_specs=[
            pl.BlockSpec(f1.shape, lambda b, j: (0, 0)),
            pl.BlockSpec((1, n1h, FFT_N2, FFT_LANES), lambda b, j: (b, 0, 0, j)),
        ],
        out_specs=pl.BlockSpec((pl.Squeezed(), 2, n1h, FFT_N2, FFT_LANES),
                               lambda b, j: (b, 0, 0, 0, j)),
        out_shape=jax.ShapeDtypeStruct((bsz, 2, n1h, FFT_N2, c), SLOT_DTYPE),
        compiler_params=_cparams(("parallel", "parallel")),
        name="fft_stage1",
    )(f1, u3.reshape(bsz, n1h, FFT_N2, c))


def _twiddle(ar, ai, tr, ti):
    return ar * tr - ai * ti, ar * ti + ai * tr


def _tw_columns(tw_ref):
    t = tw_ref[...].T
    return [t[:, j:j + 1] for j in range(FFT_T1)]


def _stage2_inputs(a_ref, trs, tis):
    xs = []
    for j in range(FFT_T1):
        xr, xi = _twiddle(a_ref[0, j].astype(F32), a_ref[1, j].astype(F32), trs[j], tis[j])
        xs.append(jnp.concatenate([xr, xi], axis=0).astype(BF16))
    return xs


def _filter_spectrum_kernel(a_ref, twr_ref, twi_ref, f2_ref, o_ref):
    n2, c = FFT_N2, D_HYENA
    f2 = f2_ref[...]
    bigs = [jnp.dot(f2, x, preferred_element_type=F32)
            for x in _stage2_inputs(a_ref, _tw_columns(twr_ref), _tw_columns(twi_ref))]
    for j, big in enumerate(bigs):
        o_ref[j, 0] = (big[0:n2, 0:c] + big[0:n2, c:2 * c]).astype(o_ref.dtype)
        o_ref[j, 1] = (big[n2:2 * n2, 0:c] - big[n2:2 * n2, c:2 * c]).astype(o_ref.dtype)


def _filter_spectrum(hfilt, tb):
    L, c2 = hfilt.shape
    n1h = tb["n1h"]
    a5 = _fft_stage1(hfilt.reshape(1, L, c2), tb["f1"])
    return pl.pallas_call(
        _filter_spectrum_kernel,
        grid=(n1h // FFT_T1,),
        in_specs=[
            pl.BlockSpec((pl.Squeezed(), 2, FFT_T1, FFT_N2, c2), lambda k: (0, 0, k, 0, 0)),
            pl.BlockSpec((FFT_T1, FFT_N2), lambda k: (k, 0)),
            pl.BlockSpec((FFT_T1, FFT_N2), lambda k: (k, 0)),
            pl.BlockSpec((2 * FFT_N2, 2 * FFT_N2), lambda k: (0, 0)),
        ],
        out_specs=pl.BlockSpec((FFT_T1, 2, FFT_N2, D_HYENA), lambda k: (k, 0, 0, 0)),
        out_shape=jax.ShapeDtypeStruct((n1h, 2, FFT_N2, D_HYENA), SLOT_DTYPE),
        compiler_params=_cparams(("parallel",)),
        name="filter_spectrum",
    )(a5, tb["twr"], tb["twi"], tb["f2"])


def _fft_mid_kernel(a_ref, kf_ref, twr_ref, twi_ref, f2_ref, f2i_ref, o_ref):
    n2 = FFT_N2
    f2, f2i = f2_ref[...], f2i_ref[...]
    hw = a_ref.shape[-1] // 2

    def slot(j, carry):
        tr, ti = twr_ref[j], twi_ref[j]
        for h in range(2):
            cs = slice(h * hw, (h + 1) * hw)
            xr, xi = _twiddle(a_ref[0, j, :, cs].astype(F32), a_ref[1, j, :, cs].astype(F32), tr, ti)
            x = jnp.concatenate([xr, xi], axis=0).astype(BF16)
            big = jnp.dot(f2, x, preferred_element_type=F32)
            sr, si = big[0:n2], big[n2:2 * n2]
            kr, ki = kf_ref[j, 0, :, cs].astype(F32), kf_ref[j, 1, :, cs].astype(F32)
            y = jnp.concatenate([sr * kr - si * ki, sr * ki + si * kr], axis=0).astype(BF16)
            back = jnp.dot(f2i, y, preferred_element_type=F32)
            br, bi = back[0:n2], back[n2:2 * n2]
            o_ref[0, j, :, cs] = (br * tr + bi * ti).astype(o_ref.dtype)
            o_ref[1, j, :, cs] = (bi * tr - br * ti).astype(o_ref.dtype)
        return carry

    lax.fori_loop(0, FFT_T1, slot, 0, unroll=2)


def _fft_mid(a5, kf, tb):
    bsz, _, n1h, n2, c = a5.shape
    slot = pl.BlockSpec((pl.Squeezed(), 2, FFT_T1, n2, c), lambda k, b: (b, 0, k, 0, 0))
    return pl.pallas_call(
        _fft_mid_kernel,
        grid=(n1h // FFT_T1, bsz),
        in_specs=[
            slot,
            pl.BlockSpec((FFT_T1, 2, n2, c), lambda k, b: (k, 0, 0, 0)),
            pl.BlockSpec((FFT_T1, n2, 1), lambda k, b: (k, 0, 0)),
            pl.BlockSpec((FFT_T1, n2, 1), lambda k, b: (k, 0, 0)),
            pl.BlockSpec((2 * n2, 2 * n2), lambda k, b: (0, 0)),
            pl.BlockSpec((2 * n2, 2 * n2), lambda k, b: (0, 0)),
        ],
        out_specs=slot,
        out_shape=jax.ShapeDtypeStruct(a5.shape, SLOT_DTYPE),
        compiler_params=_cparams(("parallel", "parallel")),
        name="fft_mid",
    )(a5, kf, tb["twr"][:, :, None], tb["twi"][:, :, None], tb["f2"], tb["f2inv"])


def _fft_out_kernel(f_ref, b_ref, o_ref):
    _outer_dft(f_ref[...], b_ref, o_ref)


def _fft_out(b5, f1inv):
    bsz, _, n1h, n2, c = b5.shape
    return pl.pallas_call(
        _fft_out_kernel,
        grid=(bsz, c // FFT_LANES),
        in_specs=[
            pl.BlockSpec(f1inv.shape, lambda b, j: (0, 0)),
            pl.BlockSpec((pl.Squeezed(), 2, n1h, n2, FFT_LANES), lambda b, j: (b, 0, 0, 0, j)),
        ],
        out_specs=pl.BlockSpec((1, n1h, n2, FFT_LANES), lambda b, j: (b, 0, 0, j)),
        out_shape=jax.ShapeDtypeStruct((bsz, n1h, n2, c), BF16),
        compiler_params=_cparams(("parallel", "parallel")),
        name="fft_out",
    )(f1inv, b5).reshape(bsz, n1h * n2, c)


def _hyena_conv(u, kf, tb):
    a5 = _fft_stage1(u, tb["f1"])
    b5 = _fft_mid(a5, kf, tb)
    return _fft_out(b5, tb["f1inv"])


def _rms(x, g):
    inv = lax.rsqrt(jnp.mean(x * x, axis=-1, keepdims=True) + EPS)
    return (x * inv) * g


FFN_CHUNK = 256


def _merge_ffn_kernel(ya_ref, yc_ref, u_ref, x0_ref, dk_ref, ga_ref, gh_ref, x_ref, wa_ref, wh_ref,
                      wo_ref, g_ref, wg_ref, wu_ref, wd_ref, gf_ref, o_ref):
    yh = x0_ref[...].astype(F32) * (yc_ref[...].astype(F32)
                                    + u_ref[...].astype(F32) * dk_ref[...])
    pa = jnp.dot(ya_ref[...], wa_ref[...], preferred_element_type=F32)
    ph = jnp.dot(yh.astype(BF16), wh_ref[...], preferred_element_type=F32)
    merged = (jax.nn.sigmoid(ga_ref[...].astype(F32)) * pa
              + jax.nn.sigmoid(gh_ref[...].astype(F32)) * ph)
    x1 = x_ref[...] + jnp.dot(merged.astype(BF16), wo_ref[...], preferred_element_type=F32)
    h = _rms(x1, g_ref[...]).astype(BF16)
    acc = x1
    for c0 in range(0, wg_ref.shape[1], FFN_CHUNK):
        gate = jnp.dot(h, wg_ref[:, c0:c0 + FFN_CHUNK], preferred_element_type=F32)
        up = jnp.dot(h, wu_ref[:, c0:c0 + FFN_CHUNK], preferred_element_type=F32)
        act = (gate * jax.nn.sigmoid(gate) * up).astype(BF16)
        acc = acc + jnp.dot(act, wd_ref[c0:c0 + FFN_CHUNK, :], preferred_element_type=F32)
    o_ref[...] = _rms(acc, gf_ref[...])


def _merge_ffn(ya, yc, u, x0, d_skip, gates, x2, wa, wh, wo, g, wg, wu, wd, gf):
    n, d = x2.shape
    dff = wg.shape[1]
    assert dff % FFN_CHUNK == 0
    tm = min(512, n)
    tok = pl.BlockSpec((tm, D_HYENA), lambda i: (i, 0))

    def resident(a):
        return pl.BlockSpec(a.shape, lambda i: (0, 0), pipeline_mode=pl.Buffered(1))

    def row(width):
        return pl.BlockSpec((1, width), lambda i: (0, 0))

    return pl.pallas_call(
        _merge_ffn_kernel,
        grid=(n // tm,),
        in_specs=[
            pl.BlockSpec((tm, D_ATTN), lambda i: (i, 0)),
            tok, tok, tok,
            row(D_HYENA),
            pl.BlockSpec((tm, d), lambda i: (i, 0)),
            pl.BlockSpec((tm, d), lambda i: (i, 1)),
            pl.BlockSpec((tm, d), lambda i: (i, 0)),
            resident(wa), resident(wh), resident(wo),
            row(d),
            resident(wg), resident(wu), resident(wd),
            row(d),
        ],
        out_specs=pl.BlockSpec((tm, d), lambda i: (i, 0)),
        out_shape=jax.ShapeDtypeStruct((n, d), F32),
        compiler_params=_cparams(("parallel",)),
        name="merge_ffn",
    )(ya, yc, u, x0, d_skip.astype(F32).reshape(1, D_HYENA), gates, gates, x2, wa, wh, wo,
      g.reshape(1, d), wg, wu, wd, gf.reshape(1, d))


def _trunk(x, p, norm_final):
    bsz, L, d = x.shape
    n = bsz * L
    x2 = x.reshape(n, d)
    qkv, u, x0, gates = _inproj_conv(x2, L, p["norm_mix"], p["w_in"], p["conv_w"], p["conv_b"])
    ya = _attention(qkv.reshape(bsz, L, qkv.shape[1]), p["rpb"])
    u = u.reshape(bsz, L, D_HYENA)
    tb = dict(_fft_tables(L))
    for name in ("f1", "f1inv", "f2", "f2inv"):
        tb[name] = jnp.asarray(tb[name]).astype(BF16)
    hfilt = _implicit_filters(L, *p["filt"])
    kf = _filter_spectrum(hfilt, tb)
    yc = _hyena_conv(u, kf, tb)
    out = _merge_ffn(ya.reshape(n, D_ATTN), yc.reshape(n, D_HYENA), u.reshape(n, D_HYENA),
                     x0, p["hyena_d"], gates, x2,
                     p["w_br_attn"], p["w_br_hyena"], p["w_out"],
                     p["norm_ffn"], p["w_gate"], p["w_up"], p["w_down"], norm_final)
    return out.reshape(bsz, L, d)


def kernel(x_prompt, x_sample, norm_mix, w_in, rpb, conv_w, conv_b, filt_w1, filt_b1, filt_w2,
           filt_b2, filt_w3, filt_b3, filt_w4, filt_freq, hyena_d, w_br_attn, w_br_hyena, w_out,
           norm_ffn, w_gate, w_up, w_down, norm_final):
    assert w_in.shape[0] == 1, "the layer definition has depth 1"
    col = lax.broadcasted_iota(jnp.int32, (1, w_in.shape[2]), 1)
    col_scale = jnp.where(col < D_ATTN, Q_SCALE, 1.0).astype(F32)
    p = dict(
        norm_mix=norm_mix[0], w_in=(w_in[0] * col_scale).astype(BF16), rpb=rpb[0].reshape(-1),
        conv_w=conv_w[0], conv_b=conv_b[0],
        filt=(filt_w1[0], filt_b1[0], filt_w2[0], filt_b2[0], filt_w3[0], filt_b3[0],
              filt_w4[0], filt_freq[0]),
        hyena_d=hyena_d[0],
        w_br_attn=w_br_attn[0].astype(BF16), w_br_hyena=w_br_hyena[0].astype(BF16),
        w_out=w_out[0].astype(BF16), norm_ffn=norm_ffn[0],
        w_gate=w_gate[0].astype(BF16), w_up=w_up[0].astype(BF16),
        w_down=w_down[0].astype(BF16),
    )
    return (_trunk(x_prompt, p, norm_final), _trunk(x_sample, p, norm_final))
```

```python
import functools
import math

import numpy as np
import jax
import jax.numpy as jnp
from jax import lax
from jax.experimental import pallas as pl
from jax.experimental.pallas import tpu as pltpu

F32 = jnp.float32
BF16 = jnp.bfloat16

GRID_W = 64
N_HEADS = 8
HEAD_DIM = 64
D_ATTN = N_HEADS * HEAD_DIM
WIN_R = 8
WIN_C = 16
D_HYENA = 512
SHORT_K = 3
FILTER_EMB = 33
FILTER_HIDDEN = 64
DECAY_TARGET = 1e-2
FAST_DECAY_PCT = 0.3
SLOW_DECAY_PCT = 1.5
EPS = 1e-6

FFT_N2 = 128
ATT_ROWS = 16
ATT_UNROLL = 8
NEG_BIG = -1e30
VMEM_LIMIT = 56 * 1024 * 1024


def _cparams(sem):
    return pltpu.CompilerParams(dimension_semantics=sem, vmem_limit_bytes=VMEM_LIMIT)


IN_CHUNK = 512
HALO = 8


def _inproj_conv_kernel(x_ref, xp_ref, xn_ref, g_ref, wqkv_ref, why_ref, wgt_ref, cw_ref, cb_ref,
                        qkv_ref, u_ref, x0_ref, gt_ref, *, tiles_per_seq):
    i = pl.program_id(0)
    g = g_ref[...]
    h = _rms(x_ref[...], g).astype(BF16)
    hh = _rms(jnp.concatenate([xp_ref[...], xn_ref[...]], axis=0), g).astype(BF16)
    for w_ref, o_ref in ((wqkv_ref, qkv_ref), (wgt_ref, gt_ref)):
        for c0 in range(0, w_ref.shape[1], IN_CHUNK):
            o_ref[:, c0:c0 + IN_CHUNK] = jnp.dot(
                h, w_ref[:, c0:c0 + IN_CHUNK], preferred_element_type=F32).astype(o_ref.dtype)
    tm = h.shape[0]
    c = D_HYENA
    row = lax.broadcasted_iota(jnp.int32, (tm, c), 0)
    first = (i % tiles_per_seq) == 0
    last = (i % tiles_per_seq) == tiles_per_seq - 1
    ys = []
    for k in range(3):
        w = why_ref[:, k * c:(k + 1) * c]
        hy = jnp.dot(h, w, preferred_element_type=F32)
        hyh = jnp.dot(hh, w, preferred_element_type=F32)
        prev_row = jnp.where(first, 0.0, hyh[HALO - 1:HALO])
        next_row = jnp.where(last, 0.0, hyh[HALO:HALO + 1])
        xm1 = jnp.where(row == 0, prev_row, pltpu.roll(hy, 1, 0))
        xp1 = jnp.where(row == tm - 1, next_row, pltpu.roll(hy, tm - 1, 0))
        cw = cw_ref[:, k * c:(k + 1) * c]
        ys.append(xm1 * cw[0:1] + hy * cw[1:2] + xp1 * cw[2:3] + cb_ref[:, k * c:(k + 1) * c])
    x0_ref[...] = ys[0].astype(x0_ref.dtype)
    u_ref[...] = (ys[1] * ys[2]).astype(u_ref.dtype)


def _inproj_conv(x2, L, g, w_bf16, conv_w, conv_b):
    n, d = x2.shape
    n_qkv, n_hy = 3 * D_ATTN, 3 * D_HYENA
    tm = min(1024, L)
    nh = n // HALO
    hb = tm // HALO

    def resident(a):
        return pl.BlockSpec(a.shape, lambda i: (0, 0), pipeline_mode=pl.Buffered(1))

    wqkv, why, wgt = w_bf16[:, :n_qkv], w_bf16[:, n_qkv:n_qkv + n_hy], w_bf16[:, n_qkv + n_hy:]
    widths = (n_qkv, D_HYENA, D_HYENA, wgt.shape[1])
    return pl.pallas_call(
        functools.partial(_inproj_conv_kernel, tiles_per_seq=L // tm),
        grid=(n // tm,),
        in_specs=[
            pl.BlockSpec((tm, d), lambda i: (i, 0)),
            pl.BlockSpec((HALO, d), lambda i: (jnp.maximum(i * hb - 1, 0), 0)),
            pl.BlockSpec((HALO, d), lambda i: (jnp.minimum((i + 1) * hb, nh - 1), 0)),
            pl.BlockSpec((1, d), lambda i: (0, 0)),
            resident(wqkv), resident(why), resident(wgt),
            pl.BlockSpec((SHORT_K, n_hy), lambda i: (0, 0)),
            pl.BlockSpec((1, n_hy), lambda i: (0, 0)),
        ],
        out_specs=[pl.BlockSpec((tm, w), lambda i: (i, 0)) for w in widths],
        out_shape=[jax.ShapeDtypeStruct((n, w), BF16) for w in widths],
        compiler_params=_cparams(("parallel",)),
        name="inproj_conv",
    )(x2, x2, x2, g.reshape(1, d), wqkv, why, wgt, conv_w, conv_b.reshape(1, n_hy))


N_DR = 2 * WIN_R - 1
N_DC = 2 * WIN_C - 1
LOG2_E = 1.4426950408889634
Q_SCALE = (HEAD_DIM ** -0.5) * LOG2_E
KV_ROWS = ATT_ROWS + WIN_R


def _build_bias_tiles(rpb_ref, pt_ref):
    k = lax.broadcasted_iota(jnp.int32, (GRID_W, 128), 0)
    lane = lax.broadcasted_iota(jnp.int32, (GRID_W, 128), 1)
    q = lane & (GRID_W - 1)
    first_half = lane < GRID_W
    cs = jnp.clip(q - WIN_C // 2, 0, GRID_W - WIN_C)
    valid = (k >= cs) & (k < cs + WIN_C)
    dc = jnp.clip(k - q, -(WIN_C - 1), WIN_C - 1) + (WIN_C - 1)

    def body(idx, carry):
        p = idx // N_DR
        d = idx % N_DR
        base0 = ((2 * p) * N_DR + d) * N_DC
        base1 = ((2 * p + 1) * N_DR + d) * N_DC
        acc = jnp.zeros((GRID_W, 128), F32)
        for off in range(N_DC):
            val = jnp.where(first_half, rpb_ref[base0 + off], rpb_ref[base1 + off])
            acc = jnp.where(dc == off, val, acc)
        pt_ref[idx] = jnp.where(valid, acc * LOG2_E, NEG_BIG)
        return carry

    lax.fori_loop(0, (N_HEADS // 2) * N_DR, body, 0)


def _kv_window_start(i, rows):
    return jnp.clip(i * ATT_ROWS - WIN_R // 2, 0, rows - KV_ROWS)


def _attn_kernel(rpb_ref, q_ref, k_ref, v_ref, o_ref, pt_ref, *, rows):
    b = pl.program_id(0)
    i = pl.program_id(1)

    @pl.when((b == 0) & (i == 0))
    def _():
        _build_bias_tiles(rpb_ref, pt_ref)

    lane = lax.broadcasted_iota(jnp.int32, (GRID_W, 128), 1)
    first_half = lane < HEAD_DIM
    ones = jnp.ones((WIN_R * GRID_W, 128), BF16)
    nkeys = WIN_R * GRID_W
    wstart = _kv_window_start(i, rows)
    nt_dims = (((1,), (1,)), ((), ()))

    pairs = range(N_HEADS // 2)

    def scores(rr):
        r = i * ATT_ROWS + rr
        rs = jnp.clip(r - WIN_R // 2, 0, rows - WIN_R)
        s = rs - r + (WIN_R - 1)
        koff = pl.multiple_of((rs - wstart) * GRID_W, GRID_W)
        q = q_ref[pl.ds(pl.multiple_of(rr * GRID_W, GRID_W), GRID_W), :]
        sts = []
        for p in pairs:
            qp = q[:, 128 * p:128 * (p + 1)]
            zero = jnp.zeros_like(qp)
            wt = jnp.concatenate([jnp.where(first_half, qp, zero),
                                  jnp.where(first_half, zero, qp)], axis=0)
            kp = k_ref[pl.ds(koff, nkeys), 128 * p:128 * (p + 1)]
            sts.append(lax.dot_general(kp, wt, nt_dims, preferred_element_type=F32))
        return sts, s, koff

    def probabilities(sts, s):
        pms = []
        for p in pairs:
            st = sts[p]
            ch = [st[GRID_W * c:GRID_W * (c + 1)] + pt_ref[p * N_DR + s + c] for c in range(WIN_R)]
            m = ch[0]
            for c in range(1, WIN_R):
                m = jnp.maximum(m, ch[c])
            m = jnp.max(m, axis=0, keepdims=True)
            pt = jnp.concatenate([jnp.exp2(c - m) for c in ch], axis=0).astype(BF16)
            pms.append(pt.T)
        return pms

    def weighted_values(rr, pms, koff):
        outs = []
        for p in pairs:
            vp = v_ref[pl.ds(koff, nkeys), 128 * p:128 * (p + 1)]
            ov = jnp.dot(pms[p], jnp.concatenate([vp, ones], axis=1), preferred_element_type=F32)
            o = ov[:, 0:128] / ov[:, 128:256]
            outs.append(jnp.where(first_half, o[0:GRID_W], o[GRID_W:2 * GRID_W]))
        qoff = pl.multiple_of(rr * GRID_W, GRID_W)
        o_ref[pl.ds(qoff, GRID_W), :] = jnp.concatenate(outs, axis=1).astype(o_ref.dtype)

    def trip(t, carry):
        r0 = t * ATT_UNROLL
        nxt = scores(r0)
        for j in range(ATT_UNROLL):
            sts, s, koff = nxt
            if j + 1 < ATT_UNROLL:
                nxt = scores(r0 + j + 1)
            weighted_values(r0 + j, probabilities(sts, s), koff)
        return carry

    lax.fori_loop(0, ATT_ROWS // ATT_UNROLL, trip, 0)


def _attention(z3, rpb_flat):
    bsz, L, _ = z3.shape
    rows = L // GRID_W
    assert rows % ATT_ROWS == 0 and rows >= KV_ROWS
    nblk = rows // ATT_ROWS
    blk = ATT_ROWS * GRID_W

    def window(col):
        return pl.BlockSpec(
            (pl.Squeezed(), pl.Element(KV_ROWS * GRID_W), pl.Element(D_ATTN)),
            lambda b, i: (b, _kv_window_start(i, rows) * GRID_W, col * D_ATTN))

    return pl.pallas_call(
        functools.partial(_attn_kernel, rows=rows),
        grid=(bsz, nblk),
        in_specs=[
            pl.BlockSpec(memory_space=pltpu.SMEM),
            pl.BlockSpec((pl.Squeezed(), blk, D_ATTN), lambda b, i: (b, i, 0)),
            window(1),
            window(2),
        ],
        out_specs=pl.BlockSpec((pl.Squeezed(), blk, D_ATTN), lambda b, i: (b, i, 0)),
        out_shape=jax.ShapeDtypeStruct((bsz, L, D_ATTN), BF16),
        scratch_shapes=[pltpu.VMEM(((N_HEADS // 2) * N_DR, GRID_W, 128), F32)],
        compiler_params=_cparams(("arbitrary", "arbitrary")),
        name="nbr_attention",
    )(rpb_flat, z3, z3, z3)


TWO_OVER_PI = 0.6366197723675814
PIO2_1 = 1.5703125
PIO2_2 = 4.837512969970703125e-4
PIO2_3 = 7.54978995489188216e-8
TRIG_FAST_LIMIT = 4096.0


def _quadrant_value(x, shift):
    kf = jnp.floor(x * TWO_OVER_PI + 0.5)
    r = ((x - kf * PIO2_1) - kf * PIO2_2) - kf * PIO2_3
    z = r * r
    s = r + r * z * (-1.6666654611e-1 + z * (8.3321608736e-3 + z * -1.9515295891e-4))
    c = (1.0 - 0.5 * z) + z * z * (4.166664568298827e-2
                                   + z * (-1.388731625493765e-3 + z * 2.443315711809948e-5))
    k = kf + shift
    half = jnp.floor(k * 0.5)
    odd = k - 2.0 * half
    flip = half - 2.0 * jnp.floor(half * 0.5)
    return (s + odd * (c - s)) * (1.0 - 2.0 * flip)


def _sin_small(x):
    return _quadrant_value(x, 0.0)


def _cos_small(x):
    return _quadrant_value(x, 1.0)


def _sin(x):
    return lax.cond(jnp.max(jnp.abs(x)) < TRIG_FAST_LIMIT, _sin_small, jnp.sin, x)


def _cos(x):
    return lax.cond(jnp.max(jnp.abs(x)) < TRIG_FAST_LIMIT, _cos_small, jnp.cos, x)


def _hi_lo(x):
    hi = x.astype(BF16)
    return hi, (x - hi.astype(F32)).astype(BF16)


def _stack_cols(w):
    hi, lo = _hi_lo(w)
    return jnp.concatenate([hi, hi, lo], axis=1)


def _stack_rows(h):
    hi, lo = _hi_lo(h)
    return jnp.concatenate([hi, lo, hi], axis=0)


def _dot_stacked(w_stacked, h):
    return jnp.dot(w_stacked, _stack_rows(h), preferred_element_type=F32)


def _filter_kernel(fb_ref, w1t_ref, w1c_ref, w1s_ref, b1_ref, w2_ref, b2_ref, w3_ref, b3_ref,
                   w4_ref, fr_ref, dl_ref, o_ref, *, L, tp):
    j = pl.program_id(0)
    fh = FILTER_HIDDEN
    pos = (lax.broadcasted_iota(jnp.int32, (1, tp), 1) + j * tp).astype(F32)
    t = pos / (L - 1.0)
    omega = (2.0 * math.pi) * pos / float(L)
    ang = fb_ref[...] * omega
    fr = fr_ref[...]
    pre = (w1t_ref[...] * t
           + _dot_stacked(w1c_ref[...], _cos(ang)) - _dot_stacked(w1s_ref[...], _sin(ang)))
    h = _sin(fr * (pre + b1_ref[...]))
    h = _sin(fr * (_dot_stacked(w2_ref[...], h) + b2_ref[...]))
    h = _sin(fr * (_dot_stacked(w3_ref[...], h) + b3_ref[...]))
    h_hi = h.astype(BF16).astype(F32)
    tr = jnp.concatenate([h_hi, h - h_hi], axis=0).T.astype(BF16)
    lane = lax.broadcasted_iota(jnp.int32, tr.shape, 1)
    lhs = jnp.concatenate([tr, jnp.where(lane < fh, tr, jnp.zeros_like(tr))], axis=1)
    out = jnp.dot(lhs, w4_ref[...], preferred_element_type=F32)
    tcol = (lax.broadcasted_iota(jnp.int32, (tp, 1), 0) + j * tp).astype(F32) / (L - 1.0)
    decay = jnp.exp(-tcol * dl_ref[...])
    c = D_HYENA
    o_ref[:, 0:c] = (out[:, 0:c] * decay).astype(o_ref.dtype)
    o_ref[:, c:2 * c] = (out[:, c:2 * c] * decay).astype(o_ref.dtype)


def _implicit_filters(L, w1, b1, w2, b2, w3, b3, w4, freq):
    bands = (FILTER_EMB - 1) // 2
    fh = FILTER_HIDDEN
    tp = min(1024, L)
    fb = jnp.linspace(1e-4, bands - 1, bands, dtype=F32).reshape(bands, 1)
    max_decay = math.log(DECAY_TARGET) / FAST_DECAY_PCT
    min_decay = math.log(DECAY_TARGET) / SLOW_DECAY_PCT
    deltas = jnp.abs(jnp.linspace(min_decay, max_decay, D_HYENA, dtype=F32)).reshape(1, D_HYENA)
    w1 = w1.astype(F32)
    w4_hi, w4_lo = _hi_lo(w4.astype(F32))
    args = (
        fb,
        w1[0:1, :].T,
        _stack_cols(w1[1:1 + bands, :].T),
        _stack_cols(w1[1 + bands:, :].T),
        b1.astype(F32).reshape(fh, 1),
        _stack_cols(w2.astype(F32).T), b2.astype(F32).reshape(fh, 1),
        _stack_cols(w3.astype(F32).T), b3.astype(F32).reshape(fh, 1),
        jnp.concatenate([w4_hi, w4_hi, w4_lo, jnp.zeros_like(w4_hi)], axis=0),
        freq.astype(F32).reshape(fh, 1),
        deltas,
    )

    def full(a):
        return pl.BlockSpec(a.shape, lambda j: (0,) * a.ndim)

    return pl.pallas_call(
        functools.partial(_filter_kernel, L=L, tp=tp),
        grid=(L // tp,),
        in_specs=[full(a) for a in args],
        out_specs=pl.BlockSpec((tp, 2 * D_HYENA), lambda j: (j, 0)),
        out_shape=jax.ShapeDtypeStruct((L, 2 * D_HYENA), BF16),
        compiler_params=_cparams(("parallel",)),
        name="implicit_filter",
    )(*args)


FFT_LANES = 128
FFT_T1 = 16


@functools.lru_cache(maxsize=None)
def _fft_tables(L):
    n = 2 * L
    n2 = FFT_N2
    n1 = n // n2
    n1h = n1 // 2
    odd = 2 * np.arange(n1h) + 1
    th = 2.0 * np.pi * (np.outer(odd, np.arange(n1h)) % (2 * n1)) / (2 * n1)
    f1 = np.concatenate([np.cos(th), -np.sin(th)], axis=0)
    tw = 2.0 * np.pi * (np.outer(odd, np.arange(n2)) % (2 * n)) / (2 * n)
    twr = np.cos(tw)
    twi = -np.sin(tw)
    t2 = 2.0 * np.pi * (np.outer(np.arange(n2), np.arange(n2)) % n2) / n2
    fr, fi = np.cos(t2), -np.sin(t2)
    f2 = np.block([[fr, -fi], [fi, fr]])
    f2inv = np.block([[fr, fi], [-fi, fr]])
    f1inv = np.concatenate([np.cos(th.T), -np.sin(th.T)], axis=1) * (2.0 / n)
    return dict(
        n1=n1, n1h=n1h,
        f1=np.asarray(f1, np.float32), f1inv=np.asarray(f1inv, np.float32),
        f2=np.asarray(f2, np.float32), f2inv=np.asarray(f2inv, np.float32),
        twr=np.asarray(twr, np.float32), twi=np.asarray(twi, np.float32),
    )


FFT_A = 16
SLOT_DTYPE = BF16


def _outer_dft(f, src_ref, dst_ref):
    n_in, n1h = src_ref.shape[0], src_ref.shape[1]
    n_out = dst_ref.shape[0]

    def body(i, carry):
        a0 = pl.multiple_of(i * FFT_A, FFT_A)
        ts = [jnp.swapaxes(src_ref[p, :, pl.ds(a0, FFT_A), :].astype(F32), 0, 1)
              for p in range(n_in)]
        outs = [[] for _ in range(n_out)]
        for j in range(0, FFT_A, 2):
            rhs = jnp.concatenate([jnp.concatenate([t[j], t[j + 1]], axis=1) for t in ts],
                                  axis=0).astype(BF16)
            res = jnp.dot(f, rhs, preferred_element_type=F32)
            for q in range(n_out):
                blk = res[q * n1h:(q + 1) * n1h]
                outs[q] += [blk[:, 0:FFT_LANES], blk[:, FFT_LANES:2 * FFT_LANES]]
        for q in range(n_out):
            dst_ref[q, :, pl.ds(a0, FFT_A), :] = jnp.swapaxes(
                jnp.stack(outs[q], axis=0), 0, 1).astype(dst_ref.dtype)
        return carry

    lax.fori_loop(0, FFT_N2 // FFT_A, body, 0, unroll=2)


def _fft_stage1_kernel(f_ref, u_ref, o_ref):
    _outer_dft(f_ref[...], u_ref, o_ref)


def _fft_stage1(u3, f1):
    bsz, L, c = u3.shape
    n1h = L // FFT_N2
    return pl.pallas_call(
        _fft_stage1_kernel,
        grid=(bsz, c // FFT_LANES),
        in_specs=[
            pl.BlockSpec(f1.shape, lambda b, j: (0, 0)),
            pl.BlockSpec((1, n1h, FFT_N2, FFT_LANES), lambda b, j: (b, 0, 0, j)),
        ],
        out_specs=pl.BlockSpec((pl.Squeezed(), 2, n1h, FFT_N2, FFT_LANES),
                               lambda b, j: (b, 0, 0, 0, j)),
        out_shape=jax.ShapeDtypeStruct((bsz, 2, n1h, FFT_N2, c), SLOT_DTYPE),
        compiler_params=_cparams(("parallel", "parallel")),
        name="fft_stage1",
    )(f1, u3.reshape(bsz, n1h, FFT_N2, c))


def _twiddle(ar, ai, tr, ti):
    return ar * tr - ai * ti, ar * ti + ai * tr


def _tw_columns(tw_ref):
    t = tw_ref[...].T
    return [t[:, j:j + 1] for j in range(FFT_T1)]


def _stage2_inputs(a_ref, trs, tis):
    xs = []
    for j in range(FFT_T1):
        xr, xi = _twiddle(a_ref[0, j].astype(F32), a_ref[1, j].astype(F32), trs[j], tis[j])
        xs.append(jnp.concatenate([xr, xi], axis=0).astype(BF16))
    return xs


def _filter_spectrum_kernel(a_ref, twr_ref, twi_ref, f2_ref, o_ref):
    n2, c = FFT_N2, D_HYENA
    f2 = f2_ref[...]
    bigs = [jnp.dot(f2, x, preferred_element_type=F32)
            for x in _stage2_inputs(a_ref, _tw_columns(twr_ref), _tw_columns(twi_ref))]
    for j, big in enumerate(bigs):
        o_ref[j, 0] = (big[0:n2, 0:c] + big[0:n2, c:2 * c]).astype(o_ref.dtype)
        o_ref[j, 1] = (big[n2:2 * n2, 0:c] - big[n2:2 * n2, c:2 * c]).astype(o_ref.dtype)


def _filter_spectrum(hfilt, tb):
    L, c2 = hfilt.shape
    n1h = tb["n1h"]
    a5 = _fft_stage1(hfilt.reshape(1, L, c2), tb["f1"])
    return pl.pallas_call(
        _filter_spectrum_kernel,
        grid=(n1h // FFT_T1,),
        in_specs=[
            pl.BlockSpec((pl.Squeezed(), 2, FFT_T1, FFT_N2, c2), lambda k: (0, 0, k, 0, 0)),
            pl.BlockSpec((FFT_T1, FFT_N2), lambda k: (k, 0)),
            pl.BlockSpec((FFT_T1, FFT_N2), lambda k: (k, 0)),
            pl.BlockSpec((2 * FFT_N2, 2 * FFT_N2), lambda k: (0, 0)),
        ],
        out_specs=pl.BlockSpec((FFT_T1, 2, FFT_N2, D_HYENA), lambda k: (k, 0, 0, 0)),
        out_shape=jax.ShapeDtypeStruct((n1h, 2, FFT_N2, D_HYENA), SLOT_DTYPE),
        compiler_params=_cparams(("parallel",)),
        name="filter_spectrum",
    )(a5, tb["twr"], tb["twi"], tb["f2"])


def _fft_mid_kernel(a_ref, kf_ref, twr_ref, twi_ref, f2_ref, f2i_ref, o_ref):
    n2 = FFT_N2
    f2, f2i = f2_ref[...], f2i_ref[...]
    trs, tis = _tw_columns(twr_ref), _tw_columns(twi_ref)
    bigs = [jnp.dot(f2, x, preferred_element_type=F32)
            for x in _stage2_inputs(a_ref, trs, tis)]
    ys = []
    for j, big in enumerate(bigs):
        sr, si = big[0:n2], big[n2:2 * n2]
        kr, ki = kf_ref[j, 0].astype(F32), kf_ref[j, 1].astype(F32)
        ys.append(jnp.concatenate([sr * kr - si * ki, sr * ki + si * kr], axis=0).astype(BF16))
    backs = [jnp.dot(f2i, y, preferred_element_type=F32) for y in ys]
    for j, back in enumerate(backs):
        br, bi = back[0:n2], back[n2:2 * n2]
        tr, ti = trs[j], tis[j]
        o_ref[0, j] = (br * tr + bi * ti).astype(o_ref.dtype)
        o_ref[1, j] = (bi * tr - br * ti).astype(o_ref.dtype)


def _fft_mid(a5, kf, tb):
    bsz, _, n1h, n2, c = a5.shape
    slot = pl.BlockSpec((pl.Squeezed(), 2, FFT_T1, n2, c), lambda k, b: (b, 0, k, 0, 0))
    return pl.pallas_call(
        _fft_mid_kernel,
        grid=(n1h // FFT_T1, bsz),
        in_specs=[
            slot,
            pl.BlockSpec((FFT_T1, 2, n2, c), lambda k, b: (k, 0, 0, 0)),
            pl.BlockSpec((FFT_T1, n2), lambda k, b: (k, 0)),
            pl.BlockSpec((FFT_T1, n2), lambda k, b: (k, 0)),
            pl.BlockSpec((2 * n2, 2 * n2), lambda k, b: (0, 0)),
            pl.BlockSpec((2 * n2, 2 * n2), lambda k, b: (0, 0)),
        ],
        out_specs=slot,
        out_shape=jax.ShapeDtypeStruct(a5.shape, SLOT_DTYPE),
        compiler_params=_cparams(("parallel", "parallel")),
        name="fft_mid",
    )(a5, kf, tb["twr"], tb["twi"], tb["f2"], tb["f2inv"])


def _fft_out_kernel(f_ref, b_ref, o_ref):
    _outer_dft(f_ref[...], b_ref, o_ref)


def _fft_out(b5, f1inv):
    bsz, _, n1h, n2, c = b5.shape
    return pl.pallas_call(
        _fft_out_kernel,
        grid=(bsz, c // FFT_LANES),
        in_specs=[
            pl.BlockSpec(f1inv.shape, lambda b, j: (0, 0)),
            pl.BlockSpec((pl.Squeezed(), 2, n1h, n2, FFT_LANES), lambda b, j: (b, 0, 0, 0, j)),
        ],
        out_specs=pl.BlockSpec((1, n1h, n2, FFT_LANES), lambda b, j: (b, 0, 0, j)),
        out_shape=jax.ShapeDtypeStruct((bsz, n1h, n2, c), BF16),
        compiler_params=_cparams(("parallel", "parallel")),
        name="fft_out",
    )(f1inv, b5).reshape(bsz, n1h * n2, c)


def _hyena_conv(u, kf, tb):
    a5 = _fft_stage1(u, tb["f1"])
    b5 = _fft_mid(a5, kf, tb)
    return _fft_out(b5, tb["f1inv"])


def _rms(x, g):
    inv = lax.rsqrt(jnp.mean(x * x, axis=-1, keepdims=True) + EPS)
    return (x * inv) * g


FFN_CHUNK = 256


def _merge_ffn_kernel(ya_ref, yc_ref, u_ref, x0_ref, dk_ref, ga_ref, gh_ref, x_ref, wa_ref, wh_ref,
                      wo_ref, g_ref, wg_ref, wu_ref, wd_ref, gf_ref, o_ref):
    tm = x_ref.shape[0]
    halves = (slice(0, tm // 2), slice(tm // 2, tm))

    def mixer_products(rs):
        yh = x0_ref[rs, :].astype(F32) * (yc_ref[rs, :].astype(F32)
                                          + u_ref[rs, :].astype(F32) * dk_ref[...])
        return (jnp.dot(ya_ref[rs, :], wa_ref[...], preferred_element_type=F32),
                jnp.dot(yh.astype(BF16), wh_ref[...], preferred_element_type=F32))

    prods = [mixer_products(rs) for rs in halves]
    x1s = []
    for rs, (pa, ph) in zip(halves, prods):
        merged = (jax.nn.sigmoid(ga_ref[rs, :].astype(F32)) * pa
                  + jax.nn.sigmoid(gh_ref[rs, :].astype(F32)) * ph)
        x1s.append(x_ref[rs, :] + jnp.dot(merged.astype(BF16), wo_ref[...],
                                          preferred_element_type=F32))
    hs = [_rms(x1, g_ref[...]).astype(BF16) for x1 in x1s]
    accs = list(x1s)
    for c0 in range(0, wg_ref.shape[1], FFN_CHUNK):
        for k in range(2):
            gate = jnp.dot(hs[k], wg_ref[:, c0:c0 + FFN_CHUNK], preferred_element_type=F32)
            up = jnp.dot(hs[k], wu_ref[:, c0:c0 + FFN_CHUNK], preferred_element_type=F32)
            act = (gate * jax.nn.sigmoid(gate) * up).astype(BF16)
            accs[k] = accs[k] + jnp.dot(act, wd_ref[c0:c0 + FFN_CHUNK, :],
                                        preferred_element_type=F32)
    for rs, acc in zip(halves, accs):
        o_ref[rs, :] = _rms(acc, gf_ref[...])


def _merge_ffn(ya, yc, u, x0, d_skip, gates, x2, wa, wh, wo, g, wg, wu, wd, gf):
    n, d = x2.shape
    dff = wg.shape[1]
    assert dff % FFN_CHUNK == 0
    tm = min(512, n)
    tok = pl.BlockSpec((tm, D_HYENA), lambda i: (i, 0))

    def resident(a):
        return pl.BlockSpec(a.shape, lambda i: (0, 0), pipeline_mode=pl.Buffered(1))

    def row(width):
        return pl.BlockSpec((1, width), lambda i: (0, 0))

    return pl.pallas_call(
        _merge_ffn_kernel,
        grid=(n // tm,),
        in_specs=[
            pl.BlockSpec((tm, D_ATTN), lambda i: (i, 0)),
            tok, tok, tok,
            row(D_HYENA),
            pl.BlockSpec((tm, d), lambda i: (i, 0)),
            pl.BlockSpec((tm, d), lambda i: (i, 1)),
            pl.BlockSpec((tm, d), lambda i: (i, 0)),
            resident(wa), resident(wh), resident(wo),
            row(d),
            resident(wg), resident(wu), resident(wd),
            row(d),
        ],
        out_specs=pl.BlockSpec((tm, d), lambda i: (i, 0)),
        out_shape=jax.ShapeDtypeStruct((n, d), F32),
        compiler_params=_cparams(("parallel",)),
        name="merge_ffn",
    )(ya, yc, u, x0, d_skip.astype(F32).reshape(1, D_HYENA), gates, gates, x2, wa, wh, wo,
      g.reshape(1, d), wg, wu, wd, gf.reshape(1, d))


def _trunk(x, p, norm_final):
    bsz, L, d = x.shape
    n = bsz * L
    x2 = x.reshape(n, d)
    qkv, u, x0, gates = _inproj_conv(x2, L, p["norm_mix"], p["w_in"], p["conv_w"], p["conv_b"])
    ya = _attention(qkv.reshape(bsz, L, qkv.shape[1]), p["rpb"])
    u = u.reshape(bsz, L, D_HYENA)
    tb = dict(_fft_tables(L))
    for name in ("f1", "f1inv", "f2", "f2inv"):
        tb[name] = jnp.asarray(tb[name]).astype(BF16)
    hfilt = _implicit_filters(L, *p["filt"])
    kf = _filter_spectrum(hfilt, tb)
    yc = _hyena_conv(u, kf, tb)
    out = _merge_ffn(ya.reshape(n, D_ATTN), yc.reshape(n, D_HYENA), u.reshape(n, D_HYENA),
                     x0, p["hyena_d"], gates, x2,
                     p["w_br_attn"], p["w_br_hyena"], p["w_out"],
                     p["norm_ffn"], p["w_gate"], p["w_up"], p["w_down"], norm_final)
    return out.reshape(bsz, L, d)


def kernel(x_prompt, x_sample, norm_mix, w_in, rpb, conv_w, conv_b, filt_w1, filt_b1, filt_w2,
           filt_b2, filt_w3, filt_b3, filt_w4, filt_freq, hyena_d, w_br_attn, w_br_hyena, w_out,
           norm_ffn, w_gate, w_up, w_down, norm_final):
    assert w_in.shape[0] == 1, "the layer definition has depth 1"
    col = lax.broadcasted_iota(jnp.int32, (1, w_in.shape[2]), 1)
    col_scale = jnp.where(col < D_ATTN, Q_SCALE, 1.0).astype(F32)
    p = dict(
        norm_mix=norm_mix[0], w_in=(w_in[0] * col_scale).astype(BF16), rpb=rpb[0].reshape(-1),
        conv_w=conv_w[0], conv_b=conv_b[0],
        filt=(filt_w1[0], filt_b1[0], filt_w2[0], filt_b2[0], filt_w3[0], filt_b3[0],
              filt_w4[0], filt_freq[0]),
        hyena_d=hyena_d[0],
        w_br_attn=w_br_attn[0].astype(BF16), w_br_hyena=w_br_hyena[0].astype(BF16),
        w_out=w_out[0].astype(BF16), norm_ffn=norm_ffn[0],
        w_gate=w_gate[0].astype(BF16), w_up=w_up[0].astype(BF16),
        w_down=w_down[0].astype(BF16),
    )
    return (_trunk(x_prompt, p, norm_final), _trunk(x_sample, p, norm_final))
```
